```python
import math
import jax
import jax.numpy as jnp
from jax import lax
import numpy as np


D_MODEL = 1024
BATCH = 16
SEQ = 4096
DEPTH = 1

MLA_HEADS = 8
MLA_NOPE_DIM = 64
MLA_ROPE_DIM = 32
MLA_V_DIM = 64
MLA_Q_LORA = 256
MLA_KV_LORA = 128
ROPE_THETA = 10000.0

NSA_HEADS = 8
NSA_KV_HEADS = 2
NSA_GROUP = NSA_HEADS // NSA_KV_HEADS
NSA_HEAD_DIM = 64
CMP_LEN = 32
CMP_STRIDE = 16
CMP_HIDDEN = 128
SLC_LEN = 64
SLC_TOPN = 16
WINDOW = 512
N_BRANCH = 3

T5_BUCKETS = 32
T5_MAX_DIST = 128

MLA_OUT = MLA_HEADS * MLA_V_DIM
NSA_OUT = NSA_HEADS * NSA_HEAD_DIM
D_MIX = MLA_OUT + NSA_OUT
D_FF = -(-8 * D_MODEL // (3 * 256)) * 256

IN_SPLITS = (
    MLA_Q_LORA,
    MLA_KV_LORA,
    MLA_ROPE_DIM,
    NSA_HEADS * NSA_HEAD_DIM,
    2 * NSA_KV_HEADS * NSA_HEAD_DIM,
    2 * NSA_KV_HEADS * NSA_HEAD_DIM,
    2 * NSA_KV_HEADS * NSA_HEAD_DIM,
    NSA_HEADS * N_BRANCH,
)
D_IN = sum(IN_SPLITS)

Q_BLOCK = 128
NSA_Q_BLOCK = 64
EPS = 1e-6
NEG_INF = -1e30
FORCE_SCORE = 1e30

kernel_name = 'hybrid_mla_nsa_block'


def rmsnorm(x, g):
    xf = x.astype(jnp.float32)
    y = xf * lax.rsqrt(jnp.mean(xf * xf, axis=-1, keepdims=True) + EPS)
    return (y * g.astype(jnp.float32)).astype(x.dtype)


def rope_tables(seq):
    pos = jnp.arange(seq, dtype=jnp.float32)
    inv = ROPE_THETA ** (-jnp.arange(0, MLA_ROPE_DIM, 2, dtype=jnp.float32) / MLA_ROPE_DIM)
    ang = pos[:, None] * inv[None, :]
    return jnp.cos(ang), jnp.sin(ang)


def apply_rope(x, cos, sin):
    half = x.shape[-1] // 2
    x1 = x[..., :half].astype(jnp.float32)
    x2 = x[..., half:].astype(jnp.float32)
    return jnp.concatenate([x1 * cos - x2 * sin, x1 * sin + x2 * cos], axis=-1).astype(x.dtype)


def t5_bucket(dist):
    n = jnp.maximum(dist, 0)
    max_exact = T5_BUCKETS // 2
    nf = jnp.maximum(n, 1).astype(jnp.float32)
    large = max_exact + (jnp.log(nf / max_exact) / math.log(T5_MAX_DIST / max_exact)
                         * (T5_BUCKETS - max_exact)).astype(jnp.int32)
    large = jnp.minimum(large, T5_BUCKETS - 1)
    return jnp.where(n < max_exact, n, large)


def mla_mixer(c_q, c_kv, k_rope, q_norm_g, w_uq, kv_norm_g, w_ukv):
    B, S, _ = c_q.shape
    cos, sin = rope_tables(S)
    q = (rmsnorm(c_q, q_norm_g) @ w_uq).reshape(B, S, MLA_HEADS, MLA_NOPE_DIM + MLA_ROPE_DIM)
    q_nope, q_pe = q[..., :MLA_NOPE_DIM], q[..., MLA_NOPE_DIM:]
    q_pe = apply_rope(q_pe, cos[None, :, None], sin[None, :, None])
    kv = (rmsnorm(c_kv, kv_norm_g) @ w_ukv).reshape(B, S, MLA_HEADS, MLA_NOPE_DIM + MLA_V_DIM)
    k_nope, v = kv[..., :MLA_NOPE_DIM], kv[..., MLA_NOPE_DIM:]
    k_pe = apply_rope(k_rope, cos[None], sin[None])
    k_pe = jnp.broadcast_to(k_pe[:, :, None, :], (B, S, MLA_HEADS, MLA_ROPE_DIM))
    q = jnp.concatenate([q_nope, q_pe], axis=-1).transpose(0, 2, 1, 3)
    k = jnp.concatenate([k_nope, k_pe], axis=-1).transpose(0, 2, 1, 3)
    v = v.transpose(0, 2, 1, 3)
    scale = (MLA_NOPE_DIM + MLA_ROPE_DIM) ** -0.5
    key_pos = jnp.arange(S)

    def attend_block(i):
        start = i * Q_BLOCK
        qb = lax.dynamic_slice_in_dim(q, start, Q_BLOCK, axis=2)
        s = jnp.einsum('bhqd,bhkd->bhqk', qb, k, preferred_element_type=jnp.float32) * scale
        q_pos = start + jnp.arange(Q_BLOCK)
        s = jnp.where(key_pos[None, :] <= q_pos[:, None], s, NEG_INF)
        p = jax.nn.softmax(s, axis=-1).astype(v.dtype)
        return jnp.einsum('bhqk,bhkd->bhqd', p, v)

    o = lax.map(attend_block, jnp.arange(S // Q_BLOCK))
    return o.transpose(1, 0, 3, 2, 4).reshape(B, S, MLA_OUT)


def nsa_compress(raw, pos_emb, w1, w2, cmp_idx):
    B = raw.shape[0]
    n_cmp = cmp_idx.shape[0]
    blk = raw[:, cmp_idx] + pos_emb[None, None, :, None, :]
    blk = blk.transpose(0, 3, 1, 2, 4).reshape(B, NSA_KV_HEADS, n_cmp, CMP_LEN * NSA_HEAD_DIM)
    return jax.nn.gelu(blk @ w1) @ w2


def nsa_mixer(q, kv_cmp, kv_slc, kv_win, gate_logits, pos_k, w1_k, w2_k, pos_v, w1_v, w2_v, t5_table):
    B, S, _ = q.shape
    Hk, G, dk = NSA_KV_HEADS, NSA_GROUP, NSA_HEAD_DIM
    q = q.reshape(B, S, Hk, G, dk).transpose(0, 2, 3, 1, 4)

    def split_kv(kv):
        kv = kv.reshape(B, S, 2, Hk, dk)
        return kv[:, :, 0], kv[:, :, 1]

    n_cmp = (S - CMP_LEN) // CMP_STRIDE + 1
    cmp_start = np.arange(n_cmp) * CMP_STRIDE
    cmp_idx = cmp_start[:, None] + np.arange(CMP_LEN)[None, :]
    cmp_end = jnp.asarray(cmp_start + CMP_LEN - 1, dtype=jnp.int32)
    k_c_raw, v_c_raw = split_kv(kv_cmp)
    k_cmp = nsa_compress(k_c_raw, pos_k, w1_k, w2_k, cmp_idx)
    v_cmp = nsa_compress(v_c_raw, pos_v, w1_v, w2_v, cmp_idx)

    n_slc = S // SLC_LEN
    n_sel = min(SLC_TOPN, n_slc)
    slc_start = np.arange(n_slc) * SLC_LEN
    overlap = np.maximum(0, np.minimum(cmp_start[:, None] + CMP_LEN, slc_start[None, :] + SLC_LEN)
                         - np.maximum(cmp_start[:, None], slc_start[None, :])).astype(np.float32) / CMP_STRIDE
    overlap = jnp.asarray(overlap)
    k_s, v_s = split_kv(kv_slc)
    k_slc = k_s.transpose(0, 2, 1, 3).reshape(B, Hk, n_slc, SLC_LEN, dk)
    v_slc = v_s.transpose(0, 2, 1, 3).reshape(B, Hk, n_slc, SLC_LEN, dk)

    k_w, v_w = split_kv(kv_win)
    pad = ((0, 0), (0, 0), (WINDOW, 0), (0, 0))
    k_win = jnp.pad(k_w.transpose(0, 2, 1, 3), pad)
    v_win = jnp.pad(v_w.transpose(0, 2, 1, 3), pad)

    gates = jax.nn.sigmoid(gate_logits.astype(jnp.float32)).reshape(B, S, Hk, G, N_BRANCH)
    gates = gates.transpose(0, 2, 3, 1, 4).astype(q.dtype)
    tbl = t5_table.T.reshape(Hk, G, T5_BUCKETS)
    scale = dk ** -0.5
    blk_ids = jnp.arange(n_slc)
    b_ix = jnp.arange(B)[:, None, None, None]
    h_ix = jnp.arange(Hk)[None, :, None, None]
    tk_ix = jnp.arange(Hk)[None, :, None, None, None]
    tg_ix = jnp.arange(G)[None, None, :, None, None]

    def nsa_block(i):
        start = i * NSA_Q_BLOCK
        q_pos = start + jnp.arange(NSA_Q_BLOCK)
        qb = lax.dynamic_slice_in_dim(q, start, NSA_Q_BLOCK, axis=3)

        dist_c = q_pos[:, None] - cmp_end[None, :]
        valid_c = dist_c >= 0
        s_c = jnp.einsum('bkgqd,bknd->bkgqn', qb, k_cmp, preferred_element_type=jnp.float32) * scale
        s_c = s_c + tbl[:, :, t5_bucket(dist_c)]
        p_c = jax.nn.softmax(jnp.where(valid_c, s_c, NEG_INF), axis=-1) * valid_c
        o_c = jnp.einsum('bkgqn,bknd->bkgqd', p_c.astype(v_cmp.dtype), v_cmp)

        imp = jnp.einsum('bkgqn,nj->bkqj', p_c, overlap)
        cur = q_pos // SLC_LEN
        forced = ((blk_ids[None, :] == 0) | (blk_ids[None, :] == cur[:, None])
                  | (blk_ids[None, :] == cur[:, None] - 1))
        causal_blk = blk_ids[None, :] <= cur[:, None]
        imp = jnp.where(forced, FORCE_SCORE, jnp.where(causal_blk, imp, NEG_INF))
        _, sel = lax.top_k(imp, n_sel)
        k_sel = k_slc[b_ix, h_ix, sel].reshape(B, Hk, NSA_Q_BLOCK, n_sel * SLC_LEN, dk)
        v_sel = v_slc[b_ix, h_ix, sel].reshape(B, Hk, NSA_Q_BLOCK, n_sel * SLC_LEN, dk)
        pos_sel = (sel[..., None] * SLC_LEN + jnp.arange(SLC_LEN)).reshape(B, Hk, NSA_Q_BLOCK, n_sel * SLC_LEN)
        dist_s = q_pos[None, None, :, None] - pos_sel
        s_s = jnp.einsum('bkgqd,bkqnd->bkgqn', qb, k_sel, preferred_element_type=jnp.float32) * scale
        s_s = s_s + tbl[tk_ix, tg_ix, t5_bucket(dist_s)[:, :, None]]
        p_s = jax.nn.softmax(jnp.where((dist_s >= 0)[:, :, None], s_s, NEG_INF), axis=-1)
        o_s = jnp.einsum('bkgqn,bkqnd->bkgqd', p_s.astype(v_sel.dtype), v_sel)

        k_band = lax.dynamic_slice_in_dim(k_win, start, NSA_Q_BLOCK + WINDOW, axis=2)
        v_band = lax.dynamic_slice_in_dim(v_win, start, NSA_Q_BLOCK + WINDOW, axis=2)
        key_pos_w = start - WINDOW + jnp.arange(NSA_Q_BLOCK + WINDOW)
        dist_w = q_pos[:, None] - key_pos_w[None, :]
        valid_w = (dist_w >= 0) & (dist_w < WINDOW) & (key_pos_w[None, :] >= 0)
        s_w = jnp.einsum('bkgqd,bknd->bkgqn', qb, k_band, preferred_element_type=jnp.float32) * scale
        s_w = s_w + tbl[:, :, t5_bucket(dist_w)]
        p_w = jax.nn.softmax(jnp.where(valid_w, s_w, NEG_INF), axis=-1)
        o_w = jnp.einsum('bkgqn,bknd->bkgqd', p_w.astype(v_band.dtype), v_band)

        g = lax.dynamic_slice_in_dim(gates, start, NSA_Q_BLOCK, axis=3)
        return g[..., 0:1] * o_c + g[..., 1:2] * o_s + g[..., 2:3] * o_w

    o = lax.map(nsa_block, jnp.arange(S // NSA_Q_BLOCK))
    return o.transpose(1, 0, 4, 2, 3, 5).reshape(B, S, NSA_OUT)


def setup_inputs(seed: int = 0) -> dict:
    key = jax.random.key(seed)
    ks = jax.random.split(key, 24)
    f32 = jnp.float32

    def nrm(k, shape, fan_in):
        return jax.random.normal(k, shape, f32) * fan_in ** -0.5

    def gain(k, shape):
        return 1.0 + 0.02 * jax.random.normal(k, shape, f32)

    L = DEPTH
    return {
        'x': jax.random.normal(ks[0], (BATCH, SEQ, D_MODEL), f32),
        'norm_mix_g': gain(ks[1], (L, D_MODEL)),
        'w_in': nrm(ks[2], (L, D_MODEL, D_IN), D_MODEL),
        'mla_q_norm_g': gain(ks[3], (L, MLA_Q_LORA)),
        'mla_w_uq': nrm(ks[4], (L, MLA_Q_LORA, MLA_HEADS * (MLA_NOPE_DIM + MLA_ROPE_DIM)), MLA_Q_LORA),
        'mla_kv_norm_g': gain(ks[5], (L, MLA_KV_LORA)),
        'mla_w_ukv': nrm(ks[6], (L, MLA_KV_LORA, MLA_HEADS * (MLA_NOPE_DIM + MLA_V_DIM)), MLA_KV_LORA),
        'nsa_cmp_pos_k': 0.1 * jax.random.normal(ks[7], (L, CMP_LEN, NSA_HEAD_DIM), f32),
        'nsa_cmp_w1_k': nrm(ks[8], (L, CMP_LEN * NSA_HEAD_DIM, CMP_HIDDEN), CMP_LEN * NSA_HEAD_DIM),
        'nsa_cmp_w2_k': nrm(ks[9], (L, CMP_HIDDEN, NSA_HEAD_DIM), CMP_HIDDEN),
        'nsa_cmp_pos_v': 0.1 * jax.random.normal(ks[10], (L, CMP_LEN, NSA_HEAD_DIM), f32),
        'nsa_cmp_w1_v': nrm(ks[11], (L, CMP_LEN * NSA_HEAD_DIM, CMP_HIDDEN), CMP_LEN * NSA_HEAD_DIM),
        'nsa_cmp_w2_v': nrm(ks[12], (L, CMP_HIDDEN, NSA_HEAD_DIM), CMP_HIDDEN),
        't5_table': 0.2 * jax.random.normal(ks[13], (T5_BUCKETS, NSA_HEADS), f32),
        'out_norm_mla_g': gain(ks[14], (L, MLA_OUT)),
        'out_norm_nsa_g': gain(ks[15], (L, NSA_OUT)),
        'w_out': nrm(ks[16], (L, D_MIX, D_MODEL), D_MIX),
        'norm_ffn_g': gain(ks[17], (L, D_MODEL)),
        'w_gate': nrm(ks[18], (L, D_MODEL, D_FF), D_MODEL),
        'w_up': nrm(ks[19], (L, D_MODEL, D_FF), D_MODEL),
        'w_down': nrm(ks[20], (L, D_FF, D_MODEL), D_FF),
        'final_norm_g': gain(ks[21], (D_MODEL,)),
    }


def reference(x, norm_mix_g, w_in, mla_q_norm_g, mla_w_uq, mla_kv_norm_g, mla_w_ukv,
              nsa_cmp_pos_k, nsa_cmp_w1_k, nsa_cmp_w2_k, nsa_cmp_pos_v, nsa_cmp_w1_v, nsa_cmp_w2_v,
              t5_table, out_norm_mla_g, out_norm_nsa_g, w_out, norm_ffn_g, w_gate, w_up, w_down,
              final_norm_g):
    cuts = [int(c) for c in np.cumsum(IN_SPLITS)[:-1]]
    h = x
    for l in range(DEPTH):
        u = rmsnorm(h, norm_mix_g[l]) @ w_in[l]
        c_q, c_kv, k_rope, q_nsa, kv_cmp, kv_slc, kv_win, gate_logits = jnp.split(u, cuts, axis=-1)
        o_mla = mla_mixer(c_q, c_kv, k_rope, mla_q_norm_g[l], mla_w_uq[l], mla_kv_norm_g[l], mla_w_ukv[l])
        o_nsa = nsa_mixer(q_nsa, kv_cmp, kv_slc, kv_win, gate_logits,
                          nsa_cmp_pos_k[l], nsa_cmp_w1_k[l], nsa_cmp_w2_k[l],
                          nsa_cmp_pos_v[l], nsa_cmp_w1_v[l], nsa_cmp_w2_v[l], t5_table)
        mixed = jnp.concatenate([rmsnorm(o_mla, out_norm_mla_g[l]), rmsnorm(o_nsa, out_norm_nsa_g[l])], axis=-1)
        h = h + mixed @ w_out[l]
        f = rmsnorm(h, norm_ffn_g[l])
        h = h + (jax.nn.silu(f @ w_gate[l]) * (f @ w_up[l])) @ w_down[l]
    return rmsnorm(h, final_norm_g)
```

```python
import functools
import math

import numpy as np
import jax
import jax.numpy as jnp
from jax import lax
from jax.experimental import pallas as pl
from jax.experimental.pallas import tpu as pltpu

D_MODEL = 1024
MLA_HEADS = 8
MLA_NOPE = 64
MLA_ROPE = 32
MLA_V = 64
MLA_Q_LORA = 256
MLA_KV_LORA = 128
ROPE_THETA = 10000.0
NSA_HEADS = 8
NSA_KV_HEADS = 2
NSA_GROUP = 4
NSA_DK = 64
CMP_LEN = 32
CMP_STRIDE = 16
CMP_HIDDEN = 128
SLC_LEN = 64
SLC_TOPN = 16
WINDOW = 512
T5_BUCKETS = 32
T5_MAX_DIST = 128
D_FF = 2816
EPS = 1e-6
NEG = -1e30
FORCE = 1e30

LANE = 128
SLC_LANES = 64

PROJ_TM = 256
MLA_TQ = 512
MLA_TK = 512
NSA_TQ = 256
NSA_TK = 256
POST_TM = 512
VMEM_LIMIT = 56 * 1024 * 1024

C_CQ = 0
C_CKV = 256
C_MISC = 384
C_QNSA = 512
C_KVCMP = 1536
C_KSLC = 1792
C_VSLC = 2048
C_KWIN = 2304
C_VWIN = 2560
D_INP = 2816
GATE_LANE0 = 64


def _dot(a, b):
    return jnp.dot(a, b, preferred_element_type=jnp.float32)


def _dot_nt(a, b):
    return lax.dot_general(a, b, (((1,), (1,)), ((), ())), preferred_element_type=jnp.float32)


def _rms(x, g):
    return x * lax.rsqrt(jnp.mean(x * x, axis=-1, keepdims=True) + EPS) * g


def _const_spec(shape):
    nd = len(shape)
    return pl.BlockSpec(shape, lambda *_: (0,) * nd, pipeline_mode=pl.Buffered(1))


def _params(n_axes):
    return pltpu.CompilerParams(dimension_semantics=("arbitrary",) * n_axes,
                                vmem_limit_bytes=VMEM_LIMIT)


def _w_in_layout():
    src = np.zeros((D_INP,), np.int32)
    mul = np.zeros((D_INP,), np.float32)
    o_krope = MLA_Q_LORA + MLA_KV_LORA
    o_qnsa = o_krope + MLA_ROPE
    o_cmp = o_qnsa + NSA_HEADS * NSA_DK
    o_slc = o_cmp + 2 * NSA_KV_HEADS * NSA_DK
    o_win = o_slc + 2 * NSA_KV_HEADS * NSA_DK
    o_gate = o_win + 2 * NSA_KV_HEADS * NSA_DK

    def put(dst, s, n, m=1.0):
        src[dst:dst + n] = np.arange(s, s + n)
        mul[dst:dst + n] = m

    put(C_CQ, 0, MLA_Q_LORA)
    put(C_CKV, MLA_Q_LORA, MLA_KV_LORA)
    half = MLA_ROPE // 2
    put(C_MISC, o_krope, MLA_ROPE)
    put(C_MISC + MLA_ROPE, o_krope + half, half, -1.0)
    put(C_MISC + MLA_ROPE + half, o_krope, half, 1.0)
    put(C_MISC + GATE_LANE0, o_gate, NSA_HEADS * 3)
    for h in range(NSA_HEADS):
        put(C_QNSA + h * LANE, o_qnsa + h * NSA_DK, NSA_DK, NSA_DK ** -0.5)
    put(C_KVCMP, o_cmp, 2 * NSA_KV_HEADS * NSA_DK)
    for hk in range(NSA_KV_HEADS):
        put(C_KSLC + hk * LANE, o_slc + hk * NSA_DK, NSA_DK)
        put(C_VSLC + hk * LANE, o_slc + (NSA_KV_HEADS + hk) * NSA_DK, NSA_DK)
        put(C_KWIN + hk * LANE, o_win + hk * NSA_DK, NSA_DK)
        put(C_VWIN + hk * LANE, o_win + (NSA_KV_HEADS + hk) * NSA_DK, NSA_DK)
    return src, mul


def _w_uq_layout():
    n = 2 * MLA_HEADS * LANE
    src = np.zeros((n,), np.int32)
    mul = np.zeros((n,), np.float32)
    dq = MLA_NOPE + MLA_ROPE
    half = MLA_ROPE // 2
    for h in range(MLA_HEADS):
        b = h * LANE
        src[b:b + dq] = np.arange(h * dq, h * dq + dq)
        mul[b:b + dq] = 1.0
        r = MLA_HEADS * LANE + h * LANE + MLA_NOPE
        pe = h * dq + MLA_NOPE
        src[r:r + half] = np.arange(pe + half, pe + 2 * half)
        mul[r:r + half] = -1.0
        src[r + half:r + 2 * half] = np.arange(pe, pe + half)
        mul[r + half:r + 2 * half] = 1.0
    return src, mul


def _w_ukv_layout():
    n = MLA_HEADS * LANE + MLA_HEADS * MLA_V
    src = np.zeros((n,), np.int32)
    mul = np.zeros((n,), np.float32)
    per = MLA_NOPE + MLA_V
    for h in range(MLA_HEADS):
        src[h * LANE:h * LANE + MLA_NOPE] = np.arange(h * per, h * per + MLA_NOPE)
        mul[h * LANE:h * LANE + MLA_NOPE] = 1.0
        d = MLA_HEADS * LANE + h * MLA_V
        src[d:d + MLA_V] = np.arange(h * per + MLA_NOPE, h * per + per)
        mul[d:d + MLA_V] = 1.0
    return src, mul


def _relayout(w, layout):
    src, mul = layout
    return (w[:, src] * mul[None, :]).astype(jnp.bfloat16)


def _kpe_placement():
    e = np.zeros((LANE, MLA_HEADS * LANE), np.float32)
    for h in range(MLA_HEADS):
        for r in range(MLA_ROPE):
            e[r, h * LANE + MLA_NOPE + r] = 1.0
    return jnp.asarray(e, jnp.bfloat16)


def _t5_thresholds(max_dist):
    n = np.arange(max_dist, dtype=np.int64)
    max_exact = T5_BUCKETS // 2
    nf = np.maximum(n, 1).astype(np.float32)
    val = (np.log(nf / np.float32(max_exact)) / np.float32(math.log(T5_MAX_DIST / max_exact))
           * np.float32(T5_BUCKETS - max_exact))
    large = np.minimum(max_exact + val.astype(np.int32), T5_BUCKETS - 1)
    bucket = np.where(n < max_exact, n, large)
    assert np.all(np.diff(bucket) >= 0)
    frac = np.abs(val[max_exact + 1:T5_MAX_DIST] - np.round(val[max_exact + 1:T5_MAX_DIST]))
    assert frac.min() > 1e-3
    thr = [int(np.argmax(bucket >= b)) for b in range(T5_BUCKETS)]
    return thr


def _proj_kernel(x_ref, gmix_ref, win_ref, gq_ref, wq_ref, gkv_ref, wkv_ref, epe_ref,
                 cq_ref, sq_ref, ck_ref, sk_ref,
                 qm_ref, km_ref, vm_ref, qn_ref, kvc_ref, ks_ref, vs_ref, kw_ref, vw_ref, misc_ref):
    tm = x_ref.shape[1]
    x = x_ref[0]
    xn = _rms(x, gmix_ref[...]).astype(jnp.bfloat16)
    u = _dot(xn, win_ref[...])

    cqn = _rms(u[:, C_CQ:C_CQ + MLA_Q_LORA], gq_ref[...]).astype(jnp.bfloat16)
    qq = _dot(cqn, wq_ref[...])
    cosq = cq_ref[...]
    sinq = sq_ref[...]
    hw = MLA_HEADS * LANE
    for h in range(MLA_HEADS):
        qh = qq[:, h * LANE:(h + 1) * LANE] * cosq + qq[:, hw + h * LANE:hw + (h + 1) * LANE] * sinq
        qm_ref[0, h] = qh.astype(jnp.bfloat16)

    ckvn = _rms(u[:, C_CKV:C_CKV + MLA_KV_LORA], gkv_ref[...]).astype(jnp.bfloat16)
    kv = _dot(ckvn, wkv_ref[...])
    misc = u[:, C_MISC:C_MISC + LANE]
    kpe = misc * ck_ref[...] + pltpu.roll(misc, LANE - MLA_ROPE, axis=1) * sk_ref[...]
    kpe_all = _dot(kpe.astype(jnp.bfloat16), epe_ref[...])
    for h in range(MLA_HEADS):
        km_ref[0, h] = (kv[:, h * LANE:(h + 1) * LANE] + kpe_all[:, h * LANE:(h + 1) * LANE]).astype(jnp.bfloat16)
    for p in range(MLA_HEADS // 2):
        vm_ref[0, p] = kv[:, hw + p * LANE:hw + (p + 1) * LANE].astype(jnp.bfloat16)

    for h in range(NSA_HEADS):
        qn_ref[0, h] = u[:, C_QNSA + h * LANE:C_QNSA + (h + 1) * LANE].astype(jnp.bfloat16)
    kvc_ref[0] = u[:, C_KVCMP:C_KVCMP + 2 * NSA_KV_HEADS * NSA_DK]
    pos = pl.program_id(1) * tm + lax.broadcasted_iota(jnp.int32, (tm, LANE), 0)
    lane = lax.broadcasted_iota(jnp.int32, (tm, LANE), 1)
    onehot = jnp.where(lane - (LANE - SLC_LANES) == pos // SLC_LEN, 1.0, 0.0)
    for hk in range(NSA_KV_HEADS):
        ks_ref[0, hk] = (u[:, C_KSLC + hk * LANE:C_KSLC + (hk + 1) * LANE] + onehot).astype(jnp.bfloat16)
        vs_ref[0, hk] = u[:, C_VSLC + hk * LANE:C_VSLC + (hk + 1) * LANE].astype(jnp.bfloat16)
        kw_ref[0, hk] = u[:, C_KWIN + hk * LANE:C_KWIN + (hk + 1) * LANE].astype(jnp.bfloat16)
        vw_ref[0, hk] = u[:, C_VWIN + hk * LANE:C_VWIN + (hk + 1) * LANE].astype(jnp.bfloat16)
    misc_ref[0] = misc


def _proj(x, gmix, w_in_p, gq, w_q2, gkv, w_kv2, epe, cosq, sinq, cosk, sink):
    B, S, D = x.shape
    tm = min(PROJ_TM, S)
    bf = jnp.bfloat16
    grid = (B, S // tm)
    tok = lambda w: pl.BlockSpec((tm, w), lambda b, i: (i, 0))
    head = lambda n: pl.BlockSpec((1, n, tm, LANE), lambda b, i: (b, 0, i, 0))
    out_shape = (
        jax.ShapeDtypeStruct((B, MLA_HEADS, S, LANE), bf),
        jax.ShapeDtypeStruct((B, MLA_HEADS, S, LANE), bf),
        jax.ShapeDtypeStruct((B, MLA_HEADS // 2, S, LANE), bf),
        jax.ShapeDtypeStruct((B, NSA_HEADS, S, LANE), bf),
        jax.ShapeDtypeStruct((B, S, 2 * NSA_KV_HEADS * NSA_DK), jnp.float32),
        jax.ShapeDtypeStruct((B, NSA_KV_HEADS, S, LANE), bf),
        jax.ShapeDtypeStruct((B, NSA_KV_HEADS, S, LANE), bf),
        jax.ShapeDtypeStruct((B, NSA_KV_HEADS, S, LANE), bf),
        jax.ShapeDtypeStruct((B, NSA_KV_HEADS, S, LANE), bf),
        jax.ShapeDtypeStruct((B, S, LANE), jnp.float32),
    )
    out_specs = (
        head(MLA_HEADS), head(MLA_HEADS), head(MLA_HEADS // 2), head(NSA_HEADS),
        pl.BlockSpec((1, tm, 2 * NSA_KV_HEADS * NSA_DK), lambda b, i: (b, i, 0)),
        head(NSA_KV_HEADS), head(NSA_KV_HEADS), head(NSA_KV_HEADS), head(NSA_KV_HEADS),
        pl.BlockSpec((1, tm, LANE), lambda b, i: (b, i, 0)),
    )
    in_specs = [
        pl.BlockSpec((1, tm, D), lambda b, i: (b, i, 0)),
        _const_spec(gmix.shape), _const_spec(w_in_p.shape), _const_spec(gq.shape),
        _const_spec(w_q2.shape), _const_spec(gkv.shape), _const_spec(w_kv2.shape),
        _const_spec(epe.shape),
        tok(LANE), tok(LANE), tok(LANE), tok(LANE),
    ]
    return pl.pallas_call(
        _proj_kernel, grid=grid, in_specs=in_specs, out_specs=out_specs, out_shape=out_shape,
        compiler_params=_params(2), name="proj",
    )(x, gmix, w_in_p, gq, w_q2, gkv, w_kv2, epe, cosq, sinq, cosk, sink)


def _compress_kernel(c_ref, pos_ref, w1_ref, w2_ref, kc_ref, vc_ref):
    nc = c_ref.shape[2]
    for kv in range(2):
        for hk in range(NSA_KV_HEADS):
            c = c_ref[0, kv * NSA_KV_HEADS + hk]
            a = _dot((c + pos_ref[kv, 0]).astype(jnp.bfloat16), w1_ref[kv, 0])
            b = _dot((c + pos_ref[kv, 1]).astype(jnp.bfloat16), w1_ref[kv, 1])
            hid = a + pltpu.roll(b, nc - 1, axis=0)
            out = _dot(jax.nn.gelu(hid).astype(jnp.bfloat16), w2_ref[kv])
            if kv == 0:
                kc_ref[0, hk] = out.astype(jnp.bfloat16)
            else:
                vc_ref[0, hk] = out.astype(jnp.bfloat16)


def _compress(kvc, pos, w1, w2):
    B, _, nc, width = kvc.shape
    out = jax.ShapeDtypeStruct((B, NSA_KV_HEADS, nc, LANE), jnp.bfloat16)
    spec = pl.BlockSpec((1, NSA_KV_HEADS, nc, LANE), lambda b: (b, 0, 0, 0))
    return pl.pallas_call(
        _compress_kernel, grid=(B,),
        in_specs=[pl.BlockSpec((1, 2 * NSA_KV_HEADS, nc, width), lambda b: (b, 0, 0, 0)),
                  _const_spec(pos.shape), _const_spec(w1.shape), _const_spec(w2.shape)],
        out_specs=(spec, spec), out_shape=(out, out),
        compiler_params=_params(1), name="compress",
    )(kvc, pos, w1, w2)


def _t5_lookup(dist, tab_ref, head, thr):
    val = jnp.full(dist.shape, tab_ref[0, head], jnp.float32)
    for b in range(1, T5_BUCKETS):
        val = jnp.where(dist >= thr[b], tab_ref[b, head], val)
    return val - tab_ref[T5_BUCKETS - 1, head]


def _bias_cmp_kernel(tab_ref, out_ref, *, thr, tq, n_cmp):
    hk = pl.program_id(0)
    t = pl.program_id(1)
    nc = out_ref.shape[3]
    i = lax.broadcasted_iota(jnp.int32, (tq, nc), 0)
    n = lax.broadcasted_iota(jnp.int32, (tq, nc), 1)
    dist = t * tq + i - (n * CMP_STRIDE + CMP_LEN - 1)
    ok = jnp.logical_and(dist >= 0, n < n_cmp)
    for g in range(NSA_GROUP):
        val = _t5_lookup(dist, tab_ref, hk * NSA_GROUP + g, thr)
        out_ref[0, 0, g * tq:(g + 1) * tq, :] = jnp.where(ok, val, NEG)


def _bias_tile_kernel(tab_ref, out_ref, *, thr, tq):
    hk = pl.program_id(0)
    tk = out_ref.shape[3]
    i = lax.broadcasted_iota(jnp.int32, (tq, tk), 0)
    j = lax.broadcasted_iota(jnp.int32, (tq, tk), 1)
    for g in range(NSA_GROUP):
        head = hk * NSA_GROUP + g
        rows = slice(g * tq, (g + 1) * tq)
        d0 = i - j
        out_ref[0, 0, rows, :] = jnp.where(d0 >= 0, _t5_lookup(d0, tab_ref, head, thr), NEG)
        out_ref[0, 1, rows, :] = _t5_lookup(d0 + tk, tab_ref, head, thr)
        out_ref[0, 2, rows, :] = jnp.where(j > i, 0.0, NEG)


def _bias_tables(t5_table, S):
    thr = _t5_thresholds(S)
    tq, tk = NSA_TQ, NSA_TK
    assert tq == tk and WINDOW == 2 * tk and tk >= T5_MAX_DIST
    nq = S // tq
    nc = S // CMP_STRIDE
    n_cmp = (S - CMP_LEN) // CMP_STRIDE + 1
    m = NSA_GROUP * tq
    smem = pl.BlockSpec(memory_space=pltpu.SMEM)
    bias_c = pl.pallas_call(
        functools.partial(_bias_cmp_kernel, thr=thr, tq=tq, n_cmp=n_cmp),
        grid=(NSA_KV_HEADS, nq), in_specs=[smem],
        out_specs=pl.BlockSpec((1, 1, m, nc), lambda h, t: (h, t, 0, 0)),
        out_shape=jax.ShapeDtypeStruct((NSA_KV_HEADS, nq, m, nc), jnp.float32),
        compiler_params=_params(2), name="bias_cmp",
    )(t5_table)
    tz = pl.pallas_call(
        functools.partial(_bias_tile_kernel, thr=thr, tq=tq),
        grid=(NSA_KV_HEADS,), in_specs=[smem],
        out_specs=pl.BlockSpec((1, 3, m, tk), lambda h: (h, 0, 0, 0)),
        out_shape=jax.ShapeDtypeStruct((NSA_KV_HEADS, 3, m, tk), jnp.float32),
        compiler_params=_params(1), name="bias_tile",
    )(t5_table)
    return bias_c, tz


def _flash_step(q, k, v, bias, m_ref, l_ref, acc_ref):
    s = _dot_nt(q, k)
    if bias is not None:
        s = s + bias
    m_prev = m_ref[...]
    m_new = jnp.maximum(m_prev, jnp.max(s, axis=-1, keepdims=True))
    alpha = jnp.exp(m_prev - m_new)
    p = jnp.exp(s - m_new)
    l_ref[...] = alpha * l_ref[...] + jnp.sum(p, axis=-1, keepdims=True)
    acc_ref[...] = alpha * acc_ref[...] + _dot(p.astype(jnp.bfloat16), v)
    m_ref[...] = m_new


def _flash_init(m_ref, l_ref, acc_ref):
    m_ref[...] = jnp.full(m_ref.shape, -jnp.inf, jnp.float32)
    l_ref[...] = jnp.zeros(l_ref.shape, jnp.float32)
    acc_ref[...] = jnp.zeros(acc_ref.shape, jnp.float32)


def _mla_kernel(q_ref, k_ref, v_ref, o_ref, m_ref, l_ref, acc_ref, *, tq, tk):
    S = q_ref.shape[2]
    nqt = S // tq
    row = lax.broadcasted_iota(jnp.int32, (tq, tk), 0)
    col = lax.broadcasted_iota(jnp.int32, (tq, tk), 1)
    causal = jnp.where(col <= row, 0.0, NEG)
    lane = lax.broadcasted_iota(jnp.int32, (tq, LANE), 1)

    def q_tile(qt, carry):
        q0 = pl.multiple_of(qt * tq, tq)
        outs = []
        for e in range(2):
            q = q_ref[0, e, pl.ds(q0, tq), :]
            _flash_init(m_ref, l_ref, acc_ref)

            def far(c, carry2):
                k0 = pl.multiple_of(c * tk, tk)
                _flash_step(q, k_ref[0, e, pl.ds(k0, tk), :], v_ref[0, 0, pl.ds(k0, tk), :],
                            None, m_ref, l_ref, acc_ref)
                return carry2

            lax.fori_loop(0, qt, far, 0)
            _flash_step(q, k_ref[0, e, pl.ds(q0, tk), :], v_ref[0, 0, pl.ds(q0, tk), :],
                        causal, m_ref, l_ref, acc_ref)
            outs.append(acc_ref[...] / l_ref[...])
        o_ref[0, pl.ds(q0, tq), :] = jnp.where(lane < MLA_V, outs[0], outs[1])
        return carry

    lax.fori_loop(0, nqt, q_tile, 0)


def _mla(q, k, v):
    B, H, S, _ = q.shape
    tq = min(MLA_TQ, S)
    tk = tq
    return pl.pallas_call(
        functools.partial(_mla_kernel, tq=tq, tk=tk),
        grid=(B, H // 2),
        in_specs=[pl.BlockSpec((1, 2, S, LANE), lambda b, p: (b, p, 0, 0)),
                  pl.BlockSpec((1, 2, S, LANE), lambda b, p: (b, p, 0, 0)),
                  pl.BlockSpec((1, 1, S, LANE), lambda b, p: (b, p, 0, 0))],
        out_specs=pl.BlockSpec((1, S, LANE), lambda b, p: (b, 0, p)),
        out_shape=jax.ShapeDtypeStruct((B, S, H * MLA_V), jnp.float32),
        scratch_shapes=[pltpu.VMEM((tq, 1), jnp.float32), pltpu.VMEM((tq, 1), jnp.float32),
                        pltpu.VMEM((tq, LANE), jnp.float32)],
        compiler_params=_params(2), name="mla",
    )(q, k, v)


def _nsa_kernel(q_ref, ks_ref, vs_ref, kw_ref, vw_ref, kc_ref, vc_ref, misc_ref, bc_ref, tz_ref,
                ovt_ref, eg_ref, o_ref,
                qall_ref, qaug_ref, m_ref, l_ref, acc_ref, os_ref, *, tq, tk):
    t = pl.program_id(2)
    G = NSA_GROUP
    m_rows = G * tq
    q0 = t * tq
    for g in range(G):
        qall_ref[g * tq:(g + 1) * tq, :] = q_ref[0, g]
    q_all = qall_ref[...]

    s = _dot_nt(q_all, kc_ref[0, 0]) + bc_ref[0, 0]
    mx = jnp.max(s, axis=-1, keepdims=True)
    p = jnp.exp(s - mx)
    lsum = jnp.sum(p, axis=-1, keepdims=True)
    ri = lax.broadcasted_iota(jnp.int32, (m_rows, 1), 0)
    row_pos = q0 + ri % tq
    r = jnp.where(row_pos >= CMP_LEN - 1, 1.0 / lsum, 0.0)
    pcb = (p * r).astype(jnp.bfloat16)
    o_c = _dot(pcb, vc_ref[0, 0])

    imp = _dot_nt(ovt_ref[...], pcb[0:tq])
    for g in range(1, G):
        imp = imp + _dot_nt(ovt_ref[...], pcb[g * tq:(g + 1) * tq])
    nb = imp.shape[0]
    jb = lax.broadcasted_iota(jnp.int32, (nb, tq), 0)
    cur = (q0 + lax.broadcasted_iota(jnp.int32, (nb, tq), 1)) // SLC_LEN
    forced = jnp.logical_or(jb == 0, jnp.logical_or(jb == cur, jb == cur - 1))
    imp = jnp.where(forced, FORCE, jnp.where(jb <= cur, imp, NEG))
    sub = 8
    slabs = [imp[r * sub:(r + 1) * sub] for r in range(nb // sub)]
    ranks = [jnp.zeros((sub, tq), jnp.int32) for _ in slabs]
    jsub = lax.broadcasted_iota(jnp.int32, (sub, tq), 0)
    for jp in range(nb):
        rowv = imp[jp:jp + 1, :]
        for r, slab in enumerate(slabs):
            lo = r * sub
            if lo > jp:
                one = jnp.where(rowv >= slab, 1, 0)
            elif lo + sub - 1 <= jp:
                one = jnp.where(rowv > slab, 1, 0)
            else:
                one = jnp.where(jsub + lo > jp, jnp.where(rowv >= slab, 1, 0), jnp.where(rowv > slab, 1, 0))
            ranks[r] = ranks[r] + one
    rank = jnp.concatenate(ranks, axis=0)
    selb = jnp.where(rank < SLC_TOPN, 0.0, NEG)
    selb_t = jnp.transpose(jnp.concatenate([jnp.zeros((LANE - nb, tq), jnp.float32), selb], axis=0))
    selb_t = selb_t.astype(jnp.bfloat16)
    lane = lax.broadcasted_iota(jnp.int32, (tq, LANE), 1)
    for g in range(G):
        qaug_ref[g * tq:(g + 1) * tq, :] = jnp.where(lane < LANE - SLC_LANES, q_ref[0, g], selb_t)
    q_aug = qaug_ref[...]

    def kv_tile(k_ref, v_ref, c):
        k0 = pl.multiple_of(c * tk, tk)
        return k_ref[0, 0, pl.ds(k0, tk), :], v_ref[0, 0, pl.ds(k0, tk), :]

    _flash_init(m_ref, l_ref, acc_ref)

    def far(c, carry):
        kt, vt = kv_tile(ks_ref, vs_ref, c)
        _flash_step(q_aug, kt, vt, None, m_ref, l_ref, acc_ref)
        return carry

    lax.fori_loop(0, jnp.maximum(t - 1, 0), far, 0)

    @pl.when(t >= 1)
    def _():
        kt, vt = kv_tile(ks_ref, vs_ref, t - 1)
        _flash_step(q_aug, kt, vt, tz_ref[0, 1], m_ref, l_ref, acc_ref)

    kt, vt = kv_tile(ks_ref, vs_ref, t)
    _flash_step(q_aug, kt, vt, tz_ref[0, 0], m_ref, l_ref, acc_ref)
    os_ref[...] = acc_ref[...] / l_ref[...]

    _flash_init(m_ref, l_ref, acc_ref)

    @pl.when(t >= 2)
    def _():
        kt, vt = kv_tile(kw_ref, vw_ref, t - 2)
        _flash_step(q_all, kt, vt, tz_ref[0, 2], m_ref, l_ref, acc_ref)

    @pl.when(t >= 1)
    def _():
        kt, vt = kv_tile(kw_ref, vw_ref, t - 1)
        _flash_step(q_all, kt, vt, tz_ref[0, 1], m_ref, l_ref, acc_ref)

    kt, vt = kv_tile(kw_ref, vw_ref, t)
    _flash_step(q_all, kt, vt, tz_ref[0, 0], m_ref, l_ref, acc_ref)
    o_w = acc_ref[...] / l_ref[...]
    o_s = os_ref[...]

    gate = jax.nn.sigmoid(misc_ref[0])
    g_hi = gate.astype(jnp.bfloat16)
    rem = gate - g_hi.astype(jnp.float32)
    g_mid = rem.astype(jnp.bfloat16)
    g_lo = (rem - g_mid.astype(jnp.float32)).astype(jnp.bfloat16)

    def expand(br):
        e = eg_ref[0, br]
        return _dot(g_hi, e) + _dot(g_mid, e) + _dot(g_lo, e)

    gc, gs, gw = expand(0), expand(1), expand(2)
    comb = []
    for g in range(G):
        rows = slice(g * tq, (g + 1) * tq)
        cols = slice(g * LANE, (g + 1) * LANE)
        comb.append(gc[:, cols] * o_c[rows] + gs[:, cols] * o_s[rows] + gw[:, cols] * o_w[rows])
    half = LANE // 2
    o_ref[0, :, 0:LANE] = jnp.where(lane < half, comb[0], pltpu.roll(comb[1], half, axis=1))
    o_ref[0, :, LANE:2 * LANE] = jnp.where(lane < half, comb[2], pltpu.roll(comb[3], half, axis=1))


def _nsa(q, ks, vs, kw, vw, kc, vc, misc, bias_c, tz, ovt, eg):
    B, _, S, _ = q.shape
    tq, tk = NSA_TQ, NSA_TK
    nq = S // tq
    nc = kc.shape[2]
    m = NSA_GROUP * tq
    full = lambda: pl.BlockSpec((1, 1, S, LANE), lambda b, h, t: (b, h, 0, 0))
    cmp_spec = lambda: pl.BlockSpec((1, 1, nc, LANE), lambda b, h, t: (b, h, 0, 0))
    return pl.pallas_call(
        functools.partial(_nsa_kernel, tq=tq, tk=tk),
        grid=(B, NSA_KV_HEADS, nq),
        in_specs=[pl.BlockSpec((1, NSA_GROUP, tq, LANE), lambda b, h, t: (b, h, t, 0)),
                  full(), full(), full(), full(), cmp_spec(), cmp_spec(),
                  pl.BlockSpec((1, tq, LANE), lambda b, h, t: (b, t, 0)),
                  pl.BlockSpec((1, 1, m, nc), lambda b, h, t: (h, t, 0, 0)),
                  pl.BlockSpec((1, 3, m, tk), lambda b, h, t: (h, 0, 0, 0)),
                  _const_spec(ovt.shape),
                  pl.BlockSpec((1, 3, LANE, NSA_GROUP * LANE), lambda b, h, t: (h, 0, 0, 0))],
        out_specs=pl.BlockSpec((1, tq, NSA_GROUP * NSA_DK), lambda b, h, t: (b, t, h)),
        out_shape=jax.ShapeDtypeStruct((B, S, NSA_HEADS * NSA_DK), jnp.float32),
        scratch_shapes=[pltpu.VMEM((m, LANE), jnp.bfloat16), pltpu.VMEM((m, LANE), jnp.bfloat16),
                        pltpu.VMEM((m, 1), jnp.float32), pltpu.VMEM((m, 1), jnp.float32),
                        pltpu.VMEM((m, LANE), jnp.float32), pltpu.VMEM((m, LANE), jnp.float32)],
        compiler_params=_params(3), name="nsa",
    )(q, ks, vs, kw, vw, kc, vc, misc, bias_c, tz, ovt, eg)


def _post_kernel(x_ref, om_ref, on_ref, gm_ref, gn_ref, wo_ref, gf_ref, wg_ref, wu_ref, wd_ref,
                 gfin_ref, o_ref):
    half = om_ref.shape[2]
    mix_m = _rms(om_ref[0], gm_ref[...]).astype(jnp.bfloat16)
    mix_n = _rms(on_ref[0], gn_ref[...]).astype(jnp.bfloat16)
    h = x_ref[0] + _dot(mix_m, wo_ref[0:half, :]) + _dot(mix_n, wo_ref[half:2 * half, :])
    f = _rms(h, gf_ref[...]).astype(jnp.bfloat16)
    a = _dot(f, wg_ref[...])
    act = (a * jax.nn.sigmoid(a) * _dot(f, wu_ref[...])).astype(jnp.bfloat16)
    h = h + _dot(act, wd_ref[...])
    o_ref[0] = _rms(h, gfin_ref[...])


def _post(x, o_mla, o_nsa, gm, gn, w_out, gf, wg, wu, wd, gfin):
    B, S, D = x.shape
    tm = min(POST_TM, S)
    tok = lambda w: pl.BlockSpec((1, tm, w), lambda b, i: (b, i, 0))
    consts = (gm, gn, w_out, gf, wg, wu, wd, gfin)
    return pl.pallas_call(
        _post_kernel, grid=(B, S // tm),
        in_specs=[tok(D), tok(o_mla.shape[2]), tok(o_nsa.shape[2])] + [_const_spec(c.shape) for c in consts],
        out_specs=tok(D), out_shape=jax.ShapeDtypeStruct((B, S, D), jnp.float32),
        compiler_params=_params(2), name="post",
    )(x, o_mla, o_nsa, *consts)


def _rope_tables(S):
    pos = jnp.arange(S, dtype=jnp.float32)
    inv = ROPE_THETA ** (-jnp.arange(0, MLA_ROPE, 2, dtype=jnp.float32) / MLA_ROPE)
    ang = pos[:, None] * inv[None, :]
    cos, sin = jnp.cos(ang), jnp.sin(ang)
    cos2 = jnp.concatenate([cos, cos], axis=-1)
    sin2 = jnp.concatenate([sin, sin], axis=-1)
    scale = (MLA_NOPE + MLA_ROPE) ** -0.5
    zq = jnp.zeros((S, LANE - MLA_NOPE - MLA_ROPE), jnp.float32)
    cosq = jnp.concatenate([jnp.ones((S, MLA_NOPE), jnp.float32), cos2, zq], axis=-1) * scale
    sinq = jnp.concatenate([jnp.zeros((S, MLA_NOPE), jnp.float32), sin2, zq], axis=-1) * scale
    zk = jnp.zeros((S, LANE - MLA_ROPE), jnp.float32)
    cosk = jnp.concatenate([cos2, zk], axis=-1)
    sink = jnp.concatenate([sin2, zk], axis=-1)
    return cosq, sinq, cosk, sink


def _overlap_t(S):
    n_cmp = (S - CMP_LEN) // CMP_STRIDE + 1
    n_slc = S // SLC_LEN
    assert n_slc <= SLC_LANES
    cs = np.arange(n_cmp) * CMP_STRIDE
    ss = np.arange(n_slc) * SLC_LEN
    ov = np.maximum(0, np.minimum(cs[:, None] + CMP_LEN, ss[None, :] + SLC_LEN)
                    - np.maximum(cs[:, None], ss[None, :])).astype(np.float32) / CMP_STRIDE
    out = np.zeros((SLC_LANES, S // CMP_STRIDE), np.float32)
    out[:n_slc, :n_cmp] = ov.T
    return jnp.asarray(out, jnp.bfloat16)


def _gate_expand():
    e = np.zeros((NSA_KV_HEADS, 3, LANE, NSA_GROUP * LANE), np.float32)
    for hk in range(NSA_KV_HEADS):
        for br in range(3):
            for g in range(NSA_GROUP):
                src = GATE_LANE0 + (hk * NSA_GROUP + g) * 3 + br
                e[hk, br, src, g * LANE:(g + 1) * LANE] = 1.0
    return jnp.asarray(e, jnp.bfloat16)


def kernel(x, norm_mix_g, w_in, mla_q_norm_g, mla_w_uq, mla_kv_norm_g, mla_w_ukv, nsa_cmp_pos_k, nsa_cmp_w1_k, nsa_cmp_w2_k, nsa_cmp_pos_v, nsa_cmp_w1_v, nsa_cmp_w2_v, t5_table, out_norm_mla_g, out_norm_nsa_g, w_out, norm_ffn_g, w_gate, w_up, w_down, final_norm_g):
    B, S, D = x.shape
    depth = w_in.shape[0]
    bf = jnp.bfloat16
    assert D == D_MODEL and S % NSA_TQ == 0 and S % CMP_STRIDE == 0
    cosq, sinq, cosk, sink = _rope_tables(S)
    bias_c, tz = _bias_tables(t5_table, S)
    ovt = _overlap_t(S)
    eg = _gate_expand()
    epe = _kpe_placement()
    nc = S // CMP_STRIDE
    cw = CMP_STRIDE * NSA_DK
    row = lambda v: v.reshape(1, -1)

    h = x
    for l in range(depth):
        w_in_p = _relayout(w_in[l], _w_in_layout())
        w_q2 = _relayout(mla_w_uq[l], _w_uq_layout())
        w_kv2 = _relayout(mla_w_ukv[l], _w_ukv_layout())
        (q_mla, k_mla, v_mla, q_nsa, kv_cmp, k_slc, v_slc, k_win, v_win, misc) = _proj(
            h, row(norm_mix_g[l]), w_in_p, row(mla_q_norm_g[l]), w_q2, row(mla_kv_norm_g[l]), w_kv2, epe,
            cosq, sinq, cosk, sink)

        kvc = kv_cmp.reshape(B, S, 2 * NSA_KV_HEADS, NSA_DK).transpose(0, 2, 1, 3).reshape(
            B, 2 * NSA_KV_HEADS, nc, cw)
        pos = jnp.stack([nsa_cmp_pos_k[l].reshape(2, 1, cw), nsa_cmp_pos_v[l].reshape(2, 1, cw)])
        w1 = jnp.stack([nsa_cmp_w1_k[l].reshape(2, cw, CMP_HIDDEN),
                        nsa_cmp_w1_v[l].reshape(2, cw, CMP_HIDDEN)]).astype(bf)
        zpad = jnp.zeros((CMP_HIDDEN, LANE - NSA_DK), jnp.float32)
        w2 = jnp.stack([jnp.concatenate([nsa_cmp_w2_k[l], zpad], axis=1),
                        jnp.concatenate([nsa_cmp_w2_v[l], zpad], axis=1)]).astype(bf)
        k_cmp, v_cmp = _compress(kvc, pos, w1, w2)

        o_mla = _mla(q_mla, k_mla, v_mla)
        o_nsa = _nsa(q_nsa, k_slc, v_slc, k_win, v_win, k_cmp, v_cmp, misc, bias_c, tz, ovt, eg)

        fin = final_norm_g if l == depth - 1 else jnp.ones_like(final_norm_g)
        assert depth == 1
        h = _post(h, o_mla, o_nsa, row(out_norm_mla_g[l]), row(out_norm_nsa_g[l]), w_out[l].astype(bf),
                  row(norm_ffn_g[l]), w_gate[l].astype(bf), w_up[l].astype(bf), w_down[l].astype(bf),
                  row(fin))
    return h
```

```python
import functools
import math

import numpy as np
import jax
import jax.numpy as jnp
from jax import lax
from jax.experimental import pallas as pl
from jax.experimental.pallas import tpu as pltpu

D_MODEL = 1024
MLA_HEADS = 8
MLA_NOPE = 64
MLA_ROPE = 32
MLA_V = 64
MLA_Q_LORA = 256
MLA_KV_LORA = 128
ROPE_THETA = 10000.0
NSA_HEADS = 8
NSA_KV_HEADS = 2
NSA_GROUP = 4
NSA_DK = 64
CMP_LEN = 32
CMP_STRIDE = 16
CMP_HIDDEN = 128
SLC_LEN = 64
SLC_TOPN = 16
WINDOW = 512
T5_BUCKETS = 32
T5_MAX_DIST = 128
D_FF = 2816
EPS = 1e-6
NEG = -1e30
FORCE = 1e30

LANE = 128
SUBLANE = 8
SLC_ROWS = 64
GATE_ROWS = 16

PROJ_TM = 256
MLA_TQ = 512
MLA_TK = 512
NSA_TQ = 256
NSA_TK = 256
POST_TM = 512
VMEM_LIMIT = 56 * 1024 * 1024

C_CQ = 0
C_CKV = 256
C_MISC = 384
C_KVCMP = 512
C_KSLC = 768
C_KWIN = 1024
D_ROW = 1280
R_QNSA = 0
R_VSLC = 512
R_VWIN = 640
R_GATE = 768
D_COL = 800


def _dot(a, b):
    return jnp.dot(a, b, preferred_element_type=jnp.float32)


def _dot_nt(a, b):
    return lax.dot_general(a, b, (((1,), (1,)), ((), ())), preferred_element_type=jnp.float32)


def _rms(x, g):
    return x * lax.rsqrt(jnp.mean(x * x, axis=-1, keepdims=True) + EPS) * g


def _const_spec(shape):
    nd = len(shape)
    return pl.BlockSpec(shape, lambda *_: (0,) * nd, pipeline_mode=pl.Buffered(1))


def _params(n_axes):
    return pltpu.CompilerParams(dimension_semantics=("arbitrary",) * n_axes,
                                vmem_limit_bytes=VMEM_LIMIT)


def _in_offsets():
    o_krope = MLA_Q_LORA + MLA_KV_LORA
    o_qnsa = o_krope + MLA_ROPE
    o_cmp = o_qnsa + NSA_HEADS * NSA_DK
    o_slc = o_cmp + 2 * NSA_KV_HEADS * NSA_DK
    o_win = o_slc + 2 * NSA_KV_HEADS * NSA_DK
    o_gate = o_win + 2 * NSA_KV_HEADS * NSA_DK
    return o_krope, o_qnsa, o_cmp, o_slc, o_win, o_gate


class _Layout:
    def __init__(self, n):
        self.src = np.zeros((n,), np.int32)
        self.mul = np.zeros((n,), np.float32)

    def put(self, dst, s, n, m=1.0):
        self.src[dst:dst + n] = np.arange(s, s + n)
        self.mul[dst:dst + n] = m

    def apply(self, w):
        return (w[:, self.src] * self.mul[None, :]).astype(jnp.bfloat16)


def _w_in_row_layout():
    o_krope, _, o_cmp, o_slc, o_win, _ = _in_offsets()
    lay = _Layout(D_ROW)
    lay.put(C_CQ, 0, MLA_Q_LORA)
    lay.put(C_CKV, MLA_Q_LORA, MLA_KV_LORA)
    half = MLA_ROPE // 2
    lay.put(C_MISC, o_krope, MLA_ROPE)
    lay.put(C_MISC + MLA_ROPE, o_krope + half, half, -1.0)
    lay.put(C_MISC + MLA_ROPE + half, o_krope, half, 1.0)
    lay.put(C_KVCMP, o_cmp, 2 * NSA_KV_HEADS * NSA_DK)
    for hk in range(NSA_KV_HEADS):
        lay.put(C_KSLC + hk * LANE, o_slc + hk * NSA_DK, NSA_DK)
        lay.put(C_KWIN + hk * LANE, o_win + hk * NSA_DK, NSA_DK)
    return lay


def _w_in_col_layout():
    _, o_qnsa, _, o_slc, o_win, o_gate = _in_offsets()
    lay = _Layout(D_COL)
    lay.put(R_QNSA, o_qnsa, NSA_HEADS * NSA_DK, NSA_DK ** -0.5)
    lay.put(R_VSLC, o_slc + NSA_KV_HEADS * NSA_DK, NSA_KV_HEADS * NSA_DK)
    lay.put(R_VWIN, o_win + NSA_KV_HEADS * NSA_DK, NSA_KV_HEADS * NSA_DK)
    for hk in range(NSA_KV_HEADS):
        lay.put(R_GATE + hk * GATE_ROWS, o_gate + hk * NSA_GROUP * 3, NSA_GROUP * 3)
    return lay


def _w_uq_layout():
    lay = _Layout(2 * MLA_HEADS * LANE)
    dq = MLA_NOPE + MLA_ROPE
    half = MLA_ROPE // 2
    for h in range(MLA_HEADS):
        lay.put(h * LANE, h * dq, dq)
        r = MLA_HEADS * LANE + h * LANE + MLA_NOPE
        pe = h * dq + MLA_NOPE
        lay.put(r, pe + half, half, -1.0)
        lay.put(r + half, pe, half, 1.0)
    return lay


def _w_uk_layout():
    lay = _Layout(MLA_HEADS * LANE)
    per = MLA_NOPE + MLA_V
    for h in range(MLA_HEADS):
        lay.put(h * LANE, h * per, MLA_NOPE)
    return lay


def _w_uv_layout():
    lay = _Layout(MLA_HEADS * MLA_V)
    per = MLA_NOPE + MLA_V
    for h in range(MLA_HEADS):
        lay.put(h * MLA_V, h * per + MLA_NOPE, MLA_V)
    return lay


def _kpe_placement():
    e = np.zeros((LANE, MLA_HEADS * LANE), np.float32)
    for h in range(MLA_HEADS):
        for r in range(MLA_ROPE):
            e[r, h * LANE + MLA_NOPE + r] = 1.0
    return jnp.asarray(e, jnp.bfloat16)


def _t5_thresholds(max_dist):
    n = np.arange(max_dist, dtype=np.int64)
    max_exact = T5_BUCKETS // 2
    nf = np.maximum(n, 1).astype(np.float32)
    val = (np.log(nf / np.float32(max_exact)) / np.float32(math.log(T5_MAX_DIST / max_exact))
           * np.float32(T5_BUCKETS - max_exact))
    large = np.minimum(max_exact + val.astype(np.int32), T5_BUCKETS - 1)
    bucket = np.where(n < max_exact, n, large)
    assert np.all(np.diff(bucket) >= 0)
    frac = np.abs(val[max_exact + 1:T5_MAX_DIST] - np.round(val[max_exact + 1:T5_MAX_DIST]))
    assert frac.min() > 1e-3
    return [int(np.argmax(bucket >= b)) for b in range(T5_BUCKETS)]


def _proj_kernel(x_ref, gmix_ref, wrow_ref, wcol_ref, gq_ref, wq_ref, gkv_ref, wk_ref, wv_ref, epe_ref,
                 cq_ref, sq_ref, ck_ref, sk_ref,
                 qm_ref, km_ref, vm_ref, qn_ref, kvc_ref, ks_ref, vs_ref, kw_ref, vw_ref, gate_ref):
    tm = x_ref.shape[1]
    bf = jnp.bfloat16
    xn = _rms(x_ref[0], gmix_ref[...]).astype(bf)
    u = _dot(xn, wrow_ref[...])
    ut = _dot_nt(wcol_ref[...], xn)

    cqn = _rms(u[:, C_CQ:C_CQ + MLA_Q_LORA], gq_ref[...]).astype(bf)
    qq = _dot_nt(wq_ref[...], cqn)
    cosq = cq_ref[...]
    sinq = sq_ref[...]
    hw = MLA_HEADS * LANE
    for h in range(MLA_HEADS):
        qh = qq[h * LANE:(h + 1) * LANE] * cosq + qq[hw + h * LANE:hw + (h + 1) * LANE] * sinq
        qm_ref[0, h] = qh.astype(bf)

    ckvn = _rms(u[:, C_CKV:C_CKV + MLA_KV_LORA], gkv_ref[...]).astype(bf)
    kk = _dot(ckvn, wk_ref[...])
    misc = u[:, C_MISC:C_MISC + LANE]
    kpe = misc * ck_ref[...] + pltpu.roll(misc, LANE - MLA_ROPE, axis=1) * sk_ref[...]
    kpe_all = _dot(kpe.astype(bf), epe_ref[...])
    for h in range(MLA_HEADS):
        km_ref[0, h] = (kk[:, h * LANE:(h + 1) * LANE] + kpe_all[:, h * LANE:(h + 1) * LANE]).astype(bf)
    vv = _dot_nt(wv_ref[...], ckvn)
    for h in range(MLA_HEADS):
        vm_ref[0, h] = vv[h * MLA_V:(h + 1) * MLA_V].astype(bf)

    for h in range(NSA_HEADS):
        qn_ref[0, h] = ut[R_QNSA + h * NSA_DK:R_QNSA + (h + 1) * NSA_DK].astype(bf)
    kvc_ref[0] = u[:, C_KVCMP:C_KVCMP + 2 * NSA_KV_HEADS * NSA_DK]
    pos = pl.program_id(1) * tm + lax.broadcasted_iota(jnp.int32, (tm, LANE), 0)
    lane = lax.broadcasted_iota(jnp.int32, (tm, LANE), 1)
    onehot = jnp.where(lane - (LANE - SLC_ROWS) == pos // SLC_LEN, 1.0, 0.0)
    for hk in range(NSA_KV_HEADS):
        ks_ref[0, hk] = (u[:, C_KSLC + hk * LANE:C_KSLC + (hk + 1) * LANE] + onehot).astype(bf)
        kw_ref[0, hk] = u[:, C_KWIN + hk * LANE:C_KWIN + (hk + 1) * LANE].astype(bf)
        vs_ref[0, hk] = ut[R_VSLC + hk * NSA_DK:R_VSLC + (hk + 1) * NSA_DK].astype(bf)
        vw_ref[0, hk] = ut[R_VWIN + hk * NSA_DK:R_VWIN + (hk + 1) * NSA_DK].astype(bf)
        gate_ref[0, hk] = jax.nn.sigmoid(ut[R_GATE + hk * GATE_ROWS:R_GATE + (hk + 1) * GATE_ROWS])


def _proj(x, gmix, w_row, w_col, gq, w_q2, gkv, w_k, w_v, epe, cosq_t, sinq_t, cosk, sink):
    B, S, D = x.shape
    tm = min(PROJ_TM, S)
    bf = jnp.bfloat16
    grid = (B, S // tm)
    tok_major = lambda n: pl.BlockSpec((1, n, tm, LANE), lambda b, i: (b, 0, i, 0))
    feat_major = lambda n, d: pl.BlockSpec((1, n, d, tm), lambda b, i: (b, 0, 0, i))
    out_shape = (
        jax.ShapeDtypeStruct((B, MLA_HEADS, LANE, S), bf),
        jax.ShapeDtypeStruct((B, MLA_HEADS, S, LANE), bf),
        jax.ShapeDtypeStruct((B, MLA_HEADS, MLA_V, S), bf),
        jax.ShapeDtypeStruct((B, NSA_HEADS, NSA_DK, S), bf),
        jax.ShapeDtypeStruct((B, S, 2 * NSA_KV_HEADS * NSA_DK), jnp.float32),
        jax.ShapeDtypeStruct((B, NSA_KV_HEADS, S, LANE), bf),
        jax.ShapeDtypeStruct((B, NSA_KV_HEADS, NSA_DK, S), bf),
        jax.ShapeDtypeStruct((B, NSA_KV_HEADS, S, LANE), bf),
        jax.ShapeDtypeStruct((B, NSA_KV_HEADS, NSA_DK, S), bf),
        jax.ShapeDtypeStruct((B, NSA_KV_HEADS, GATE_ROWS, S), jnp.float32),
    )
    out_specs = (
        feat_major(MLA_HEADS, LANE), tok_major(MLA_HEADS), feat_major(MLA_HEADS, MLA_V),
        feat_major(NSA_HEADS, NSA_DK),
        pl.BlockSpec((1, tm, 2 * NSA_KV_HEADS * NSA_DK), lambda b, i: (b, i, 0)),
        tok_major(NSA_KV_HEADS), feat_major(NSA_KV_HEADS, NSA_DK),
        tok_major(NSA_KV_HEADS), feat_major(NSA_KV_HEADS, NSA_DK),
        feat_major(NSA_KV_HEADS, GATE_ROWS),
    )
    consts = (gmix, w_row, w_col, gq, w_q2, gkv, w_k, w_v, epe)
    in_specs = ([pl.BlockSpec((1, tm, D), lambda b, i: (b, i, 0))] + [_const_spec(c.shape) for c in consts]
                + [pl.BlockSpec((LANE, tm), lambda b, i: (0, i)), pl.BlockSpec((LANE, tm), lambda b, i: (0, i)),
                   pl.BlockSpec((tm, LANE), lambda b, i: (i, 0)), pl.BlockSpec((tm, LANE), lambda b, i: (i, 0))])
    return pl.pallas_call(
        _proj_kernel, grid=grid, in_specs=in_specs, out_specs=out_specs, out_shape=out_shape,
        compiler_params=_params(2), name="proj",
    )(x, *consts, cosq_t, sinq_t, cosk, sink)


def _compress_kernel(c_ref, pos_ref, w1_ref, w2k_ref, w2vt_ref, kc_ref, vc_ref):
    nc = c_ref.shape[2]
    bf = jnp.bfloat16
    for kv in range(2):
        for hk in range(NSA_KV_HEADS):
            c = c_ref[0, kv * NSA_KV_HEADS + hk]
            a = _dot((c + pos_ref[kv, 0]).astype(bf), w1_ref[kv, 0])
            b = _dot((c + pos_ref[kv, 1]).astype(bf), w1_ref[kv, 1])
            hid = a + pltpu.roll(b, nc - 1, axis=0)
            act = jax.nn.gelu(hid).astype(bf)
            if kv == 0:
                kc_ref[0, hk] = _dot(act, w2k_ref[...]).astype(bf)
            else:
                vc_ref[0, hk] = _dot_nt(w2vt_ref[...], act).astype(bf)


def _compress(kvc, pos, w1, w2k, w2vt):
    B, _, nc, width = kvc.shape
    return pl.pallas_call(
        _compress_kernel, grid=(B,),
        in_specs=[pl.BlockSpec((1, 2 * NSA_KV_HEADS, nc, width), lambda b: (b, 0, 0, 0)),
                  _const_spec(pos.shape), _const_spec(w1.shape), _const_spec(w2k.shape),
                  _const_spec(w2vt.shape)],
        out_specs=(pl.BlockSpec((1, NSA_KV_HEADS, nc, LANE), lambda b: (b, 0, 0, 0)),
                   pl.BlockSpec((1, NSA_KV_HEADS, NSA_DK, nc), lambda b: (b, 0, 0, 0))),
        out_shape=(jax.ShapeDtypeStruct((B, NSA_KV_HEADS, nc, LANE), jnp.bfloat16),
                   jax.ShapeDtypeStruct((B, NSA_KV_HEADS, NSA_DK, nc), jnp.bfloat16)),
        compiler_params=_params(1), name="compress",
    )(kvc, pos, w1, w2k, w2vt)


def _t5_lookup(dist, tab_ref, head, thr):
    val = jnp.full(dist.shape, tab_ref[0, head], jnp.float32)
    for b in range(1, T5_BUCKETS):
        val = jnp.where(dist >= thr[b], tab_ref[b, head], val)
    return val - tab_ref[T5_BUCKETS - 1, head]


def _bias_cmp_kernel(tab_ref, out_ref, *, thr, tq, n_cmp):
    hk = pl.program_id(0)
    t = pl.program_id(1)
    nc = out_ref.shape[2]
    n = lax.broadcasted_iota(jnp.int32, (nc, tq), 0)
    i = lax.broadcasted_iota(jnp.int32, (nc, tq), 1)
    dist = t * tq + i - (n * CMP_STRIDE + CMP_LEN - 1)
    ok = jnp.logical_and(dist >= 0, n < n_cmp)
    for g in range(NSA_GROUP):
        val = _t5_lookup(dist, tab_ref, hk * NSA_GROUP + g, thr)
        out_ref[0, 0, :, g * tq:(g + 1) * tq] = jnp.where(ok, val, NEG)


def _bias_tile_kernel(tab_ref, out_ref, *, thr, tq):
    hk = pl.program_id(0)
    tk = out_ref.shape[2]
    j = lax.broadcasted_iota(jnp.int32, (tk, tq), 0)
    i = lax.broadcasted_iota(jnp.int32, (tk, tq), 1)
    for g in range(NSA_GROUP):
        head = hk * NSA_GROUP + g
        cols = slice(g * tq, (g + 1) * tq)
        d0 = i - j
        out_ref[0, 0, :, cols] = jnp.where(d0 >= 0, _t5_lookup(d0, tab_ref, head, thr), NEG)
        out_ref[0, 1, :, cols] = _t5_lookup(d0 + tk, tab_ref, head, thr)
        out_ref[0, 2, :, cols] = jnp.where(j > i, 0.0, NEG)


def _bias_tables(t5_table, S):
    thr = _t5_thresholds(S)
    tq, tk = NSA_TQ, NSA_TK
    assert tq == tk and WINDOW == 2 * tk and tk >= T5_MAX_DIST
    nq = S // tq
    nc = S // CMP_STRIDE
    n_cmp = (S - CMP_LEN) // CMP_STRIDE + 1
    m = NSA_GROUP * tq
    smem = pl.BlockSpec(memory_space=pltpu.SMEM)
    bias_c = pl.pallas_call(
        functools.partial(_bias_cmp_kernel, thr=thr, tq=tq, n_cmp=n_cmp),
        grid=(NSA_KV_HEADS, nq), in_specs=[smem],
        out_specs=pl.BlockSpec((1, 1, nc, m), lambda h, t: (h, t, 0, 0)),
        out_shape=jax.ShapeDtypeStruct((NSA_KV_HEADS, nq, nc, m), jnp.float32),
        compiler_params=_params(2), name="bias_cmp",
    )(t5_table)
    tz = pl.pallas_call(
        functools.partial(_bias_tile_kernel, thr=thr, tq=tq),
        grid=(NSA_KV_HEADS,), in_specs=[smem],
        out_specs=pl.BlockSpec((1, 3, tk, m), lambda h: (h, 0, 0, 0)),
        out_shape=jax.ShapeDtypeStruct((NSA_KV_HEADS, 3, tk, m), jnp.float32),
        compiler_params=_params(1), name="bias_tile",
    )(t5_table)
    return bias_c, tz


def _flash_step(k, q_t, v_t, bias_t, m_ref, l_ref, acc_ref):
    s = _dot(k, q_t)
    if bias_t is not None:
        s = s + bias_t
    m_prev = m_ref[...]
    m_new = jnp.maximum(m_prev, jnp.max(s, axis=0, keepdims=True))
    alpha = jnp.exp(m_prev - m_new)
    p = jnp.exp(s - m_new)
    l_ref[...] = alpha * l_ref[...] + jnp.sum(p, axis=0, keepdims=True)
    acc_ref[...] = alpha * acc_ref[...] + _dot(v_t, p.astype(jnp.bfloat16))
    m_ref[...] = m_new


def _flash_init(m_ref, l_ref, acc_ref):
    m_ref[...] = jnp.full(m_ref.shape, -jnp.inf, jnp.float32)
    l_ref[...] = jnp.zeros(l_ref.shape, jnp.float32)
    acc_ref[...] = jnp.zeros(acc_ref.shape, jnp.float32)


def _mla_kernel(q_ref, k_ref, v_ref, o_ref, m_ref, l_ref, acc_ref, *, tq, tk):
    S = k_ref.shape[2]
    nqt = S // tq
    kk = lax.broadcasted_iota(jnp.int32, (tk, tq), 0)
    qq = lax.broadcasted_iota(jnp.int32, (tk, tq), 1)
    causal = jnp.where(kk <= qq, 0.0, NEG)

    def q_tile(qt, carry):
        q0 = pl.multiple_of(qt * tq, tq)
        q_t = [q_ref[0, e, :, pl.ds(q0, tq)] for e in range(2)]
        for e in range(2):
            _flash_init(m_ref.at[e], l_ref.at[e], acc_ref.at[e])

        def step(k0, bias):
            for e in range(2):
                _flash_step(k_ref[0, e, pl.ds(k0, tk), :], q_t[e], v_ref[0, e, :, pl.ds(k0, tk)], bias,
                            m_ref.at[e], l_ref.at[e], acc_ref.at[e])

        def far(c, carry2):
            step(pl.multiple_of(c * tk, tk), None)
            return carry2

        lax.fori_loop(0, qt, far, 0)
        step(q0, causal)
        o_t = jnp.concatenate([acc_ref[e] / l_ref[e] for e in range(2)], axis=0)
        o_ref[0, pl.ds(q0, tq), :] = jnp.transpose(o_t)
        return carry

    lax.fori_loop(0, nqt, q_tile, 0)


def _mla(q_t, k, v_t):
    B, H, S, _ = k.shape
    tq = min(MLA_TQ, S)
    tk = tq
    return pl.pallas_call(
        functools.partial(_mla_kernel, tq=tq, tk=tk),
        grid=(B, H // 2),
        in_specs=[pl.BlockSpec((1, 2, LANE, S), lambda b, p: (b, p, 0, 0)),
                  pl.BlockSpec((1, 2, S, LANE), lambda b, p: (b, p, 0, 0)),
                  pl.BlockSpec((1, 2, MLA_V, S), lambda b, p: (b, p, 0, 0))],
        out_specs=pl.BlockSpec((1, S, 2 * MLA_V), lambda b, p: (b, 0, p)),
        out_shape=jax.ShapeDtypeStruct((B, S, H * MLA_V), jnp.float32),
        scratch_shapes=[pltpu.VMEM((2, 1, tq), jnp.float32), pltpu.VMEM((2, 1, tq), jnp.float32),
                        pltpu.VMEM((2, MLA_V, tq), jnp.float32)],
        compiler_params=_params(2), name="mla",
    )(q_t, k, v_t)


def _nsa_kernel(q_ref, ks_ref, vs_ref, kw_ref, vw_ref, kc_ref, vc_ref, gate_ref, bc_ref, tz_ref, ovt_ref,
                o_ref, qaug_ref, m_ref, l_ref, acc_ref, os_ref, *, tq, tk):
    t = pl.program_id(2)
    G = NSA_GROUP
    ncol = G * tq
    bf = jnp.bfloat16
    q0 = t * tq
    for g in range(G):
        qaug_ref[0:NSA_DK, g * tq:(g + 1) * tq] = q_ref[0, g]
    qaug_ref[NSA_DK:, :] = jnp.zeros((SLC_ROWS, ncol), bf)

    s = _dot(kc_ref[0, 0], qaug_ref[...]) + bc_ref[0, 0]
    mx = jnp.max(s, axis=0, keepdims=True)
    p = jnp.exp(s - mx)
    lsum = jnp.sum(p, axis=0, keepdims=True)
    ci = lax.broadcasted_iota(jnp.int32, (1, ncol), 1)
    col_pos = q0 + ci % tq
    r = jnp.where(col_pos >= CMP_LEN - 1, 1.0 / lsum, 0.0)
    pcb = (p * r).astype(bf)
    o_c = _dot(vc_ref[0, 0], pcb)

    imp = _dot(ovt_ref[...], pcb[:, 0:tq])
    for g in range(1, G):
        imp = imp + _dot(ovt_ref[...], pcb[:, g * tq:(g + 1) * tq])
    nb = imp.shape[0]
    jb = lax.broadcasted_iota(jnp.int32, (nb, tq), 0)
    cur = (q0 + lax.broadcasted_iota(jnp.int32, (nb, tq), 1)) // SLC_LEN
    forced = jnp.logical_or(jb == 0, jnp.logical_or(jb == cur, jb == cur - 1))
    imp = jnp.where(forced, FORCE, jnp.where(jb <= cur, imp, NEG))
    sub = SUBLANE
    slabs = [imp[r0 * sub:(r0 + 1) * sub] for r0 in range(nb // sub)]
    ranks = [jnp.zeros((sub, tq), jnp.int32) for _ in slabs]
    jsub = lax.broadcasted_iota(jnp.int32, (sub, tq), 0)
    for jp in range(nb):
        rowv = imp[jp:jp + 1, :]
        for r0, slab in enumerate(slabs):
            lo = r0 * sub
            if lo > jp:
                one = jnp.where(rowv >= slab, 1, 0)
            elif lo + sub - 1 <= jp:
                one = jnp.where(rowv > slab, 1, 0)
            else:
                one = jnp.where(jsub + lo > jp, jnp.where(rowv >= slab, 1, 0), jnp.where(rowv > slab, 1, 0))
            ranks[r0] = ranks[r0] + one
    rank = jnp.concatenate(ranks, axis=0)
    selb = jnp.where(rank < SLC_TOPN, 0.0, NEG).astype(bf)
    for g in range(G):
        qaug_ref[NSA_DK:, g * tq:(g + 1) * tq] = selb
    q_aug = qaug_ref[...]

    def tiles(k_ref, v_ref, c):
        k0 = pl.multiple_of(c * tk, tk)
        return k_ref[0, 0, pl.ds(k0, tk), :], v_ref[0, 0, :, pl.ds(k0, tk)]

    _flash_init(m_ref, l_ref, acc_ref)

    def far(c, carry):
        kt, vt = tiles(ks_ref, vs_ref, c)
        _flash_step(kt, q_aug, vt, None, m_ref, l_ref, acc_ref)
        return carry

    lax.fori_loop(0, jnp.maximum(t - 1, 0), far, 0)

    @pl.when(t >= 1)
    def _():
        kt, vt = tiles(ks_ref, vs_ref, t - 1)
        _flash_step(kt, q_aug, vt, tz_ref[0, 1], m_ref, l_ref, acc_ref)

    kt, vt = tiles(ks_ref, vs_ref, t)
    _flash_step(kt, q_aug, vt, tz_ref[0, 0], m_ref, l_ref, acc_ref)
    os_ref[...] = acc_ref[...] / l_ref[...]

    _flash_init(m_ref, l_ref, acc_ref)

    @pl.when(t >= 2)
    def _():
        kt, vt = tiles(kw_ref, vw_ref, t - 2)
        _flash_step(kt, q_aug, vt, tz_ref[0, 2], m_ref, l_ref, acc_ref)

    @pl.when(t >= 1)
    def _():
        kt, vt = tiles(kw_ref, vw_ref, t - 1)
        _flash_step(kt, q_aug, vt, tz_ref[0, 1], m_ref, l_ref, acc_ref)

    kt, vt = tiles(kw_ref, vw_ref, t)
    _flash_step(kt, q_aug, vt, tz_ref[0, 0], m_ref, l_ref, acc_ref)
    o_w = acc_ref[...] / l_ref[...]
    o_s = os_ref[...]

    gate = gate_ref[0, 0]
    comb = []
    for g in range(G):
        cols = slice(g * tq, (g + 1) * tq)
        gr = lambda br: gate[g * 3 + br:g * 3 + br + 1, :]
        comb.append(gr(0) * o_c[:, cols] + gr(1) * o_s[:, cols] + gr(2) * o_w[:, cols])
    o_ref[0] = jnp.transpose(jnp.concatenate(comb, axis=0))


def _nsa(q_t, ks, vs_t, kw, vw_t, kc, vc_t, gate_t, bias_c, tz, ovt):
    B, _, S, _ = ks.shape
    tq, tk = NSA_TQ, NSA_TK
    nq = S // tq
    nc = kc.shape[2]
    m = NSA_GROUP * tq
    tok = lambda: pl.BlockSpec((1, 1, S, LANE), lambda b, h, t: (b, h, 0, 0))
    feat = lambda: pl.BlockSpec((1, 1, NSA_DK, S), lambda b, h, t: (b, h, 0, 0))
    return pl.pallas_call(
        functools.partial(_nsa_kernel, tq=tq, tk=tk),
        grid=(B, NSA_KV_HEADS, nq),
        in_specs=[pl.BlockSpec((1, NSA_GROUP, NSA_DK, tq), lambda b, h, t: (b, h, 0, t)),
                  tok(), feat(), tok(), feat(),
                  pl.BlockSpec((1, 1, nc, LANE), lambda b, h, t: (b, h, 0, 0)),
                  pl.BlockSpec((1, 1, NSA_DK, nc), lambda b, h, t: (b, h, 0, 0)),
                  pl.BlockSpec((1, 1, GATE_ROWS, tq), lambda b, h, t: (b, h, 0, t)),
                  pl.BlockSpec((1, 1, nc, m), lambda b, h, t: (h, t, 0, 0)),
                  pl.BlockSpec((1, 3, tk, m), lambda b, h, t: (h, 0, 0, 0)),
                  _const_spec(ovt.shape)],
        out_specs=pl.BlockSpec((1, tq, NSA_GROUP * NSA_DK), lambda b, h, t: (b, t, h)),
        out_shape=jax.ShapeDtypeStruct((B, S, NSA_HEADS * NSA_DK), jnp.float32),
        scratch_shapes=[pltpu.VMEM((NSA_DK + SLC_ROWS, m), jnp.bfloat16),
                        pltpu.VMEM((1, m), jnp.float32), pltpu.VMEM((1, m), jnp.float32),
                        pltpu.VMEM((NSA_DK, m), jnp.float32), pltpu.VMEM((NSA_DK, m), jnp.float32)],
        compiler_params=_params(3), name="nsa",
    )(q_t, ks, vs_t, kw, vw_t, kc, vc_t, gate_t, bias_c, tz, ovt)


def _post_kernel(x_ref, om_ref, on_ref, gm_ref, gn_ref, wo_ref, gf_ref, wg_ref, wu_ref, wd_ref,
                 gfin_ref, o_ref):
    half = om_ref.shape[2]
    mix_m = _rms(om_ref[0], gm_ref[...]).astype(jnp.bfloat16)
    mix_n = _rms(on_ref[0], gn_ref[...]).astype(jnp.bfloat16)
    h = x_ref[0] + _dot(mix_m, wo_ref[0:half, :]) + _dot(mix_n, wo_ref[half:2 * half, :])
    f = _rms(h, gf_ref[...]).astype(jnp.bfloat16)
    a = _dot(f, wg_ref[...])
    act = (a * jax.nn.sigmoid(a) * _dot(f, wu_ref[...])).astype(jnp.bfloat16)
    h = h + _dot(act, wd_ref[...])
    o_ref[0] = _rms(h, gfin_ref[...])


def _post(x, o_mla, o_nsa, gm, gn, w_out, gf, wg, wu, wd, gfin):
    B, S, D = x.shape
    tm = min(POST_TM, S)
    tok = lambda w: pl.BlockSpec((1, tm, w), lambda b, i: (b, i, 0))
    consts = (gm, gn, w_out, gf, wg, wu, wd, gfin)
    return pl.pallas_call(
        _post_kernel, grid=(B, S // tm),
        in_specs=[tok(D), tok(o_mla.shape[2]), tok(o_nsa.shape[2])] + [_const_spec(c.shape) for c in consts],
        out_specs=tok(D), out_shape=jax.ShapeDtypeStruct((B, S, D), jnp.float32),
        compiler_params=_params(2), name="post",
    )(x, o_mla, o_nsa, *consts)


def _rope_tables(S):
    pos = jnp.arange(S, dtype=jnp.float32)
    inv = ROPE_THETA ** (-jnp.arange(0, MLA_ROPE, 2, dtype=jnp.float32) / MLA_ROPE)
    ang = pos[:, None] * inv[None, :]
    cos, sin = jnp.cos(ang), jnp.sin(ang)
    cos2 = jnp.concatenate([cos, cos], axis=-1)
    sin2 = jnp.concatenate([sin, sin], axis=-1)
    scale = (MLA_NOPE + MLA_ROPE) ** -0.5
    zq = jnp.zeros((S, LANE - MLA_NOPE - MLA_ROPE), jnp.float32)
    cosq = jnp.concatenate([jnp.ones((S, MLA_NOPE), jnp.float32), cos2, zq], axis=-1) * scale
    sinq = jnp.concatenate([jnp.zeros((S, MLA_NOPE), jnp.float32), sin2, zq], axis=-1) * scale
    zk = jnp.zeros((S, LANE - MLA_ROPE), jnp.float32)
    cosk = jnp.concatenate([cos2, zk], axis=-1)
    sink = jnp.concatenate([sin2, zk], axis=-1)
    return cosq.T, sinq.T, cosk, sink


def _overlap_t(S):
    n_cmp = (S - CMP_LEN) // CMP_STRIDE + 1
    n_slc = S // SLC_LEN
    assert n_slc <= SLC_ROWS
    cs = np.arange(n_cmp) * CMP_STRIDE
    ss = np.arange(n_slc) * SLC_LEN
    ov = np.maximum(0, np.minimum(cs[:, None] + CMP_LEN, ss[None, :] + SLC_LEN)
                    - np.maximum(cs[:, None], ss[None, :])).astype(np.float32) / CMP_STRIDE
    out = np.zeros((SLC_ROWS, S // CMP_STRIDE), np.float32)
    out[:n_slc, :n_cmp] = ov.T
    return jnp.asarray(out, jnp.bfloat16)


def kernel(x, norm_mix_g, w_in, mla_q_norm_g, mla_w_uq, mla_kv_norm_g, mla_w_ukv, nsa_cmp_pos_k, nsa_cmp_w1_k, nsa_cmp_w2_k, nsa_cmp_pos_v, nsa_cmp_w1_v, nsa_cmp_w2_v, t5_table, out_norm_mla_g, out_norm_nsa_g, w_out, norm_ffn_g, w_gate, w_up, w_down, final_norm_g):
    B, S, D = x.shape
    assert w_in.shape[0] == 1
    assert D == D_MODEL and S % NSA_TQ == 0 and S % CMP_STRIDE == 0
    bf = jnp.bfloat16
    l = 0
    cosq_t, sinq_t, cosk, sink = _rope_tables(S)
    bias_c, tz = _bias_tables(t5_table, S)
    ovt = _overlap_t(S)
    epe = _kpe_placement()
    nc = S // CMP_STRIDE
    cw = CMP_STRIDE * NSA_DK
    row = lambda v: v.reshape(1, -1)

    w_row = _w_in_row_layout().apply(w_in[l])
    w_col = _w_in_col_layout().apply(w_in[l]).T
    w_q2 = _w_uq_layout().apply(mla_w_uq[l]).T
    w_k = _w_uk_layout().apply(mla_w_ukv[l])
    w_v = _w_uv_layout().apply(mla_w_ukv[l]).T
    (q_mla, k_mla, v_mla, q_nsa, kv_cmp, k_slc, v_slc, k_win, v_win, gates) = _proj(
        x, row(norm_mix_g[l]), w_row, w_col, row(mla_q_norm_g[l]), w_q2, row(mla_kv_norm_g[l]), w_k, w_v, epe,
        cosq_t, sinq_t, cosk, sink)

    kvc = kv_cmp.reshape(B, S, 2 * NSA_KV_HEADS, NSA_DK).transpose(0, 2, 1, 3).reshape(
        B, 2 * NSA_KV_HEADS, nc, cw)
    pos = jnp.stack([nsa_cmp_pos_k[l].reshape(2, 1, cw), nsa_cmp_pos_v[l].reshape(2, 1, cw)])
    w1 = jnp.stack([nsa_cmp_w1_k[l].reshape(2, cw, CMP_HIDDEN),
                    nsa_cmp_w1_v[l].reshape(2, cw, CMP_HIDDEN)]).astype(bf)
    w2k = jnp.concatenate([nsa_cmp_w2_k[l], jnp.zeros((CMP_HIDDEN, LANE - NSA_DK), jnp.float32)],
                          axis=1).astype(bf)
    w2vt = nsa_cmp_w2_v[l].T.astype(bf)
    k_cmp, v_cmp = _compress(kvc, pos, w1, w2k, w2vt)

    o_mla = _mla(q_mla, k_mla, v_mla)
    o_nsa = _nsa(q_nsa, k_slc, v_slc, k_win, v_win, k_cmp, v_cmp, gates, bias_c, tz, ovt)

    return _post(x, o_mla, o_nsa, row(out_norm_mla_g[l]), row(out_norm_nsa_g[l]), w_out[l].astype(bf),
                 row(norm_ffn_g[l]), w_gate[l].astype(bf), w_up[l].astype(bf), w_down[l].astype(bf),
                 row(final_norm_g))
```

```python
import functools
import math

import numpy as np
import jax
import jax.numpy as jnp
from jax import lax
from jax.experimental import pallas as pl
from jax.experimental.pallas import tpu as pltpu

D_MODEL = 1024
MLA_HEADS = 8
MLA_NOPE = 64
MLA_ROPE = 32
MLA_V = 64
MLA_Q_LORA = 256
MLA_KV_LORA = 128
ROPE_THETA = 10000.0
NSA_HEADS = 8
NSA_KV_HEADS = 2
NSA_GROUP = 4
NSA_DK = 64
CMP_LEN = 32
CMP_STRIDE = 16
CMP_HIDDEN = 128
SLC_LEN = 64
SLC_TOPN = 16
WINDOW = 512
T5_BUCKETS = 32
T5_MAX_DIST = 128
D_FF = 2816
EPS = 1e-6
NEG = -1e30
FORCE = 1e30

LANE = 128
SUBLANE = 8
SLC_ROWS = 64
GATE_ROWS = 16

PROJ_TM = 256
MLA_TQ = 512
MLA_TK = 256
MLA_CW = 256
FLASH_LOOKAHEAD = 4
NSA_TQ = 256
NSA_TK = 256
POST_TM = 512
VMEM_LIMIT = 56 * 1024 * 1024

C_CQ = 0
C_CKV = 256
C_MISC = 384
C_KVCMP = 512
C_KSLC = 768
C_KWIN = 1024
D_ROW = 1280
R_QNSA = 0
R_VSLC = 512
R_VWIN = 640
R_GATE = 768
D_COL = 800


def _dot(a, b):
    return jnp.dot(a, b, preferred_element_type=jnp.float32)


def _dot_nt(a, b):
    return lax.dot_general(a, b, (((1,), (1,)), ((), ())), preferred_element_type=jnp.float32)


def _rms(x, g):
    return x * lax.rsqrt(jnp.mean(x * x, axis=-1, keepdims=True) + EPS) * g


def _const_spec(shape):
    nd = len(shape)
    return pl.BlockSpec(shape, lambda *_: (0,) * nd, pipeline_mode=pl.Buffered(1))


def _params(n_axes):
    return pltpu.CompilerParams(dimension_semantics=("arbitrary",) * n_axes,
                                vmem_limit_bytes=VMEM_LIMIT)


def _in_offsets():
    o_krope = MLA_Q_LORA + MLA_KV_LORA
    o_qnsa = o_krope + MLA_ROPE
    o_cmp = o_qnsa + NSA_HEADS * NSA_DK
    o_slc = o_cmp + 2 * NSA_KV_HEADS * NSA_DK
    o_win = o_slc + 2 * NSA_KV_HEADS * NSA_DK
    o_gate = o_win + 2 * NSA_KV_HEADS * NSA_DK
    return o_krope, o_qnsa, o_cmp, o_slc, o_win, o_gate


class _Layout:
    def __init__(self, n):
        self.src = np.zeros((n,), np.int32)
        self.mul = np.zeros((n,), np.float32)

    def put(self, dst, s, n, m=1.0):
        self.src[dst:dst + n] = np.arange(s, s + n)
        self.mul[dst:dst + n] = m

    def apply(self, w):
        return (w[:, self.src] * self.mul[None, :]).astype(jnp.bfloat16)


def _w_in_row_layout():
    o_krope, _, o_cmp, o_slc, o_win, _ = _in_offsets()
    lay = _Layout(D_ROW)
    lay.put(C_CQ, 0, MLA_Q_LORA)
    lay.put(C_CKV, MLA_Q_LORA, MLA_KV_LORA)
    half = MLA_ROPE // 2
    lay.put(C_MISC, o_krope, MLA_ROPE)
    lay.put(C_MISC + MLA_ROPE, o_krope + half, half, -1.0)
    lay.put(C_MISC + MLA_ROPE + half, o_krope, half, 1.0)
    lay.put(C_KVCMP, o_cmp, 2 * NSA_KV_HEADS * NSA_DK)
    for hk in range(NSA_KV_HEADS):
        lay.put(C_KSLC + hk * LANE, o_slc + hk * NSA_DK, NSA_DK)
        lay.put(C_KWIN + hk * LANE, o_win + hk * NSA_DK, NSA_DK)
    return lay


def _w_in_col_layout():
    _, o_qnsa, _, o_slc, o_win, o_gate = _in_offsets()
    lay = _Layout(D_COL)
    lay.put(R_QNSA, o_qnsa, NSA_HEADS * NSA_DK, NSA_DK ** -0.5)
    lay.put(R_VSLC, o_slc + NSA_KV_HEADS * NSA_DK, NSA_KV_HEADS * NSA_DK)
    lay.put(R_VWIN, o_win + NSA_KV_HEADS * NSA_DK, NSA_KV_HEADS * NSA_DK)
    for hk in range(NSA_KV_HEADS):
        lay.put(R_GATE + hk * GATE_ROWS, o_gate + hk * NSA_GROUP * 3, NSA_GROUP * 3)
    return lay


def _w_uq_layout():
    lay = _Layout(2 * MLA_HEADS * LANE)
    dq = MLA_NOPE + MLA_ROPE
    half = MLA_ROPE // 2
    for h in range(MLA_HEADS):
        lay.put(h * LANE, h * dq, dq)
        r = MLA_HEADS * LANE + h * LANE + MLA_NOPE
        pe = h * dq + MLA_NOPE
        lay.put(r, pe + half, half, -1.0)
        lay.put(r + half, pe, half, 1.0)
    return lay


def _w_uk_layout():
    lay = _Layout(MLA_HEADS * LANE)
    per = MLA_NOPE + MLA_V
    for h in range(MLA_HEADS):
        lay.put(h * LANE, h * per, MLA_NOPE)
    return lay


def _w_uv_layout():
    lay = _Layout(MLA_HEADS * MLA_V)
    per = MLA_NOPE + MLA_V
    for h in range(MLA_HEADS):
        lay.put(h * MLA_V, h * per + MLA_NOPE, MLA_V)
    return lay


def _kpe_placement():
    e = np.zeros((LANE, MLA_HEADS * LANE), np.float32)
    for h in range(MLA_HEADS):
        for r in range(MLA_ROPE):
            e[r, h * LANE + MLA_NOPE + r] = 1.0
    return jnp.asarray(e, jnp.bfloat16)


def _t5_thresholds(max_dist):
    n = np.arange(max_dist, dtype=np.int64)
    max_exact = T5_BUCKETS // 2
    nf = np.maximum(n, 1).astype(np.float32)
    val = (np.log(nf / np.float32(max_exact)) / np.float32(math.log(T5_MAX_DIST / max_exact))
           * np.float32(T5_BUCKETS - max_exact))
    large = np.minimum(max_exact + val.astype(np.int32), T5_BUCKETS - 1)
    bucket = np.where(n < max_exact, n, large)
    assert np.all(np.diff(bucket) >= 0)
    frac = np.abs(val[max_exact + 1:T5_MAX_DIST] - np.round(val[max_exact + 1:T5_MAX_DIST]))
    assert frac.min() > 1e-3
    return [int(np.argmax(bucket >= b)) for b in range(T5_BUCKETS)]


def _proj_kernel(x_ref, gmix_ref, wrow_ref, wcol_ref, gq_ref, wq_ref, gkv_ref, wk_ref, wv_ref, epe_ref,
                 cq_ref, sq_ref, ck_ref, sk_ref,
                 qm_ref, km_ref, vm_ref, qn_ref, kvc_ref, ks_ref, vs_ref, kw_ref, vw_ref, gate_ref):
    tm = x_ref.shape[1]
    bf = jnp.bfloat16
    xn = _rms(x_ref[0], gmix_ref[...]).astype(bf)
    u = _dot(xn, wrow_ref[...])
    ut = _dot_nt(wcol_ref[...], xn)

    cqn = _rms(u[:, C_CQ:C_CQ + MLA_Q_LORA], gq_ref[...]).astype(bf)
    qq = _dot_nt(wq_ref[...], cqn)
    cosq = cq_ref[...]
    sinq = sq_ref[...]
    hw = MLA_HEADS * LANE
    for h in range(MLA_HEADS):
        qh = qq[h * LANE:(h + 1) * LANE] * cosq + qq[hw + h * LANE:hw + (h + 1) * LANE] * sinq
        qm_ref[0, h] = qh.astype(bf)

    ckvn = _rms(u[:, C_CKV:C_CKV + MLA_KV_LORA], gkv_ref[...]).astype(bf)
    kk = _dot(ckvn, wk_ref[...])
    misc = u[:, C_MISC:C_MISC + LANE]
    kpe = misc * ck_ref[...] + pltpu.roll(misc, LANE - MLA_ROPE, axis=1) * sk_ref[...]
    kpe_all = _dot(kpe.astype(bf), epe_ref[...])
    for h in range(MLA_HEADS):
        km_ref[0, h] = (kk[:, h * LANE:(h + 1) * LANE] + kpe_all[:, h * LANE:(h + 1) * LANE]).astype(bf)
    vv = _dot_nt(wv_ref[...], ckvn)
    for h in range(MLA_HEADS):
        vm_ref[0, h] = vv[h * MLA_V:(h + 1) * MLA_V].astype(bf)

    for h in range(NSA_HEADS):
        qn_ref[0, h] = ut[R_QNSA + h * NSA_DK:R_QNSA + (h + 1) * NSA_DK].astype(bf)
    kvc_ref[0] = u[:, C_KVCMP:C_KVCMP + 2 * NSA_KV_HEADS * NSA_DK]
    pos = pl.program_id(1) * tm + lax.broadcasted_iota(jnp.int32, (tm, LANE), 0)
    lane = lax.broadcasted_iota(jnp.int32, (tm, LANE), 1)
    onehot = jnp.where(lane - (LANE - SLC_ROWS) == pos // SLC_LEN, 1.0, 0.0)
    for hk in range(NSA_KV_HEADS):
        ks_ref[0, hk] = (u[:, C_KSLC + hk * LANE:C_KSLC + (hk + 1) * LANE] + onehot).astype(bf)
        kw_ref[0, hk] = u[:, C_KWIN + hk * LANE:C_KWIN + (hk + 1) * LANE].astype(bf)
        vs_ref[0, hk] = ut[R_VSLC + hk * NSA_DK:R_VSLC + (hk + 1) * NSA_DK].astype(bf)
        vw_ref[0, hk] = ut[R_VWIN + hk * NSA_DK:R_VWIN + (hk + 1) * NSA_DK].astype(bf)
        gate_ref[0, hk] = jax.nn.sigmoid(ut[R_GATE + hk * GATE_ROWS:R_GATE + (hk + 1) * GATE_ROWS])


def _proj(x, gmix, w_row, w_col, gq, w_q2, gkv, w_k, w_v, epe, cosq_t, sinq_t, cosk, sink):
    B, S, D = x.shape
    tm = min(PROJ_TM, S)
    bf = jnp.bfloat16
    grid = (B, S // tm)
    tok_major = lambda n: pl.BlockSpec((1, n, tm, LANE), lambda b, i: (b, 0, i, 0))
    feat_major = lambda n, d: pl.BlockSpec((1, n, d, tm), lambda b, i: (b, 0, 0, i))
    out_shape = (
        jax.ShapeDtypeStruct((B, MLA_HEADS, LANE, S), bf),
        jax.ShapeDtypeStruct((B, MLA_HEADS, S, LANE), bf),
        jax.ShapeDtypeStruct((B, MLA_HEADS, MLA_V, S), bf),
        jax.ShapeDtypeStruct((B, NSA_HEADS, NSA_DK, S), bf),
        jax.ShapeDtypeStruct((B, S, 2 * NSA_KV_HEADS * NSA_DK), jnp.float32),
        jax.ShapeDtypeStruct((B, NSA_KV_HEADS, S, LANE), bf),
        jax.ShapeDtypeStruct((B, NSA_KV_HEADS, NSA_DK, S), bf),
        jax.ShapeDtypeStruct((B, NSA_KV_HEADS, S, LANE), bf),
        jax.ShapeDtypeStruct((B, NSA_KV_HEADS, NSA_DK, S), bf),
        jax.ShapeDtypeStruct((B, NSA_KV_HEADS, GATE_ROWS, S), jnp.float32),
    )
    out_specs = (
        feat_major(MLA_HEADS, LANE), tok_major(MLA_HEADS), feat_major(MLA_HEADS, MLA_V),
        feat_major(NSA_HEADS, NSA_DK),
        pl.BlockSpec((1, tm, 2 * NSA_KV_HEADS * NSA_DK), lambda b, i: (b, i, 0)),
        tok_major(NSA_KV_HEADS), feat_major(NSA_KV_HEADS, NSA_DK),
        tok_major(NSA_KV_HEADS), feat_major(NSA_KV_HEADS, NSA_DK),
        feat_major(NSA_KV_HEADS, GATE_ROWS),
    )
    consts = (gmix, w_row, w_col, gq, w_q2, gkv, w_k, w_v, epe)
    in_specs = ([pl.BlockSpec((1, tm, D), lambda b, i: (b, i, 0))] + [_const_spec(c.shape) for c in consts]
                + [pl.BlockSpec((LANE, tm), lambda b, i: (0, i)), pl.BlockSpec((LANE, tm), lambda b, i: (0, i)),
                   pl.BlockSpec((tm, LANE), lambda b, i: (i, 0)), pl.BlockSpec((tm, LANE), lambda b, i: (i, 0))])
    return pl.pallas_call(
        _proj_kernel, grid=grid, in_specs=in_specs, out_specs=out_specs, out_shape=out_shape,
        compiler_params=_params(2), name="proj",
    )(x, *consts, cosq_t, sinq_t, cosk, sink)


def _compress_kernel(c_ref, pos_ref, w1_ref, w2k_ref, w2vt_ref, kc_ref, vc_ref):
    nc = c_ref.shape[2]
    bf = jnp.bfloat16
    for kv in range(2):
        for hk in range(NSA_KV_HEADS):
            c = c_ref[0, kv * NSA_KV_HEADS + hk]
            a = _dot((c + pos_ref[kv, 0]).astype(bf), w1_ref[kv, 0])
            b = _dot((c + pos_ref[kv, 1]).astype(bf), w1_ref[kv, 1])
            hid = a + pltpu.roll(b, nc - 1, axis=0)
            act = jax.nn.gelu(hid).astype(bf)
            if kv == 0:
                kc_ref[0, hk] = _dot(act, w2k_ref[...]).astype(bf)
            else:
                vc_ref[0, hk] = _dot_nt(w2vt_ref[...], act).astype(bf)


def _compress(kvc, pos, w1, w2k, w2vt):
    B, _, nc, width = kvc.shape
    return pl.pallas_call(
        _compress_kernel, grid=(B,),
        in_specs=[pl.BlockSpec((1, 2 * NSA_KV_HEADS, nc, width), lambda b: (b, 0, 0, 0)),
                  _const_spec(pos.shape), _const_spec(w1.shape), _const_spec(w2k.shape),
                  _const_spec(w2vt.shape)],
        out_specs=(pl.BlockSpec((1, NSA_KV_HEADS, nc, LANE), lambda b: (b, 0, 0, 0)),
                   pl.BlockSpec((1, NSA_KV_HEADS, NSA_DK, nc), lambda b: (b, 0, 0, 0))),
        out_shape=(jax.ShapeDtypeStruct((B, NSA_KV_HEADS, nc, LANE), jnp.bfloat16),
                   jax.ShapeDtypeStruct((B, NSA_KV_HEADS, NSA_DK, nc), jnp.bfloat16)),
        compiler_params=_params(1), name="compress",
    )(kvc, pos, w1, w2k, w2vt)


def _t5_lookup(dist, tab_ref, head, thr):
    val = jnp.full(dist.shape, tab_ref[0, head], jnp.float32)
    for b in range(1, T5_BUCKETS):
        val = jnp.where(dist >= thr[b], tab_ref[b, head], val)
    return val - tab_ref[T5_BUCKETS - 1, head]


def _bias_cmp_kernel(tab_ref, out_ref, *, thr, tq, n_cmp):
    hk = pl.program_id(0)
    t = pl.program_id(1)
    nc = out_ref.shape[2]
    n = lax.broadcasted_iota(jnp.int32, (nc, tq), 0)
    i = lax.broadcasted_iota(jnp.int32, (nc, tq), 1)
    dist = t * tq + i - (n * CMP_STRIDE + CMP_LEN - 1)
    ok = jnp.logical_and(dist >= 0, n < n_cmp)
    for g in range(NSA_GROUP):
        val = _t5_lookup(dist, tab_ref, hk * NSA_GROUP + g, thr)
        out_ref[0, 0, :, g * tq:(g + 1) * tq] = jnp.where(ok, val, NEG)


def _bias_tile_kernel(tab_ref, out_ref, *, thr, tq):
    hk = pl.program_id(0)
    tk = out_ref.shape[2]
    j = lax.broadcasted_iota(jnp.int32, (tk, tq), 0)
    i = lax.broadcasted_iota(jnp.int32, (tk, tq), 1)
    for g in range(NSA_GROUP):
        head = hk * NSA_GROUP + g
        cols = slice(g * tq, (g + 1) * tq)
        d0 = i - j
        out_ref[0, 0, :, cols] = jnp.where(d0 >= 0, _t5_lookup(d0, tab_ref, head, thr), NEG)
        out_ref[0, 1, :, cols] = _t5_lookup(d0 + tk, tab_ref, head, thr)
        out_ref[0, 2, :, cols] = jnp.where(j > i, 0.0, NEG)
        out_ref[0, 3, :, cols] = jnp.full((tk, tq), NEG, jnp.float32)


def _bias_tables(t5_table, S):
    thr = _t5_thresholds(S)
    tq, tk = NSA_TQ, NSA_TK
    assert tq == tk and WINDOW == 2 * tk and tk >= T5_MAX_DIST
    nq = S // tq
    nc = S // CMP_STRIDE
    n_cmp = (S - CMP_LEN) // CMP_STRIDE + 1
    m = NSA_GROUP * tq
    smem = pl.BlockSpec(memory_space=pltpu.SMEM)
    bias_c = pl.pallas_call(
        functools.partial(_bias_cmp_kernel, thr=thr, tq=tq, n_cmp=n_cmp),
        grid=(NSA_KV_HEADS, nq), in_specs=[smem],
        out_specs=pl.BlockSpec((1, 1, nc, m), lambda h, t: (h, t, 0, 0)),
        out_shape=jax.ShapeDtypeStruct((NSA_KV_HEADS, nq, nc, m), jnp.float32),
        compiler_params=_params(2), name="bias_cmp",
    )(t5_table)
    tz = pl.pallas_call(
        functools.partial(_bias_tile_kernel, thr=thr, tq=tq),
        grid=(NSA_KV_HEADS,), in_specs=[smem],
        out_specs=pl.BlockSpec((1, 4, tk, m), lambda h: (h, 0, 0, 0)),
        out_shape=jax.ShapeDtypeStruct((NSA_KV_HEADS, 4, tk, m), jnp.float32),
        compiler_params=_params(1), name="bias_tile",
    )(t5_table)
    return bias_c, tz


def _flash_chains(chains):
    def scores(chain):
        k, q_t, _, bias_t = chain[:4]
        s = _dot(k, q_t)
        return s if bias_t is None else s + bias_t

    pending = [scores(c) for c in chains[:FLASH_LOOKAHEAD]]
    for i, chain in enumerate(chains):
        s = pending.pop(0)
        if i + FLASH_LOOKAHEAD < len(chains):
            pending.append(scores(chains[i + FLASH_LOOKAHEAD]))
        v_t, m_ref, l_ref, acc_ref = chain[2], chain[4], chain[5], chain[6]
        m_prev = m_ref[...]
        m_new = jnp.maximum(m_prev, jnp.max(s, axis=0, keepdims=True))
        alpha = jnp.exp(m_prev - m_new)
        p = jnp.exp(s - m_new)
        l_ref[...] = alpha * l_ref[...] + jnp.sum(p, axis=0, keepdims=True)
        acc_ref[...] = alpha * acc_ref[...] + _dot(v_t, p.astype(jnp.bfloat16))
        m_ref[...] = m_new


def _col_chains(k, q_ref, v_t, bias_ref, m_ref, l_ref, acc_ref, ncol, width):
    chains = []
    for c0 in range(0, ncol, width):
        cols = slice(c0, c0 + width)
        chains.append((k, q_ref[:, cols], v_t, None if bias_ref is None else bias_ref[:, cols],
                       m_ref.at[:, cols], l_ref.at[:, cols], acc_ref.at[:, cols]))
    return chains


def _flash_init(m_ref, l_ref, acc_ref):
    m_ref[...] = jnp.full(m_ref.shape, -jnp.inf, jnp.float32)
    l_ref[...] = jnp.zeros(l_ref.shape, jnp.float32)
    acc_ref[...] = jnp.zeros(acc_ref.shape, jnp.float32)


def _mla_kernel(q_ref, k_ref, v_ref, o_ref, m_ref, l_ref, acc_ref, *, tq, tk, cw):
    S = k_ref.shape[2]
    nqt = S // tq
    kk = lax.broadcasted_iota(jnp.int32, (tk, cw), 0)
    qq = lax.broadcasted_iota(jnp.int32, (tk, cw), 1)

    def q_tile(qt, carry):
        q0 = pl.multiple_of(qt * tq, tq)
        for e in range(2):
            _flash_init(m_ref.at[e], l_ref.at[e], acc_ref.at[e])

        def step(k0, key_off):
            chains = []
            for e in range(2):
                k = k_ref[0, e, pl.ds(k0, tk), :]
                v_t = v_ref[0, e, :, pl.ds(k0, tk)]
                for c0 in range(0, tq, cw):
                    bias = None
                    if key_off is not None:
                        if key_off > c0 + cw - 1:
                            continue
                        if key_off + tk - 1 > c0:
                            bias = jnp.where(kk + key_off <= qq + c0, 0.0, NEG)
                    cols = slice(c0, c0 + cw)
                    chains.append((k, q_ref[0, e, :, pl.ds(pl.multiple_of(q0 + c0, cw), cw)], v_t, bias,
                                   m_ref.at[e, :, cols], l_ref.at[e, :, cols], acc_ref.at[e, :, cols]))
            return chains

        per = tq // tk

        def far(c, carry2):
            chains = []
            for j in range(per):
                chains += step(pl.multiple_of(c * tq + j * tk, tk), None)
            _flash_chains(chains)
            return carry2

        lax.fori_loop(0, qt, far, 0)
        chains = []
        for j in range(per):
            chains += step(pl.multiple_of(q0 + j * tk, tk), j * tk)
        _flash_chains(chains)
        o_t = jnp.concatenate([acc_ref[e] / l_ref[e] for e in range(2)], axis=0)
        o_ref[0, pl.ds(q0, tq), :] = jnp.transpose(o_t)
        return carry

    lax.fori_loop(0, nqt, q_tile, 0)


def _mla(q_t, k, v_t):
    B, H, S, _ = k.shape
    tq = min(MLA_TQ, S)
    tk = min(MLA_TK, tq)
    cw = min(MLA_CW, tq)
    return pl.pallas_call(
        functools.partial(_mla_kernel, tq=tq, tk=tk, cw=cw),
        grid=(B, H // 2),
        in_specs=[pl.BlockSpec((1, 2, LANE, S), lambda b, p: (b, p, 0, 0)),
                  pl.BlockSpec((1, 2, S, LANE), lambda b, p: (b, p, 0, 0)),
                  pl.BlockSpec((1, 2, MLA_V, S), lambda b, p: (b, p, 0, 0))],
        out_specs=pl.BlockSpec((1, S, 2 * MLA_V), lambda b, p: (b, 0, p)),
        out_shape=jax.ShapeDtypeStruct((B, S, H * MLA_V), jnp.float32),
        scratch_shapes=[pltpu.VMEM((2, 1, tq), jnp.float32), pltpu.VMEM((2, 1, tq), jnp.float32),
                        pltpu.VMEM((2, MLA_V, tq), jnp.float32)],
        compiler_params=_params(2), name="mla",
    )(q_t, k, v_t)


def _nsa_kernel(q_ref, ks_ref, vs_ref, kw_ref, vw_ref, kc_ref, vc_ref, gate_ref, bc_ref, tz_ref, ovt_ref,
                o_ref, qaug_ref, qpad_ref, m_ref, l_ref, acc_ref, mw_ref, lw_ref, accw_ref, oc_ref, *, tq, tk):
    t = pl.program_id(2)
    G = NSA_GROUP
    ncol = G * tq
    bf = jnp.bfloat16
    q0 = t * tq
    for g in range(G):
        qpad_ref[0:NSA_DK, g * tq:(g + 1) * tq] = q_ref[0, g]
        qaug_ref[0:NSA_DK, g * tq:(g + 1) * tq] = q_ref[0, g]
    qpad_ref[NSA_DK:, :] = jnp.zeros((SLC_ROWS, ncol), bf)

    def tile_chains(k_ref, v_ref, qx_ref, c, bias_idx, stats):
        k0 = pl.multiple_of(c * tk, tk)
        bias_ref = None if bias_idx is None else tz_ref.at[0, bias_idx]
        return _col_chains(k_ref[0, 0, pl.ds(k0, tk), :], qx_ref, v_ref[0, 0, :, pl.ds(k0, tk)],
                           bias_ref, *stats, ncol, tq)

    prev1 = jnp.maximum(t - 1, 0)
    bias1 = jnp.where(t >= 1, 1, 3)

    win = (mw_ref, lw_ref, accw_ref)
    _flash_init(*win)
    _flash_chains(tile_chains(kw_ref, vw_ref, qpad_ref, jnp.maximum(t - 2, 0), jnp.where(t >= 2, 2, 3), win)
                  + tile_chains(kw_ref, vw_ref, qpad_ref, prev1, bias1, win)
                  + tile_chains(kw_ref, vw_ref, qpad_ref, t, 0, win))

    s = _dot(kc_ref[0, 0], qpad_ref[...]) + bc_ref[0, 0]
    mx = jnp.max(s, axis=0, keepdims=True)
    p = jnp.exp(s - mx)
    lsum = jnp.sum(p, axis=0, keepdims=True)
    ci = lax.broadcasted_iota(jnp.int32, (1, ncol), 1)
    col_pos = q0 + ci % tq
    r = jnp.where(col_pos >= CMP_LEN - 1, 1.0 / lsum, 0.0)
    pcb = (p * r).astype(bf)
    oc_ref[...] = _dot(vc_ref[0, 0], pcb)

    imp = _dot(ovt_ref[...], pcb[:, 0:tq])
    for g in range(1, G):
        imp = imp + _dot(ovt_ref[...], pcb[:, g * tq:(g + 1) * tq])
    nb = imp.shape[0]
    jb = lax.broadcasted_iota(jnp.int32, (nb, tq), 0)
    cur = (q0 + lax.broadcasted_iota(jnp.int32, (nb, tq), 1)) // SLC_LEN
    forced = jnp.logical_or(jb == 0, jnp.logical_or(jb == cur, jb == cur - 1))
    imp = jnp.where(forced, FORCE, jnp.where(jb <= cur, imp, NEG))
    sub = SUBLANE
    slabs = [imp[r0 * sub:(r0 + 1) * sub] for r0 in range(nb // sub)]
    ranks = [jnp.zeros((sub, tq), jnp.int32) for _ in slabs]
    jsub = lax.broadcasted_iota(jnp.int32, (sub, tq), 0)
    for jp in range(nb):
        rowv = imp[jp:jp + 1, :]
        for r0, slab in enumerate(slabs):
            lo = r0 * sub
            if lo > jp:
                one = jnp.where(rowv >= slab, 1, 0)
            elif lo + sub - 1 <= jp:
                one = jnp.where(rowv > slab, 1, 0)
            else:
                one = jnp.where(jsub + lo > jp, jnp.where(rowv >= slab, 1, 0), jnp.where(rowv > slab, 1, 0))
            ranks[r0] = ranks[r0] + one
    rank = jnp.concatenate(ranks, axis=0)
    selb = jnp.where(rank < SLC_TOPN, 0.0, NEG).astype(bf)
    for g in range(G):
        qaug_ref[NSA_DK:, g * tq:(g + 1) * tq] = selb

    slc = (m_ref, l_ref, acc_ref)
    _flash_init(*slc)
    n_far = jnp.maximum(t - 1, 0)

    def far_pair(c, carry):
        _flash_chains(tile_chains(ks_ref, vs_ref, qaug_ref, 2 * c, None, slc)
                      + tile_chains(ks_ref, vs_ref, qaug_ref, 2 * c + 1, None, slc))
        return carry

    def far_last(c, carry):
        _flash_chains(tile_chains(ks_ref, vs_ref, qaug_ref, n_far - 1, None, slc))
        return carry

    lax.fori_loop(0, n_far // 2, far_pair, 0)
    lax.fori_loop(0, n_far % 2, far_last, 0)
    _flash_chains(tile_chains(ks_ref, vs_ref, qaug_ref, prev1, bias1, slc)
                  + tile_chains(ks_ref, vs_ref, qaug_ref, t, 0, slc))
    o_s = acc_ref[...] / l_ref[...]
    o_w = accw_ref[...] / lw_ref[...]
    o_c = oc_ref[...]

    gate = gate_ref[0, 0]
    comb = []
    for g in range(G):
        cols = slice(g * tq, (g + 1) * tq)
        gr = lambda br: gate[g * 3 + br:g * 3 + br + 1, :]
        comb.append(gr(0) * o_c[:, cols] + gr(1) * o_s[:, cols] + gr(2) * o_w[:, cols])
    o_ref[0] = jnp.transpose(jnp.concatenate(comb, axis=0))


def _nsa(q_t, ks, vs_t, kw, vw_t, kc, vc_t, gate_t, bias_c, tz, ovt):
    B, _, S, _ = ks.shape
    tq, tk = NSA_TQ, NSA_TK
    nq = S // tq
    nc = kc.shape[2]
    m = NSA_GROUP * tq
    tok = lambda: pl.BlockSpec((1, 1, S, LANE), lambda b, h, t: (b, h, 0, 0))
    feat = lambda: pl.BlockSpec((1, 1, NSA_DK, S), lambda b, h, t: (b, h, 0, 0))
    return pl.pallas_call(
        functools.partial(_nsa_kernel, tq=tq, tk=tk),
        grid=(B, NSA_KV_HEADS, nq),
        in_specs=[pl.BlockSpec((1, NSA_GROUP, NSA_DK, tq), lambda b, h, t: (b, h, 0, t)),
                  tok(), feat(), tok(), feat(),
                  pl.BlockSpec((1, 1, nc, LANE), lambda b, h, t: (b, h, 0, 0)),
                  pl.BlockSpec((1, 1, NSA_DK, nc), lambda b, h, t: (b, h, 0, 0)),
                  pl.BlockSpec((1, 1, GATE_ROWS, tq), lambda b, h, t: (b, h, 0, t)),
                  pl.BlockSpec((1, 1, nc, m), lambda b, h, t: (h, t, 0, 0)),
                  pl.BlockSpec((1, 4, tk, m), lambda b, h, t: (h, 0, 0, 0)),
                  _const_spec(ovt.shape)],
        out_specs=pl.BlockSpec((1, tq, NSA_GROUP * NSA_DK), lambda b, h, t: (b, t, h)),
        out_shape=jax.ShapeDtypeStruct((B, S, NSA_HEADS * NSA_DK), jnp.float32),
        scratch_shapes=[pltpu.VMEM((NSA_DK + SLC_ROWS, m), jnp.bfloat16),
                        pltpu.VMEM((NSA_DK + SLC_ROWS, m), jnp.bfloat16),
                        pltpu.VMEM((1, m), jnp.float32), pltpu.VMEM((1, m), jnp.float32),
                        pltpu.VMEM((NSA_DK, m), jnp.float32),
                        pltpu.VMEM((1, m), jnp.float32), pltpu.VMEM((1, m), jnp.float32),
                        pltpu.VMEM((NSA_DK, m), jnp.float32),
                        pltpu.VMEM((NSA_DK, m), jnp.float32)],
        compiler_params=_params(3), name="nsa",
    )(q_t, ks, vs_t, kw, vw_t, kc, vc_t, gate_t, bias_c, tz, ovt)


def _post_kernel(x_ref, om_ref, on_ref, gm_ref, gn_ref, wo_ref, gf_ref, wg_ref, wu_ref, wd_ref,
                 gfin_ref, o_ref):
    half = om_ref.shape[2]
    mix_m = _rms(om_ref[0], gm_ref[...]).astype(jnp.bfloat16)
    mix_n = _rms(on_ref[0], gn_ref[...]).astype(jnp.bfloat16)
    h = x_ref[0] + _dot(mix_m, wo_ref[0:half, :]) + _dot(mix_n, wo_ref[half:2 * half, :])
    f = _rms(h, gf_ref[...]).astype(jnp.bfloat16)
    a = _dot(f, wg_ref[...])
    act = (a * jax.nn.sigmoid(a) * _dot(f, wu_ref[...])).astype(jnp.bfloat16)
    h = h + _dot(act, wd_ref[...])
    o_ref[0] = _rms(h, gfin_ref[...])


def _post(x, o_mla, o_nsa, gm, gn, w_out, gf, wg, wu, wd, gfin):
    B, S, D = x.shape
    tm = min(POST_TM, S)
    tok = lambda w: pl.BlockSpec((1, tm, w), lambda b, i: (b, i, 0))
    consts = (gm, gn, w_out, gf, wg, wu, wd, gfin)
    return pl.pallas_call(
        _post_kernel, grid=(B, S // tm),
        in_specs=[tok(D), tok(o_mla.shape[2]), tok(o_nsa.shape[2])] + [_const_spec(c.shape) for c in consts],
        out_specs=tok(D), out_shape=jax.ShapeDtypeStruct((B, S, D), jnp.float32),
        compiler_params=_params(2), name="post",
    )(x, o_mla, o_nsa, *consts)


def _rope_tables(S):
    pos = jnp.arange(S, dtype=jnp.float32)
    inv = ROPE_THETA ** (-jnp.arange(0, MLA_ROPE, 2, dtype=jnp.float32) / MLA_ROPE)
    ang = pos[:, None] * inv[None, :]
    cos, sin = jnp.cos(ang), jnp.sin(ang)
    cos2 = jnp.concatenate([cos, cos], axis=-1)
    sin2 = jnp.concatenate([sin, sin], axis=-1)
    scale = (MLA_NOPE + MLA_ROPE) ** -0.5
    zq = jnp.zeros((S, LANE - MLA_NOPE - MLA_ROPE), jnp.float32)
    cosq = jnp.concatenate([jnp.ones((S, MLA_NOPE), jnp.float32), cos2, zq], axis=-1) * scale
    sinq = jnp.concatenate([jnp.zeros((S, MLA_NOPE), jnp.float32), sin2, zq], axis=-1) * scale
    zk = jnp.zeros((S, LANE - MLA_ROPE), jnp.float32)
    cosk = jnp.concatenate([cos2, zk], axis=-1)
    sink = jnp.concatenate([sin2, zk], axis=-1)
    return cosq.T, sinq.T, cosk, sink


def _overlap_t(S):
    n_cmp = (S - CMP_LEN) // CMP_STRIDE + 1
    n_slc = S // SLC_LEN
    assert n_slc <= SLC_ROWS
    cs = np.arange(n_cmp) * CMP_STRIDE
    ss = np.arange(n_slc) * SLC_LEN
    ov = np.maximum(0, np.minimum(cs[:, None] + CMP_LEN, ss[None, :] + SLC_LEN)
                    - np.maximum(cs[:, None], ss[None, :])).astype(np.float32) / CMP_STRIDE
    out = np.zeros((SLC_ROWS, S // CMP_STRIDE), np.float32)
    out[:n_slc, :n_cmp] = ov.T
    return jnp.asarray(out, jnp.bfloat16)


def kernel(x, norm_mix_g, w_in, mla_q_norm_g, mla_w_uq, mla_kv_norm_g, mla_w_ukv, nsa_cmp_pos_k, nsa_cmp_w1_k, nsa_cmp_w2_k, nsa_cmp_pos_v, nsa_cmp_w1_v, nsa_cmp_w2_v, t5_table, out_norm_mla_g, out_norm_nsa_g, w_out, norm_ffn_g, w_gate, w_up, w_down, final_norm_g):
    B, S, D = x.shape
    assert w_in.shape[0] == 1
    assert D == D_MODEL and S % NSA_TQ == 0 and S % CMP_STRIDE == 0
    bf = jnp.bfloat16
    l = 0
    cosq_t, sinq_t, cosk, sink = _rope_tables(S)
    bias_c, tz = _bias_tables(t5_table, S)
    ovt = _overlap_t(S)
    epe = _kpe_placement()
    nc = S // CMP_STRIDE
    cw = CMP_STRIDE * NSA_DK
    row = lambda v: v.reshape(1, -1)

    w_row = _w_in_row_layout().apply(w_in[l])
    w_col = _w_in_col_layout().apply(w_in[l]).T
    w_q2 = _w_uq_layout().apply(mla_w_uq[l]).T
    w_k = _w_uk_layout().apply(mla_w_ukv[l])
    w_v = _w_uv_layout().apply(mla_w_ukv[l]).T
    (q_mla, k_mla, v_mla, q_nsa, kv_cmp, k_slc, v_slc, k_win, v_win, gates) = _proj(
        x, row(norm_mix_g[l]), w_row, w_col, row(mla_q_norm_g[l]), w_q2, row(mla_kv_norm_g[l]), w_k, w_v, epe,
        cosq_t, sinq_t, cosk, sink)

    kvc = kv_cmp.reshape(B, S, 2 * NSA_KV_HEADS, NSA_DK).transpose(0, 2, 1, 3).reshape(
        B, 2 * NSA_KV_HEADS, nc, cw)
    pos = jnp.stack([nsa_cmp_pos_k[l].reshape(2, 1, cw), nsa_cmp_pos_v[l].reshape(2, 1, cw)])
    w1 = jnp.stack([nsa_cmp_w1_k[l].reshape(2, cw, CMP_HIDDEN),
                    nsa_cmp_w1_v[l].reshape(2, cw, CMP_HIDDEN)]).astype(bf)
    w2k = jnp.concatenate([nsa_cmp_w2_k[l], jnp.zeros((CMP_HIDDEN, LANE - NSA_DK), jnp.float32)],
                          axis=1).astype(bf)
    w2vt = nsa_cmp_w2_v[l].T.astype(bf)
    k_cmp, v_cmp = _compress(kvc, pos, w1, w2k, w2vt)

    o_mla = _mla(q_mla, k_mla, v_mla)
    o_nsa = _nsa(q_nsa, k_slc, v_slc, k_win, v_win, k_cmp, v_cmp, gates, bias_c, tz, ovt)

    return _post(x, o_mla, o_nsa, row(out_norm_mla_g[l]), row(out_norm_nsa_g[l]), w_out[l].astype(bf),
                 row(norm_ffn_g[l]), w_gate[l].astype(bf), w_up[l].astype(bf), w_down[l].astype(bf),
                 row(final_norm_g))
```

```python
import functools
import math

import numpy as np
import jax
import jax.numpy as jnp
from jax import lax
from jax.experimental import pallas as pl
from jax.experimental.pallas import tpu as pltpu

D_MODEL = 1024
MLA_HEADS = 8
MLA_NOPE = 64
MLA_ROPE = 32
MLA_V = 64
MLA_Q_LORA = 256
MLA_KV_LORA = 128
ROPE_THETA = 10000.0
NSA_HEADS = 8
NSA_KV_HEADS = 2
NSA_GROUP = 4
NSA_DK = 64
CMP_LEN = 32
CMP_STRIDE = 16
CMP_HIDDEN = 128
SLC_LEN = 64
SLC_TOPN = 16
WINDOW = 512
T5_BUCKETS = 32
T5_MAX_DIST = 128
D_FF = 2816
EPS = 1e-6
NEG = -1e30
FORCE = 1e30

LANE = 128
SUBLANE = 8
SLC_ROWS = 64
GATE_ROWS = 16
ONES_ROWS = 16
LOG2E = math.log2(math.e)

PROJ_TM = 256
MLA_TQ = 512
MLA_TK = 256
MLA_CW = 256
FLASH_LOOKAHEAD = 4
NSA_TQ = 256
NSA_TK = 256
POST_TM = 512
VMEM_LIMIT = 56 * 1024 * 1024

C_CQ = 0
C_CKV = 256
C_MISC = 384
C_KVCMP = 512
C_KSLC = 768
C_KWIN = 1024
D_ROW = 1280
R_QNSA = 0
R_VSLC = 512
R_VWIN = 640
R_GATE = 768
D_COL = 800


def _dot(a, b):
    return jnp.dot(a, b, preferred_element_type=jnp.float32)


def _dot_nt(a, b):
    return lax.dot_general(a, b, (((1,), (1,)), ((), ())), preferred_element_type=jnp.float32)


def _rms(x, g):
    return x * lax.rsqrt(jnp.mean(x * x, axis=-1, keepdims=True) + EPS) * g


def _const_spec(shape):
    nd = len(shape)
    return pl.BlockSpec(shape, lambda *_: (0,) * nd, pipeline_mode=pl.Buffered(1))


def _params(n_axes):
    return pltpu.CompilerParams(dimension_semantics=("arbitrary",) * n_axes,
                                vmem_limit_bytes=VMEM_LIMIT)


def _in_offsets():
    o_krope = MLA_Q_LORA + MLA_KV_LORA
    o_qnsa = o_krope + MLA_ROPE
    o_cmp = o_qnsa + NSA_HEADS * NSA_DK
    o_slc = o_cmp + 2 * NSA_KV_HEADS * NSA_DK
    o_win = o_slc + 2 * NSA_KV_HEADS * NSA_DK
    o_gate = o_win + 2 * NSA_KV_HEADS * NSA_DK
    return o_krope, o_qnsa, o_cmp, o_slc, o_win, o_gate


class _Layout:
    def __init__(self, n):
        self.src = np.zeros((n,), np.int32)
        self.mul = np.zeros((n,), np.float32)

    def put(self, dst, s, n, m=1.0):
        self.src[dst:dst + n] = np.arange(s, s + n)
        self.mul[dst:dst + n] = m

    def apply(self, w):
        return (w[:, self.src] * self.mul[None, :]).astype(jnp.bfloat16)


def _w_in_row_layout():
    o_krope, _, o_cmp, o_slc, o_win, _ = _in_offsets()
    lay = _Layout(D_ROW)
    lay.put(C_CQ, 0, MLA_Q_LORA)
    lay.put(C_CKV, MLA_Q_LORA, MLA_KV_LORA)
    half = MLA_ROPE // 2
    lay.put(C_MISC, o_krope, MLA_ROPE)
    lay.put(C_MISC + MLA_ROPE, o_krope + half, half, -1.0)
    lay.put(C_MISC + MLA_ROPE + half, o_krope, half, 1.0)
    lay.put(C_KVCMP, o_cmp, 2 * NSA_KV_HEADS * NSA_DK)
    for hk in range(NSA_KV_HEADS):
        lay.put(C_KSLC + hk * LANE, o_slc + hk * NSA_DK, NSA_DK)
        lay.put(C_KWIN + hk * LANE, o_win + hk * NSA_DK, NSA_DK)
    return lay


def _w_in_col_layout():
    _, o_qnsa, _, o_slc, o_win, o_gate = _in_offsets()
    lay = _Layout(D_COL)
    lay.put(R_QNSA, o_qnsa, NSA_HEADS * NSA_DK, NSA_DK ** -0.5)
    lay.put(R_VSLC, o_slc + NSA_KV_HEADS * NSA_DK, NSA_KV_HEADS * NSA_DK)
    lay.put(R_VWIN, o_win + NSA_KV_HEADS * NSA_DK, NSA_KV_HEADS * NSA_DK)
    for hk in range(NSA_KV_HEADS):
        lay.put(R_GATE + hk * GATE_ROWS, o_gate + hk * NSA_GROUP * 3, NSA_GROUP * 3)
    return lay


def _w_uq_layout():
    lay = _Layout(2 * MLA_HEADS * LANE)
    dq = MLA_NOPE + MLA_ROPE
    half = MLA_ROPE // 2
    for h in range(MLA_HEADS):
        lay.put(h * LANE, h * dq, dq)
        r = MLA_HEADS * LANE + h * LANE + MLA_NOPE
        pe = h * dq + MLA_NOPE
        lay.put(r, pe + half, half, -1.0)
        lay.put(r + half, pe, half, 1.0)
    return lay


def _w_uk_layout():
    lay = _Layout(MLA_HEADS * LANE)
    per = MLA_NOPE + MLA_V
    for h in range(MLA_HEADS):
        lay.put(h * LANE, h * per, MLA_NOPE)
    return lay


def _w_uv_layout():
    lay = _Layout(MLA_HEADS * MLA_V)
    per = MLA_NOPE + MLA_V
    for h in range(MLA_HEADS):
        lay.put(h * MLA_V, h * per + MLA_NOPE, MLA_V)
    return lay


def _kpe_placement():
    e = np.zeros((LANE, MLA_HEADS * LANE), np.float32)
    for h in range(MLA_HEADS):
        for r in range(MLA_ROPE):
            e[r, h * LANE + MLA_NOPE + r] = 1.0
    return jnp.asarray(e, jnp.bfloat16)


def _t5_thresholds(max_dist):
    n = np.arange(max_dist, dtype=np.int64)
    max_exact = T5_BUCKETS // 2
    nf = np.maximum(n, 1).astype(np.float32)
    val = (np.log(nf / np.float32(max_exact)) / np.float32(math.log(T5_MAX_DIST / max_exact))
           * np.float32(T5_BUCKETS - max_exact))
    large = np.minimum(max_exact + val.astype(np.int32), T5_BUCKETS - 1)
    bucket = np.where(n < max_exact, n, large)
    assert np.all(np.diff(bucket) >= 0)
    frac = np.abs(val[max_exact + 1:T5_MAX_DIST] - np.round(val[max_exact + 1:T5_MAX_DIST]))
    assert frac.min() > 1e-3
    return [int(np.argmax(bucket >= b)) for b in range(T5_BUCKETS)]


def _proj_kernel(x_ref, gmix_ref, wrow_ref, wcol_ref, gq_ref, wq_ref, gkv_ref, wk_ref, wv_ref, epe_ref,
                 cq_ref, sq_ref, ck_ref, sk_ref,
                 qm_ref, km_ref, vm_ref, qn_ref, kvc_ref, ks_ref, vs_ref, kw_ref, vw_ref, gate_ref):
    tm = x_ref.shape[1]
    bf = jnp.bfloat16
    xn = _rms(x_ref[0], gmix_ref[...]).astype(bf)
    u = _dot(xn, wrow_ref[...])
    ut = _dot_nt(wcol_ref[...], xn)

    cqn = _rms(u[:, C_CQ:C_CQ + MLA_Q_LORA], gq_ref[...]).astype(bf)
    qq = _dot_nt(wq_ref[...], cqn)
    cosq = cq_ref[...]
    sinq = sq_ref[...]
    hw = MLA_HEADS * LANE
    for h in range(MLA_HEADS):
        qh = qq[h * LANE:(h + 1) * LANE] * cosq + qq[hw + h * LANE:hw + (h + 1) * LANE] * sinq
        qm_ref[0, h] = qh.astype(bf)

    ckvn = _rms(u[:, C_CKV:C_CKV + MLA_KV_LORA], gkv_ref[...]).astype(bf)
    kk = _dot(ckvn, wk_ref[...])
    misc = u[:, C_MISC:C_MISC + LANE]
    kpe = misc * ck_ref[...] + pltpu.roll(misc, LANE - MLA_ROPE, axis=1) * sk_ref[...]
    kpe_all = _dot(kpe.astype(bf), epe_ref[...])
    for h in range(MLA_HEADS):
        km_ref[0, h] = (kk[:, h * LANE:(h + 1) * LANE] + kpe_all[:, h * LANE:(h + 1) * LANE]).astype(bf)
    vv = _dot_nt(wv_ref[...], ckvn)
    for h in range(MLA_HEADS):
        vm_ref[0, h] = vv[h * MLA_V:(h + 1) * MLA_V].astype(bf)

    for h in range(NSA_HEADS):
        qn_ref[0, h] = (ut[R_QNSA + h * NSA_DK:R_QNSA + (h + 1) * NSA_DK] * LOG2E).astype(bf)
    kvc_ref[0] = u[:, C_KVCMP:C_KVCMP + 2 * NSA_KV_HEADS * NSA_DK]
    pos = pl.program_id(1) * tm + lax.broadcasted_iota(jnp.int32, (tm, LANE), 0)
    lane = lax.broadcasted_iota(jnp.int32, (tm, LANE), 1)
    onehot = jnp.where(lane - (LANE - SLC_ROWS) == pos // SLC_LEN, 1.0, 0.0)
    for hk in range(NSA_KV_HEADS):
        ks_ref[0, hk] = (u[:, C_KSLC + hk * LANE:C_KSLC + (hk + 1) * LANE] + onehot).astype(bf)
        kw_ref[0, hk] = u[:, C_KWIN + hk * LANE:C_KWIN + (hk + 1) * LANE].astype(bf)
        vs_ref[0, hk] = ut[R_VSLC + hk * NSA_DK:R_VSLC + (hk + 1) * NSA_DK].astype(bf)
        vw_ref[0, hk] = ut[R_VWIN + hk * NSA_DK:R_VWIN + (hk + 1) * NSA_DK].astype(bf)
        gate_ref[0, hk] = jax.nn.sigmoid(ut[R_GATE + hk * GATE_ROWS:R_GATE + (hk + 1) * GATE_ROWS])


def _proj(x, gmix, w_row, w_col, gq, w_q2, gkv, w_k, w_v, epe, cosq_t, sinq_t, cosk, sink):
    B, S, D = x.shape
    tm = min(PROJ_TM, S)
    bf = jnp.bfloat16
    grid = (B, S // tm)
    tok_major = lambda n: pl.BlockSpec((1, n, tm, LANE), lambda b, i: (b, 0, i, 0))
    feat_major = lambda n, d: pl.BlockSpec((1, n, d, tm), lambda b, i: (b, 0, 0, i))
    out_shape = (
        jax.ShapeDtypeStruct((B, MLA_HEADS, LANE, S), bf),
        jax.ShapeDtypeStruct((B, MLA_HEADS, S, LANE), bf),
        jax.ShapeDtypeStruct((B, MLA_HEADS, MLA_V, S), bf),
        jax.ShapeDtypeStruct((B, NSA_HEADS, NSA_DK, S), bf),
        jax.ShapeDtypeStruct((B, S, 2 * NSA_KV_HEADS * NSA_DK), jnp.float32),
        jax.ShapeDtypeStruct((B, NSA_KV_HEADS, S, LANE), bf),
        jax.ShapeDtypeStruct((B, NSA_KV_HEADS, NSA_DK, S), bf),
        jax.ShapeDtypeStruct((B, NSA_KV_HEADS, S, LANE), bf),
        jax.ShapeDtypeStruct((B, NSA_KV_HEADS, NSA_DK, S), bf),
        jax.ShapeDtypeStruct((B, NSA_KV_HEADS, GATE_ROWS, S), jnp.float32),
    )
    out_specs = (
        feat_major(MLA_HEADS, LANE), tok_major(MLA_HEADS), feat_major(MLA_HEADS, MLA_V),
        feat_major(NSA_HEADS, NSA_DK),
        pl.BlockSpec((1, tm, 2 * NSA_KV_HEADS * NSA_DK), lambda b, i: (b, i, 0)),
        tok_major(NSA_KV_HEADS), feat_major(NSA_KV_HEADS, NSA_DK),
        tok_major(NSA_KV_HEADS), feat_major(NSA_KV_HEADS, NSA_DK),
        feat_major(NSA_KV_HEADS, GATE_ROWS),
    )
    consts = (gmix, w_row, w_col, gq, w_q2, gkv, w_k, w_v, epe)
    in_specs = ([pl.BlockSpec((1, tm, D), lambda b, i: (b, i, 0))] + [_const_spec(c.shape) for c in consts]
                + [pl.BlockSpec((LANE, tm), lambda b, i: (0, i)), pl.BlockSpec((LANE, tm), lambda b, i: (0, i)),
                   pl.BlockSpec((tm, LANE), lambda b, i: (i, 0)), pl.BlockSpec((tm, LANE), lambda b, i: (i, 0))])
    return pl.pallas_call(
        _proj_kernel, grid=grid, in_specs=in_specs, out_specs=out_specs, out_shape=out_shape,
        compiler_params=_params(2), name="proj",
    )(x, *consts, cosq_t, sinq_t, cosk, sink)


def _compress_kernel(c_ref, pos_ref, w1_ref, w2k_ref, w2vt_ref, kc_ref, vc_ref):
    nc = c_ref.shape[2]
    bf = jnp.bfloat16
    for kv in range(2):
        for hk in range(NSA_KV_HEADS):
            c = c_ref[0, kv * NSA_KV_HEADS + hk]
            a = _dot((c + pos_ref[kv, 0]).astype(bf), w1_ref[kv, 0])
            b = _dot((c + pos_ref[kv, 1]).astype(bf), w1_ref[kv, 1])
            hid = a + pltpu.roll(b, nc - 1, axis=0)
            act = jax.nn.gelu(hid).astype(bf)
            if kv == 0:
                kc_ref[0, hk] = _dot(act, w2k_ref[...]).astype(bf)
            else:
                vc_ref[0, hk] = _dot_nt(w2vt_ref[...], act).astype(bf)


def _compress(kvc, pos, w1, w2k, w2vt):
    B, _, nc, width = kvc.shape
    return pl.pallas_call(
        _compress_kernel, grid=(B,),
        in_specs=[pl.BlockSpec((1, 2 * NSA_KV_HEADS, nc, width), lambda b: (b, 0, 0, 0)),
                  _const_spec(pos.shape), _const_spec(w1.shape), _const_spec(w2k.shape),
                  _const_spec(w2vt.shape)],
        out_specs=(pl.BlockSpec((1, NSA_KV_HEADS, nc, LANE), lambda b: (b, 0, 0, 0)),
                   pl.BlockSpec((1, NSA_KV_HEADS, NSA_DK, nc), lambda b: (b, 0, 0, 0))),
        out_shape=(jax.ShapeDtypeStruct((B, NSA_KV_HEADS, nc, LANE), jnp.bfloat16),
                   jax.ShapeDtypeStruct((B, NSA_KV_HEADS, NSA_DK, nc), jnp.bfloat16)),
        compiler_params=_params(1), name="compress",
    )(kvc, pos, w1, w2k, w2vt)


def _t5_lookup(dist, tab_ref, head, thr):
    val = jnp.full(dist.shape, tab_ref[0, head], jnp.float32)
    for b in range(1, T5_BUCKETS):
        val = jnp.where(dist >= thr[b], tab_ref[b, head], val)
    return (val - tab_ref[T5_BUCKETS - 1, head]) * LOG2E


def _bias_cmp_kernel(tab_ref, out_ref, *, thr, tq, n_cmp):
    hk = pl.program_id(0)
    t = pl.program_id(1)
    nc = out_ref.shape[2]
    n = lax.broadcasted_iota(jnp.int32, (nc, tq), 0)
    i = lax.broadcasted_iota(jnp.int32, (nc, tq), 1)
    dist = t * tq + i - (n * CMP_STRIDE + CMP_LEN - 1)
    ok = jnp.logical_and(dist >= 0, n < n_cmp)
    for g in range(NSA_GROUP):
        val = _t5_lookup(dist, tab_ref, hk * NSA_GROUP + g, thr)
        out_ref[0, 0, :, g * tq:(g + 1) * tq] = jnp.where(ok, val, NEG)


def _bias_tile_kernel(tab_ref, out_ref, *, thr, tq):
    hk = pl.program_id(0)
    tk = out_ref.shape[2]
    j = lax.broadcasted_iota(jnp.int32, (tk, tq), 0)
    i = lax.broadcasted_iota(jnp.int32, (tk, tq), 1)
    for g in range(NSA_GROUP):
        head = hk * NSA_GROUP + g
        cols = slice(g * tq, (g + 1) * tq)
        d0 = i - j
        out_ref[0, 0, :, cols] = jnp.where(d0 >= 0, _t5_lookup(d0, tab_ref, head, thr), NEG)
        out_ref[0, 1, :, cols] = _t5_lookup(d0 + tk, tab_ref, head, thr)
        out_ref[0, 2, :, cols] = jnp.where(j > i, 0.0, NEG)
        out_ref[0, 3, :, cols] = jnp.full((tk, tq), NEG, jnp.float32)


def _bias_tables(t5_table, S):
    thr = _t5_thresholds(S)
    tq, tk = NSA_TQ, NSA_TK
    assert tq == tk and WINDOW == 2 * tk and tk >= T5_MAX_DIST
    nq = S // tq
    nc = S // CMP_STRIDE
    n_cmp = (S - CMP_LEN) // CMP_STRIDE + 1
    m = NSA_GROUP * tq
    smem = pl.BlockSpec(memory_space=pltpu.SMEM)
    bias_c = pl.pallas_call(
        functools.partial(_bias_cmp_kernel, thr=thr, tq=tq, n_cmp=n_cmp),
        grid=(NSA_KV_HEADS, nq), in_specs=[smem],
        out_specs=pl.BlockSpec((1, 1, nc, m), lambda h, t: (h, t, 0, 0)),
        out_shape=jax.ShapeDtypeStruct((NSA_KV_HEADS, nq, nc, m), jnp.float32),
        compiler_params=_params(2), name="bias_cmp",
    )(t5_table)
    tz = pl.pallas_call(
        functools.partial(_bias_tile_kernel, thr=thr, tq=tq),
        grid=(NSA_KV_HEADS,), in_specs=[smem],
        out_specs=pl.BlockSpec((1, 4, tk, m), lambda h: (h, 0, 0, 0)),
        out_shape=jax.ShapeDtypeStruct((NSA_KV_HEADS, 4, tk, m), jnp.float32),
        compiler_params=_params(1), name="bias_tile",
    )(t5_table)
    return bias_c, tz


def _flash_chains(chains):
    def scores(chain):
        k, q_t, _, bias_t = chain[:4]
        s = _dot(k, q_t)
        return s if bias_t is None else s + bias_t

    pending = [scores(c) for c in chains[:FLASH_LOOKAHEAD]]
    for i, chain in enumerate(chains):
        s = pending.pop(0)
        if i + FLASH_LOOKAHEAD < len(chains):
            pending.append(scores(chains[i + FLASH_LOOKAHEAD]))
        v_aug, m_ref, acc_ref = chain[2], chain[4], chain[5]
        m_prev = m_ref[...]
        m_new = jnp.maximum(m_prev, jnp.max(s, axis=0, keepdims=True))
        alpha = jnp.exp2(m_prev - m_new)
        p = jnp.exp2(s - m_new).astype(jnp.bfloat16)
        acc_ref[...] = alpha * acc_ref[...] + _dot(v_aug, p)
        m_ref[...] = m_new


def _with_ones(v_t):
    tk = v_t.shape[1]
    row = lax.broadcasted_iota(jnp.int32, (ONES_ROWS, tk), 0)
    return jnp.concatenate([v_t, jnp.where(row == 0, 1.0, 0.0).astype(v_t.dtype)], axis=0)


def _col_chains(k, q_ref, v_aug, bias_ref, m_ref, acc_ref, ncol, width):
    chains = []
    for c0 in range(0, ncol, width):
        cols = slice(c0, c0 + width)
        chains.append((k, q_ref[:, cols], v_aug, None if bias_ref is None else bias_ref[:, cols],
                       m_ref.at[:, cols], acc_ref.at[:, cols]))
    return chains


def _flash_init(m_ref, acc_ref):
    m_ref[...] = jnp.full(m_ref.shape, -jnp.inf, jnp.float32)
    acc_ref[...] = jnp.zeros(acc_ref.shape, jnp.float32)


def _flash_result(acc_ref, dv):
    acc = acc_ref[...]
    return acc[0:dv] / acc[dv:dv + 1]


def _mla_kernel(q_ref, k_ref, v_ref, o_ref, m_ref, acc_ref, *, tq, tk, cw):
    S = k_ref.shape[2]
    nqt = S // tq
    kk = lax.broadcasted_iota(jnp.int32, (tk, cw), 0)
    qq = lax.broadcasted_iota(jnp.int32, (tk, cw), 1)

    def q_tile(qt, carry):
        q0 = pl.multiple_of(qt * tq, tq)
        for e in range(2):
            _flash_init(m_ref.at[e], acc_ref.at[e])

        def step(k0, key_off):
            chains = []
            for e in range(2):
                k = k_ref[0, e, pl.ds(k0, tk), :]
                v_aug = _with_ones(v_ref[0, e, :, pl.ds(k0, tk)])
                for c0 in range(0, tq, cw):
                    bias = None
                    if key_off is not None:
                        if key_off > c0 + cw - 1:
                            continue
                        if key_off + tk - 1 > c0:
                            bias = jnp.where(kk + key_off <= qq + c0, 0.0, NEG)
                    cols = slice(c0, c0 + cw)
                    chains.append((k, q_ref[0, e, :, pl.ds(pl.multiple_of(q0 + c0, cw), cw)], v_aug, bias,
                                   m_ref.at[e, :, cols], acc_ref.at[e, :, cols]))
            return chains

        per = tq // tk

        def far(c, carry2):
            chains = []
            for j in range(per):
                chains += step(pl.multiple_of(c * tq + j * tk, tk), None)
            _flash_chains(chains)
            return carry2

        lax.fori_loop(0, qt, far, 0)
        chains = []
        for j in range(per):
            chains += step(pl.multiple_of(q0 + j * tk, tk), j * tk)
        _flash_chains(chains)
        o_t = jnp.concatenate([_flash_result(acc_ref.at[e], MLA_V) for e in range(2)], axis=0)
        o_ref[0, pl.ds(q0, tq), :] = jnp.transpose(o_t)
        return carry

    lax.fori_loop(0, nqt, q_tile, 0)


def _mla(q_t, k, v_t):
    B, H, S, _ = k.shape
    tq = min(MLA_TQ, S)
    tk = min(MLA_TK, tq)
    cw = min(MLA_CW, tq)
    return pl.pallas_call(
        functools.partial(_mla_kernel, tq=tq, tk=tk, cw=cw),
        grid=(B, H // 2),
        in_specs=[pl.BlockSpec((1, 2, LANE, S), lambda b, p: (b, p, 0, 0)),
                  pl.BlockSpec((1, 2, S, LANE), lambda b, p: (b, p, 0, 0)),
                  pl.BlockSpec((1, 2, MLA_V, S), lambda b, p: (b, p, 0, 0))],
        out_specs=pl.BlockSpec((1, S, 2 * MLA_V), lambda b, p: (b, 0, p)),
        out_shape=jax.ShapeDtypeStruct((B, S, H * MLA_V), jnp.float32),
        scratch_shapes=[pltpu.VMEM((2, 1, tq), jnp.float32),
                        pltpu.VMEM((2, MLA_V + ONES_ROWS, tq), jnp.float32)],
        compiler_params=_params(2), name="mla",
    )(q_t, k, v_t)


def _nsa_kernel(q_ref, ks_ref, vs_ref, kw_ref, vw_ref, kc_ref, vc_ref, gate_ref, bc_ref, tz_ref, ovt_ref,
                o_ref, qaug_ref, qpad_ref, m_ref, acc_ref, mw_ref, accw_ref, oc_ref, *, tq, tk):
    t = pl.program_id(2)
    G = NSA_GROUP
    ncol = G * tq
    bf = jnp.bfloat16
    q0 = t * tq
    for g in range(G):
        qpad_ref[0:NSA_DK, g * tq:(g + 1) * tq] = q_ref[0, g]
        qaug_ref[0:NSA_DK, g * tq:(g + 1) * tq] = q_ref[0, g]
    qpad_ref[NSA_DK:, :] = jnp.zeros((SLC_ROWS, ncol), bf)

    def tile_chains(k_ref, v_ref, qx_ref, c, bias_idx, stats):
        k0 = pl.multiple_of(c * tk, tk)
        bias_ref = None if bias_idx is None else tz_ref.at[0, bias_idx]
        return _col_chains(k_ref[0, 0, pl.ds(k0, tk), :], qx_ref, _with_ones(v_ref[0, 0, :, pl.ds(k0, tk)]),
                           bias_ref, *stats, ncol, tq)

    prev1 = jnp.maximum(t - 1, 0)
    bias1 = jnp.where(t >= 1, 1, 3)

    win = (mw_ref, accw_ref)
    _flash_init(*win)
    _flash_chains(tile_chains(kw_ref, vw_ref, qpad_ref, jnp.maximum(t - 2, 0), jnp.where(t >= 2, 2, 3), win)
                  + tile_chains(kw_ref, vw_ref, qpad_ref, prev1, bias1, win)
                  + tile_chains(kw_ref, vw_ref, qpad_ref, t, 0, win))

    s = _dot(kc_ref[0, 0], qpad_ref[...]) + bc_ref[0, 0]
    mx = jnp.max(s, axis=0, keepdims=True)
    p = jnp.exp2(s - mx)
    lsum = jnp.sum(p, axis=0, keepdims=True)
    ci = lax.broadcasted_iota(jnp.int32, (1, ncol), 1)
    col_pos = q0 + ci % tq
    r = jnp.where(col_pos >= CMP_LEN - 1, 1.0 / lsum, 0.0)
    pcb = (p * r).astype(bf)
    oc_ref[...] = _dot(vc_ref[0, 0], pcb)

    imp = _dot(ovt_ref[...], pcb[:, 0:tq])
    for g in range(1, G):
        imp = imp + _dot(ovt_ref[...], pcb[:, g * tq:(g + 1) * tq])
    nb = imp.shape[0]
    jb = lax.broadcasted_iota(jnp.int32, (nb, tq), 0)
    cur = (q0 + lax.broadcasted_iota(jnp.int32, (nb, tq), 1)) // SLC_LEN
    forced = jnp.logical_or(jb == 0, jnp.logical_or(jb == cur, jb == cur - 1))
    imp = jnp.where(forced, FORCE, jnp.where(jb <= cur, imp, NEG))
    sub = SUBLANE
    slabs = [imp[r0 * sub:(r0 + 1) * sub] for r0 in range(nb // sub)]
    ranks = [jnp.zeros((sub, tq), jnp.int32) for _ in slabs]
    jsub = lax.broadcasted_iota(jnp.int32, (sub, tq), 0)
    for jp in range(nb):
        rowv = imp[jp:jp + 1, :]
        for r0, slab in enumerate(slabs):
            lo = r0 * sub
            if lo > jp:
                one = jnp.where(rowv >= slab, 1, 0)
            elif lo + sub - 1 <= jp:
                one = jnp.where(rowv > slab, 1, 0)
            else:
                one = jnp.where(jsub + lo > jp, jnp.where(rowv >= slab, 1, 0), jnp.where(rowv > slab, 1, 0))
            ranks[r0] = ranks[r0] + one
    rank = jnp.concatenate(ranks, axis=0)
    selb = jnp.where(rank < SLC_TOPN, 0.0, NEG).astype(bf)
    for g in range(G):
        qaug_ref[NSA_DK:, g * tq:(g + 1) * tq] = selb

    slc = (m_ref, acc_ref)
    _flash_init(*slc)
    n_far = jnp.maximum(t - 1, 0)

    def far_pair(c, carry):
        _flash_chains(tile_chains(ks_ref, vs_ref, qaug_ref, 2 * c, None, slc)
                      + tile_chains(ks_ref, vs_ref, qaug_ref, 2 * c + 1, None, slc))
        return carry

    def far_last(c, carry):
        _flash_chains(tile_chains(ks_ref, vs_ref, qaug_ref, n_far - 1, None, slc))
        return carry

    lax.fori_loop(0, n_far // 2, far_pair, 0)
    lax.fori_loop(0, n_far % 2, far_last, 0)
    _flash_chains(tile_chains(ks_ref, vs_ref, qaug_ref, prev1, bias1, slc)
                  + tile_chains(ks_ref, vs_ref, qaug_ref, t, 0, slc))
    o_s = _flash_result(acc_ref, NSA_DK)
    o_w = _flash_result(accw_ref, NSA_DK)
    o_c = oc_ref[...]

    gate = gate_ref[0, 0]
    comb = []
    for g in range(G):
        cols = slice(g * tq, (g + 1) * tq)
        gr = lambda br: gate[g * 3 + br:g * 3 + br + 1, :]
        comb.append(gr(0) * o_c[:, cols] + gr(1) * o_s[:, cols] + gr(2) * o_w[:, cols])
    o_ref[0] = jnp.transpose(jnp.concatenate(comb, axis=0))


def _nsa(q_t, ks, vs_t, kw, vw_t, kc, vc_t, gate_t, bias_c, tz, ovt):
    B, _, S, _ = ks.shape
    tq, tk = NSA_TQ, NSA_TK
    nq = S // tq
    nc = kc.shape[2]
    m = NSA_GROUP * tq
    tok = lambda: pl.BlockSpec((1, 1, S, LANE), lambda b, h, t: (b, h, 0, 0))
    feat = lambda: pl.BlockSpec((1, 1, NSA_DK, S), lambda b, h, t: (b, h, 0, 0))
    return pl.pallas_call(
        functools.partial(_nsa_kernel, tq=tq, tk=tk),
        grid=(B, NSA_KV_HEADS, nq),
        in_specs=[pl.BlockSpec((1, NSA_GROUP, NSA_DK, tq), lambda b, h, t: (b, h, 0, t)),
                  tok(), feat(), tok(), feat(),
                  pl.BlockSpec((1, 1, nc, LANE), lambda b, h, t: (b, h, 0, 0)),
                  pl.BlockSpec((1, 1, NSA_DK, nc), lambda b, h, t: (b, h, 0, 0)),
                  pl.BlockSpec((1, 1, GATE_ROWS, tq), lambda b, h, t: (b, h, 0, t)),
                  pl.BlockSpec((1, 1, nc, m), lambda b, h, t: (h, t, 0, 0)),
                  pl.BlockSpec((1, 4, tk, m), lambda b, h, t: (h, 0, 0, 0)),
                  _const_spec(ovt.shape)],
        out_specs=pl.BlockSpec((1, tq, NSA_GROUP * NSA_DK), lambda b, h, t: (b, t, h)),
        out_shape=jax.ShapeDtypeStruct((B, S, NSA_HEADS * NSA_DK), jnp.float32),
        scratch_shapes=[pltpu.VMEM((NSA_DK + SLC_ROWS, m), jnp.bfloat16),
                        pltpu.VMEM((NSA_DK + SLC_ROWS, m), jnp.bfloat16),
                        pltpu.VMEM((1, m), jnp.float32), pltpu.VMEM((NSA_DK + ONES_ROWS, m), jnp.float32),
                        pltpu.VMEM((1, m), jnp.float32), pltpu.VMEM((NSA_DK + ONES_ROWS, m), jnp.float32),
                        pltpu.VMEM((NSA_DK, m), jnp.float32)],
        compiler_params=_params(3), name="nsa",
    )(q_t, ks, vs_t, kw, vw_t, kc, vc_t, gate_t, bias_c, tz, ovt)


def _post_kernel(x_ref, om_ref, on_ref, gm_ref, gn_ref, wo_ref, gf_ref, wg_ref, wu_ref, wd_ref,
                 gfin_ref, o_ref):
    half = om_ref.shape[2]
    mix_m = _rms(om_ref[0], gm_ref[...]).astype(jnp.bfloat16)
    mix_n = _rms(on_ref[0], gn_ref[...]).astype(jnp.bfloat16)
    h = x_ref[0] + _dot(mix_m, wo_ref[0:half, :]) + _dot(mix_n, wo_ref[half:2 * half, :])
    f = _rms(h, gf_ref[...]).astype(jnp.bfloat16)
    a = _dot(f, wg_ref[...])
    act = (a * jax.nn.sigmoid(a) * _dot(f, wu_ref[...])).astype(jnp.bfloat16)
    h = h + _dot(act, wd_ref[...])
    o_ref[0] = _rms(h, gfin_ref[...])


def _post(x, o_mla, o_nsa, gm, gn, w_out, gf, wg, wu, wd, gfin):
    B, S, D = x.shape
    tm = min(POST_TM, S)
    tok = lambda w: pl.BlockSpec((1, tm, w), lambda b, i: (b, i, 0))
    consts = (gm, gn, w_out, gf, wg, wu, wd, gfin)
    return pl.pallas_call(
        _post_kernel, grid=(B, S // tm),
        in_specs=[tok(D), tok(o_mla.shape[2]), tok(o_nsa.shape[2])] + [_const_spec(c.shape) for c in consts],
        out_specs=tok(D), out_shape=jax.ShapeDtypeStruct((B, S, D), jnp.float32),
        compiler_params=_params(2), name="post",
    )(x, o_mla, o_nsa, *consts)


def _rope_tables(S):
    pos = jnp.arange(S, dtype=jnp.float32)
    inv = ROPE_THETA ** (-jnp.arange(0, MLA_ROPE, 2, dtype=jnp.float32) / MLA_ROPE)
    ang = pos[:, None] * inv[None, :]
    cos, sin = jnp.cos(ang), jnp.sin(ang)
    cos2 = jnp.concatenate([cos, cos], axis=-1)
    sin2 = jnp.concatenate([sin, sin], axis=-1)
    scale = (MLA_NOPE + MLA_ROPE) ** -0.5 * LOG2E
    zq = jnp.zeros((S, LANE - MLA_NOPE - MLA_ROPE), jnp.float32)
    cosq = jnp.concatenate([jnp.ones((S, MLA_NOPE), jnp.float32), cos2, zq], axis=-1) * scale
    sinq = jnp.concatenate([jnp.zeros((S, MLA_NOPE), jnp.float32), sin2, zq], axis=-1) * scale
    zk = jnp.zeros((S, LANE - MLA_ROPE), jnp.float32)
    cosk = jnp.concatenate([cos2, zk], axis=-1)
    sink = jnp.concatenate([sin2, zk], axis=-1)
    return cosq.T, sinq.T, cosk, sink


def _overlap_t(S):
    n_cmp = (S - CMP_LEN) // CMP_STRIDE + 1
    n_slc = S // SLC_LEN
    assert n_slc <= SLC_ROWS
    cs = np.arange(n_cmp) * CMP_STRIDE
    ss = np.arange(n_slc) * SLC_LEN
    ov = np.maximum(0, np.minimum(cs[:, None] + CMP_LEN, ss[None, :] + SLC_LEN)
                    - np.maximum(cs[:, None], ss[None, :])).astype(np.float32) / CMP_STRIDE
    out = np.zeros((SLC_ROWS, S // CMP_STRIDE), np.float32)
    out[:n_slc, :n_cmp] = ov.T
    return jnp.asarray(out, jnp.bfloat16)


def kernel(x, norm_mix_g, w_in, mla_q_norm_g, mla_w_uq, mla_kv_norm_g, mla_w_ukv, nsa_cmp_pos_k, nsa_cmp_w1_k, nsa_cmp_w2_k, nsa_cmp_pos_v, nsa_cmp_w1_v, nsa_cmp_w2_v, t5_table, out_norm_mla_g, out_norm_nsa_g, w_out, norm_ffn_g, w_gate, w_up, w_down, final_norm_g):
    B, S, D = x.shape
    assert w_in.shape[0] == 1
    assert D == D_MODEL and S % NSA_TQ == 0 and S % CMP_STRIDE == 0
    bf = jnp.bfloat16
    l = 0
    cosq_t, sinq_t, cosk, sink = _rope_tables(S)
    bias_c, tz = _bias_tables(t5_table, S)
    ovt = _overlap_t(S)
    epe = _kpe_placement()
    nc = S // CMP_STRIDE
    cw = CMP_STRIDE * NSA_DK
    row = lambda v: v.reshape(1, -1)

    w_row = _w_in_row_layout().apply(w_in[l])
    w_col = _w_in_col_layout().apply(w_in[l]).T
    w_q2 = _w_uq_layout().apply(mla_w_uq[l]).T
    w_k = _w_uk_layout().apply(mla_w_ukv[l])
    w_v = _w_uv_layout().apply(mla_w_ukv[l]).T
    (q_mla, k_mla, v_mla, q_nsa, kv_cmp, k_slc, v_slc, k_win, v_win, gates) = _proj(
        x, row(norm_mix_g[l]), w_row, w_col, row(mla_q_norm_g[l]), w_q2, row(mla_kv_norm_g[l]), w_k, w_v, epe,
        cosq_t, sinq_t, cosk, sink)

    kvc = kv_cmp.reshape(B, S, 2 * NSA_KV_HEADS, NSA_DK).transpose(0, 2, 1, 3).reshape(
        B, 2 * NSA_KV_HEADS, nc, cw)
    pos = jnp.stack([nsa_cmp_pos_k[l].reshape(2, 1, cw), nsa_cmp_pos_v[l].reshape(2, 1, cw)])
    w1 = jnp.stack([nsa_cmp_w1_k[l].reshape(2, cw, CMP_HIDDEN),
                    nsa_cmp_w1_v[l].reshape(2, cw, CMP_HIDDEN)]).astype(bf)
    w2k = jnp.concatenate([nsa_cmp_w2_k[l], jnp.zeros((CMP_HIDDEN, LANE - NSA_DK), jnp.float32)],
                          axis=1).astype(bf)
    w2vt = nsa_cmp_w2_v[l].T.astype(bf)
    k_cmp, v_cmp = _compress(kvc, pos, w1, w2k, w2vt)

    o_mla = _mla(q_mla, k_mla, v_mla)
    o_nsa = _nsa(q_nsa, k_slc, v_slc, k_win, v_win, k_cmp, v_cmp, gates, bias_c, tz, ovt)

    return _post(x, o_mla, o_nsa, row(out_norm_mla_g[l]), row(out_norm_nsa_g[l]), w_out[l].astype(bf),
                 row(norm_ffn_g[l]), w_gate[l].astype(bf), w_up[l].astype(bf), w_down[l].astype(bf),
                 row(final_norm_g))
```

```python
import functools
import math

import numpy as np
import jax
import jax.numpy as jnp
from jax import lax
from jax.experimental import pallas as pl
from jax.experimental.pallas import tpu as pltpu

D_MODEL = 1024
MLA_HEADS = 8
MLA_NOPE = 64
MLA_ROPE = 32
MLA_V = 64
MLA_Q_LORA = 256
MLA_KV_LORA = 128
ROPE_THETA = 10000.0
NSA_HEADS = 8
NSA_KV_HEADS = 2
NSA_GROUP = 4
NSA_DK = 64
CMP_LEN = 32
CMP_STRIDE = 16
CMP_HIDDEN = 128
SLC_LEN = 64
SLC_TOPN = 16
WINDOW = 512
T5_BUCKETS = 32
T5_MAX_DIST = 128
D_FF = 2816
EPS = 1e-6
NEG = -1e30
FORCE = 1e30

LANE = 128
SUBLANE = 8
SLC_ROWS = 64
GATE_ROWS = 16
ONES_ROWS = 16
LOG2E = math.log2(math.e)

PROJ_TM = 256
MLA_TQ = 1024
MLA_TK = 256
MLA_TRIP_KEYS = 1024
MLA_CW = 256
FLASH_LOOKAHEAD = 4
NSA_TQ = 256
NSA_TK = 256
NSA_FAR_GROUP = 4
POST_TM = 512
VMEM_LIMIT = 56 * 1024 * 1024

C_CQ = 0
C_CKV = 256
C_MISC = 384
C_KVCMP = 512
C_KSLC = 768
C_KWIN = 1024
D_ROW = 1280
R_QNSA = 0
R_VSLC = 512
R_VWIN = 640
R_GATE = 768
D_COL = 800


def _dot(a, b):
    return jnp.dot(a, b, preferred_element_type=jnp.float32)


def _dot_nt(a, b):
    return lax.dot_general(a, b, (((1,), (1,)), ((), ())), preferred_element_type=jnp.float32)


def _rms(x, g):
    return x * lax.rsqrt(jnp.mean(x * x, axis=-1, keepdims=True) + EPS) * g


def _const_spec(shape):
    nd = len(shape)
    return pl.BlockSpec(shape, lambda *_: (0,) * nd, pipeline_mode=pl.Buffered(1))


def _params(n_axes):
    return pltpu.CompilerParams(dimension_semantics=("arbitrary",) * n_axes,
                                vmem_limit_bytes=VMEM_LIMIT)


def _in_offsets():
    o_krope = MLA_Q_LORA + MLA_KV_LORA
    o_qnsa = o_krope + MLA_ROPE
    o_cmp = o_qnsa + NSA_HEADS * NSA_DK
    o_slc = o_cmp + 2 * NSA_KV_HEADS * NSA_DK
    o_win = o_slc + 2 * NSA_KV_HEADS * NSA_DK
    o_gate = o_win + 2 * NSA_KV_HEADS * NSA_DK
    return o_krope, o_qnsa, o_cmp, o_slc, o_win, o_gate


class _Layout:
    def __init__(self, n):
        self.src = np.zeros((n,), np.int32)
        self.mul = np.zeros((n,), np.float32)

    def put(self, dst, s, n, m=1.0):
        self.src[dst:dst + n] = np.arange(s, s + n)
        self.mul[dst:dst + n] = m

    def apply(self, w):
        return (w[:, self.src] * self.mul[None, :]).astype(jnp.bfloat16)


def _w_in_row_layout():
    o_krope, _, o_cmp, o_slc, o_win, _ = _in_offsets()
    lay = _Layout(D_ROW)
    lay.put(C_CQ, 0, MLA_Q_LORA)
    lay.put(C_CKV, MLA_Q_LORA, MLA_KV_LORA)
    half = MLA_ROPE // 2
    lay.put(C_MISC, o_krope, MLA_ROPE)
    lay.put(C_MISC + MLA_ROPE, o_krope + half, half, -1.0)
    lay.put(C_MISC + MLA_ROPE + half, o_krope, half, 1.0)
    lay.put(C_KVCMP, o_cmp, 2 * NSA_KV_HEADS * NSA_DK)
    for hk in range(NSA_KV_HEADS):
        lay.put(C_KSLC + hk * LANE, o_slc + hk * NSA_DK, NSA_DK)
        lay.put(C_KWIN + hk * LANE, o_win + hk * NSA_DK, NSA_DK)
    return lay


def _w_in_col_layout():
    _, o_qnsa, _, o_slc, o_win, o_gate = _in_offsets()
    lay = _Layout(D_COL)
    lay.put(R_QNSA, o_qnsa, NSA_HEADS * NSA_DK, NSA_DK ** -0.5)
    lay.put(R_VSLC, o_slc + NSA_KV_HEADS * NSA_DK, NSA_KV_HEADS * NSA_DK)
    lay.put(R_VWIN, o_win + NSA_KV_HEADS * NSA_DK, NSA_KV_HEADS * NSA_DK)
    for hk in range(NSA_KV_HEADS):
        lay.put(R_GATE + hk * GATE_ROWS, o_gate + hk * NSA_GROUP * 3, NSA_GROUP * 3)
    return lay


def _w_uq_layout():
    lay = _Layout(2 * MLA_HEADS * LANE)
    dq = MLA_NOPE + MLA_ROPE
    half = MLA_ROPE // 2
    for h in range(MLA_HEADS):
        lay.put(h * LANE, h * dq, dq)
        r = MLA_HEADS * LANE + h * LANE + MLA_NOPE
        pe = h * dq + MLA_NOPE
        lay.put(r, pe + half, half, -1.0)
        lay.put(r + half, pe, half, 1.0)
    return lay


def _w_uk_layout():
    lay = _Layout(MLA_HEADS * LANE)
    per = MLA_NOPE + MLA_V
    for h in range(MLA_HEADS):
        lay.put(h * LANE, h * per, MLA_NOPE)
    return lay


def _w_uv_layout():
    lay = _Layout(MLA_HEADS * MLA_V)
    per = MLA_NOPE + MLA_V
    for h in range(MLA_HEADS):
        lay.put(h * MLA_V, h * per + MLA_NOPE, MLA_V)
    return lay


def _kpe_placement():
    e = np.zeros((LANE, MLA_HEADS * LANE), np.float32)
    for h in range(MLA_HEADS):
        for r in range(MLA_ROPE):
            e[r, h * LANE + MLA_NOPE + r] = 1.0
    return jnp.asarray(e, jnp.bfloat16)


def _t5_thresholds(max_dist):
    n = np.arange(max_dist, dtype=np.int64)
    max_exact = T5_BUCKETS // 2
    nf = np.maximum(n, 1).astype(np.float32)
    val = (np.log(nf / np.float32(max_exact)) / np.float32(math.log(T5_MAX_DIST / max_exact))
           * np.float32(T5_BUCKETS - max_exact))
    large = np.minimum(max_exact + val.astype(np.int32), T5_BUCKETS - 1)
    bucket = np.where(n < max_exact, n, large)
    assert np.all(np.diff(bucket) >= 0)
    frac = np.abs(val[max_exact + 1:T5_MAX_DIST] - np.round(val[max_exact + 1:T5_MAX_DIST]))
    assert frac.min() > 1e-3
    return [int(np.argmax(bucket >= b)) for b in range(T5_BUCKETS)]


def _proj_kernel(x_ref, gmix_ref, wrow_ref, wcol_ref, gq_ref, wq_ref, gkv_ref, wk_ref, wv_ref, epe_ref,
                 cq_ref, sq_ref, ck_ref, sk_ref,
                 qm_ref, km_ref, vm_ref, qn_ref, kvc_ref, ks_ref, vs_ref, kw_ref, vw_ref, gate_ref):
    tm = x_ref.shape[1]
    bf = jnp.bfloat16
    xn = _rms(x_ref[0], gmix_ref[...]).astype(bf)
    u = _dot(xn, wrow_ref[...])
    ut = _dot_nt(wcol_ref[...], xn)

    cqn = _rms(u[:, C_CQ:C_CQ + MLA_Q_LORA], gq_ref[...]).astype(bf)
    qq = _dot_nt(wq_ref[...], cqn)
    cosq = cq_ref[...]
    sinq = sq_ref[...]
    hw = MLA_HEADS * LANE
    for h in range(MLA_HEADS):
        qh = qq[h * LANE:(h + 1) * LANE] * cosq + qq[hw + h * LANE:hw + (h + 1) * LANE] * sinq
        qm_ref[0, h] = qh.astype(bf)

    ckvn = _rms(u[:, C_CKV:C_CKV + MLA_KV_LORA], gkv_ref[...]).astype(bf)
    kk = _dot(ckvn, wk_ref[...])
    misc = u[:, C_MISC:C_MISC + LANE]
    kpe = misc * ck_ref[...] + pltpu.roll(misc, LANE - MLA_ROPE, axis=1) * sk_ref[...]
    kpe_all = _dot(kpe.astype(bf), epe_ref[...])
    for h in range(MLA_HEADS):
        km_ref[0, h] = (kk[:, h * LANE:(h + 1) * LANE] + kpe_all[:, h * LANE:(h + 1) * LANE]).astype(bf)
    vv = _dot_nt(wv_ref[...], ckvn)
    for h in range(MLA_HEADS):
        vm_ref[0, h] = vv[h * MLA_V:(h + 1) * MLA_V].astype(bf)

    for h in range(NSA_HEADS):
        qn_ref[0, h] = (ut[R_QNSA + h * NSA_DK:R_QNSA + (h + 1) * NSA_DK] * LOG2E).astype(bf)
    kvc_ref[0] = u[:, C_KVCMP:C_KVCMP + 2 * NSA_KV_HEADS * NSA_DK]
    pos = pl.program_id(1) * tm + lax.broadcasted_iota(jnp.int32, (tm, LANE), 0)
    lane = lax.broadcasted_iota(jnp.int32, (tm, LANE), 1)
    onehot = jnp.where(lane - (LANE - SLC_ROWS) == pos // SLC_LEN, 1.0, 0.0)
    for hk in range(NSA_KV_HEADS):
        ks_ref[0, hk] = (u[:, C_KSLC + hk * LANE:C_KSLC + (hk + 1) * LANE] + onehot).astype(bf)
        kw_ref[0, hk] = u[:, C_KWIN + hk * LANE:C_KWIN + (hk + 1) * LANE].astype(bf)
        vs_ref[0, hk] = ut[R_VSLC + hk * NSA_DK:R_VSLC + (hk + 1) * NSA_DK].astype(bf)
        vw_ref[0, hk] = ut[R_VWIN + hk * NSA_DK:R_VWIN + (hk + 1) * NSA_DK].astype(bf)
        gate_ref[0, hk] = jax.nn.sigmoid(ut[R_GATE + hk * GATE_ROWS:R_GATE + (hk + 1) * GATE_ROWS])


def _proj(x, gmix, w_row, w_col, gq, w_q2, gkv, w_k, w_v, epe, cosq_t, sinq_t, cosk, sink):
    B, S, D = x.shape
    tm = min(PROJ_TM, S)
    bf = jnp.bfloat16
    grid = (B, S // tm)
    tok_major = lambda n: pl.BlockSpec((1, n, tm, LANE), lambda b, i: (b, 0, i, 0))
    feat_major = lambda n, d: pl.BlockSpec((1, n, d, tm), lambda b, i: (b, 0, 0, i))
    out_shape = (
        jax.ShapeDtypeStruct((B, MLA_HEADS, LANE, S), bf),
        jax.ShapeDtypeStruct((B, MLA_HEADS, S, LANE), bf),
        jax.ShapeDtypeStruct((B, MLA_HEADS, MLA_V, S), bf),
        jax.ShapeDtypeStruct((B, NSA_HEADS, NSA_DK, S), bf),
        jax.ShapeDtypeStruct((B, S, 2 * NSA_KV_HEADS * NSA_DK), jnp.float32),
        jax.ShapeDtypeStruct((B, NSA_KV_HEADS, S, LANE), bf),
        jax.ShapeDtypeStruct((B, NSA_KV_HEADS, NSA_DK, S), bf),
        jax.ShapeDtypeStruct((B, NSA_KV_HEADS, S, LANE), bf),
        jax.ShapeDtypeStruct((B, NSA_KV_HEADS, NSA_DK, S), bf),
        jax.ShapeDtypeStruct((B, NSA_KV_HEADS, GATE_ROWS, S), jnp.float32),
    )
    out_specs = (
        feat_major(MLA_HEADS, LANE), tok_major(MLA_HEADS), feat_major(MLA_HEADS, MLA_V),
        feat_major(NSA_HEADS, NSA_DK),
        pl.BlockSpec((1, tm, 2 * NSA_KV_HEADS * NSA_DK), lambda b, i: (b, i, 0)),
        tok_major(NSA_KV_HEADS), feat_major(NSA_KV_HEADS, NSA_DK),
        tok_major(NSA_KV_HEADS), feat_major(NSA_KV_HEADS, NSA_DK),
        feat_major(NSA_KV_HEADS, GATE_ROWS),
    )
    consts = (gmix, w_row, w_col, gq, w_q2, gkv, w_k, w_v, epe)
    in_specs = ([pl.BlockSpec((1, tm, D), lambda b, i: (b, i, 0))] + [_const_spec(c.shape) for c in consts]
                + [pl.BlockSpec((LANE, tm), lambda b, i: (0, i)), pl.BlockSpec((LANE, tm), lambda b, i: (0, i)),
                   pl.BlockSpec((tm, LANE), lambda b, i: (i, 0)), pl.BlockSpec((tm, LANE), lambda b, i: (i, 0))])
    return pl.pallas_call(
        _proj_kernel, grid=grid, in_specs=in_specs, out_specs=out_specs, out_shape=out_shape,
        compiler_params=_params(2), name="proj",
    )(x, *consts, cosq_t, sinq_t, cosk, sink)


def _compress_kernel(c_ref, pos_ref, w1_ref, w2k_ref, w2vt_ref, kc_ref, vc_ref):
    nc = c_ref.shape[2]
    bf = jnp.bfloat16
    for kv in range(2):
        for hk in range(NSA_KV_HEADS):
            c = c_ref[0, kv * NSA_KV_HEADS + hk]
            a = _dot((c + pos_ref[kv, 0]).astype(bf), w1_ref[kv, 0])
            b = _dot((c + pos_ref[kv, 1]).astype(bf), w1_ref[kv, 1])
            hid = a + pltpu.roll(b, nc - 1, axis=0)
            act = jax.nn.gelu(hid).astype(bf)
            if kv == 0:
                kc_ref[0, hk] = _dot(act, w2k_ref[...]).astype(bf)
            else:
                vc_ref[0, hk] = _dot_nt(w2vt_ref[...], act).astype(bf)


def _compress(kvc, pos, w1, w2k, w2vt):
    B, _, nc, width = kvc.shape
    return pl.pallas_call(
        _compress_kernel, grid=(B,),
        in_specs=[pl.BlockSpec((1, 2 * NSA_KV_HEADS, nc, width), lambda b: (b, 0, 0, 0)),
                  _const_spec(pos.shape), _const_spec(w1.shape), _const_spec(w2k.shape),
                  _const_spec(w2vt.shape)],
        out_specs=(pl.BlockSpec((1, NSA_KV_HEADS, nc, LANE), lambda b: (b, 0, 0, 0)),
                   pl.BlockSpec((1, NSA_KV_HEADS, NSA_DK, nc), lambda b: (b, 0, 0, 0))),
        out_shape=(jax.ShapeDtypeStruct((B, NSA_KV_HEADS, nc, LANE), jnp.bfloat16),
                   jax.ShapeDtypeStruct((B, NSA_KV_HEADS, NSA_DK, nc), jnp.bfloat16)),
        compiler_params=_params(1), name="compress",
    )(kvc, pos, w1, w2k, w2vt)


def _t5_lookup(dist, tab_ref, head, thr):
    val = jnp.full(dist.shape, tab_ref[0, head], jnp.float32)
    for b in range(1, T5_BUCKETS):
        val = jnp.where(dist >= thr[b], tab_ref[b, head], val)
    return (val - tab_ref[T5_BUCKETS - 1, head]) * LOG2E


def _bias_cmp_kernel(tab_ref, out_ref, *, thr, tq, n_cmp):
    hk = pl.program_id(0)
    t = pl.program_id(1)
    nc = out_ref.shape[2]
    n = lax.broadcasted_iota(jnp.int32, (nc, tq), 0)
    i = lax.broadcasted_iota(jnp.int32, (nc, tq), 1)
    dist = t * tq + i - (n * CMP_STRIDE + CMP_LEN - 1)
    ok = jnp.logical_and(dist >= 0, n < n_cmp)
    for g in range(NSA_GROUP):
        val = _t5_lookup(dist, tab_ref, hk * NSA_GROUP + g, thr)
        out_ref[0, 0, :, g * tq:(g + 1) * tq] = jnp.where(ok, val, NEG)


def _bias_tile_kernel(tab_ref, out_ref, *, thr, tq):
    hk = pl.program_id(0)
    tk = out_ref.shape[2]
    j = lax.broadcasted_iota(jnp.int32, (tk, tq), 0)
    i = lax.broadcasted_iota(jnp.int32, (tk, tq), 1)
    for g in range(NSA_GROUP):
        head = hk * NSA_GROUP + g
        cols = slice(g * tq, (g + 1) * tq)
        d0 = i - j
        out_ref[0, 0, :, cols] = jnp.where(d0 >= 0, _t5_lookup(d0, tab_ref, head, thr), NEG)
        out_ref[0, 1, :, cols] = _t5_lookup(d0 + tk, tab_ref, head, thr)
        out_ref[0, 2, :, cols] = jnp.where(j > i, 0.0, NEG)
        out_ref[0, 3, :, cols] = jnp.full((tk, tq), NEG, jnp.float32)


def _bias_tables(t5_table, S):
    thr = _t5_thresholds(S)
    tq, tk = NSA_TQ, NSA_TK
    assert tq == tk and WINDOW == 2 * tk and tk >= T5_MAX_DIST
    nq = S // tq
    nc = S // CMP_STRIDE
    n_cmp = (S - CMP_LEN) // CMP_STRIDE + 1
    m = NSA_GROUP * tq
    smem = pl.BlockSpec(memory_space=pltpu.SMEM)
    bias_c = pl.pallas_call(
        functools.partial(_bias_cmp_kernel, thr=thr, tq=tq, n_cmp=n_cmp),
        grid=(NSA_KV_HEADS, nq), in_specs=[smem],
        out_specs=pl.BlockSpec((1, 1, nc, m), lambda h, t: (h, t, 0, 0)),
        out_shape=jax.ShapeDtypeStruct((NSA_KV_HEADS, nq, nc, m), jnp.float32),
        compiler_params=_params(2), name="bias_cmp",
    )(t5_table)
    tz = pl.pallas_call(
        functools.partial(_bias_tile_kernel, thr=thr, tq=tq),
        grid=(NSA_KV_HEADS,), in_specs=[smem],
        out_specs=pl.BlockSpec((1, 4, tk, m), lambda h: (h, 0, 0, 0)),
        out_shape=jax.ShapeDtypeStruct((NSA_KV_HEADS, 4, tk, m), jnp.float32),
        compiler_params=_params(1), name="bias_tile",
    )(t5_table)
    return bias_c, tz


def _flash_chains(chains):
    def scores(chain):
        k, q_t, _, bias_t = chain[:4]
        s = _dot(k, q_t)
        return s if bias_t is None else s + bias_t

    pending = [scores(c) for c in chains[:FLASH_LOOKAHEAD]]
    for i, chain in enumerate(chains):
        s = pending.pop(0)
        if i + FLASH_LOOKAHEAD < len(chains):
            pending.append(scores(chains[i + FLASH_LOOKAHEAD]))
        v_aug, m_ref, acc_ref = chain[2], chain[4], chain[5]
        m_prev = m_ref[...]
        m_new = jnp.maximum(m_prev, jnp.max(s, axis=0, keepdims=True))
        alpha = jnp.exp2(m_prev - m_new)
        p = jnp.exp2(s - m_new).astype(jnp.bfloat16)
        acc_ref[...] = alpha * acc_ref[...] + _dot(v_aug, p)
        m_ref[...] = m_new


def _with_ones(v_t):
    tk = v_t.shape[1]
    row = lax.broadcasted_iota(jnp.int32, (ONES_ROWS, tk), 0)
    return jnp.concatenate([v_t, jnp.where(row == 0, 1.0, 0.0).astype(v_t.dtype)], axis=0)


def _col_chains(k, q_ref, v_aug, bias_ref, m_ref, acc_ref, ncol, width):
    chains = []
    for c0 in range(0, ncol, width):
        cols = slice(c0, c0 + width)
        chains.append((k, q_ref[:, cols], v_aug, None if bias_ref is None else bias_ref[:, cols],
                       m_ref.at[:, cols], acc_ref.at[:, cols]))
    return chains


def _flash_init(m_ref, acc_ref):
    m_ref[...] = jnp.full(m_ref.shape, -jnp.inf, jnp.float32)
    acc_ref[...] = jnp.zeros(acc_ref.shape, jnp.float32)


def _flash_result(acc_ref, dv):
    acc = acc_ref[...]
    return acc[0:dv] / acc[dv:dv + 1]


def _mla_kernel(q_ref, k_ref, v_ref, o_ref, m_ref, acc_ref, *, tq, tk, cw):
    S = k_ref.shape[2]
    nqt = S // tq
    kk = lax.broadcasted_iota(jnp.int32, (tk, cw), 0)
    qq = lax.broadcasted_iota(jnp.int32, (tk, cw), 1)

    def q_tile(qt, carry):
        q0 = pl.multiple_of(qt * tq, tq)
        for e in range(2):
            _flash_init(m_ref.at[e], acc_ref.at[e])

        def step(k0, key_off):
            chains = []
            for e in range(2):
                k = k_ref[0, e, pl.ds(k0, tk), :]
                v_aug = _with_ones(v_ref[0, e, :, pl.ds(k0, tk)])
                for c0 in range(0, tq, cw):
                    bias = None
                    if key_off is not None:
                        if key_off > c0 + cw - 1:
                            continue
                        if key_off + tk - 1 > c0:
                            bias = jnp.where(kk + key_off <= qq + c0, 0.0, NEG)
                    cols = slice(c0, c0 + cw)
                    chains.append((k, q_ref[0, e, :, pl.ds(pl.multiple_of(q0 + c0, cw), cw)], v_aug, bias,
                                   m_ref.at[e, :, cols], acc_ref.at[e, :, cols]))
            return chains

        per = MLA_TRIP_KEYS // tk

        def far(c, carry2):
            chains = []
            for j in range(per):
                chains += step(pl.multiple_of(c * MLA_TRIP_KEYS + j * tk, tk), None)
            _flash_chains(chains)
            return carry2

        lax.fori_loop(0, qt * (tq // MLA_TRIP_KEYS), far, 0)
        chains = []
        for j in range(tq // tk):
            chains += step(pl.multiple_of(q0 + j * tk, tk), j * tk)
        _flash_chains(chains)
        o_t = jnp.concatenate([_flash_result(acc_ref.at[e], MLA_V) for e in range(2)], axis=0)
        o_ref[0, pl.ds(q0, tq), :] = jnp.transpose(o_t)
        return carry

    lax.fori_loop(0, nqt, q_tile, 0)


def _mla(q_t, k, v_t):
    B, H, S, _ = k.shape
    tq = min(MLA_TQ, S)
    tk = min(MLA_TK, tq)
    cw = min(MLA_CW, tq)
    return pl.pallas_call(
        functools.partial(_mla_kernel, tq=tq, tk=tk, cw=cw),
        grid=(B, H // 2),
        in_specs=[pl.BlockSpec((1, 2, LANE, S), lambda b, p: (b, p, 0, 0)),
                  pl.BlockSpec((1, 2, S, LANE), lambda b, p: (b, p, 0, 0)),
                  pl.BlockSpec((1, 2, MLA_V, S), lambda b, p: (b, p, 0, 0))],
        out_specs=pl.BlockSpec((1, S, 2 * MLA_V), lambda b, p: (b, 0, p)),
        out_shape=jax.ShapeDtypeStruct((B, S, H * MLA_V), jnp.float32),
        scratch_shapes=[pltpu.VMEM((2, 1, tq), jnp.float32),
                        pltpu.VMEM((2, MLA_V + ONES_ROWS, tq), jnp.float32)],
        compiler_params=_params(2), name="mla",
    )(q_t, k, v_t)


def _nsa_kernel(q_ref, ks_ref, vs_ref, kw_ref, vw_ref, kc_ref, vc_ref, gate_ref, bc_ref, tz_ref, ovt_ref,
                o_ref, qaug_ref, qpad_ref, m_ref, acc_ref, mw_ref, accw_ref, oc_ref, *, tq, tk):
    t = pl.program_id(2)
    G = NSA_GROUP
    ncol = G * tq
    bf = jnp.bfloat16
    q0 = t * tq
    for g in range(G):
        qpad_ref[0:NSA_DK, g * tq:(g + 1) * tq] = q_ref[0, g]
        qaug_ref[0:NSA_DK, g * tq:(g + 1) * tq] = q_ref[0, g]
    qpad_ref[NSA_DK:, :] = jnp.zeros((SLC_ROWS, ncol), bf)

    def tile_chains(k_ref, v_ref, qx_ref, c, bias_idx, stats):
        k0 = pl.multiple_of(c * tk, tk)
        bias_ref = None if bias_idx is None else tz_ref.at[0, bias_idx]
        return _col_chains(k_ref[0, 0, pl.ds(k0, tk), :], qx_ref, _with_ones(v_ref[0, 0, :, pl.ds(k0, tk)]),
                           bias_ref, *stats, ncol, tq)

    prev1 = jnp.maximum(t - 1, 0)
    bias1 = jnp.where(t >= 1, 1, 3)

    win = (mw_ref, accw_ref)
    _flash_init(*win)
    _flash_chains(tile_chains(kw_ref, vw_ref, qpad_ref, jnp.maximum(t - 2, 0), jnp.where(t >= 2, 2, 3), win)
                  + tile_chains(kw_ref, vw_ref, qpad_ref, prev1, bias1, win)
                  + tile_chains(kw_ref, vw_ref, qpad_ref, t, 0, win))

    s = _dot(kc_ref[0, 0], qpad_ref[...]) + bc_ref[0, 0]
    mx = jnp.max(s, axis=0, keepdims=True)
    p = jnp.exp2(s - mx)
    lsum = jnp.sum(p, axis=0, keepdims=True)
    ci = lax.broadcasted_iota(jnp.int32, (1, ncol), 1)
    col_pos = q0 + ci % tq
    r = jnp.where(col_pos >= CMP_LEN - 1, 1.0 / lsum, 0.0)
    pcb = (p * r).astype(bf)
    oc_ref[...] = _dot(vc_ref[0, 0], pcb)

    imp = _dot(ovt_ref[...], pcb[:, 0:tq])
    for g in range(1, G):
        imp = imp + _dot(ovt_ref[...], pcb[:, g * tq:(g + 1) * tq])
    nb = imp.shape[0]
    jb = lax.broadcasted_iota(jnp.int32, (nb, tq), 0)
    cur = (q0 + lax.broadcasted_iota(jnp.int32, (nb, tq), 1)) // SLC_LEN
    forced = jnp.logical_or(jb == 0, jnp.logical_or(jb == cur, jb == cur - 1))
    imp = jnp.where(forced, FORCE, jnp.where(jb <= cur, imp, NEG))
    sub = SUBLANE
    slabs = [imp[r0 * sub:(r0 + 1) * sub] for r0 in range(nb // sub)]
    ranks = [jnp.zeros((sub, tq), jnp.int32) for _ in slabs]
    jsub = lax.broadcasted_iota(jnp.int32, (sub, tq), 0)
    for jp in range(nb):
        rowv = imp[jp:jp + 1, :]
        for r0, slab in enumerate(slabs):
            lo = r0 * sub
            if lo > jp:
                one = jnp.where(rowv >= slab, 1, 0)
            elif lo + sub - 1 <= jp:
                one = jnp.where(rowv > slab, 1, 0)
            else:
                one = jnp.where(jsub + lo > jp, jnp.where(rowv >= slab, 1, 0), jnp.where(rowv > slab, 1, 0))
            ranks[r0] = ranks[r0] + one
    rank = jnp.concatenate(ranks, axis=0)
    selb = jnp.where(rank < SLC_TOPN, 0.0, NEG).astype(bf)
    for g in range(G):
        qaug_ref[NSA_DK:, g * tq:(g + 1) * tq] = selb

    slc = (m_ref, acc_ref)
    _flash_init(*slc)
    n_far = jnp.maximum(t - 1, 0)

    done = 0
    size = NSA_FAR_GROUP
    while size >= 1:
        trips = (n_far - done) // size

        def far(c, carry, size=size, done=done):
            chains = []
            for j in range(size):
                chains += tile_chains(ks_ref, vs_ref, qaug_ref, done + c * size + j, None, slc)
            _flash_chains(chains)
            return carry

        lax.fori_loop(0, trips, far, 0)
        done = done + trips * size
        size //= 2
    _flash_chains(tile_chains(ks_ref, vs_ref, qaug_ref, prev1, bias1, slc)
                  + tile_chains(ks_ref, vs_ref, qaug_ref, t, 0, slc))
    o_s = _flash_result(acc_ref, NSA_DK)
    o_w = _flash_result(accw_ref, NSA_DK)
    o_c = oc_ref[...]

    gate = gate_ref[0, 0]
    comb = []
    for g in range(G):
        cols = slice(g * tq, (g + 1) * tq)
        gr = lambda br: gate[g * 3 + br:g * 3 + br + 1, :]
        comb.append(gr(0) * o_c[:, cols] + gr(1) * o_s[:, cols] + gr(2) * o_w[:, cols])
    o_ref[0] = jnp.transpose(jnp.concatenate(comb, axis=0))


def _nsa(q_t, ks, vs_t, kw, vw_t, kc, vc_t, gate_t, bias_c, tz, ovt):
    B, _, S, _ = ks.shape
    tq, tk = NSA_TQ, NSA_TK
    nq = S // tq
    nc = kc.shape[2]
    m = NSA_GROUP * tq
    tok = lambda: pl.BlockSpec((1, 1, S, LANE), lambda b, h, t: (b, h, 0, 0))
    feat = lambda: pl.BlockSpec((1, 1, NSA_DK, S), lambda b, h, t: (b, h, 0, 0))
    return pl.pallas_call(
        functools.partial(_nsa_kernel, tq=tq, tk=tk),
        grid=(B, NSA_KV_HEADS, nq),
        in_specs=[pl.BlockSpec((1, NSA_GROUP, NSA_DK, tq), lambda b, h, t: (b, h, 0, t)),
                  tok(), feat(), tok(), feat(),
                  pl.BlockSpec((1, 1, nc, LANE), lambda b, h, t: (b, h, 0, 0)),
                  pl.BlockSpec((1, 1, NSA_DK, nc), lambda b, h, t: (b, h, 0, 0)),
                  pl.BlockSpec((1, 1, GATE_ROWS, tq), lambda b, h, t: (b, h, 0, t)),
                  pl.BlockSpec((1, 1, nc, m), lambda b, h, t: (h, t, 0, 0)),
                  pl.BlockSpec((1, 4, tk, m), lambda b, h, t: (h, 0, 0, 0)),
                  _const_spec(ovt.shape)],
        out_specs=pl.BlockSpec((1, tq, NSA_GROUP * NSA_DK), lambda b, h, t: (b, t, h)),
        out_shape=jax.ShapeDtypeStruct((B, S, NSA_HEADS * NSA_DK), jnp.float32),
        scratch_shapes=[pltpu.VMEM((NSA_DK + SLC_ROWS, m), jnp.bfloat16),
                        pltpu.VMEM((NSA_DK + SLC_ROWS, m), jnp.bfloat16),
                        pltpu.VMEM((1, m), jnp.float32), pltpu.VMEM((NSA_DK + ONES_ROWS, m), jnp.float32),
                        pltpu.VMEM((1, m), jnp.float32), pltpu.VMEM((NSA_DK + ONES_ROWS, m), jnp.float32),
                        pltpu.VMEM((NSA_DK, m), jnp.float32)],
        compiler_params=_params(3), name="nsa",
    )(q_t, ks, vs_t, kw, vw_t, kc, vc_t, gate_t, bias_c, tz, ovt)


def _post_kernel(x_ref, om_ref, on_ref, gm_ref, gn_ref, wo_ref, gf_ref, wg_ref, wu_ref, wd_ref,
                 gfin_ref, o_ref):
    half = om_ref.shape[2]
    mix_m = _rms(om_ref[0], gm_ref[...]).astype(jnp.bfloat16)
    mix_n = _rms(on_ref[0], gn_ref[...]).astype(jnp.bfloat16)
    h = x_ref[0] + _dot(mix_m, wo_ref[0:half, :]) + _dot(mix_n, wo_ref[half:2 * half, :])
    f = _rms(h, gf_ref[...]).astype(jnp.bfloat16)
    a = _dot(f, wg_ref[...])
    act = (a * jax.nn.sigmoid(a) * _dot(f, wu_ref[...])).astype(jnp.bfloat16)
    h = h + _dot(act, wd_ref[...])
    o_ref[0] = _rms(h, gfin_ref[...])


def _post(x, o_mla, o_nsa, gm, gn, w_out, gf, wg, wu, wd, gfin):
    B, S, D = x.shape
    tm = min(POST_TM, S)
    tok = lambda w: pl.BlockSpec((1, tm, w), lambda b, i: (b, i, 0))
    consts = (gm, gn, w_out, gf, wg, wu, wd, gfin)
    return pl.pallas_call(
        _post_kernel, grid=(B, S // tm),
        in_specs=[tok(D), tok(o_mla.shape[2]), tok(o_nsa.shape[2])] + [_const_spec(c.shape) for c in consts],
        out_specs=tok(D), out_shape=jax.ShapeDtypeStruct((B, S, D), jnp.float32),
        compiler_params=_params(2), name="post",
    )(x, o_mla, o_nsa, *consts)


def _rope_tables(S):
    pos = jnp.arange(S, dtype=jnp.float32)
    inv = ROPE_THETA ** (-jnp.arange(0, MLA_ROPE, 2, dtype=jnp.float32) / MLA_ROPE)
    ang = pos[:, None] * inv[None, :]
    cos, sin = jnp.cos(ang), jnp.sin(ang)
    cos2 = jnp.concatenate([cos, cos], axis=-1)
    sin2 = jnp.concatenate([sin, sin], axis=-1)
    scale = (MLA_NOPE + MLA_ROPE) ** -0.5 * LOG2E
    zq = jnp.zeros((S, LANE - MLA_NOPE - MLA_ROPE), jnp.float32)
    cosq = jnp.concatenate([jnp.ones((S, MLA_NOPE), jnp.float32), cos2, zq], axis=-1) * scale
    sinq = jnp.concatenate([jnp.zeros((S, MLA_NOPE), jnp.float32), sin2, zq], axis=-1) * scale
    zk = jnp.zeros((S, LANE - MLA_ROPE), jnp.float32)
    cosk = jnp.concatenate([cos2, zk], axis=-1)
    sink = jnp.concatenate([sin2, zk], axis=-1)
    return cosq.T, sinq.T, cosk, sink


def _overlap_t(S):
    n_cmp = (S - CMP_LEN) // CMP_STRIDE + 1
    n_slc = S // SLC_LEN
    assert n_slc <= SLC_ROWS
    cs = np.arange(n_cmp) * CMP_STRIDE
    ss = np.arange(n_slc) * SLC_LEN
    ov = np.maximum(0, np.minimum(cs[:, None] + CMP_LEN, ss[None, :] + SLC_LEN)
                    - np.maximum(cs[:, None], ss[None, :])).astype(np.float32) / CMP_STRIDE
    out = np.zeros((SLC_ROWS, S // CMP_STRIDE), np.float32)
    out[:n_slc, :n_cmp] = ov.T
    return jnp.asarray(out, jnp.bfloat16)


def kernel(x, norm_mix_g, w_in, mla_q_norm_g, mla_w_uq, mla_kv_norm_g, mla_w_ukv, nsa_cmp_pos_k, nsa_cmp_w1_k, nsa_cmp_w2_k, nsa_cmp_pos_v, nsa_cmp_w1_v, nsa_cmp_w2_v, t5_table, out_norm_mla_g, out_norm_nsa_g, w_out, norm_ffn_g, w_gate, w_up, w_down, final_norm_g):
    B, S, D = x.shape
    assert w_in.shape[0] == 1
    assert D == D_MODEL and S % NSA_TQ == 0 and S % CMP_STRIDE == 0
    bf = jnp.bfloat16
    l = 0
    cosq_t, sinq_t, cosk, sink = _rope_tables(S)
    bias_c, tz = _bias_tables(t5_table, S)
    ovt = _overlap_t(S)
    epe = _kpe_placement()
    nc = S // CMP_STRIDE
    cw = CMP_STRIDE * NSA_DK
    row = lambda v: v.reshape(1, -1)

    w_row = _w_in_row_layout().apply(w_in[l])
    w_col = _w_in_col_layout().apply(w_in[l]).T
    w_q2 = _w_uq_layout().apply(mla_w_uq[l]).T
    w_k = _w_uk_layout().apply(mla_w_ukv[l])
    w_v = _w_uv_layout().apply(mla_w_ukv[l]).T
    (q_mla, k_mla, v_mla, q_nsa, kv_cmp, k_slc, v_slc, k_win, v_win, gates) = _proj(
        x, row(norm_mix_g[l]), w_row, w_col, row(mla_q_norm_g[l]), w_q2, row(mla_kv_norm_g[l]), w_k, w_v, epe,
        cosq_t, sinq_t, cosk, sink)

    kvc = kv_cmp.reshape(B, S, 2 * NSA_KV_HEADS, NSA_DK).transpose(0, 2, 1, 3).reshape(
        B, 2 * NSA_KV_HEADS, nc, cw)
    pos = jnp.stack([nsa_cmp_pos_k[l].reshape(2, 1, cw), nsa_cmp_pos_v[l].reshape(2, 1, cw)])
    w1 = jnp.stack([nsa_cmp_w1_k[l].reshape(2, cw, CMP_HIDDEN),
                    nsa_cmp_w1_v[l].reshape(2, cw, CMP_HIDDEN)]).astype(bf)
    w2k = jnp.concatenate([nsa_cmp_w2_k[l], jnp.zeros((CMP_HIDDEN, LANE - NSA_DK), jnp.float32)],
                          axis=1).astype(bf)
    w2vt = nsa_cmp_w2_v[l].T.astype(bf)
    k_cmp, v_cmp = _compress(kvc, pos, w1, w2k, w2vt)

    o_mla = _mla(q_mla, k_mla, v_mla)
    o_nsa = _nsa(q_nsa, k_slc, v_slc, k_win, v_win, k_cmp, v_cmp, gates, bias_c, tz, ovt)

    return _post(x, o_mla, o_nsa, row(out_norm_mla_g[l]), row(out_norm_nsa_g[l]), w_out[l].astype(bf),
                 row(norm_ffn_g[l]), w_gate[l].astype(bf), w_up[l].astype(bf), w_down[l].astype(bf),
                 row(final_norm_g))
```

```python
import functools
import math

import numpy as np
import jax
import jax.numpy as jnp
from jax import lax
from jax.experimental import pallas as pl
from jax.experimental.pallas import tpu as pltpu

D_MODEL = 1024
MLA_HEADS = 8
MLA_NOPE = 64
MLA_ROPE = 32
MLA_V = 64
MLA_Q_LORA = 256
MLA_KV_LORA = 128
ROPE_THETA = 10000.0
NSA_HEADS = 8
NSA_KV_HEADS = 2
NSA_GROUP = 4
NSA_DK = 64
CMP_LEN = 32
CMP_STRIDE = 16
CMP_HIDDEN = 128
SLC_LEN = 64
SLC_TOPN = 16
WINDOW = 512
T5_BUCKETS = 32
T5_MAX_DIST = 128
D_FF = 2816
EPS = 1e-6
NEG = -1e30
FORCE = 1e30

LANE = 128
SUBLANE = 8
SLC_ROWS = 64
GATE_ROWS = 16
ONES_ROWS = 16
LOG2E = math.log2(math.e)

PROJ_TM = 256
MLA_TQ = 1024
MLA_TK = 256
MLA_TRIP_KEYS = 1024
MLA_CW = 256
FLASH_LOOKAHEAD = 4
NSA_TQ = 256
NSA_TK = 256
NSA_FAR_GROUP = 4
POST_TM = 512
VMEM_LIMIT = 56 * 1024 * 1024

C_CQ = 0
C_CKV = 256
C_MISC = 384
C_KVCMP = 512
C_KSLC = 768
C_KWIN = 1024
D_ROW = 1280
R_QNSA = 0
R_VSLC = 512
R_VWIN = 640
R_GATE = 768
D_COL = 800


def _dot(a, b):
    return jnp.dot(a, b, preferred_element_type=jnp.float32)


def _dot_nt(a, b):
    return lax.dot_general(a, b, (((1,), (1,)), ((), ())), preferred_element_type=jnp.float32)


def _rms(x, g):
    return x * lax.rsqrt(jnp.mean(x * x, axis=-1, keepdims=True) + EPS) * g


def _const_spec(shape):
    nd = len(shape)
    return pl.BlockSpec(shape, lambda *_: (0,) * nd, pipeline_mode=pl.Buffered(1))


def _params(n_axes):
    return pltpu.CompilerParams(dimension_semantics=("arbitrary",) * n_axes,
                                vmem_limit_bytes=VMEM_LIMIT)


def _in_offsets():
    o_krope = MLA_Q_LORA + MLA_KV_LORA
    o_qnsa = o_krope + MLA_ROPE
    o_cmp = o_qnsa + NSA_HEADS * NSA_DK
    o_slc = o_cmp + 2 * NSA_KV_HEADS * NSA_DK
    o_win = o_slc + 2 * NSA_KV_HEADS * NSA_DK
    o_gate = o_win + 2 * NSA_KV_HEADS * NSA_DK
    return o_krope, o_qnsa, o_cmp, o_slc, o_win, o_gate


class _Layout:
    def __init__(self, n):
        self.src = np.zeros((n,), np.int32)
        self.mul = np.zeros((n,), np.float32)

    def put(self, dst, s, n, m=1.0):
        self.src[dst:dst + n] = np.arange(s, s + n)
        self.mul[dst:dst + n] = m

    def apply(self, w):
        return (w[:, self.src] * self.mul[None, :]).astype(jnp.bfloat16)


def _w_in_row_layout():
    o_krope, _, o_cmp, o_slc, o_win, _ = _in_offsets()
    lay = _Layout(D_ROW)
    lay.put(C_CQ, 0, MLA_Q_LORA)
    lay.put(C_CKV, MLA_Q_LORA, MLA_KV_LORA)
    half = MLA_ROPE // 2
    lay.put(C_MISC, o_krope, MLA_ROPE)
    lay.put(C_MISC + MLA_ROPE, o_krope + half, half, -1.0)
    lay.put(C_MISC + MLA_ROPE + half, o_krope, half, 1.0)
    lay.put(C_KVCMP, o_cmp, 2 * NSA_KV_HEADS * NSA_DK)
    for hk in range(NSA_KV_HEADS):
        lay.put(C_KSLC + hk * LANE, o_slc + hk * NSA_DK, NSA_DK)
        lay.put(C_KWIN + hk * LANE, o_win + hk * NSA_DK, NSA_DK)
    return lay


def _w_in_col_layout():
    _, o_qnsa, _, o_slc, o_win, o_gate = _in_offsets()
    lay = _Layout(D_COL)
    lay.put(R_QNSA, o_qnsa, NSA_HEADS * NSA_DK, NSA_DK ** -0.5)
    lay.put(R_VSLC, o_slc + NSA_KV_HEADS * NSA_DK, NSA_KV_HEADS * NSA_DK)
    lay.put(R_VWIN, o_win + NSA_KV_HEADS * NSA_DK, NSA_KV_HEADS * NSA_DK)
    for hk in range(NSA_KV_HEADS):
        lay.put(R_GATE + hk * GATE_ROWS, o_gate + hk * NSA_GROUP * 3, NSA_GROUP * 3)
    return lay


def _w_uq_layout():
    lay = _Layout(2 * MLA_HEADS * LANE)
    dq = MLA_NOPE + MLA_ROPE
    half = MLA_ROPE // 2
    for h in range(MLA_HEADS):
        lay.put(h * LANE, h * dq, dq)
        r = MLA_HEADS * LANE + h * LANE + MLA_NOPE
        pe = h * dq + MLA_NOPE
        lay.put(r, pe + half, half, -1.0)
        lay.put(r + half, pe, half, 1.0)
    return lay


def _w_uk_layout():
    lay = _Layout(MLA_HEADS * LANE)
    per = MLA_NOPE + MLA_V
    for h in range(MLA_HEADS):
        lay.put(h * LANE, h * per, MLA_NOPE)
    return lay


def _w_uv_layout():
    lay = _Layout(MLA_HEADS * MLA_V)
    per = MLA_NOPE + MLA_V
    for h in range(MLA_HEADS):
        lay.put(h * MLA_V, h * per + MLA_NOPE, MLA_V)
    return lay


def _kpe_placement():
    e = np.zeros((LANE, MLA_HEADS * LANE), np.float32)
    for h in range(MLA_HEADS):
        for r in range(MLA_ROPE):
            e[r, h * LANE + MLA_NOPE + r] = 1.0
    return jnp.asarray(e, jnp.bfloat16)


def _t5_thresholds(max_dist):
    n = np.arange(max_dist, dtype=np.int64)
    max_exact = T5_BUCKETS // 2
    nf = np.maximum(n, 1).astype(np.float32)
    val = (np.log(nf / np.float32(max_exact)) / np.float32(math.log(T5_MAX_DIST / max_exact))
           * np.float32(T5_BUCKETS - max_exact))
    large = np.minimum(max_exact + val.astype(np.int32), T5_BUCKETS - 1)
    bucket = np.where(n < max_exact, n, large)
    assert np.all(np.diff(bucket) >= 0)
    frac = np.abs(val[max_exact + 1:T5_MAX_DIST] - np.round(val[max_exact + 1:T5_MAX_DIST]))
    assert frac.min() > 1e-3
    return [int(np.argmax(bucket >= b)) for b in range(T5_BUCKETS)]


def _proj_kernel(x_ref, gmix_ref, wrow_ref, wcol_ref, gq_ref, wq_ref, gkv_ref, wk_ref, wv_ref, epe_ref,
                 cq_ref, sq_ref, ck_ref, sk_ref,
                 qm_ref, km_ref, vm_ref, qn_ref, kvc_ref, ks_ref, vs_ref, kw_ref, vw_ref, gate_ref):
    tm = x_ref.shape[1]
    bf = jnp.bfloat16
    xn = _rms(x_ref[0], gmix_ref[...]).astype(bf)
    u = _dot(xn, wrow_ref[...])
    ut = _dot_nt(wcol_ref[...], xn)

    cqn = _rms(u[:, C_CQ:C_CQ + MLA_Q_LORA], gq_ref[...]).astype(bf)
    qq = _dot_nt(wq_ref[...], cqn)
    cosq = cq_ref[...]
    sinq = sq_ref[...]
    hw = MLA_HEADS * LANE
    for h in range(MLA_HEADS):
        qh = qq[h * LANE:(h + 1) * LANE] * cosq + qq[hw + h * LANE:hw + (h + 1) * LANE] * sinq
        qm_ref[0, h] = qh.astype(bf)

    ckvn = _rms(u[:, C_CKV:C_CKV + MLA_KV_LORA], gkv_ref[...]).astype(bf)
    kk = _dot(ckvn, wk_ref[...])
    misc = u[:, C_MISC:C_MISC + LANE]
    kpe = misc * ck_ref[...] + pltpu.roll(misc, LANE - MLA_ROPE, axis=1) * sk_ref[...]
    kpe_all = _dot(kpe.astype(bf), epe_ref[...])
    for h in range(MLA_HEADS):
        km_ref[0, h] = (kk[:, h * LANE:(h + 1) * LANE] + kpe_all[:, h * LANE:(h + 1) * LANE]).astype(bf)
    vv = _dot_nt(wv_ref[...], ckvn)
    for h in range(MLA_HEADS):
        vm_ref[0, h] = vv[h * MLA_V:(h + 1) * MLA_V].astype(bf)

    for h in range(NSA_HEADS):
        qn_ref[0, h] = (ut[R_QNSA + h * NSA_DK:R_QNSA + (h + 1) * NSA_DK] * LOG2E).astype(bf)
    kvc_ref[0] = u[:, C_KVCMP:C_KVCMP + 2 * NSA_KV_HEADS * NSA_DK]
    pos = pl.program_id(1) * tm + lax.broadcasted_iota(jnp.int32, (tm, LANE), 0)
    lane = lax.broadcasted_iota(jnp.int32, (tm, LANE), 1)
    onehot = jnp.where(lane - (LANE - SLC_ROWS) == pos // SLC_LEN, 1.0, 0.0)
    for hk in range(NSA_KV_HEADS):
        ks_ref[0, hk] = (u[:, C_KSLC + hk * LANE:C_KSLC + (hk + 1) * LANE] + onehot).astype(bf)
        kw_ref[0, hk] = u[:, C_KWIN + hk * LANE:C_KWIN + (hk + 1) * LANE].astype(bf)
        vs_ref[0, hk] = ut[R_VSLC + hk * NSA_DK:R_VSLC + (hk + 1) * NSA_DK].astype(bf)
        vw_ref[0, hk] = ut[R_VWIN + hk * NSA_DK:R_VWIN + (hk + 1) * NSA_DK].astype(bf)
        gate_ref[0, hk] = jax.nn.sigmoid(ut[R_GATE + hk * GATE_ROWS:R_GATE + (hk + 1) * GATE_ROWS])


def _proj(x, gmix, w_row, w_col, gq, w_q2, gkv, w_k, w_v, epe, cosq_t, sinq_t, cosk, sink):
    B, S, D = x.shape
    tm = min(PROJ_TM, S)
    bf = jnp.bfloat16
    grid = (B, S // tm)
    tok_major = lambda n: pl.BlockSpec((1, n, tm, LANE), lambda b, i: (b, 0, i, 0))
    feat_major = lambda n, d: pl.BlockSpec((1, n, d, tm), lambda b, i: (b, 0, 0, i))
    out_shape = (
        jax.ShapeDtypeStruct((B, MLA_HEADS, LANE, S), bf),
        jax.ShapeDtypeStruct((B, MLA_HEADS, S, LANE), bf),
        jax.ShapeDtypeStruct((B, MLA_HEADS, MLA_V, S), bf),
        jax.ShapeDtypeStruct((B, NSA_HEADS, NSA_DK, S), bf),
        jax.ShapeDtypeStruct((B, S, 2 * NSA_KV_HEADS * NSA_DK), jnp.float32),
        jax.ShapeDtypeStruct((B, NSA_KV_HEADS, S, LANE), bf),
        jax.ShapeDtypeStruct((B, NSA_KV_HEADS, NSA_DK, S), bf),
        jax.ShapeDtypeStruct((B, NSA_KV_HEADS, S, LANE), bf),
        jax.ShapeDtypeStruct((B, NSA_KV_HEADS, NSA_DK, S), bf),
        jax.ShapeDtypeStruct((B, NSA_KV_HEADS, GATE_ROWS, S), jnp.float32),
    )
    out_specs = (
        feat_major(MLA_HEADS, LANE), tok_major(MLA_HEADS), feat_major(MLA_HEADS, MLA_V),
        feat_major(NSA_HEADS, NSA_DK),
        pl.BlockSpec((1, tm, 2 * NSA_KV_HEADS * NSA_DK), lambda b, i: (b, i, 0)),
        tok_major(NSA_KV_HEADS), feat_major(NSA_KV_HEADS, NSA_DK),
        tok_major(NSA_KV_HEADS), feat_major(NSA_KV_HEADS, NSA_DK),
        feat_major(NSA_KV_HEADS, GATE_ROWS),
    )
    consts = (gmix, w_row, w_col, gq, w_q2, gkv, w_k, w_v, epe)
    in_specs = ([pl.BlockSpec((1, tm, D), lambda b, i: (b, i, 0))] + [_const_spec(c.shape) for c in consts]
                + [pl.BlockSpec((LANE, tm), lambda b, i: (0, i)), pl.BlockSpec((LANE, tm), lambda b, i: (0, i)),
                   pl.BlockSpec((tm, LANE), lambda b, i: (i, 0)), pl.BlockSpec((tm, LANE), lambda b, i: (i, 0))])
    return pl.pallas_call(
        _proj_kernel, grid=grid, in_specs=in_specs, out_specs=out_specs, out_shape=out_shape,
        compiler_params=_params(2), name="proj",
    )(x, *consts, cosq_t, sinq_t, cosk, sink)


def _compress_kernel(c_ref, pos_ref, w1_ref, w2k_ref, w2vt_ref, kc_ref, vc_ref):
    nc = c_ref.shape[2]
    bf = jnp.bfloat16
    for kv in range(2):
        for hk in range(NSA_KV_HEADS):
            c = c_ref[0, kv * NSA_KV_HEADS + hk]
            a = _dot((c + pos_ref[kv, 0]).astype(bf), w1_ref[kv, 0])
            b = _dot((c + pos_ref[kv, 1]).astype(bf), w1_ref[kv, 1])
            hid = a + pltpu.roll(b, nc - 1, axis=0)
            act = jax.nn.gelu(hid).astype(bf)
            if kv == 0:
                kc_ref[0, hk] = _dot(act, w2k_ref[...]).astype(bf)
            else:
                vc_ref[0, hk] = _dot_nt(w2vt_ref[...], act).astype(bf)


def _compress(kvc, pos, w1, w2k, w2vt):
    B, _, nc, width = kvc.shape
    return pl.pallas_call(
        _compress_kernel, grid=(B,),
        in_specs=[pl.BlockSpec((1, 2 * NSA_KV_HEADS, nc, width), lambda b: (b, 0, 0, 0)),
                  _const_spec(pos.shape), _const_spec(w1.shape), _const_spec(w2k.shape),
                  _const_spec(w2vt.shape)],
        out_specs=(pl.BlockSpec((1, NSA_KV_HEADS, nc, LANE), lambda b: (b, 0, 0, 0)),
                   pl.BlockSpec((1, NSA_KV_HEADS, NSA_DK, nc), lambda b: (b, 0, 0, 0))),
        out_shape=(jax.ShapeDtypeStruct((B, NSA_KV_HEADS, nc, LANE), jnp.bfloat16),
                   jax.ShapeDtypeStruct((B, NSA_KV_HEADS, NSA_DK, nc), jnp.bfloat16)),
        compiler_params=_params(1), name="compress",
    )(kvc, pos, w1, w2k, w2vt)


def _t5_lookup(dist, tab_ref, head, thr):
    val = jnp.full(dist.shape, tab_ref[0, head], jnp.float32)
    for b in range(1, T5_BUCKETS):
        val = jnp.where(dist >= thr[b], tab_ref[b, head], val)
    return (val - tab_ref[T5_BUCKETS - 1, head]) * LOG2E


def _bias_cmp_kernel(tab_ref, out_ref, *, thr, tq, n_cmp):
    hk = pl.program_id(0)
    t = pl.program_id(1)
    nc = out_ref.shape[2]
    n = lax.broadcasted_iota(jnp.int32, (nc, tq), 0)
    i = lax.broadcasted_iota(jnp.int32, (nc, tq), 1)
    dist = t * tq + i - (n * CMP_STRIDE + CMP_LEN - 1)
    ok = jnp.logical_and(dist >= 0, n < n_cmp)
    for g in range(NSA_GROUP):
        val = _t5_lookup(dist, tab_ref, hk * NSA_GROUP + g, thr)
        out_ref[0, 0, :, g * tq:(g + 1) * tq] = jnp.where(ok, val, NEG)


def _bias_tile_kernel(tab_ref, out_ref, *, thr, tq):
    hk = pl.program_id(0)
    tk = out_ref.shape[2]
    j = lax.broadcasted_iota(jnp.int32, (tk, tq), 0)
    i = lax.broadcasted_iota(jnp.int32, (tk, tq), 1)
    for g in range(NSA_GROUP):
        head = hk * NSA_GROUP + g
        cols = slice(g * tq, (g + 1) * tq)
        d0 = i - j
        out_ref[0, 0, :, cols] = jnp.where(d0 >= 0, _t5_lookup(d0, tab_ref, head, thr), NEG)
        out_ref[0, 1, :, cols] = _t5_lookup(d0 + tk, tab_ref, head, thr)
        out_ref[0, 2, :, cols] = jnp.where(j > i, 0.0, NEG)
        out_ref[0, 3, :, cols] = jnp.full((tk, tq), NEG, jnp.float32)


def _bias_tables(t5_table, S):
    thr = _t5_thresholds(S)
    tq, tk = NSA_TQ, NSA_TK
    assert tq == tk and WINDOW == 2 * tk and tk >= T5_MAX_DIST
    nq = S // tq
    nc = S // CMP_STRIDE
    n_cmp = (S - CMP_LEN) // CMP_STRIDE + 1
    m = NSA_GROUP * tq
    smem = pl.BlockSpec(memory_space=pltpu.SMEM)
    bias_c = pl.pallas_call(
        functools.partial(_bias_cmp_kernel, thr=thr, tq=tq, n_cmp=n_cmp),
        grid=(NSA_KV_HEADS, nq), in_specs=[smem],
        out_specs=pl.BlockSpec((1, 1, nc, m), lambda h, t: (h, t, 0, 0)),
        out_shape=jax.ShapeDtypeStruct((NSA_KV_HEADS, nq, nc, m), jnp.float32),
        compiler_params=_params(2), name="bias_cmp",
    )(t5_table)
    tz = pl.pallas_call(
        functools.partial(_bias_tile_kernel, thr=thr, tq=tq),
        grid=(NSA_KV_HEADS,), in_specs=[smem],
        out_specs=pl.BlockSpec((1, 4, tk, m), lambda h: (h, 0, 0, 0)),
        out_shape=jax.ShapeDtypeStruct((NSA_KV_HEADS, 4, tk, m), jnp.float32),
        compiler_params=_params(1), name="bias_tile",
    )(t5_table)
    return bias_c, tz


def _flash_chains(chains, side_work=()):
    def scores(chain):
        k, q_t, _, bias_t = chain[:4]
        s = _dot(k, q_t)
        return s if bias_t is None else s + bias_t

    side_work = list(side_work)
    per_chain = -(-len(side_work) // len(chains))
    pending = [scores(c) for c in chains[:FLASH_LOOKAHEAD]]
    for i, chain in enumerate(chains):
        for thunk in side_work[i * per_chain:(i + 1) * per_chain]:
            thunk()
        s = pending.pop(0)
        if i + FLASH_LOOKAHEAD < len(chains):
            pending.append(scores(chains[i + FLASH_LOOKAHEAD]))
        v_aug, m_ref, acc_ref = chain[2], chain[4], chain[5]
        m_prev = m_ref[...]
        m_new = jnp.maximum(m_prev, jnp.max(s, axis=0, keepdims=True))
        alpha = jnp.exp2(m_prev - m_new)
        p = jnp.exp2(s - m_new).astype(jnp.bfloat16)
        acc_ref[...] = alpha * acc_ref[...] + _dot(v_aug, p)
        m_ref[...] = m_new


def _with_ones(v_t):
    tk = v_t.shape[1]
    row = lax.broadcasted_iota(jnp.int32, (ONES_ROWS, tk), 0)
    return jnp.concatenate([v_t, jnp.where(row == 0, 1.0, 0.0).astype(v_t.dtype)], axis=0)


def _col_chains(k, q_ref, v_aug, bias_ref, m_ref, acc_ref, ncol, width):
    chains = []
    for c0 in range(0, ncol, width):
        cols = slice(c0, c0 + width)
        chains.append((k, q_ref[:, cols], v_aug, None if bias_ref is None else bias_ref[:, cols],
                       m_ref.at[:, cols], acc_ref.at[:, cols]))
    return chains


def _flash_init(m_ref, acc_ref):
    m_ref[...] = jnp.full(m_ref.shape, -jnp.inf, jnp.float32)
    acc_ref[...] = jnp.zeros(acc_ref.shape, jnp.float32)


def _flash_result(acc_ref, dv):
    acc = acc_ref[...]
    return acc[0:dv] / acc[dv:dv + 1]


def _mla_kernel(q_ref, k_ref, v_ref, o_ref, m_ref, acc_ref, *, tq, tk, cw):
    S = k_ref.shape[2]
    nqt = S // tq
    kk = lax.broadcasted_iota(jnp.int32, (tk, cw), 0)
    qq = lax.broadcasted_iota(jnp.int32, (tk, cw), 1)

    def q_tile(qt, carry):
        q0 = pl.multiple_of(qt * tq, tq)
        for e in range(2):
            _flash_init(m_ref.at[e], acc_ref.at[e])

        def step(k0, key_off):
            chains = []
            for e in range(2):
                k = k_ref[0, e, pl.ds(k0, tk), :]
                v_aug = _with_ones(v_ref[0, e, :, pl.ds(k0, tk)])
                for c0 in range(0, tq, cw):
                    bias = None
                    if key_off is not None:
                        if key_off > c0 + cw - 1:
                            continue
                        if key_off + tk - 1 > c0:
                            bias = jnp.where(kk + key_off <= qq + c0, 0.0, NEG)
                    cols = slice(c0, c0 + cw)
                    chains.append((k, q_ref[0, e, :, pl.ds(pl.multiple_of(q0 + c0, cw), cw)], v_aug, bias,
                                   m_ref.at[e, :, cols], acc_ref.at[e, :, cols]))
            return chains

        per = MLA_TRIP_KEYS // tk

        def far(c, carry2):
            chains = []
            for j in range(per):
                chains += step(pl.multiple_of(c * MLA_TRIP_KEYS + j * tk, tk), None)
            _flash_chains(chains)
            return carry2

        lax.fori_loop(0, qt * (tq // MLA_TRIP_KEYS), far, 0)
        chains = []
        for j in range(tq // tk):
            chains += step(pl.multiple_of(q0 + j * tk, tk), j * tk)
        _flash_chains(chains)
        o_t = jnp.concatenate([_flash_result(acc_ref.at[e], MLA_V) for e in range(2)], axis=0)
        o_ref[0, pl.ds(q0, tq), :] = jnp.transpose(o_t)
        return carry

    lax.fori_loop(0, nqt, q_tile, 0)


def _mla(q_t, k, v_t):
    B, H, S, _ = k.shape
    tq = min(MLA_TQ, S)
    tk = min(MLA_TK, tq)
    cw = min(MLA_CW, tq)
    return pl.pallas_call(
        functools.partial(_mla_kernel, tq=tq, tk=tk, cw=cw),
        grid=(B, H // 2),
        in_specs=[pl.BlockSpec((1, 2, LANE, S), lambda b, p: (b, p, 0, 0)),
                  pl.BlockSpec((1, 2, S, LANE), lambda b, p: (b, p, 0, 0)),
                  pl.BlockSpec((1, 2, MLA_V, S), lambda b, p: (b, p, 0, 0))],
        out_specs=pl.BlockSpec((1, S, 2 * MLA_V), lambda b, p: (b, 0, p)),
        out_shape=jax.ShapeDtypeStruct((B, S, H * MLA_V), jnp.float32),
        scratch_shapes=[pltpu.VMEM((2, 1, tq), jnp.float32),
                        pltpu.VMEM((2, MLA_V + ONES_ROWS, tq), jnp.float32)],
        compiler_params=_params(2), name="mla",
    )(q_t, k, v_t)


def _nsa_kernel(q_ref, ks_ref, vs_ref, kw_ref, vw_ref, kc_ref, vc_ref, gate_ref, bc_ref, tz_ref, ovt_ref,
                o_ref, qaug_ref, qpad_ref, m_ref, acc_ref, mw_ref, accw_ref, oc_ref, *, tq, tk):
    t = pl.program_id(1)
    G = NSA_GROUP
    HK = range(NSA_KV_HEADS)
    ncol = G * tq
    bf = jnp.bfloat16
    q0 = t * tq
    for hk in HK:
        for g in range(G):
            qpad_ref[hk, 0:NSA_DK, g * tq:(g + 1) * tq] = q_ref[0, hk * G + g]
            qaug_ref[hk, 0:NSA_DK, g * tq:(g + 1) * tq] = q_ref[0, hk * G + g]
        qpad_ref[hk, NSA_DK:, :] = jnp.zeros((SLC_ROWS, ncol), bf)

    def tile_chains(k_ref, v_ref, qx_ref, c, bias_idx, stats):
        k0 = pl.multiple_of(c * tk, tk)
        chains = []
        for hk in HK:
            bias_ref = None if bias_idx is None else tz_ref.at[hk, bias_idx]
            chains += _col_chains(k_ref[0, hk, pl.ds(k0, tk), :], qx_ref.at[hk],
                                  _with_ones(v_ref[0, hk, :, pl.ds(k0, tk)]),
                                  bias_ref, stats[0].at[hk], stats[1].at[hk], ncol, tq)
        return chains

    prev1 = jnp.maximum(t - 1, 0)
    bias1 = jnp.where(t >= 1, 1, 3)

    nb = ovt_ref.shape[0]
    sub = SUBLANE
    ci = lax.broadcasted_iota(jnp.int32, (1, ncol), 1)
    col_ok = q0 + ci % tq >= CMP_LEN - 1
    jb = lax.broadcasted_iota(jnp.int32, (nb, tq), 0)
    cur = (q0 + lax.broadcasted_iota(jnp.int32, (nb, tq), 1)) // SLC_LEN
    forced = jnp.logical_or(jb == 0, jnp.logical_or(jb == cur, jb == cur - 1))
    jsub = lax.broadcasted_iota(jnp.int32, (sub, tq), 0)
    imps, slabs, ranks = [], [], []
    for hk in HK:
        s = _dot(kc_ref[0, hk], qpad_ref[hk]) + bc_ref[hk, 0]
        mx = jnp.max(s, axis=0, keepdims=True)
        p = jnp.exp2(s - mx)
        lsum = jnp.sum(p, axis=0, keepdims=True)
        pcb = (p * jnp.where(col_ok, 1.0 / lsum, 0.0)).astype(bf)
        oc_ref[hk] = _dot(vc_ref[0, hk], pcb)
        imp = _dot(ovt_ref[...], pcb[:, 0:tq])
        for g in range(1, G):
            imp = imp + _dot(ovt_ref[...], pcb[:, g * tq:(g + 1) * tq])
        imp = jnp.where(forced, FORCE, jnp.where(jb <= cur, imp, NEG))
        imps.append(imp)
        slabs.append([imp[r0 * sub:(r0 + 1) * sub] for r0 in range(nb // sub)])
        ranks.append([jnp.zeros((sub, tq), jnp.int32) for _ in range(nb // sub)])

    def rank_step(hk, jp):
        rowv = imps[hk][jp:jp + 1, :]
        for r0, slab in enumerate(slabs[hk]):
            lo = r0 * sub
            if lo > jp:
                one = jnp.where(rowv >= slab, 1, 0)
            elif lo + sub - 1 <= jp:
                one = jnp.where(rowv > slab, 1, 0)
            else:
                one = jnp.where(jsub + lo > jp, jnp.where(rowv >= slab, 1, 0), jnp.where(rowv > slab, 1, 0))
            ranks[hk][r0] = ranks[hk][r0] + one

    win = (mw_ref, accw_ref)
    _flash_init(*win)
    _flash_chains(tile_chains(kw_ref, vw_ref, qpad_ref, jnp.maximum(t - 2, 0), jnp.where(t >= 2, 2, 3), win)
                  + tile_chains(kw_ref, vw_ref, qpad_ref, prev1, bias1, win)
                  + tile_chains(kw_ref, vw_ref, qpad_ref, t, 0, win),
                  side_work=[functools.partial(rank_step, hk, jp) for jp in range(nb) for hk in HK])
    for hk in HK:
        rank = jnp.concatenate(ranks[hk], axis=0)
        selb = jnp.where(rank < SLC_TOPN, 0.0, NEG).astype(bf)
        for g in range(G):
            qaug_ref[hk, NSA_DK:, g * tq:(g + 1) * tq] = selb

    slc = (m_ref, acc_ref)
    _flash_init(*slc)
    n_far = jnp.maximum(t - 1, 0)

    done = 0
    size = NSA_FAR_GROUP
    while size >= 1:
        trips = (n_far - done) // size

        def far(c, carry, size=size, done=done):
            chains = []
            for j in range(size):
                chains += tile_chains(ks_ref, vs_ref, qaug_ref, done + c * size + j, None, slc)
            _flash_chains(chains)
            return carry

        lax.fori_loop(0, trips, far, 0)
        done = done + trips * size
        size //= 2
    _flash_chains(tile_chains(ks_ref, vs_ref, qaug_ref, prev1, bias1, slc)
                  + tile_chains(ks_ref, vs_ref, qaug_ref, t, 0, slc))

    for hk in HK:
        o_s = _flash_result(acc_ref.at[hk], NSA_DK)
        o_w = _flash_result(accw_ref.at[hk], NSA_DK)
        o_c = oc_ref[hk]
        gate = gate_ref[0, hk]
        comb = []
        for g in range(G):
            cols = slice(g * tq, (g + 1) * tq)
            gr = lambda br: gate[g * 3 + br:g * 3 + br + 1, :]
            comb.append(gr(0) * o_c[:, cols] + gr(1) * o_s[:, cols] + gr(2) * o_w[:, cols])
        width = G * NSA_DK
        o_ref[0, :, hk * width:(hk + 1) * width] = jnp.transpose(jnp.concatenate(comb, axis=0))


def _nsa(q_t, ks, vs_t, kw, vw_t, kc, vc_t, gate_t, bias_c, tz, ovt):
    B, _, S, _ = ks.shape
    tq, tk = NSA_TQ, NSA_TK
    nq = S // tq
    nc = kc.shape[2]
    m = NSA_GROUP * tq
    hkv = NSA_KV_HEADS
    tok = lambda: pl.BlockSpec((1, hkv, S, LANE), lambda b, t: (b, 0, 0, 0))
    feat = lambda: pl.BlockSpec((1, hkv, NSA_DK, S), lambda b, t: (b, 0, 0, 0))
    return pl.pallas_call(
        functools.partial(_nsa_kernel, tq=tq, tk=tk),
        grid=(B, nq),
        in_specs=[pl.BlockSpec((1, NSA_HEADS, NSA_DK, tq), lambda b, t: (b, 0, 0, t)),
                  tok(), feat(), tok(), feat(),
                  pl.BlockSpec((1, hkv, nc, LANE), lambda b, t: (b, 0, 0, 0)),
                  pl.BlockSpec((1, hkv, NSA_DK, nc), lambda b, t: (b, 0, 0, 0)),
                  pl.BlockSpec((1, hkv, GATE_ROWS, tq), lambda b, t: (b, 0, 0, t)),
                  pl.BlockSpec((hkv, 1, nc, m), lambda b, t: (0, t, 0, 0)),
                  _const_spec(tz.shape),
                  _const_spec(ovt.shape)],
        out_specs=pl.BlockSpec((1, tq, NSA_HEADS * NSA_DK), lambda b, t: (b, t, 0)),
        out_shape=jax.ShapeDtypeStruct((B, S, NSA_HEADS * NSA_DK), jnp.float32),
        scratch_shapes=[pltpu.VMEM((hkv, NSA_DK + SLC_ROWS, m), jnp.bfloat16),
                        pltpu.VMEM((hkv, NSA_DK + SLC_ROWS, m), jnp.bfloat16),
                        pltpu.VMEM((hkv, 1, m), jnp.float32),
                        pltpu.VMEM((hkv, NSA_DK + ONES_ROWS, m), jnp.float32),
                        pltpu.VMEM((hkv, 1, m), jnp.float32),
                        pltpu.VMEM((hkv, NSA_DK + ONES_ROWS, m), jnp.float32),
                        pltpu.VMEM((hkv, NSA_DK, m), jnp.float32)],
        compiler_params=_params(2), name="nsa",
    )(q_t, ks, vs_t, kw, vw_t, kc, vc_t, gate_t, bias_c, tz, ovt)


def _post_kernel(x_ref, om_ref, on_ref, gm_ref, gn_ref, wo_ref, gf_ref, wg_ref, wu_ref, wd_ref,
                 gfin_ref, o_ref):
    half = om_ref.shape[2]
    mix_m = _rms(om_ref[0], gm_ref[...]).astype(jnp.bfloat16)
    mix_n = _rms(on_ref[0], gn_ref[...]).astype(jnp.bfloat16)
    h = x_ref[0] + _dot(mix_m, wo_ref[0:half, :]) + _dot(mix_n, wo_ref[half:2 * half, :])
    f = _rms(h, gf_ref[...]).astype(jnp.bfloat16)
    a = _dot(f, wg_ref[...])
    act = (a * jax.nn.sigmoid(a) * _dot(f, wu_ref[...])).astype(jnp.bfloat16)
    h = h + _dot(act, wd_ref[...])
    o_ref[0] = _rms(h, gfin_ref[...])


def _post(x, o_mla, o_nsa, gm, gn, w_out, gf, wg, wu, wd, gfin):
    B, S, D = x.shape
    tm = min(POST_TM, S)
    tok = lambda w: pl.BlockSpec((1, tm, w), lambda b, i: (b, i, 0))
    consts = (gm, gn, w_out, gf, wg, wu, wd, gfin)
    return pl.pallas_call(
        _post_kernel, grid=(B, S // tm),
        in_specs=[tok(D), tok(o_mla.shape[2]), tok(o_nsa.shape[2])] + [_const_spec(c.shape) for c in consts],
        out_specs=tok(D), out_shape=jax.ShapeDtypeStruct((B, S, D), jnp.float32),
        compiler_params=_params(2), name="post",
    )(x, o_mla, o_nsa, *consts)


def _rope_tables(S):
    pos = jnp.arange(S, dtype=jnp.float32)
    inv = ROPE_THETA ** (-jnp.arange(0, MLA_ROPE, 2, dtype=jnp.float32) / MLA_ROPE)
    ang = pos[:, None] * inv[None, :]
    cos, sin = jnp.cos(ang), jnp.sin(ang)
    cos2 = jnp.concatenate([cos, cos], axis=-1)
    sin2 = jnp.concatenate([sin, sin], axis=-1)
    scale = (MLA_NOPE + MLA_ROPE) ** -0.5 * LOG2E
    zq = jnp.zeros((S, LANE - MLA_NOPE - MLA_ROPE), jnp.float32)
    cosq = jnp.concatenate([jnp.ones((S, MLA_NOPE), jnp.float32), cos2, zq], axis=-1) * scale
    sinq = jnp.concatenate([jnp.zeros((S, MLA_NOPE), jnp.float32), sin2, zq], axis=-1) * scale
    zk = jnp.zeros((S, LANE - MLA_ROPE), jnp.float32)
    cosk = jnp.concatenate([cos2, zk], axis=-1)
    sink = jnp.concatenate([sin2, zk], axis=-1)
    return cosq.T, sinq.T, cosk, sink


def _overlap_t(S):
    n_cmp = (S - CMP_LEN) // CMP_STRIDE + 1
    n_slc = S // SLC_LEN
    assert n_slc <= SLC_ROWS
    cs = np.arange(n_cmp) * CMP_STRIDE
    ss = np.arange(n_slc) * SLC_LEN
    ov = np.maximum(0, np.minimum(cs[:, None] + CMP_LEN, ss[None, :] + SLC_LEN)
                    - np.maximum(cs[:, None], ss[None, :])).astype(np.float32) / CMP_STRIDE
    out = np.zeros((SLC_ROWS, S // CMP_STRIDE), np.float32)
    out[:n_slc, :n_cmp] = ov.T
    return jnp.asarray(out, jnp.bfloat16)


def kernel(x, norm_mix_g, w_in, mla_q_norm_g, mla_w_uq, mla_kv_norm_g, mla_w_ukv, nsa_cmp_pos_k, nsa_cmp_w1_k, nsa_cmp_w2_k, nsa_cmp_pos_v, nsa_cmp_w1_v, nsa_cmp_w2_v, t5_table, out_norm_mla_g, out_norm_nsa_g, w_out, norm_ffn_g, w_gate, w_up, w_down, final_norm_g):
    B, S, D = x.shape
    assert w_in.shape[0] == 1
    assert D == D_MODEL and S % NSA_TQ == 0 and S % CMP_STRIDE == 0
    bf = jnp.bfloat16
    l = 0
    cosq_t, sinq_t, cosk, sink = _rope_tables(S)
    bias_c, tz = _bias_tables(t5_table, S)
    ovt = _overlap_t(S)
    epe = _kpe_placement()
    nc = S // CMP_STRIDE
    cw = CMP_STRIDE * NSA_DK
    row = lambda v: v.reshape(1, -1)

    w_row = _w_in_row_layout().apply(w_in[l])
    w_col = _w_in_col_layout().apply(w_in[l]).T
    w_q2 = _w_uq_layout().apply(mla_w_uq[l]).T
    w_k = _w_uk_layout().apply(mla_w_ukv[l])
    w_v = _w_uv_layout().apply(mla_w_ukv[l]).T
    (q_mla, k_mla, v_mla, q_nsa, kv_cmp, k_slc, v_slc, k_win, v_win, gates) = _proj(
        x, row(norm_mix_g[l]), w_row, w_col, row(mla_q_norm_g[l]), w_q2, row(mla_kv_norm_g[l]), w_k, w_v, epe,
        cosq_t, sinq_t, cosk, sink)

    kvc = kv_cmp.reshape(B, S, 2 * NSA_KV_HEADS, NSA_DK).transpose(0, 2, 1, 3).reshape(
        B, 2 * NSA_KV_HEADS, nc, cw)
    pos = jnp.stack([nsa_cmp_pos_k[l].reshape(2, 1, cw), nsa_cmp_pos_v[l].reshape(2, 1, cw)])
    w1 = jnp.stack([nsa_cmp_w1_k[l].reshape(2, cw, CMP_HIDDEN),
                    nsa_cmp_w1_v[l].reshape(2, cw, CMP_HIDDEN)]).astype(bf)
    w2k = jnp.concatenate([nsa_cmp_w2_k[l], jnp.zeros((CMP_HIDDEN, LANE - NSA_DK), jnp.float32)],
                          axis=1).astype(bf)
    w2vt = nsa_cmp_w2_v[l].T.astype(bf)
    k_cmp, v_cmp = _compress(kvc, pos, w1, w2k, w2vt)

    o_mla = _mla(q_mla, k_mla, v_mla)
    o_nsa = _nsa(q_nsa, k_slc, v_slc, k_win, v_win, k_cmp, v_cmp, gates, bias_c, tz, ovt)

    return _post(x, o_mla, o_nsa, row(out_norm_mla_g[l]), row(out_norm_nsa_g[l]), w_out[l].astype(bf),
                 row(norm_ffn_g[l]), w_gate[l].astype(bf), w_up[l].astype(bf), w_down[l].astype(bf),
                 row(final_norm_g))
```

```python
import functools
import math

import numpy as np
import jax
import jax.numpy as jnp
from jax import lax
from jax.experimental import pallas as pl
from jax.experimental.pallas import tpu as pltpu

D_MODEL = 1024
MLA_HEADS = 8
MLA_NOPE = 64
MLA_ROPE = 32
MLA_V = 64
MLA_Q_LORA = 256
MLA_KV_LORA = 128
ROPE_THETA = 10000.0
NSA_HEADS = 8
NSA_KV_HEADS = 2
NSA_GROUP = 4
NSA_DK = 64
CMP_LEN = 32
CMP_STRIDE = 16
CMP_HIDDEN = 128
SLC_LEN = 64
SLC_TOPN = 16
WINDOW = 512
T5_BUCKETS = 32
T5_MAX_DIST = 128
D_FF = 2816
EPS = 1e-6
NEG = -1e30
FORCE = 1e30

LANE = 128
SUBLANE = 8
SLC_ROWS = 64
GATE_ROWS = 16
ONES_ROWS = 16
LOG2E = math.log2(math.e)

PROJ_TM = 256
MLA_TQ = 1024
MLA_TK = 256
MLA_TRIP_KEYS = 1024
MLA_CW = 256
FLASH_LOOKAHEAD = 4
NSA_TQ = 256
NSA_TK = 256
NSA_FAR_GROUP = 4
POST_TM = 512
VMEM_LIMIT = 56 * 1024 * 1024

C_CQ = 0
C_CKV = 256
C_MISC = 384
C_KVCMP = 512
C_KSLC = 768
C_KWIN = 1024
D_ROW = 1280
R_QNSA = 0
R_VSLC = 512
R_VWIN = 640
R_GATE = 768
D_COL = 800


def _dot(a, b):
    return jnp.dot(a, b, preferred_element_type=jnp.float32)


def _dot_nt(a, b):
    return lax.dot_general(a, b, (((1,), (1,)), ((), ())), preferred_element_type=jnp.float32)


def _rms(x, g):
    return x * lax.rsqrt(jnp.mean(x * x, axis=-1, keepdims=True) + EPS) * g


def _const_spec(shape):
    nd = len(shape)
    return pl.BlockSpec(shape, lambda *_: (0,) * nd, pipeline_mode=pl.Buffered(1))


def _params(n_axes):
    return pltpu.CompilerParams(dimension_semantics=("arbitrary",) * n_axes,
                                vmem_limit_bytes=VMEM_LIMIT)


def _in_offsets():
    o_krope = MLA_Q_LORA + MLA_KV_LORA
    o_qnsa = o_krope + MLA_ROPE
    o_cmp = o_qnsa + NSA_HEADS * NSA_DK
    o_slc = o_cmp + 2 * NSA_KV_HEADS * NSA_DK
    o_win = o_slc + 2 * NSA_KV_HEADS * NSA_DK
    o_gate = o_win + 2 * NSA_KV_HEADS * NSA_DK
    return o_krope, o_qnsa, o_cmp, o_slc, o_win, o_gate


class _Layout:
    def __init__(self, n):
        self.src = np.zeros((n,), np.int32)
        self.mul = np.zeros((n,), np.float32)

    def put(self, dst, s, n, m=1.0):
        self.src[dst:dst + n] = np.arange(s, s + n)
        self.mul[dst:dst + n] = m

    def apply(self, w):
        return (w[:, self.src] * self.mul[None, :]).astype(jnp.bfloat16)


def _w_in_row_layout():
    o_krope, _, o_cmp, o_slc, o_win, _ = _in_offsets()
    lay = _Layout(D_ROW)
    lay.put(C_CQ, 0, MLA_Q_LORA)
    lay.put(C_CKV, MLA_Q_LORA, MLA_KV_LORA)
    half = MLA_ROPE // 2
    lay.put(C_MISC, o_krope, MLA_ROPE)
    lay.put(C_MISC + MLA_ROPE, o_krope + half, half, -1.0)
    lay.put(C_MISC + MLA_ROPE + half, o_krope, half, 1.0)
    lay.put(C_KVCMP, o_cmp, 2 * NSA_KV_HEADS * NSA_DK)
    for hk in range(NSA_KV_HEADS):
        lay.put(C_KSLC + hk * LANE, o_slc + hk * NSA_DK, NSA_DK)
        lay.put(C_KWIN + hk * LANE, o_win + hk * NSA_DK, NSA_DK)
    return lay


def _w_in_col_layout():
    _, o_qnsa, _, o_slc, o_win, o_gate = _in_offsets()
    lay = _Layout(D_COL)
    lay.put(R_QNSA, o_qnsa, NSA_HEADS * NSA_DK, NSA_DK ** -0.5)
    lay.put(R_VSLC, o_slc + NSA_KV_HEADS * NSA_DK, NSA_KV_HEADS * NSA_DK)
    lay.put(R_VWIN, o_win + NSA_KV_HEADS * NSA_DK, NSA_KV_HEADS * NSA_DK)
    for hk in range(NSA_KV_HEADS):
        lay.put(R_GATE + hk * GATE_ROWS, o_gate + hk * NSA_GROUP * 3, NSA_GROUP * 3)
    return lay


def _w_uq_layout():
    lay = _Layout(2 * MLA_HEADS * LANE)
    dq = MLA_NOPE + MLA_ROPE
    half = MLA_ROPE // 2
    for h in range(MLA_HEADS):
        lay.put(h * LANE, h * dq, dq)
        r = MLA_HEADS * LANE + h * LANE + MLA_NOPE
        pe = h * dq + MLA_NOPE
        lay.put(r, pe + half, half, -1.0)
        lay.put(r + half, pe, half, 1.0)
    return lay


def _w_uk_layout():
    lay = _Layout(MLA_HEADS * LANE)
    per = MLA_NOPE + MLA_V
    for h in range(MLA_HEADS):
        lay.put(h * LANE, h * per, MLA_NOPE)
    return lay


def _w_uv_layout():
    lay = _Layout(MLA_HEADS * MLA_V)
    per = MLA_NOPE + MLA_V
    for h in range(MLA_HEADS):
        lay.put(h * MLA_V, h * per + MLA_NOPE, MLA_V)
    return lay


def _kpe_placement():
    e = np.zeros((LANE, MLA_HEADS * LANE), np.float32)
    for h in range(MLA_HEADS):
        for r in range(MLA_ROPE):
            e[r, h * LANE + MLA_NOPE + r] = 1.0
    return jnp.asarray(e, jnp.bfloat16)


def _t5_thresholds(max_dist):
    n = np.arange(max_dist, dtype=np.int64)
    max_exact = T5_BUCKETS // 2
    nf = np.maximum(n, 1).astype(np.float32)
    val = (np.log(nf / np.float32(max_exact)) / np.float32(math.log(T5_MAX_DIST / max_exact))
           * np.float32(T5_BUCKETS - max_exact))
    large = np.minimum(max_exact + val.astype(np.int32), T5_BUCKETS - 1)
    bucket = np.where(n < max_exact, n, large)
    assert np.all(np.diff(bucket) >= 0)
    frac = np.abs(val[max_exact + 1:T5_MAX_DIST] - np.round(val[max_exact + 1:T5_MAX_DIST]))
    assert frac.min() > 1e-3
    return [int(np.argmax(bucket >= b)) for b in range(T5_BUCKETS)]


def _proj_kernel(x_ref, gmix_ref, wrow_ref, wcol_ref, gq_ref, wq_ref, gkv_ref, wk_ref, wv_ref, epe_ref,
                 cq_ref, sq_ref, ck_ref, sk_ref,
                 qm_ref, km_ref, vm_ref, qn_ref, kvc_ref, ks_ref, vs_ref, kw_ref, vw_ref, gate_ref):
    tm = x_ref.shape[1]
    bf = jnp.bfloat16
    xn = _rms(x_ref[0], gmix_ref[...]).astype(bf)
    u = _dot(xn, wrow_ref[...])
    ut = _dot_nt(wcol_ref[...], xn)

    cqn = _rms(u[:, C_CQ:C_CQ + MLA_Q_LORA], gq_ref[...]).astype(bf)
    qq = _dot_nt(wq_ref[...], cqn)
    cosq = cq_ref[...]
    sinq = sq_ref[...]
    hw = MLA_HEADS * LANE
    for h in range(MLA_HEADS):
        qh = qq[h * LANE:(h + 1) * LANE] * cosq + qq[hw + h * LANE:hw + (h + 1) * LANE] * sinq
        qm_ref[0, h] = qh.astype(bf)

    ckvn = _rms(u[:, C_CKV:C_CKV + MLA_KV_LORA], gkv_ref[...]).astype(bf)
    kk = _dot(ckvn, wk_ref[...])
    misc = u[:, C_MISC:C_MISC + LANE]
    kpe = misc * ck_ref[...] + pltpu.roll(misc, LANE - MLA_ROPE, axis=1) * sk_ref[...]
    kpe_all = _dot(kpe.astype(bf), epe_ref[...])
    for h in range(MLA_HEADS):
        km_ref[0, h] = (kk[:, h * LANE:(h + 1) * LANE] + kpe_all[:, h * LANE:(h + 1) * LANE]).astype(bf)
    vv = _dot_nt(wv_ref[...], ckvn)
    for h in range(MLA_HEADS):
        vm_ref[0, h] = vv[h * MLA_V:(h + 1) * MLA_V].astype(bf)

    for h in range(NSA_HEADS):
        qn_ref[0, h] = (ut[R_QNSA + h * NSA_DK:R_QNSA + (h + 1) * NSA_DK] * LOG2E).astype(bf)
    kvc_ref[0] = u[:, C_KVCMP:C_KVCMP + 2 * NSA_KV_HEADS * NSA_DK]
    pos = pl.program_id(1) * tm + lax.broadcasted_iota(jnp.int32, (tm, LANE), 0)
    lane = lax.broadcasted_iota(jnp.int32, (tm, LANE), 1)
    onehot = jnp.where(lane - (LANE - SLC_ROWS) == pos // SLC_LEN, 1.0, 0.0)
    for hk in range(NSA_KV_HEADS):
        ks_ref[0, hk] = (u[:, C_KSLC + hk * LANE:C_KSLC + (hk + 1) * LANE] + onehot).astype(bf)
        kw_ref[0, hk] = u[:, C_KWIN + hk * LANE:C_KWIN + (hk + 1) * LANE].astype(bf)
        vs_ref[0, hk] = ut[R_VSLC + hk * NSA_DK:R_VSLC + (hk + 1) * NSA_DK].astype(bf)
        vw_ref[0, hk] = ut[R_VWIN + hk * NSA_DK:R_VWIN + (hk + 1) * NSA_DK].astype(bf)
        gate_ref[0, hk] = jax.nn.sigmoid(ut[R_GATE + hk * GATE_ROWS:R_GATE + (hk + 1) * GATE_ROWS])


def _proj(x, gmix, w_row, w_col, gq, w_q2, gkv, w_k, w_v, epe, cosq_t, sinq_t, cosk, sink):
    B, S, D = x.shape
    tm = min(PROJ_TM, S)
    bf = jnp.bfloat16
    grid = (B, S // tm)
    tok_major = lambda n: pl.BlockSpec((1, n, tm, LANE), lambda b, i: (b, 0, i, 0))
    feat_major = lambda n, d: pl.BlockSpec((1, n, d, tm), lambda b, i: (b, 0, 0, i))
    out_shape = (
        jax.ShapeDtypeStruct((B, MLA_HEADS, LANE, S), bf),
        jax.ShapeDtypeStruct((B, MLA_HEADS, S, LANE), bf),
        jax.ShapeDtypeStruct((B, MLA_HEADS, MLA_V, S), bf),
        jax.ShapeDtypeStruct((B, NSA_HEADS, NSA_DK, S), bf),
        jax.ShapeDtypeStruct((B, S, 2 * NSA_KV_HEADS * NSA_DK), jnp.float32),
        jax.ShapeDtypeStruct((B, NSA_KV_HEADS, S, LANE), bf),
        jax.ShapeDtypeStruct((B, NSA_KV_HEADS, NSA_DK, S), bf),
        jax.ShapeDtypeStruct((B, NSA_KV_HEADS, S, LANE), bf),
        jax.ShapeDtypeStruct((B, NSA_KV_HEADS, NSA_DK, S), bf),
        jax.ShapeDtypeStruct((B, NSA_KV_HEADS, GATE_ROWS, S), jnp.float32),
    )
    out_specs = (
        feat_major(MLA_HEADS, LANE), tok_major(MLA_HEADS), feat_major(MLA_HEADS, MLA_V),
        feat_major(NSA_HEADS, NSA_DK),
        pl.BlockSpec((1, tm, 2 * NSA_KV_HEADS * NSA_DK), lambda b, i: (b, i, 0)),
        tok_major(NSA_KV_HEADS), feat_major(NSA_KV_HEADS, NSA_DK),
        tok_major(NSA_KV_HEADS), feat_major(NSA_KV_HEADS, NSA_DK),
        feat_major(NSA_KV_HEADS, GATE_ROWS),
    )
    consts = (gmix, w_row, w_col, gq, w_q2, gkv, w_k, w_v, epe)
    in_specs = ([pl.BlockSpec((1, tm, D), lambda b, i: (b, i, 0))] + [_const_spec(c.shape) for c in consts]
                + [pl.BlockSpec((LANE, tm), lambda b, i: (0, i)), pl.BlockSpec((LANE, tm), lambda b, i: (0, i)),
                   pl.BlockSpec((tm, LANE), lambda b, i: (i, 0)), pl.BlockSpec((tm, LANE), lambda b, i: (i, 0))])
    return pl.pallas_call(
        _proj_kernel, grid=grid, in_specs=in_specs, out_specs=out_specs, out_shape=out_shape,
        compiler_params=_params(2), name="proj",
    )(x, *consts, cosq_t, sinq_t, cosk, sink)


def _compress_kernel(c_ref, pos_ref, w1_ref, w2k_ref, w2vt_ref, kc_ref, vc_ref):
    nc = c_ref.shape[2]
    bf = jnp.bfloat16
    for kv in range(2):
        for hk in range(NSA_KV_HEADS):
            c = c_ref[0, kv * NSA_KV_HEADS + hk]
            a = _dot((c + pos_ref[kv, 0]).astype(bf), w1_ref[kv, 0])
            b = _dot((c + pos_ref[kv, 1]).astype(bf), w1_ref[kv, 1])
            hid = a + pltpu.roll(b, nc - 1, axis=0)
            act = jax.nn.gelu(hid).astype(bf)
            if kv == 0:
                kc_ref[0, hk] = _dot(act, w2k_ref[...]).astype(bf)
            else:
                vc_ref[0, hk] = _dot_nt(w2vt_ref[...], act).astype(bf)


def _compress(kvc, pos, w1, w2k, w2vt):
    B, _, nc, width = kvc.shape
    return pl.pallas_call(
        _compress_kernel, grid=(B,),
        in_specs=[pl.BlockSpec((1, 2 * NSA_KV_HEADS, nc, width), lambda b: (b, 0, 0, 0)),
                  _const_spec(pos.shape), _const_spec(w1.shape), _const_spec(w2k.shape),
                  _const_spec(w2vt.shape)],
        out_specs=(pl.BlockSpec((1, NSA_KV_HEADS, nc, LANE), lambda b: (b, 0, 0, 0)),
                   pl.BlockSpec((1, NSA_KV_HEADS, NSA_DK, nc), lambda b: (b, 0, 0, 0))),
        out_shape=(jax.ShapeDtypeStruct((B, NSA_KV_HEADS, nc, LANE), jnp.bfloat16),
                   jax.ShapeDtypeStruct((B, NSA_KV_HEADS, NSA_DK, nc), jnp.bfloat16)),
        compiler_params=_params(1), name="compress",
    )(kvc, pos, w1, w2k, w2vt)


def _t5_lookup(dist, tab_ref, head, thr):
    val = jnp.full(dist.shape, tab_ref[0, head], jnp.float32)
    for b in range(1, T5_BUCKETS):
        val = jnp.where(dist >= thr[b], tab_ref[b, head], val)
    return (val - tab_ref[T5_BUCKETS - 1, head]) * LOG2E


def _bias_cmp_kernel(tab_ref, out_ref, *, thr, tq, n_cmp):
    hk = pl.program_id(0)
    t = pl.program_id(1)
    nc = out_ref.shape[2]
    n = lax.broadcasted_iota(jnp.int32, (nc, tq), 0)
    i = lax.broadcasted_iota(jnp.int32, (nc, tq), 1)
    dist = t * tq + i - (n * CMP_STRIDE + CMP_LEN - 1)
    ok = jnp.logical_and(dist >= 0, n < n_cmp)
    for g in range(NSA_GROUP):
        val = _t5_lookup(dist, tab_ref, hk * NSA_GROUP + g, thr)
        out_ref[0, 0, :, g * tq:(g + 1) * tq] = jnp.where(ok, val, NEG)


def _bias_tile_kernel(tab_ref, out_ref, *, thr, tq):
    hk = pl.program_id(0)
    tk = out_ref.shape[2]
    j = lax.broadcasted_iota(jnp.int32, (tk, tq), 0)
    i = lax.broadcasted_iota(jnp.int32, (tk, tq), 1)
    for g in range(NSA_GROUP):
        head = hk * NSA_GROUP + g
        cols = slice(g * tq, (g + 1) * tq)
        d0 = i - j
        out_ref[0, 0, :, cols] = jnp.where(d0 >= 0, _t5_lookup(d0, tab_ref, head, thr), NEG)
        out_ref[0, 1, :, cols] = _t5_lookup(d0 + tk, tab_ref, head, thr)
        out_ref[0, 2, :, cols] = jnp.where(j > i, 0.0, NEG)
        out_ref[0, 3, :, cols] = jnp.full((tk, tq), NEG, jnp.float32)


def _bias_tables(t5_table, S):
    thr = _t5_thresholds(S)
    tq, tk = NSA_TQ, NSA_TK
    assert tq == tk and WINDOW == 2 * tk and tk >= T5_MAX_DIST
    nq = S // tq
    nc = S // CMP_STRIDE
    n_cmp = (S - CMP_LEN) // CMP_STRIDE + 1
    m = NSA_GROUP * tq
    smem = pl.BlockSpec(memory_space=pltpu.SMEM)
    bias_c = pl.pallas_call(
        functools.partial(_bias_cmp_kernel, thr=thr, tq=tq, n_cmp=n_cmp),
        grid=(NSA_KV_HEADS, nq), in_specs=[smem],
        out_specs=pl.BlockSpec((1, 1, nc, m), lambda h, t: (h, t, 0, 0)),
        out_shape=jax.ShapeDtypeStruct((NSA_KV_HEADS, nq, nc, m), jnp.float32),
        compiler_params=_params(2), name="bias_cmp",
    )(t5_table)
    tz = pl.pallas_call(
        functools.partial(_bias_tile_kernel, thr=thr, tq=tq),
        grid=(NSA_KV_HEADS,), in_specs=[smem],
        out_specs=pl.BlockSpec((1, 4, tk, m), lambda h: (h, 0, 0, 0)),
        out_shape=jax.ShapeDtypeStruct((NSA_KV_HEADS, 4, tk, m), jnp.float32),
        compiler_params=_params(1), name="bias_tile",
    )(t5_table)
    return bias_c, tz


def _flash_chains(chains, side_work=()):
    def scores(chain):
        k, q_t, _, bias_t = chain[:4]
        s = _dot(k, q_t)
        return s if bias_t is None else s + bias_t

    side_work = list(side_work)
    per_chain = -(-len(side_work) // len(chains))
    pending = [scores(c) for c in chains[:FLASH_LOOKAHEAD]]
    for i, chain in enumerate(chains):
        for thunk in side_work[i * per_chain:(i + 1) * per_chain]:
            thunk()
        s = pending.pop(0)
        if i + FLASH_LOOKAHEAD < len(chains):
            pending.append(scores(chains[i + FLASH_LOOKAHEAD]))
        v_aug, m_ref, acc_ref = chain[2], chain[4], chain[5]
        m_prev = m_ref[...]
        m_new = jnp.maximum(m_prev, jnp.max(s, axis=0, keepdims=True))
        alpha = jnp.exp2(m_prev - m_new)
        p = jnp.exp2(s - m_new).astype(jnp.bfloat16)
        acc_ref[...] = alpha * acc_ref[...] + _dot(v_aug, p)
        m_ref[...] = m_new


def _with_ones(v_t):
    tk = v_t.shape[1]
    row = lax.broadcasted_iota(jnp.int32, (ONES_ROWS, tk), 0)
    return jnp.concatenate([v_t, jnp.where(row == 0, 1.0, 0.0).astype(v_t.dtype)], axis=0)


def _col_chains(k, q_ref, v_aug, bias_ref, m_ref, acc_ref, ncol, width):
    chains = []
    for c0 in range(0, ncol, width):
        cols = slice(c0, c0 + width)
        chains.append((k, q_ref[:, cols], v_aug, None if bias_ref is None else bias_ref[:, cols],
                       m_ref.at[:, cols], acc_ref.at[:, cols]))
    return chains


def _flash_init(m_ref, acc_ref):
    m_ref[...] = jnp.full(m_ref.shape, -jnp.inf, jnp.float32)
    acc_ref[...] = jnp.zeros(acc_ref.shape, jnp.float32)


def _flash_result(acc_ref, dv):
    acc = acc_ref[...]
    return acc[0:dv] / acc[dv:dv + 1]


def _mla_kernel(q_ref, k_ref, v_ref, o_ref, m_ref, acc_ref, *, tq, tk, cw):
    S = k_ref.shape[2]
    nqt = S // tq
    kk = lax.broadcasted_iota(jnp.int32, (tk, cw), 0)
    qq = lax.broadcasted_iota(jnp.int32, (tk, cw), 1)

    def q_tile(qt, carry):
        q0 = pl.multiple_of(qt * tq, tq)
        for e in range(2):
            _flash_init(m_ref.at[e], acc_ref.at[e])

        def step(k0, key_off):
            chains = []
            for e in range(2):
                k = k_ref[0, e, pl.ds(k0, tk), :]
                v_aug = _with_ones(v_ref[0, e, :, pl.ds(k0, tk)])
                for c0 in range(0, tq, cw):
                    bias = None
                    if key_off is not None:
                        if key_off > c0 + cw - 1:
                            continue
                        if key_off + tk - 1 > c0:
                            bias = jnp.where(kk + key_off <= qq + c0, 0.0, NEG)
                    cols = slice(c0, c0 + cw)
                    chains.append((k, q_ref[0, e, :, pl.ds(pl.multiple_of(q0 + c0, cw), cw)], v_aug, bias,
                                   m_ref.at[e, :, cols], acc_ref.at[e, :, cols]))
            return chains

        per = MLA_TRIP_KEYS // tk

        def far(c, carry2):
            chains = []
            for j in range(per):
                chains += step(pl.multiple_of(c * MLA_TRIP_KEYS + j * tk, tk), None)
            _flash_chains(chains)
            return carry2

        lax.fori_loop(0, qt * (tq // MLA_TRIP_KEYS), far, 0)
        chains = []
        for j in range(tq // tk):
            chains += step(pl.multiple_of(q0 + j * tk, tk), j * tk)
        _flash_chains(chains)
        o_t = jnp.concatenate([_flash_result(acc_ref.at[e], MLA_V) for e in range(2)], axis=0)
        o_ref[0, pl.ds(q0, tq), :] = jnp.transpose(o_t)
        return carry

    lax.fori_loop(0, nqt, q_tile, 0)


def _mla(q_t, k, v_t):
    B, H, S, _ = k.shape
    tq = min(MLA_TQ, S)
    tk = min(MLA_TK, tq)
    cw = min(MLA_CW, tq)
    return pl.pallas_call(
        functools.partial(_mla_kernel, tq=tq, tk=tk, cw=cw),
        grid=(B, H // 2),
        in_specs=[pl.BlockSpec((1, 2, LANE, S), lambda b, p: (b, p, 0, 0)),
                  pl.BlockSpec((1, 2, S, LANE), lambda b, p: (b, p, 0, 0)),
                  pl.BlockSpec((1, 2, MLA_V, S), lambda b, p: (b, p, 0, 0))],
        out_specs=pl.BlockSpec((1, S, 2 * MLA_V), lambda b, p: (b, 0, p)),
        out_shape=jax.ShapeDtypeStruct((B, S, H * MLA_V), jnp.float32),
        scratch_shapes=[pltpu.VMEM((2, 1, tq), jnp.float32),
                        pltpu.VMEM((2, MLA_V + ONES_ROWS, tq), jnp.float32)],
        compiler_params=_params(2), name="mla",
    )(q_t, k, v_t)


def _nsa_kernel(q_ref, ks_ref, vs_ref, kw_ref, vw_ref, kc_ref, vc_ref, gate_ref, bc_ref, tz_ref, ovt_ref,
                o_ref, qaug_ref, qpad_ref, m_ref, acc_ref, mw_ref, accw_ref, oc_ref, imp_ref, *, tq, tk):
    t = pl.program_id(1)
    G = NSA_GROUP
    HK = range(NSA_KV_HEADS)
    ncol = G * tq
    bf = jnp.bfloat16
    q0 = t * tq
    for hk in HK:
        for g in range(G):
            qpad_ref[hk, 0:NSA_DK, g * tq:(g + 1) * tq] = q_ref[0, hk * G + g]
            qaug_ref[hk, 0:NSA_DK, g * tq:(g + 1) * tq] = q_ref[0, hk * G + g]
        qpad_ref[hk, NSA_DK:, :] = jnp.zeros((SLC_ROWS, ncol), bf)

    def tile_chains(k_ref, v_ref, qx_ref, c, bias_idx, stats):
        k0 = pl.multiple_of(c * tk, tk)
        chains = []
        for hk in HK:
            bias_ref = None if bias_idx is None else tz_ref.at[hk, bias_idx]
            chains += _col_chains(k_ref[0, hk, pl.ds(k0, tk), :], qx_ref.at[hk],
                                  _with_ones(v_ref[0, hk, :, pl.ds(k0, tk)]),
                                  bias_ref, stats[0].at[hk], stats[1].at[hk], ncol, tq)
        return chains

    prev1 = jnp.maximum(t - 1, 0)
    bias1 = jnp.where(t >= 1, 1, 3)

    nb = ovt_ref.shape[0]
    sub = SUBLANE
    ci = lax.broadcasted_iota(jnp.int32, (1, ncol), 1)
    col_ok = q0 + ci % tq >= CMP_LEN - 1
    jb = lax.broadcasted_iota(jnp.int32, (nb, tq), 0)
    cur = (q0 + lax.broadcasted_iota(jnp.int32, (nb, tq), 1)) // SLC_LEN
    forced = jnp.logical_or(jb == 0, jnp.logical_or(jb == cur, jb == cur - 1))
    jsub = lax.broadcasted_iota(jnp.int32, (sub, tq), 0)
    for hk in HK:
        s = _dot(kc_ref[0, hk], qpad_ref[hk]) + bc_ref[hk, 0]
        mx = jnp.max(s, axis=0, keepdims=True)
        p = jnp.exp2(s - mx)
        lsum = jnp.sum(p, axis=0, keepdims=True)
        pcb = (p * jnp.where(col_ok, 1.0 / lsum, 0.0)).astype(bf)
        both = _dot(jnp.concatenate([vc_ref[0, hk], ovt_ref[...]], axis=0), pcb)
        oc_ref[hk] = both[0:NSA_DK]
        imp = both[NSA_DK:, 0:tq]
        for g in range(1, G):
            imp = imp + both[NSA_DK:, g * tq:(g + 1) * tq]
        imp_ref[hk] = jnp.where(forced, FORCE, jnp.where(jb <= cur, imp, NEG))

    win = (mw_ref, accw_ref)
    _flash_init(*win)
    _flash_chains(tile_chains(kw_ref, vw_ref, qpad_ref, jnp.maximum(t - 2, 0), jnp.where(t >= 2, 2, 3), win)
                  + tile_chains(kw_ref, vw_ref, qpad_ref, prev1, bias1, win)
                  + tile_chains(kw_ref, vw_ref, qpad_ref, t, 0, win))

    tiles_per_group = 16 * SLC_LEN // tq
    n_groups = nb // 16

    def write_mask(hk, selb_rows):
        selb = jnp.concatenate(selb_rows, axis=0).astype(bf)
        for g in range(G):
            qaug_ref[hk, NSA_DK:, g * tq:(g + 1) * tq] = selb

    def select(n_act):
        tail = [jnp.full((nb - n_act, tq), NEG, jnp.float32)] if n_act < nb else []
        for hk in HK:
            imp = imp_ref[hk, 0:n_act, :]
            if n_act <= SLC_TOPN:
                write_mask(hk, [jnp.where(imp > 0.5 * NEG, 0.0, NEG)] + tail)
                continue
            slabs = [imp[r0 * sub:(r0 + 1) * sub] for r0 in range(n_act // sub)]
            ranks = [jnp.zeros((sub, tq), jnp.int32) for _ in slabs]
            for jp in range(n_act):
                rowv = imp[jp:jp + 1, :]
                for r0, slab in enumerate(slabs):
                    lo = r0 * sub
                    if lo > jp:
                        one = jnp.where(rowv >= slab, 1, 0)
                    elif lo + sub - 1 <= jp:
                        one = jnp.where(rowv > slab, 1, 0)
                    else:
                        one = jnp.where(jsub + lo > jp, jnp.where(rowv >= slab, 1, 0),
                                        jnp.where(rowv > slab, 1, 0))
                    ranks[r0] = ranks[r0] + one
            write_mask(hk, [jnp.where(r < SLC_TOPN, 0.0, NEG) for r in ranks] + tail)

    for grp in range(n_groups):
        pl.when(t // tiles_per_group == grp)(functools.partial(select, 16 * (grp + 1)))

    slc = (m_ref, acc_ref)
    _flash_init(*slc)
    n_far = jnp.maximum(t - 1, 0)

    done = 0
    size = NSA_FAR_GROUP
    while size >= 1:
        trips = (n_far - done) // size

        def far(c, carry, size=size, done=done):
            chains = []
            for j in range(size):
                chains += tile_chains(ks_ref, vs_ref, qaug_ref, done + c * size + j, None, slc)
            _flash_chains(chains)
            return carry

        lax.fori_loop(0, trips, far, 0)
        done = done + trips * size
        size //= 2
    _flash_chains(tile_chains(ks_ref, vs_ref, qaug_ref, prev1, bias1, slc)
                  + tile_chains(ks_ref, vs_ref, qaug_ref, t, 0, slc))

    for hk in HK:
        o_s = _flash_result(acc_ref.at[hk], NSA_DK)
        o_w = _flash_result(accw_ref.at[hk], NSA_DK)
        o_c = oc_ref[hk]
        gate = gate_ref[0, hk]
        comb = []
        for g in range(G):
            cols = slice(g * tq, (g + 1) * tq)
            gr = lambda br: gate[g * 3 + br:g * 3 + br + 1, :]
            comb.append(gr(0) * o_c[:, cols] + gr(1) * o_s[:, cols] + gr(2) * o_w[:, cols])
        width = G * NSA_DK
        o_ref[0, :, hk * width:(hk + 1) * width] = jnp.transpose(jnp.concatenate(comb, axis=0))


def _nsa(q_t, ks, vs_t, kw, vw_t, kc, vc_t, gate_t, bias_c, tz, ovt):
    B, _, S, _ = ks.shape
    tq, tk = NSA_TQ, NSA_TK
    nq = S // tq
    nc = kc.shape[2]
    m = NSA_GROUP * tq
    hkv = NSA_KV_HEADS
    tok = lambda: pl.BlockSpec((1, hkv, S, LANE), lambda b, t: (b, 0, 0, 0))
    feat = lambda: pl.BlockSpec((1, hkv, NSA_DK, S), lambda b, t: (b, 0, 0, 0))
    return pl.pallas_call(
        functools.partial(_nsa_kernel, tq=tq, tk=tk),
        grid=(B, nq),
        in_specs=[pl.BlockSpec((1, NSA_HEADS, NSA_DK, tq), lambda b, t: (b, 0, 0, t)),
                  tok(), feat(), tok(), feat(),
                  pl.BlockSpec((1, hkv, nc, LANE), lambda b, t: (b, 0, 0, 0)),
                  pl.BlockSpec((1, hkv, NSA_DK, nc), lambda b, t: (b, 0, 0, 0)),
                  pl.BlockSpec((1, hkv, GATE_ROWS, tq), lambda b, t: (b, 0, 0, t)),
                  pl.BlockSpec((hkv, 1, nc, m), lambda b, t: (0, t, 0, 0)),
                  _const_spec(tz.shape),
                  _const_spec(ovt.shape)],
        out_specs=pl.BlockSpec((1, tq, NSA_HEADS * NSA_DK), lambda b, t: (b, t, 0)),
        out_shape=jax.ShapeDtypeStruct((B, S, NSA_HEADS * NSA_DK), jnp.float32),
        scratch_shapes=[pltpu.VMEM((hkv, NSA_DK + SLC_ROWS, m), jnp.bfloat16),
                        pltpu.VMEM((hkv, NSA_DK + SLC_ROWS, m), jnp.bfloat16),
                        pltpu.VMEM((hkv, 1, m), jnp.float32),
                        pltpu.VMEM((hkv, NSA_DK + ONES_ROWS, m), jnp.float32),
                        pltpu.VMEM((hkv, 1, m), jnp.float32),
                        pltpu.VMEM((hkv, NSA_DK + ONES_ROWS, m), jnp.float32),
                        pltpu.VMEM((hkv, NSA_DK, m), jnp.float32),
                        pltpu.VMEM((hkv, SLC_ROWS, tq), jnp.float32)],
        compiler_params=_params(2), name="nsa",
    )(q_t, ks, vs_t, kw, vw_t, kc, vc_t, gate_t, bias_c, tz, ovt)


def _post_kernel(x_ref, om_ref, on_ref, gm_ref, gn_ref, wo_ref, gf_ref, wg_ref, wu_ref, wd_ref,
                 gfin_ref, o_ref):
    half = om_ref.shape[2]
    mix_m = _rms(om_ref[0], gm_ref[...]).astype(jnp.bfloat16)
    mix_n = _rms(on_ref[0], gn_ref[...]).astype(jnp.bfloat16)
    h = x_ref[0] + _dot(mix_m, wo_ref[0:half, :]) + _dot(mix_n, wo_ref[half:2 * half, :])
    f = _rms(h, gf_ref[...]).astype(jnp.bfloat16)
    a = _dot(f, wg_ref[...])
    act = (a * jax.nn.sigmoid(a) * _dot(f, wu_ref[...])).astype(jnp.bfloat16)
    h = h + _dot(act, wd_ref[...])
    o_ref[0] = _rms(h, gfin_ref[...])


def _post(x, o_mla, o_nsa, gm, gn, w_out, gf, wg, wu, wd, gfin):
    B, S, D = x.shape
    tm = min(POST_TM, S)
    tok = lambda w: pl.BlockSpec((1, tm, w), lambda b, i: (b, i, 0))
    consts = (gm, gn, w_out, gf, wg, wu, wd, gfin)
    return pl.pallas_call(
        _post_kernel, grid=(B, S // tm),
        in_specs=[tok(D), tok(o_mla.shape[2]), tok(o_nsa.shape[2])] + [_const_spec(c.shape) for c in consts],
        out_specs=tok(D), out_shape=jax.ShapeDtypeStruct((B, S, D), jnp.float32),
        compiler_params=_params(2), name="post",
    )(x, o_mla, o_nsa, *consts)


def _rope_tables(S):
    pos = jnp.arange(S, dtype=jnp.float32)
    inv = ROPE_THETA ** (-jnp.arange(0, MLA_ROPE, 2, dtype=jnp.float32) / MLA_ROPE)
    ang = pos[:, None] * inv[None, :]
    cos, sin = jnp.cos(ang), jnp.sin(ang)
    cos2 = jnp.concatenate([cos, cos], axis=-1)
    sin2 = jnp.concatenate([sin, sin], axis=-1)
    scale = (MLA_NOPE + MLA_ROPE) ** -0.5 * LOG2E
    zq = jnp.zeros((S, LANE - MLA_NOPE - MLA_ROPE), jnp.float32)
    cosq = jnp.concatenate([jnp.ones((S, MLA_NOPE), jnp.float32), cos2, zq], axis=-1) * scale
    sinq = jnp.concatenate([jnp.zeros((S, MLA_NOPE), jnp.float32), sin2, zq], axis=-1) * scale
    zk = jnp.zeros((S, LANE - MLA_ROPE), jnp.float32)
    cosk = jnp.concatenate([cos2, zk], axis=-1)
    sink = jnp.concatenate([sin2, zk], axis=-1)
    return cosq.T, sinq.T, cosk, sink


def _overlap_t(S):
    n_cmp = (S - CMP_LEN) // CMP_STRIDE + 1
    n_slc = S // SLC_LEN
    assert n_slc <= SLC_ROWS
    cs = np.arange(n_cmp) * CMP_STRIDE
    ss = np.arange(n_slc) * SLC_LEN
    ov = np.maximum(0, np.minimum(cs[:, None] + CMP_LEN, ss[None, :] + SLC_LEN)
                    - np.maximum(cs[:, None], ss[None, :])).astype(np.float32) / CMP_STRIDE
    out = np.zeros((SLC_ROWS, S // CMP_STRIDE), np.float32)
    out[:n_slc, :n_cmp] = ov.T
    return jnp.asarray(out, jnp.bfloat16)


def kernel(x, norm_mix_g, w_in, mla_q_norm_g, mla_w_uq, mla_kv_norm_g, mla_w_ukv, nsa_cmp_pos_k, nsa_cmp_w1_k, nsa_cmp_w2_k, nsa_cmp_pos_v, nsa_cmp_w1_v, nsa_cmp_w2_v, t5_table, out_norm_mla_g, out_norm_nsa_g, w_out, norm_ffn_g, w_gate, w_up, w_down, final_norm_g):
    B, S, D = x.shape
    assert w_in.shape[0] == 1
    assert D == D_MODEL and S % NSA_TQ == 0 and S % CMP_STRIDE == 0
    bf = jnp.bfloat16
    l = 0
    cosq_t, sinq_t, cosk, sink = _rope_tables(S)
    bias_c, tz = _bias_tables(t5_table, S)
    ovt = _overlap_t(S)
    epe = _kpe_placement()
    nc = S // CMP_STRIDE
    cw = CMP_STRIDE * NSA_DK
    row = lambda v: v.reshape(1, -1)

    w_row = _w_in_row_layout().apply(w_in[l])
    w_col = _w_in_col_layout().apply(w_in[l]).T
    w_q2 = _w_uq_layout().apply(mla_w_uq[l]).T
    w_k = _w_uk_layout().apply(mla_w_ukv[l])
    w_v = _w_uv_layout().apply(mla_w_ukv[l]).T
    (q_mla, k_mla, v_mla, q_nsa, kv_cmp, k_slc, v_slc, k_win, v_win, gates) = _proj(
        x, row(norm_mix_g[l]), w_row, w_col, row(mla_q_norm_g[l]), w_q2, row(mla_kv_norm_g[l]), w_k, w_v, epe,
        cosq_t, sinq_t, cosk, sink)

    kvc = kv_cmp.reshape(B, S, 2 * NSA_KV_HEADS, NSA_DK).transpose(0, 2, 1, 3).reshape(
        B, 2 * NSA_KV_HEADS, nc, cw)
    pos = jnp.stack([nsa_cmp_pos_k[l].reshape(2, 1, cw), nsa_cmp_pos_v[l].reshape(2, 1, cw)])
    w1 = jnp.stack([nsa_cmp_w1_k[l].reshape(2, cw, CMP_HIDDEN),
                    nsa_cmp_w1_v[l].reshape(2, cw, CMP_HIDDEN)]).astype(bf)
    w2k = jnp.concatenate([nsa_cmp_w2_k[l], jnp.zeros((CMP_HIDDEN, LANE - NSA_DK), jnp.float32)],
                          axis=1).astype(bf)
    w2vt = nsa_cmp_w2_v[l].T.astype(bf)
    k_cmp, v_cmp = _compress(kvc, pos, w1, w2k, w2vt)

    o_mla = _mla(q_mla, k_mla, v_mla)
    o_nsa = _nsa(q_nsa, k_slc, v_slc, k_win, v_win, k_cmp, v_cmp, gates, bias_c, tz, ovt)

    return _post(x, o_mla, o_nsa, row(out_norm_mla_g[l]), row(out_norm_nsa_g[l]), w_out[l].astype(bf),
                 row(norm_ffn_g[l]), w_gate[l].astype(bf), w_up[l].astype(bf), w_down[l].astype(bf),
                 row(final_norm_g))
```

```python
import functools
import math

import numpy as np
import jax
import jax.numpy as jnp
from jax import lax
from jax.experimental import pallas as pl
from jax.experimental.pallas import tpu as pltpu

D_MODEL = 1024
MLA_HEADS = 8
MLA_NOPE = 64
MLA_ROPE = 32
MLA_V = 64
MLA_Q_LORA = 256
MLA_KV_LORA = 128
ROPE_THETA = 10000.0
NSA_HEADS = 8
NSA_KV_HEADS = 2
NSA_GROUP = 4
NSA_DK = 64
CMP_LEN = 32
CMP_STRIDE = 16
CMP_HIDDEN = 128
SLC_LEN = 64
SLC_TOPN = 16
WINDOW = 512
T5_BUCKETS = 32
T5_MAX_DIST = 128
D_FF = 2816
EPS = 1e-6
NEG = -1e30
FORCE = 1e30

LANE = 128
SUBLANE = 8
SLC_ROWS = 64
GATE_ROWS = 16
ONES_ROWS = 16
LOG2E = math.log2(math.e)

PROJ_TM = 256
MLA_TQ = 1024
MLA_TK = 256
MLA_FAR_TK = 256
MLA_TRIP_KEYS = 1024
MLA_CW = 256
FLASH_LOOKAHEAD = 4
NSA_TQ = 256
NSA_TK = 256
NSA_FAR_GROUP = 4
POST_TM = 512
VMEM_LIMIT = 56 * 1024 * 1024

C_CQ = 0
C_CKV = 256
C_MISC = 384
C_KVCMP = 512
C_KSLC = 768
C_KWIN = 1024
D_ROW = 1280
R_QNSA = 0
R_VSLC = 512
R_VWIN = 640
R_GATE = 768
D_COL = 800


def _dot(a, b):
    return jnp.dot(a, b, preferred_element_type=jnp.float32)


def _dot_nt(a, b):
    return lax.dot_general(a, b, (((1,), (1,)), ((), ())), preferred_element_type=jnp.float32)


def _rms(x, g):
    return x * lax.rsqrt(jnp.mean(x * x, axis=-1, keepdims=True) + EPS) * g


def _const_spec(shape):
    nd = len(shape)
    return pl.BlockSpec(shape, lambda *_: (0,) * nd, pipeline_mode=pl.Buffered(1))


def _params(n_axes):
    return pltpu.CompilerParams(dimension_semantics=("arbitrary",) * n_axes,
                                vmem_limit_bytes=VMEM_LIMIT)


def _in_offsets():
    o_krope = MLA_Q_LORA + MLA_KV_LORA
    o_qnsa = o_krope + MLA_ROPE
    o_cmp = o_qnsa + NSA_HEADS * NSA_DK
    o_slc = o_cmp + 2 * NSA_KV_HEADS * NSA_DK
    o_win = o_slc + 2 * NSA_KV_HEADS * NSA_DK
    o_gate = o_win + 2 * NSA_KV_HEADS * NSA_DK
    return o_krope, o_qnsa, o_cmp, o_slc, o_win, o_gate


class _Layout:
    def __init__(self, n):
        self.src = np.zeros((n,), np.int32)
        self.mul = np.zeros((n,), np.float32)

    def put(self, dst, s, n, m=1.0):
        self.src[dst:dst + n] = np.arange(s, s + n)
        self.mul[dst:dst + n] = m

    def apply(self, w):
        return (w[:, self.src] * self.mul[None, :]).astype(jnp.bfloat16)


def _w_in_row_layout():
    o_krope, _, o_cmp, o_slc, o_win, _ = _in_offsets()
    lay = _Layout(D_ROW)
    lay.put(C_CQ, 0, MLA_Q_LORA)
    lay.put(C_CKV, MLA_Q_LORA, MLA_KV_LORA)
    half = MLA_ROPE // 2
    lay.put(C_MISC + MLA_NOPE, o_krope, MLA_ROPE)
    lay.put(C_MISC + MLA_NOPE + MLA_ROPE, o_krope + half, half, -1.0)
    lay.put(C_MISC + MLA_NOPE + MLA_ROPE + half, o_krope, half, 1.0)
    lay.put(C_KVCMP, o_cmp, 2 * NSA_KV_HEADS * NSA_DK)
    for hk in range(NSA_KV_HEADS):
        lay.put(C_KSLC + hk * LANE, o_slc + hk * NSA_DK, NSA_DK)
        lay.put(C_KWIN + hk * LANE, o_win + hk * NSA_DK, NSA_DK)
    return lay


def _w_in_col_layout():
    _, o_qnsa, _, o_slc, o_win, o_gate = _in_offsets()
    lay = _Layout(D_COL)
    lay.put(R_QNSA, o_qnsa, NSA_HEADS * NSA_DK, NSA_DK ** -0.5)
    lay.put(R_VSLC, o_slc + NSA_KV_HEADS * NSA_DK, NSA_KV_HEADS * NSA_DK)
    lay.put(R_VWIN, o_win + NSA_KV_HEADS * NSA_DK, NSA_KV_HEADS * NSA_DK)
    for hk in range(NSA_KV_HEADS):
        lay.put(R_GATE + hk * GATE_ROWS, o_gate + hk * NSA_GROUP * 3, NSA_GROUP * 3)
    return lay


def _w_uq_layout():
    lay = _Layout(MLA_HEADS * LANE)
    dq = MLA_NOPE + MLA_ROPE
    for h in range(MLA_HEADS):
        lay.put(h * LANE, h * dq, dq)
    return lay


def _w_uk_layout():
    lay = _Layout(MLA_HEADS * LANE)
    per = MLA_NOPE + MLA_V
    for h in range(MLA_HEADS):
        lay.put(h * LANE, h * per, MLA_NOPE)
    return lay


def _w_uv_layout():
    lay = _Layout(MLA_HEADS * MLA_V)
    per = MLA_NOPE + MLA_V
    for h in range(MLA_HEADS):
        lay.put(h * MLA_V, h * per + MLA_NOPE, MLA_V)
    return lay


def _t5_thresholds(max_dist):
    n = np.arange(max_dist, dtype=np.int64)
    max_exact = T5_BUCKETS // 2
    nf = np.maximum(n, 1).astype(np.float32)
    val = (np.log(nf / np.float32(max_exact)) / np.float32(math.log(T5_MAX_DIST / max_exact))
           * np.float32(T5_BUCKETS - max_exact))
    large = np.minimum(max_exact + val.astype(np.int32), T5_BUCKETS - 1)
    bucket = np.where(n < max_exact, n, large)
    assert np.all(np.diff(bucket) >= 0)
    frac = np.abs(val[max_exact + 1:T5_MAX_DIST] - np.round(val[max_exact + 1:T5_MAX_DIST]))
    assert frac.min() > 1e-3
    return [int(np.argmax(bucket >= b)) for b in range(T5_BUCKETS)]


def _proj_kernel(x_ref, gmix_ref, wrow_ref, wcol_ref, gq_ref, wq_ref, gkv_ref, wk_ref, wv_ref,
                 cq_ref, sq_ref, ck_ref, sk_ref,
                 qm_ref, km_ref, vm_ref, qn_ref, kvc_ref, ks_ref, vs_ref, kw_ref, vw_ref, gate_ref):
    tm = x_ref.shape[1]
    bf = jnp.bfloat16
    xn = _rms(x_ref[0], gmix_ref[...]).astype(bf)
    u = _dot(xn, wrow_ref[...])
    ut = _dot_nt(wcol_ref[...], xn)

    cqn = _rms(u[:, C_CQ:C_CQ + MLA_Q_LORA], gq_ref[...]).astype(bf)
    qq = _dot_nt(wq_ref[...], cqn)
    cosq = cq_ref[...]
    sinq = sq_ref[...]
    r0, r1, r2 = MLA_NOPE, MLA_NOPE + MLA_ROPE // 2, MLA_NOPE + MLA_ROPE
    for h in range(MLA_HEADS):
        q = qq[h * LANE:(h + 1) * LANE]
        rot = jnp.concatenate([q[0:r0], -q[r1:r2], q[r0:r1], q[r2:]], axis=0)
        qm_ref[0, h] = (q * cosq + rot * sinq).astype(bf)

    ckvn = _rms(u[:, C_CKV:C_CKV + MLA_KV_LORA], gkv_ref[...]).astype(bf)
    kk = _dot(ckvn, wk_ref[...])
    misc = u[:, C_MISC:C_MISC + LANE]
    kpe = misc * ck_ref[...] + pltpu.roll(misc, LANE - MLA_ROPE, axis=1) * sk_ref[...]
    for h in range(MLA_HEADS):
        km_ref[0, h] = (kk[:, h * LANE:(h + 1) * LANE] + kpe).astype(bf)
    vv = _dot_nt(wv_ref[...], ckvn)
    for h in range(MLA_HEADS):
        vm_ref[0, h] = vv[h * MLA_V:(h + 1) * MLA_V].astype(bf)

    for h in range(NSA_HEADS):
        qn_ref[0, h] = (ut[R_QNSA + h * NSA_DK:R_QNSA + (h + 1) * NSA_DK] * LOG2E).astype(bf)
    for j in range(2 * NSA_KV_HEADS):
        chunk = u[:, C_KVCMP + (j // 2) * LANE:C_KVCMP + (j // 2 + 1) * LANE]
        if j % 2:
            chunk = pltpu.roll(chunk, LANE - NSA_DK, axis=1)
        kvc_ref[0, j] = chunk[:, 0:NSA_DK]
    pos = pl.program_id(1) * tm + lax.broadcasted_iota(jnp.int32, (tm, LANE), 0)
    lane = lax.broadcasted_iota(jnp.int32, (tm, LANE), 1)
    onehot = jnp.where(lane - (LANE - SLC_ROWS) == pos // SLC_LEN, 1.0, 0.0)
    for hk in range(NSA_KV_HEADS):
        ks_ref[0, hk] = (u[:, C_KSLC + hk * LANE:C_KSLC + (hk + 1) * LANE] + onehot).astype(bf)
        kw_ref[0, hk] = u[:, C_KWIN + hk * LANE:C_KWIN + (hk + 1) * LANE].astype(bf)
        vs_ref[0, hk] = ut[R_VSLC + hk * NSA_DK:R_VSLC + (hk + 1) * NSA_DK].astype(bf)
        vw_ref[0, hk] = ut[R_VWIN + hk * NSA_DK:R_VWIN + (hk + 1) * NSA_DK].astype(bf)
        gate_ref[0, hk] = jax.nn.sigmoid(ut[R_GATE + hk * GATE_ROWS:R_GATE + (hk + 1) * GATE_ROWS])


def _proj(x, gmix, w_row, w_col, gq, w_q2, gkv, w_k, w_v, cosq_t, sinq_t, cosk, sink):
    B, S, D = x.shape
    tm = min(PROJ_TM, S)
    bf = jnp.bfloat16
    grid = (B, S // tm)
    tok_major = lambda n: pl.BlockSpec((1, n, tm, LANE), lambda b, i: (b, 0, i, 0))
    feat_major = lambda n, d: pl.BlockSpec((1, n, d, tm), lambda b, i: (b, 0, 0, i))
    out_shape = (
        jax.ShapeDtypeStruct((B, MLA_HEADS, LANE, S), bf),
        jax.ShapeDtypeStruct((B, MLA_HEADS, S, LANE), bf),
        jax.ShapeDtypeStruct((B, MLA_HEADS, MLA_V, S), bf),
        jax.ShapeDtypeStruct((B, NSA_HEADS, NSA_DK, S), bf),
        jax.ShapeDtypeStruct((B, 2 * NSA_KV_HEADS, S, NSA_DK), jnp.float32),
        jax.ShapeDtypeStruct((B, NSA_KV_HEADS, S, LANE), bf),
        jax.ShapeDtypeStruct((B, NSA_KV_HEADS, NSA_DK, S), bf),
        jax.ShapeDtypeStruct((B, NSA_KV_HEADS, S, LANE), bf),
        jax.ShapeDtypeStruct((B, NSA_KV_HEADS, NSA_DK, S), bf),
        jax.ShapeDtypeStruct((B, NSA_KV_HEADS, GATE_ROWS, S), jnp.float32),
    )
    out_specs = (
        feat_major(MLA_HEADS, LANE), tok_major(MLA_HEADS), feat_major(MLA_HEADS, MLA_V),
        feat_major(NSA_HEADS, NSA_DK),
        pl.BlockSpec((1, 2 * NSA_KV_HEADS, tm, NSA_DK), lambda b, i: (b, 0, i, 0)),
        tok_major(NSA_KV_HEADS), feat_major(NSA_KV_HEADS, NSA_DK),
        tok_major(NSA_KV_HEADS), feat_major(NSA_KV_HEADS, NSA_DK),
        feat_major(NSA_KV_HEADS, GATE_ROWS),
    )
    consts = (gmix, w_row, w_col, gq, w_q2, gkv, w_k, w_v)
    in_specs = ([pl.BlockSpec((1, tm, D), lambda b, i: (b, i, 0))] + [_const_spec(c.shape) for c in consts]
                + [pl.BlockSpec((LANE, tm), lambda b, i: (0, i)), pl.BlockSpec((LANE, tm), lambda b, i: (0, i)),
                   pl.BlockSpec((tm, LANE), lambda b, i: (i, 0)), pl.BlockSpec((tm, LANE), lambda b, i: (i, 0))])
    return pl.pallas_call(
        _proj_kernel, grid=grid, in_specs=in_specs, out_specs=out_specs, out_shape=out_shape,
        compiler_params=_params(2), name="proj",
    )(x, *consts, cosq_t, sinq_t, cosk, sink)


def _compress_kernel(c_ref, pos_ref, w1_ref, w2k_ref, w2vt_ref, kc_ref, vc_ref):
    nc = c_ref.shape[2]
    bf = jnp.bfloat16
    for kv in range(2):
        for hk in range(NSA_KV_HEADS):
            c = c_ref[0, kv * NSA_KV_HEADS + hk]
            a = _dot((c + pos_ref[kv, 0]).astype(bf), w1_ref[kv, 0])
            b = _dot((c + pos_ref[kv, 1]).astype(bf), w1_ref[kv, 1])
            hid = a + pltpu.roll(b, nc - 1, axis=0)
            act = jax.nn.gelu(hid).astype(bf)
            if kv == 0:
                kc_ref[0, hk] = _dot(act, w2k_ref[...]).astype(bf)
            else:
                vc_ref[0, hk] = _dot_nt(w2vt_ref[...], act).astype(bf)


def _compress(kvc, pos, w1, w2k, w2vt):
    B, _, nc, width = kvc.shape
    return pl.pallas_call(
        _compress_kernel, grid=(B,),
        in_specs=[pl.BlockSpec((1, 2 * NSA_KV_HEADS, nc, width), lambda b: (b, 0, 0, 0)),
                  _const_spec(pos.shape), _const_spec(w1.shape), _const_spec(w2k.shape),
                  _const_spec(w2vt.shape)],
        out_specs=(pl.BlockSpec((1, NSA_KV_HEADS, nc, LANE), lambda b: (b, 0, 0, 0)),
                   pl.BlockSpec((1, NSA_KV_HEADS, NSA_DK, nc), lambda b: (b, 0, 0, 0))),
        out_shape=(jax.ShapeDtypeStruct((B, NSA_KV_HEADS, nc, LANE), jnp.bfloat16),
                   jax.ShapeDtypeStruct((B, NSA_KV_HEADS, NSA_DK, nc), jnp.bfloat16)),
        compiler_params=_params(1), name="compress",
    )(kvc, pos, w1, w2k, w2vt)


def _t5_lookup(dist, tab_ref, head, thr):
    val = jnp.full(dist.shape, tab_ref[0, head], jnp.float32)
    for b in range(1, T5_BUCKETS):
        val = jnp.where(dist >= thr[b], tab_ref[b, head], val)
    return (val - tab_ref[T5_BUCKETS - 1, head]) * LOG2E


def _bias_cmp_kernel(tab_ref, out_ref, *, thr, tq, n_cmp):
    hk = pl.program_id(0)
    t = pl.program_id(1)
    nc = out_ref.shape[2]
    n = lax.broadcasted_iota(jnp.int32, (nc, tq), 0)
    i = lax.broadcasted_iota(jnp.int32, (nc, tq), 1)
    dist = t * tq + i - (n * CMP_STRIDE + CMP_LEN - 1)
    ok = jnp.logical_and(dist >= 0, n < n_cmp)
    for g in range(NSA_GROUP):
        val = _t5_lookup(dist, tab_ref, hk * NSA_GROUP + g, thr)
        out_ref[0, 0, :, g * tq:(g + 1) * tq] = jnp.where(ok, val, NEG)


def _bias_tile_kernel(tab_ref, out_ref, *, thr, tq):
    hk = pl.program_id(0)
    tk = out_ref.shape[2]
    j = lax.broadcasted_iota(jnp.int32, (tk, tq), 0)
    i = lax.broadcasted_iota(jnp.int32, (tk, tq), 1)
    for g in range(NSA_GROUP):
        head = hk * NSA_GROUP + g
        cols = slice(g * tq, (g + 1) * tq)
        d0 = i - j
        out_ref[0, 0, :, cols] = jnp.where(d0 >= 0, _t5_lookup(d0, tab_ref, head, thr), NEG)
        out_ref[0, 1, :, cols] = _t5_lookup(d0 + tk, tab_ref, head, thr)
        out_ref[0, 2, :, cols] = jnp.where(j > i, 0.0, NEG)
        out_ref[0, 3, :, cols] = jnp.full((tk, tq), NEG, jnp.float32)


def _bias_tables(t5_table, S):
    thr = _t5_thresholds(S)
    tq, tk = NSA_TQ, NSA_TK
    assert tq == tk and WINDOW == 2 * tk and tk >= T5_MAX_DIST
    nq = S // tq
    nc = S // CMP_STRIDE
    n_cmp = (S - CMP_LEN) // CMP_STRIDE + 1
    m = NSA_GROUP * tq
    smem = pl.BlockSpec(memory_space=pltpu.SMEM)
    bias_c = pl.pallas_call(
        functools.partial(_bias_cmp_kernel, thr=thr, tq=tq, n_cmp=n_cmp),
        grid=(NSA_KV_HEADS, nq), in_specs=[smem],
        out_specs=pl.BlockSpec((1, 1, nc, m), lambda h, t: (h, t, 0, 0)),
        out_shape=jax.ShapeDtypeStruct((NSA_KV_HEADS, nq, nc, m), jnp.float32),
        compiler_params=_params(2), name="bias_cmp",
    )(t5_table)
    tz = pl.pallas_call(
        functools.partial(_bias_tile_kernel, thr=thr, tq=tq),
        grid=(NSA_KV_HEADS,), in_specs=[smem],
        out_specs=pl.BlockSpec((1, 4, tk, m), lambda h: (h, 0, 0, 0)),
        out_shape=jax.ShapeDtypeStruct((NSA_KV_HEADS, 4, tk, m), jnp.float32),
        compiler_params=_params(1), name="bias_tile",
    )(t5_table)
    return bias_c, tz


def _flash_chains(chains, side_work=()):
    def scores(chain):
        k, q_t, _, bias_t = chain[:4]
        s = _dot(k, q_t)
        return s if bias_t is None else s + bias_t

    side_work = list(side_work)
    per_chain = -(-len(side_work) // len(chains))
    pending = [scores(c) for c in chains[:FLASH_LOOKAHEAD]]
    for i, chain in enumerate(chains):
        for thunk in side_work[i * per_chain:(i + 1) * per_chain]:
            thunk()
        s = pending.pop(0)
        if i + FLASH_LOOKAHEAD < len(chains):
            pending.append(scores(chains[i + FLASH_LOOKAHEAD]))
        v_aug, m_ref, acc_ref = chain[2], chain[4], chain[5]
        m_prev = m_ref[...]
        m_new = jnp.maximum(m_prev, jnp.max(s, axis=0, keepdims=True))
        alpha = jnp.exp2(m_prev - m_new)
        p = jnp.exp2(s - m_new).astype(jnp.bfloat16)
        acc_ref[...] = alpha * acc_ref[...] + _dot(v_aug, p)
        m_ref[...] = m_new


def _with_ones(v_t):
    tk = v_t.shape[1]
    row = lax.broadcasted_iota(jnp.int32, (ONES_ROWS, tk), 0)
    return jnp.concatenate([v_t, jnp.where(row == 0, 1.0, 0.0).astype(v_t.dtype)], axis=0)


def _col_chains(k, q_ref, v_aug, bias_ref, m_ref, acc_ref, ncol, width):
    chains = []
    for c0 in range(0, ncol, width):
        cols = slice(c0, c0 + width)
        chains.append((k, q_ref[:, cols], v_aug, None if bias_ref is None else bias_ref[:, cols],
                       m_ref.at[:, cols], acc_ref.at[:, cols]))
    return chains


def _flash_init(m_ref, acc_ref):
    m_ref[...] = jnp.full(m_ref.shape, -jnp.inf, jnp.float32)
    acc_ref[...] = jnp.zeros(acc_ref.shape, jnp.float32)


def _flash_result(acc_ref, dv):
    return acc_ref[0:dv, :] * (1.0 / acc_ref[dv:dv + 1, :])


def _mla_kernel(q_ref, k_ref, v_ref, o_ref, m_ref, acc_ref, *, tq, tk, cw):
    S = k_ref.shape[2]
    nqt = S // tq
    kk = lax.broadcasted_iota(jnp.int32, (tk, cw), 0)
    qq = lax.broadcasted_iota(jnp.int32, (tk, cw), 1)

    def q_tile(qt, carry):
        q0 = pl.multiple_of(qt * tq, tq)
        for e in range(2):
            _flash_init(m_ref.at[e], acc_ref.at[e])

        def step(k0, key_off, rows=tk):
            chains = []
            for e in range(2):
                k = k_ref[0, e, pl.ds(k0, rows), :]
                v_aug = _with_ones(v_ref[0, e, :, pl.ds(k0, rows)])
                for c0 in range(0, tq, cw):
                    bias = None
                    if key_off is not None:
                        if key_off > c0 + cw - 1:
                            continue
                        if key_off + tk - 1 > c0:
                            bias = jnp.where(kk + key_off <= qq + c0, 0.0, NEG)
                    cols = slice(c0, c0 + cw)
                    chains.append((k, q_ref[0, e, :, pl.ds(pl.multiple_of(q0 + c0, cw), cw)], v_aug, bias,
                                   m_ref.at[e, :, cols], acc_ref.at[e, :, cols]))
            return chains

        per = MLA_TRIP_KEYS // MLA_FAR_TK

        def far(c, carry2):
            chains = []
            for j in range(per):
                chains += step(pl.multiple_of(c * MLA_TRIP_KEYS + j * MLA_FAR_TK, MLA_FAR_TK), None, MLA_FAR_TK)
            _flash_chains(chains)
            return carry2

        lax.fori_loop(0, qt * (tq // MLA_TRIP_KEYS), far, 0)
        chains = []
        for j in range(tq // tk):
            chains += step(pl.multiple_of(q0 + j * tk, tk), j * tk)
        _flash_chains(chains)
        o_t = jnp.concatenate([_flash_result(acc_ref.at[e], MLA_V) for e in range(2)], axis=0)
        o_ref[0, pl.ds(q0, tq), :] = jnp.transpose(o_t)
        return carry

    lax.fori_loop(0, nqt, q_tile, 0)


def _mla(q_t, k, v_t):
    B, H, S, _ = k.shape
    tq = min(MLA_TQ, S)
    tk = min(MLA_TK, tq)
    cw = min(MLA_CW, tq)
    return pl.pallas_call(
        functools.partial(_mla_kernel, tq=tq, tk=tk, cw=cw),
        grid=(B, H // 2),
        in_specs=[pl.BlockSpec((1, 2, LANE, S), lambda b, p: (b, p, 0, 0)),
                  pl.BlockSpec((1, 2, S, LANE), lambda b, p: (b, p, 0, 0)),
                  pl.BlockSpec((1, 2, MLA_V, S), lambda b, p: (b, p, 0, 0))],
        out_specs=pl.BlockSpec((1, S, 2 * MLA_V), lambda b, p: (b, 0, p)),
        out_shape=jax.ShapeDtypeStruct((B, S, H * MLA_V), jnp.float32),
        scratch_shapes=[pltpu.VMEM((2, 1, tq), jnp.float32),
                        pltpu.VMEM((2, MLA_V + ONES_ROWS, tq), jnp.float32)],
        compiler_params=_params(2), name="mla",
    )(q_t, k, v_t)


def _nsa_kernel(q_ref, ks_ref, vs_ref, kw_ref, vw_ref, kc_ref, vc_ref, gate_ref, bc_ref, tz_ref, ovt_ref,
                o_ref, qaug_ref, qpad_ref, m_ref, acc_ref, mw_ref, accw_ref, oc_ref, imp_ref, *, tq, tk):
    t = pl.program_id(1)
    G = NSA_GROUP
    HK = range(NSA_KV_HEADS)
    ncol = G * tq
    bf = jnp.bfloat16
    q0 = t * tq
    for hk in HK:
        for g in range(G):
            qpad_ref[hk, 0:NSA_DK, g * tq:(g + 1) * tq] = q_ref[0, hk * G + g]
            qaug_ref[hk, 0:NSA_DK, g * tq:(g + 1) * tq] = q_ref[0, hk * G + g]
        qpad_ref[hk, NSA_DK:, :] = jnp.zeros((SLC_ROWS, ncol), bf)

    def tile_chains(k_ref, v_ref, qx_ref, c, bias_idx, stats):
        k0 = pl.multiple_of(c * tk, tk)
        chains = []
        for hk in HK:
            bias_ref = None if bias_idx is None else tz_ref.at[hk, bias_idx]
            chains += _col_chains(k_ref[0, hk, pl.ds(k0, tk), :], qx_ref.at[hk],
                                  _with_ones(v_ref[0, hk, :, pl.ds(k0, tk)]),
                                  bias_ref, stats[0].at[hk], stats[1].at[hk], ncol, tq)
        return chains

    prev1 = jnp.maximum(t - 1, 0)
    bias1 = jnp.where(t >= 1, 1, 3)

    nb = ovt_ref.shape[0]
    sub = SUBLANE
    ci = lax.broadcasted_iota(jnp.int32, (1, ncol), 1)
    col_ok = q0 + ci % tq >= CMP_LEN - 1
    jb = lax.broadcasted_iota(jnp.int32, (nb, tq), 0)
    cur = (q0 + lax.broadcasted_iota(jnp.int32, (nb, tq), 1)) // SLC_LEN
    forced = jnp.logical_or(jb == 0, jnp.logical_or(jb == cur, jb == cur - 1))
    jsub = lax.broadcasted_iota(jnp.int32, (sub, tq), 0)
    for hk in HK:
        s = _dot(kc_ref[0, hk], qpad_ref[hk]) + bc_ref[hk, 0]
        mx = jnp.max(s, axis=0, keepdims=True)
        p = jnp.exp2(s - mx)
        lsum = jnp.sum(p, axis=0, keepdims=True)
        pcb = (p * jnp.where(col_ok, 1.0 / lsum, 0.0)).astype(bf)
        both = _dot(jnp.concatenate([vc_ref[0, hk], ovt_ref[...]], axis=0), pcb)
        oc_ref[hk] = both[0:NSA_DK]
        imp = both[NSA_DK:, 0:tq]
        for g in range(1, G):
            imp = imp + both[NSA_DK:, g * tq:(g + 1) * tq]
        imp_ref[hk] = jnp.where(forced, FORCE, jnp.where(jb <= cur, imp, NEG))

    win = (mw_ref, accw_ref)
    _flash_init(*win)
    _flash_chains(tile_chains(kw_ref, vw_ref, qpad_ref, jnp.maximum(t - 2, 0), jnp.where(t >= 2, 2, 3), win)
                  + tile_chains(kw_ref, vw_ref, qpad_ref, prev1, bias1, win)
                  + tile_chains(kw_ref, vw_ref, qpad_ref, t, 0, win))

    tiles_per_group = 16 * SLC_LEN // tq
    n_groups = nb // 16

    def write_mask(hk, selb_rows):
        selb = jnp.concatenate(selb_rows, axis=0).astype(bf)
        for g in range(G):
            qaug_ref[hk, NSA_DK:, g * tq:(g + 1) * tq] = selb

    def select(n_act):
        tail = [jnp.full((nb - n_act, tq), NEG, jnp.float32)] if n_act < nb else []
        for hk in HK:
            imp = imp_ref[hk, 0:n_act, :]
            if n_act <= SLC_TOPN:
                write_mask(hk, [jnp.where(imp > 0.5 * NEG, 0.0, NEG)] + tail)
                continue
            slabs = [imp[r0 * sub:(r0 + 1) * sub] for r0 in range(n_act // sub)]
            ranks = [jnp.zeros((sub, tq), jnp.int32) for _ in slabs]
            for jp in range(n_act):
                rowv = imp[jp:jp + 1, :]
                for r0, slab in enumerate(slabs):
                    lo = r0 * sub
                    if lo > jp:
                        one = jnp.where(rowv >= slab, 1, 0)
                    elif lo + sub - 1 <= jp:
                        one = jnp.where(rowv > slab, 1, 0)
                    else:
                        one = jnp.where(jsub + lo > jp, jnp.where(rowv >= slab, 1, 0),
                                        jnp.where(rowv > slab, 1, 0))
                    ranks[r0] = ranks[r0] + one
            write_mask(hk, [jnp.where(r < SLC_TOPN, 0.0, NEG) for r in ranks] + tail)

    for grp in range(n_groups):
        pl.when(t // tiles_per_group == grp)(functools.partial(select, 16 * (grp + 1)))

    slc = (m_ref, acc_ref)
    _flash_init(*slc)
    n_far = jnp.maximum(t - 1, 0)

    done = 0
    size = NSA_FAR_GROUP
    while size >= 1:
        trips = (n_far - done) // size

        def far(c, carry, size=size, done=done):
            chains = []
            for j in range(size):
                chains += tile_chains(ks_ref, vs_ref, qaug_ref, done + c * size + j, None, slc)
            _flash_chains(chains)
            return carry

        lax.fori_loop(0, trips, far, 0)
        done = done + trips * size
        size //= 2
    _flash_chains(tile_chains(ks_ref, vs_ref, qaug_ref, prev1, bias1, slc)
                  + tile_chains(ks_ref, vs_ref, qaug_ref, t, 0, slc))

    for hk in HK:
        gate = gate_ref[0, hk]
        comb = []
        for g in range(G):
            cols = slice(g * tq, (g + 1) * tq)
            gr = lambda br: gate[g * 3 + br:g * 3 + br + 1, :]
            w_s = gr(1) / acc_ref[hk, NSA_DK:NSA_DK + 1, cols]
            w_w = gr(2) / accw_ref[hk, NSA_DK:NSA_DK + 1, cols]
            comb.append(gr(0) * oc_ref[hk, :, cols] + w_s * acc_ref[hk, 0:NSA_DK, cols]
                        + w_w * accw_ref[hk, 0:NSA_DK, cols])
        width = G * NSA_DK
        o_ref[0, :, hk * width:(hk + 1) * width] = jnp.transpose(jnp.concatenate(comb, axis=0))


def _nsa(q_t, ks, vs_t, kw, vw_t, kc, vc_t, gate_t, bias_c, tz, ovt):
    B, _, S, _ = ks.shape
    tq, tk = NSA_TQ, NSA_TK
    nq = S // tq
    nc = kc.shape[2]
    m = NSA_GROUP * tq
    hkv = NSA_KV_HEADS
    tok = lambda: pl.BlockSpec((1, hkv, S, LANE), lambda b, t: (b, 0, 0, 0))
    feat = lambda: pl.BlockSpec((1, hkv, NSA_DK, S), lambda b, t: (b, 0, 0, 0))
    return pl.pallas_call(
        functools.partial(_nsa_kernel, tq=tq, tk=tk),
        grid=(B, nq),
        in_specs=[pl.BlockSpec((1, NSA_HEADS, NSA_DK, tq), lambda b, t: (b, 0, 0, t)),
                  tok(), feat(), tok(), feat(),
                  pl.BlockSpec((1, hkv, nc, LANE), lambda b, t: (b, 0, 0, 0)),
                  pl.BlockSpec((1, hkv, NSA_DK, nc), lambda b, t: (b, 0, 0, 0)),
                  pl.BlockSpec((1, hkv, GATE_ROWS, tq), lambda b, t: (b, 0, 0, t)),
                  pl.BlockSpec((hkv, 1, nc, m), lambda b, t: (0, t, 0, 0)),
                  _const_spec(tz.shape),
                  _const_spec(ovt.shape)],
        out_specs=pl.BlockSpec((1, tq, NSA_HEADS * NSA_DK), lambda b, t: (b, t, 0)),
        out_shape=jax.ShapeDtypeStruct((B, S, NSA_HEADS * NSA_DK), jnp.float32),
        scratch_shapes=[pltpu.VMEM((hkv, NSA_DK + SLC_ROWS, m), jnp.bfloat16),
                        pltpu.VMEM((hkv, NSA_DK + SLC_ROWS, m), jnp.bfloat16),
                        pltpu.VMEM((hkv, 1, m), jnp.float32),
                        pltpu.VMEM((hkv, NSA_DK + ONES_ROWS, m), jnp.float32),
                        pltpu.VMEM((hkv, 1, m), jnp.float32),
                        pltpu.VMEM((hkv, NSA_DK + ONES_ROWS, m), jnp.float32),
                        pltpu.VMEM((hkv, NSA_DK, m), jnp.float32),
                        pltpu.VMEM((hkv, SLC_ROWS, tq), jnp.float32)],
        compiler_params=_params(2), name="nsa",
    )(q_t, ks, vs_t, kw, vw_t, kc, vc_t, gate_t, bias_c, tz, ovt)


def _post_kernel(x_ref, om_ref, on_ref, gm_ref, gn_ref, wo_ref, gf_ref, wg_ref, wu_ref, wd_ref,
                 gfin_ref, o_ref):
    half = om_ref.shape[2]
    mix_m = _rms(om_ref[0], gm_ref[...]).astype(jnp.bfloat16)
    mix_n = _rms(on_ref[0], gn_ref[...]).astype(jnp.bfloat16)
    h = x_ref[0] + _dot(mix_m, wo_ref[0:half, :]) + _dot(mix_n, wo_ref[half:2 * half, :])
    f = _rms(h, gf_ref[...]).astype(jnp.bfloat16)
    a = _dot(f, wg_ref[...])
    act = (a * jax.nn.sigmoid(a) * _dot(f, wu_ref[...])).astype(jnp.bfloat16)
    h = h + _dot(act, wd_ref[...])
    o_ref[0] = _rms(h, gfin_ref[...])


def _post(x, o_mla, o_nsa, gm, gn, w_out, gf, wg, wu, wd, gfin):
    B, S, D = x.shape
    tm = min(POST_TM, S)
    tok = lambda w: pl.BlockSpec((1, tm, w), lambda b, i: (b, i, 0))
    consts = (gm, gn, w_out, gf, wg, wu, wd, gfin)
    return pl.pallas_call(
        _post_kernel, grid=(B, S // tm),
        in_specs=[tok(D), tok(o_mla.shape[2]), tok(o_nsa.shape[2])] + [_const_spec(c.shape) for c in consts],
        out_specs=tok(D), out_shape=jax.ShapeDtypeStruct((B, S, D), jnp.float32),
        compiler_params=_params(2), name="post",
    )(x, o_mla, o_nsa, *consts)


def _rope_tables(S):
    pos = jnp.arange(S, dtype=jnp.float32)
    inv = ROPE_THETA ** (-jnp.arange(0, MLA_ROPE, 2, dtype=jnp.float32) / MLA_ROPE)
    ang = pos[:, None] * inv[None, :]
    cos, sin = jnp.cos(ang), jnp.sin(ang)
    cos2 = jnp.concatenate([cos, cos], axis=-1)
    sin2 = jnp.concatenate([sin, sin], axis=-1)
    scale = (MLA_NOPE + MLA_ROPE) ** -0.5 * LOG2E
    zq = jnp.zeros((S, LANE - MLA_NOPE - MLA_ROPE), jnp.float32)
    cosq = jnp.concatenate([jnp.ones((S, MLA_NOPE), jnp.float32), cos2, zq], axis=-1) * scale
    sinq = jnp.concatenate([jnp.zeros((S, MLA_NOPE), jnp.float32), sin2, zq], axis=-1) * scale
    zk0 = jnp.zeros((S, MLA_NOPE), jnp.float32)
    cosk = jnp.concatenate([zk0, cos2, zq], axis=-1)
    sink = jnp.concatenate([zk0, sin2, zq], axis=-1)
    return cosq.T, sinq.T, cosk, sink


def _overlap_t(S):
    n_cmp = (S - CMP_LEN) // CMP_STRIDE + 1
    n_slc = S // SLC_LEN
    assert n_slc <= SLC_ROWS
    cs = np.arange(n_cmp) * CMP_STRIDE
    ss = np.arange(n_slc) * SLC_LEN
    ov = np.maximum(0, np.minimum(cs[:, None] + CMP_LEN, ss[None, :] + SLC_LEN)
                    - np.maximum(cs[:, None], ss[None, :])).astype(np.float32) / CMP_STRIDE
    out = np.zeros((SLC_ROWS, S // CMP_STRIDE), np.float32)
    out[:n_slc, :n_cmp] = ov.T
    return jnp.asarray(out, jnp.bfloat16)


def kernel(x, norm_mix_g, w_in, mla_q_norm_g, mla_w_uq, mla_kv_norm_g, mla_w_ukv, nsa_cmp_pos_k, nsa_cmp_w1_k, nsa_cmp_w2_k, nsa_cmp_pos_v, nsa_cmp_w1_v, nsa_cmp_w2_v, t5_table, out_norm_mla_g, out_norm_nsa_g, w_out, norm_ffn_g, w_gate, w_up, w_down, final_norm_g):
    B, S, D = x.shape
    assert w_in.shape[0] == 1
    assert D == D_MODEL and S % NSA_TQ == 0 and S % CMP_STRIDE == 0
    bf = jnp.bfloat16
    l = 0
    cosq_t, sinq_t, cosk, sink = _rope_tables(S)
    bias_c, tz = _bias_tables(t5_table, S)
    ovt = _overlap_t(S)
    nc = S // CMP_STRIDE
    cw = CMP_STRIDE * NSA_DK
    row = lambda v: v.reshape(1, -1)

    w_row = _w_in_row_layout().apply(w_in[l])
    w_col = _w_in_col_layout().apply(w_in[l]).T
    w_q2 = _w_uq_layout().apply(mla_w_uq[l]).T
    w_k = _w_uk_layout().apply(mla_w_ukv[l])
    w_v = _w_uv_layout().apply(mla_w_ukv[l]).T
    (q_mla, k_mla, v_mla, q_nsa, kv_cmp, k_slc, v_slc, k_win, v_win, gates) = _proj(
        x, row(norm_mix_g[l]), w_row, w_col, row(mla_q_norm_g[l]), w_q2, row(mla_kv_norm_g[l]), w_k, w_v,
        cosq_t, sinq_t, cosk, sink)

    kvc = kv_cmp.reshape(B, 2 * NSA_KV_HEADS, nc, cw)
    pos = jnp.stack([nsa_cmp_pos_k[l].reshape(2, 1, cw), nsa_cmp_pos_v[l].reshape(2, 1, cw)])
    w1 = jnp.stack([nsa_cmp_w1_k[l].reshape(2, cw, CMP_HIDDEN),
                    nsa_cmp_w1_v[l].reshape(2, cw, CMP_HIDDEN)]).astype(bf)
    w2k = jnp.concatenate([nsa_cmp_w2_k[l], jnp.zeros((CMP_HIDDEN, LANE - NSA_DK), jnp.float32)],
                          axis=1).astype(bf)
    w2vt = nsa_cmp_w2_v[l].T.astype(bf)
    k_cmp, v_cmp = _compress(kvc, pos, w1, w2k, w2vt)

    o_mla = _mla(q_mla, k_mla, v_mla)
    o_nsa = _nsa(q_nsa, k_slc, v_slc, k_win, v_win, k_cmp, v_cmp, gates, bias_c, tz, ovt)

    return _post(x, o_mla, o_nsa, row(out_norm_mla_g[l]), row(out_norm_nsa_g[l]), w_out[l].astype(bf),
                 row(norm_ffn_g[l]), w_gate[l].astype(bf), w_up[l].astype(bf), w_down[l].astype(bf),
                 row(final_norm_g))
```

```python
import functools
import math

import numpy as np
import jax
import jax.numpy as jnp
from jax import lax
from jax.experimental import pallas as pl
from jax.experimental.pallas import tpu as pltpu

D_MODEL = 1024
MLA_HEADS = 8
MLA_NOPE = 64
MLA_ROPE = 32
MLA_V = 64
MLA_Q_LORA = 256
MLA_KV_LORA = 128
ROPE_THETA = 10000.0
NSA_HEADS = 8
NSA_KV_HEADS = 2
NSA_GROUP = 4
NSA_DK = 64
CMP_LEN = 32
CMP_STRIDE = 16
CMP_HIDDEN = 128
SLC_LEN = 64
SLC_TOPN = 16
WINDOW = 512
T5_BUCKETS = 32
T5_MAX_DIST = 128
D_FF = 2816
EPS = 1e-6
NEG = -1e30
FORCE = 1e30

LANE = 128
SUBLANE = 8
SLC_ROWS = 64
GATE_ROWS = 16
ONES_ROWS = 16
LOG2E = math.log2(math.e)

PROJ_TM = 512
MLA_TQ = 1024
MLA_TK = 256
MLA_FAR_TK = 256
MLA_TRIP_KEYS = 1024
MLA_CW = 256
FLASH_LOOKAHEAD = 4
NSA_TQ = 256
NSA_TK = 256
NSA_FAR_GROUP = 4
POST_TM = 512
VMEM_LIMIT = 56 * 1024 * 1024

C_CQ = 0
C_CKV = 256
C_MISC = 384
C_KVCMP = 512
C_KSLC = 768
C_KWIN = 1024
D_ROW = 1280
R_QNSA = 0
R_VSLC = 512
R_VWIN = 640
R_GATE = 768
D_COL = 800


def _dot(a, b):
    return jnp.dot(a, b, preferred_element_type=jnp.float32)


def _dot_nt(a, b):
    return lax.dot_general(a, b, (((1,), (1,)), ((), ())), preferred_element_type=jnp.float32)


def _rms(x, g):
    return x * lax.rsqrt(jnp.mean(x * x, axis=-1, keepdims=True) + EPS) * g


def _const_spec(shape):
    nd = len(shape)
    return pl.BlockSpec(shape, lambda *_: (0,) * nd, pipeline_mode=pl.Buffered(1))


def _params(n_axes):
    return pltpu.CompilerParams(dimension_semantics=("arbitrary",) * n_axes,
                                vmem_limit_bytes=VMEM_LIMIT)


def _in_offsets():
    o_krope = MLA_Q_LORA + MLA_KV_LORA
    o_qnsa = o_krope + MLA_ROPE
    o_cmp = o_qnsa + NSA_HEADS * NSA_DK
    o_slc = o_cmp + 2 * NSA_KV_HEADS * NSA_DK
    o_win = o_slc + 2 * NSA_KV_HEADS * NSA_DK
    o_gate = o_win + 2 * NSA_KV_HEADS * NSA_DK
    return o_krope, o_qnsa, o_cmp, o_slc, o_win, o_gate


class _Layout:
    def __init__(self, n):
        self.src = np.zeros((n,), np.int32)
        self.mul = np.zeros((n,), np.float32)

    def put(self, dst, s, n, m=1.0):
        self.src[dst:dst + n] = np.arange(s, s + n)
        self.mul[dst:dst + n] = m

    def apply(self, w):
        return (w[:, self.src] * self.mul[None, :]).astype(jnp.bfloat16)


def _w_in_row_layout():
    o_krope, _, o_cmp, o_slc, o_win, _ = _in_offsets()
    lay = _Layout(D_ROW)
    lay.put(C_CQ, 0, MLA_Q_LORA)
    lay.put(C_CKV, MLA_Q_LORA, MLA_KV_LORA)
    half = MLA_ROPE // 2
    lay.put(C_MISC + MLA_NOPE, o_krope, MLA_ROPE)
    lay.put(C_MISC + MLA_NOPE + MLA_ROPE, o_krope + half, half, -1.0)
    lay.put(C_MISC + MLA_NOPE + MLA_ROPE + half, o_krope, half, 1.0)
    lay.put(C_KVCMP, o_cmp, 2 * NSA_KV_HEADS * NSA_DK)
    for hk in range(NSA_KV_HEADS):
        lay.put(C_KSLC + hk * LANE, o_slc + hk * NSA_DK, NSA_DK)
        lay.put(C_KWIN + hk * LANE, o_win + hk * NSA_DK, NSA_DK)
    return lay


def _w_in_col_layout():
    _, o_qnsa, _, o_slc, o_win, o_gate = _in_offsets()
    lay = _Layout(D_COL)
    lay.put(R_QNSA, o_qnsa, NSA_HEADS * NSA_DK, NSA_DK ** -0.5)
    lay.put(R_VSLC, o_slc + NSA_KV_HEADS * NSA_DK, NSA_KV_HEADS * NSA_DK)
    lay.put(R_VWIN, o_win + NSA_KV_HEADS * NSA_DK, NSA_KV_HEADS * NSA_DK)
    for hk in range(NSA_KV_HEADS):
        lay.put(R_GATE + hk * GATE_ROWS, o_gate + hk * NSA_GROUP * 3, NSA_GROUP * 3)
    return lay


def _w_uq_layout():
    lay = _Layout(MLA_HEADS * LANE)
    dq = MLA_NOPE + MLA_ROPE
    for h in range(MLA_HEADS):
        lay.put(h * LANE, h * dq, dq)
    return lay


def _w_uk_layout():
    lay = _Layout(MLA_HEADS * LANE)
    per = MLA_NOPE + MLA_V
    for h in range(MLA_HEADS):
        lay.put(h * LANE, h * per, MLA_NOPE)
    return lay


def _w_uv_layout():
    lay = _Layout(MLA_HEADS * MLA_V)
    per = MLA_NOPE + MLA_V
    for h in range(MLA_HEADS):
        lay.put(h * MLA_V, h * per + MLA_NOPE, MLA_V)
    return lay


def _t5_thresholds(max_dist):
    n = np.arange(max_dist, dtype=np.int64)
    max_exact = T5_BUCKETS // 2
    nf = np.maximum(n, 1).astype(np.float32)
    val = (np.log(nf / np.float32(max_exact)) / np.float32(math.log(T5_MAX_DIST / max_exact))
           * np.float32(T5_BUCKETS - max_exact))
    large = np.minimum(max_exact + val.astype(np.int32), T5_BUCKETS - 1)
    bucket = np.where(n < max_exact, n, large)
    assert np.all(np.diff(bucket) >= 0)
    frac = np.abs(val[max_exact + 1:T5_MAX_DIST] - np.round(val[max_exact + 1:T5_MAX_DIST]))
    assert frac.min() > 1e-3
    return [int(np.argmax(bucket >= b)) for b in range(T5_BUCKETS)]


def _proj_kernel(x_ref, gmix_ref, wrow_ref, wcol_ref, gq_ref, wq_ref, gkv_ref, wk_ref, wv_ref,
                 cq_ref, sq_ref, ck_ref, sk_ref,
                 qm_ref, km_ref, vm_ref, qn_ref, kvc_ref, ks_ref, vs_ref, kw_ref, vw_ref, gate_ref):
    tm = x_ref.shape[1]
    bf = jnp.bfloat16
    xn = _rms(x_ref[0], gmix_ref[...]).astype(bf)
    u = _dot(xn, wrow_ref[...])
    ut = _dot_nt(wcol_ref[...], xn)

    cqn = _rms(u[:, C_CQ:C_CQ + MLA_Q_LORA], gq_ref[...]).astype(bf)
    qq = _dot_nt(wq_ref[...], cqn)
    cosq = cq_ref[...]
    sinq = sq_ref[...]
    r0, r1, r2 = MLA_NOPE, MLA_NOPE + MLA_ROPE // 2, MLA_NOPE + MLA_ROPE
    for h in range(MLA_HEADS):
        q = qq[h * LANE:(h + 1) * LANE]
        rot = jnp.concatenate([q[0:r0], -q[r1:r2], q[r0:r1], q[r2:]], axis=0)
        qm_ref[0, h] = (q * cosq + rot * sinq).astype(bf)

    ckvn = _rms(u[:, C_CKV:C_CKV + MLA_KV_LORA], gkv_ref[...]).astype(bf)
    kk = _dot(ckvn, wk_ref[...])
    misc = u[:, C_MISC:C_MISC + LANE]
    kpe = misc * ck_ref[...] + pltpu.roll(misc, LANE - MLA_ROPE, axis=1) * sk_ref[...]
    for h in range(MLA_HEADS):
        km_ref[0, h] = (kk[:, h * LANE:(h + 1) * LANE] + kpe).astype(bf)
    vv = _dot_nt(wv_ref[...], ckvn)
    for h in range(MLA_HEADS):
        vm_ref[0, h] = vv[h * MLA_V:(h + 1) * MLA_V].astype(bf)

    for h in range(NSA_HEADS):
        qn_ref[0, h] = (ut[R_QNSA + h * NSA_DK:R_QNSA + (h + 1) * NSA_DK] * LOG2E).astype(bf)
    for kv in range(2):
        kvc_ref[0, kv] = u[:, C_KVCMP + kv * LANE:C_KVCMP + (kv + 1) * LANE]
    pos = pl.program_id(1) * tm + lax.broadcasted_iota(jnp.int32, (tm, LANE), 0)
    lane = lax.broadcasted_iota(jnp.int32, (tm, LANE), 1)
    onehot = jnp.where(lane - (LANE - SLC_ROWS) == pos // SLC_LEN, 1.0, 0.0)
    for hk in range(NSA_KV_HEADS):
        ks_ref[0, hk] = (u[:, C_KSLC + hk * LANE:C_KSLC + (hk + 1) * LANE] + onehot).astype(bf)
        kw_ref[0, hk] = u[:, C_KWIN + hk * LANE:C_KWIN + (hk + 1) * LANE].astype(bf)
        vs_ref[0, hk] = ut[R_VSLC + hk * NSA_DK:R_VSLC + (hk + 1) * NSA_DK].astype(bf)
        vw_ref[0, hk] = ut[R_VWIN + hk * NSA_DK:R_VWIN + (hk + 1) * NSA_DK].astype(bf)
        gate_ref[0, hk] = jax.nn.sigmoid(ut[R_GATE + hk * GATE_ROWS:R_GATE + (hk + 1) * GATE_ROWS])


def _proj(x, gmix, w_row, w_col, gq, w_q2, gkv, w_k, w_v, cosq_t, sinq_t, cosk, sink):
    B, S, D = x.shape
    tm = min(PROJ_TM, S)
    bf = jnp.bfloat16
    grid = (B, S // tm)
    tok_major = lambda n: pl.BlockSpec((1, n, tm, LANE), lambda b, i: (b, 0, i, 0))
    feat_major = lambda n, d: pl.BlockSpec((1, n, d, tm), lambda b, i: (b, 0, 0, i))
    out_shape = (
        jax.ShapeDtypeStruct((B, MLA_HEADS, LANE, S), bf),
        jax.ShapeDtypeStruct((B, MLA_HEADS, S, LANE), bf),
        jax.ShapeDtypeStruct((B, MLA_HEADS, MLA_V, S), bf),
        jax.ShapeDtypeStruct((B, NSA_HEADS, NSA_DK, S), bf),
        jax.ShapeDtypeStruct((B, 2, S, LANE), jnp.float32),
        jax.ShapeDtypeStruct((B, NSA_KV_HEADS, S, LANE), bf),
        jax.ShapeDtypeStruct((B, NSA_KV_HEADS, NSA_DK, S), bf),
        jax.ShapeDtypeStruct((B, NSA_KV_HEADS, S, LANE), bf),
        jax.ShapeDtypeStruct((B, NSA_KV_HEADS, NSA_DK, S), bf),
        jax.ShapeDtypeStruct((B, NSA_KV_HEADS, GATE_ROWS, S), jnp.float32),
    )
    out_specs = (
        feat_major(MLA_HEADS, LANE), tok_major(MLA_HEADS), feat_major(MLA_HEADS, MLA_V),
        feat_major(NSA_HEADS, NSA_DK),
        tok_major(2),
        tok_major(NSA_KV_HEADS), feat_major(NSA_KV_HEADS, NSA_DK),
        tok_major(NSA_KV_HEADS), feat_major(NSA_KV_HEADS, NSA_DK),
        feat_major(NSA_KV_HEADS, GATE_ROWS),
    )
    consts = (gmix, w_row, w_col, gq, w_q2, gkv, w_k, w_v)
    in_specs = ([pl.BlockSpec((1, tm, D), lambda b, i: (b, i, 0))] + [_const_spec(c.shape) for c in consts]
                + [pl.BlockSpec((LANE, tm), lambda b, i: (0, i)), pl.BlockSpec((LANE, tm), lambda b, i: (0, i)),
                   pl.BlockSpec((tm, LANE), lambda b, i: (i, 0)), pl.BlockSpec((tm, LANE), lambda b, i: (i, 0))])
    return pl.pallas_call(
        _proj_kernel, grid=grid, in_specs=in_specs, out_specs=out_specs, out_shape=out_shape,
        compiler_params=_params(2), name="proj",
    )(x, *consts, cosq_t, sinq_t, cosk, sink)


def _compress_kernel(c_ref, pos_ref, w1_ref, w2k_ref, w2vt_ref, kc_ref, vc_ref):
    nc = kc_ref.shape[2]
    bf = jnp.bfloat16
    hw = NSA_KV_HEADS * NSA_DK
    for kv in range(2):
        halves = []
        for half in range(CMP_LEN // CMP_STRIDE):
            acc = None
            for r in range(CMP_STRIDE):
                l = half * CMP_STRIDE + r
                x = c_ref[0, kv, pl.ds(r, nc, stride=CMP_STRIDE), :]
                part = _dot((x + pos_ref[kv, l]).astype(bf), w1_ref[kv, l])
                acc = part if acc is None else acc + part
            halves.append(acc)
        hid = halves[0] + pltpu.roll(halves[1], nc - 1, axis=0)
        act = jax.nn.gelu(hid).astype(bf)
        for hk in range(NSA_KV_HEADS):
            act_h = act[:, hk * CMP_HIDDEN:(hk + 1) * CMP_HIDDEN]
            if kv == 0:
                kc_ref[0, hk] = _dot(act_h, w2k_ref[...]).astype(bf)
            else:
                vc_ref[0, hk] = _dot_nt(w2vt_ref[...], act_h).astype(bf)


def _compress(kvc, pos, w1, w2k, w2vt):
    B, _, S, width = kvc.shape
    nc = S // CMP_STRIDE
    return pl.pallas_call(
        _compress_kernel, grid=(B,),
        in_specs=[pl.BlockSpec((1, 2, S, width), lambda b: (b, 0, 0, 0)),
                  _const_spec(pos.shape), _const_spec(w1.shape), _const_spec(w2k.shape),
                  _const_spec(w2vt.shape)],
        out_specs=(pl.BlockSpec((1, NSA_KV_HEADS, nc, LANE), lambda b: (b, 0, 0, 0)),
                   pl.BlockSpec((1, NSA_KV_HEADS, NSA_DK, nc), lambda b: (b, 0, 0, 0))),
        out_shape=(jax.ShapeDtypeStruct((B, NSA_KV_HEADS, nc, LANE), jnp.bfloat16),
                   jax.ShapeDtypeStruct((B, NSA_KV_HEADS, NSA_DK, nc), jnp.bfloat16)),
        compiler_params=_params(1), name="compress",
    )(kvc, pos, w1, w2k, w2vt)


def _t5_lookup(dist, tab_ref, head, thr):
    val = jnp.full(dist.shape, tab_ref[0, head], jnp.float32)
    for b in range(1, T5_BUCKETS):
        val = jnp.where(dist >= thr[b], tab_ref[b, head], val)
    return (val - tab_ref[T5_BUCKETS - 1, head]) * LOG2E


def _bias_cmp_kernel(tab_ref, out_ref, *, thr, tq, n_cmp):
    hk = pl.program_id(0)
    t = pl.program_id(1)
    nc = out_ref.shape[2]
    per_tile = tq // CMP_STRIDE
    band = per_tile + (T5_MAX_DIST + CMP_LEN) // CMP_STRIDE + SUBLANE
    band = min(-(-band // SUBLANE) * SUBLANE, nc)
    start = jnp.clip(t * per_tile - (band - per_tile), 0, nc - band)
    start = pl.multiple_of(start // SUBLANE * SUBLANE, SUBLANE)
    n_all = lax.broadcasted_iota(jnp.int32, (nc, NSA_GROUP * tq), 0)
    out_ref[0, 0] = jnp.where(n_all < start, 0.0, NEG)
    n = start + lax.broadcasted_iota(jnp.int32, (band, tq), 0)
    i = lax.broadcasted_iota(jnp.int32, (band, tq), 1)
    dist = t * tq + i - (n * CMP_STRIDE + CMP_LEN - 1)
    ok = jnp.logical_and(dist >= 0, n < n_cmp)
    for g in range(NSA_GROUP):
        val = _t5_lookup(dist, tab_ref, hk * NSA_GROUP + g, thr)
        out_ref[0, 0, pl.ds(start, band), g * tq:(g + 1) * tq] = jnp.where(ok, val, NEG)


def _bias_tile_kernel(tab_ref, out_ref, *, thr, tq):
    hk = pl.program_id(0)
    tk = out_ref.shape[2]
    j = lax.broadcasted_iota(jnp.int32, (tk, tq), 0)
    i = lax.broadcasted_iota(jnp.int32, (tk, tq), 1)
    for g in range(NSA_GROUP):
        head = hk * NSA_GROUP + g
        cols = slice(g * tq, (g + 1) * tq)
        d0 = i - j
        out_ref[0, 0, :, cols] = jnp.where(d0 >= 0, _t5_lookup(d0, tab_ref, head, thr), NEG)
        out_ref[0, 1, :, cols] = _t5_lookup(d0 + tk, tab_ref, head, thr)
        out_ref[0, 2, :, cols] = jnp.where(j > i, 0.0, NEG)
        out_ref[0, 3, :, cols] = jnp.full((tk, tq), NEG, jnp.float32)


def _bias_tables(t5_table, S):
    thr = _t5_thresholds(S)
    tq, tk = NSA_TQ, NSA_TK
    assert tq == tk and WINDOW == 2 * tk and tk >= T5_MAX_DIST
    nq = S // tq
    nc = S // CMP_STRIDE
    n_cmp = (S - CMP_LEN) // CMP_STRIDE + 1
    m = NSA_GROUP * tq
    smem = pl.BlockSpec(memory_space=pltpu.SMEM)
    bias_c = pl.pallas_call(
        functools.partial(_bias_cmp_kernel, thr=thr, tq=tq, n_cmp=n_cmp),
        grid=(NSA_KV_HEADS, nq), in_specs=[smem],
        out_specs=pl.BlockSpec((1, 1, nc, m), lambda h, t: (h, t, 0, 0)),
        out_shape=jax.ShapeDtypeStruct((NSA_KV_HEADS, nq, nc, m), jnp.float32),
        compiler_params=_params(2), name="bias_cmp",
    )(t5_table)
    tz = pl.pallas_call(
        functools.partial(_bias_tile_kernel, thr=thr, tq=tq),
        grid=(NSA_KV_HEADS,), in_specs=[smem],
        out_specs=pl.BlockSpec((1, 4, tk, m), lambda h: (h, 0, 0, 0)),
        out_shape=jax.ShapeDtypeStruct((NSA_KV_HEADS, 4, tk, m), jnp.float32),
        compiler_params=_params(1), name="bias_tile",
    )(t5_table)
    return bias_c, tz


def _flash_chains(chains, side_work=()):
    def scores(chain):
        k, q_t, _, bias_t = chain[:4]
        s = _dot(k, q_t)
        return s if bias_t is None else s + bias_t

    side_work = list(side_work)
    per_chain = -(-len(side_work) // len(chains))
    pending = [scores(c) for c in chains[:FLASH_LOOKAHEAD]]
    for i, chain in enumerate(chains):
        for thunk in side_work[i * per_chain:(i + 1) * per_chain]:
            thunk()
        s = pending.pop(0)
        if i + FLASH_LOOKAHEAD < len(chains):
            pending.append(scores(chains[i + FLASH_LOOKAHEAD]))
        v_aug, m_ref, acc_ref = chain[2], chain[4], chain[5]
        m_prev = m_ref[...]
        m_new = jnp.maximum(m_prev, jnp.max(s, axis=0, keepdims=True))
        alpha = jnp.exp2(m_prev - m_new)
        p = jnp.exp2(s - m_new).astype(jnp.bfloat16)
        acc_ref[...] = alpha * acc_ref[...] + _dot(v_aug, p)
        m_ref[...] = m_new


def _with_ones(v_t):
    tk = v_t.shape[1]
    row = lax.broadcasted_iota(jnp.int32, (ONES_ROWS, tk), 0)
    return jnp.concatenate([v_t, jnp.where(row == 0, 1.0, 0.0).astype(v_t.dtype)], axis=0)


def _col_chains(k, q_ref, v_aug, bias_ref, m_ref, acc_ref, ncol, width):
    chains = []
    for c0 in range(0, ncol, width):
        cols = slice(c0, c0 + width)
        chains.append((k, q_ref[:, cols], v_aug, None if bias_ref is None else bias_ref[:, cols],
                       m_ref.at[:, cols], acc_ref.at[:, cols]))
    return chains


def _flash_init(m_ref, acc_ref):
    m_ref[...] = jnp.full(m_ref.shape, -jnp.inf, jnp.float32)
    acc_ref[...] = jnp.zeros(acc_ref.shape, jnp.float32)


def _flash_result(acc_ref, dv):
    return acc_ref[0:dv, :] * (1.0 / acc_ref[dv:dv + 1, :])


def _mla_kernel(q_ref, k_ref, v_ref, o_ref, m_ref, acc_ref, *, tq, tk, cw):
    S = k_ref.shape[2]
    nqt = S // tq
    kk = lax.broadcasted_iota(jnp.int32, (tk, cw), 0)
    qq = lax.broadcasted_iota(jnp.int32, (tk, cw), 1)

    def q_tile(qt, carry):
        q0 = pl.multiple_of(qt * tq, tq)
        for e in range(2):
            _flash_init(m_ref.at[e], acc_ref.at[e])

        def step(k0, key_off, rows=tk):
            chains = []
            for e in range(2):
                k = k_ref[0, e, pl.ds(k0, rows), :]
                v_aug = _with_ones(v_ref[0, e, :, pl.ds(k0, rows)])
                for c0 in range(0, tq, cw):
                    bias = None
                    if key_off is not None:
                        if key_off > c0 + cw - 1:
                            continue
                        if key_off + tk - 1 > c0:
                            bias = jnp.where(kk + key_off <= qq + c0, 0.0, NEG)
                    cols = slice(c0, c0 + cw)
                    chains.append((k, q_ref[0, e, :, pl.ds(pl.multiple_of(q0 + c0, cw), cw)], v_aug, bias,
                                   m_ref.at[e, :, cols], acc_ref.at[e, :, cols]))
            return chains

        per = MLA_TRIP_KEYS // MLA_FAR_TK

        def far(c, carry2):
            chains = []
            for j in range(per):
                chains += step(pl.multiple_of(c * MLA_TRIP_KEYS + j * MLA_FAR_TK, MLA_FAR_TK), None, MLA_FAR_TK)
            _flash_chains(chains)
            return carry2

        lax.fori_loop(0, qt * (tq // MLA_TRIP_KEYS), far, 0)
        chains = []
        for j in range(tq // tk):
            chains += step(pl.multiple_of(q0 + j * tk, tk), j * tk)
        _flash_chains(chains)
        o_t = jnp.concatenate([_flash_result(acc_ref.at[e], MLA_V) for e in range(2)], axis=0)
        o_ref[0, pl.ds(q0, tq), :] = jnp.transpose(o_t)
        return carry

    lax.fori_loop(0, nqt, q_tile, 0)


def _mla(q_t, k, v_t):
    B, H, S, _ = k.shape
    tq = min(MLA_TQ, S)
    tk = min(MLA_TK, tq)
    cw = min(MLA_CW, tq)
    return pl.pallas_call(
        functools.partial(_mla_kernel, tq=tq, tk=tk, cw=cw),
        grid=(B, H // 2),
        in_specs=[pl.BlockSpec((1, 2, LANE, S), lambda b, p: (b, p, 0, 0)),
                  pl.BlockSpec((1, 2, S, LANE), lambda b, p: (b, p, 0, 0)),
                  pl.BlockSpec((1, 2, MLA_V, S), lambda b, p: (b, p, 0, 0))],
        out_specs=pl.BlockSpec((1, S, 2 * MLA_V), lambda b, p: (b, 0, p)),
        out_shape=jax.ShapeDtypeStruct((B, S, H * MLA_V), jnp.float32),
        scratch_shapes=[pltpu.VMEM((2, 1, tq), jnp.float32),
                        pltpu.VMEM((2, MLA_V + ONES_ROWS, tq), jnp.float32)],
        compiler_params=_params(2), name="mla",
    )(q_t, k, v_t)


def _nsa_kernel(q_ref, ks_ref, vs_ref, kw_ref, vw_ref, kc_ref, vc_ref, gate_ref, bc_ref, tz_ref, ovt_ref,
                o_ref, qaug_ref, qpad_ref, m_ref, acc_ref, mw_ref, accw_ref, oc_ref, imp_ref, *, tq, tk):
    t = pl.program_id(1)
    G = NSA_GROUP
    HK = range(NSA_KV_HEADS)
    ncol = G * tq
    bf = jnp.bfloat16
    q0 = t * tq
    for hk in HK:
        for g in range(G):
            qpad_ref[hk, 0:NSA_DK, g * tq:(g + 1) * tq] = q_ref[0, hk * G + g]
            qaug_ref[hk, 0:NSA_DK, g * tq:(g + 1) * tq] = q_ref[0, hk * G + g]
        qpad_ref[hk, NSA_DK:, :] = jnp.zeros((SLC_ROWS, ncol), bf)

    def tile_chains(k_ref, v_ref, qx_ref, c, bias_idx, stats):
        k0 = pl.multiple_of(c * tk, tk)
        chains = []
        for hk in HK:
            bias_ref = None if bias_idx is None else tz_ref.at[hk, bias_idx]
            chains += _col_chains(k_ref[0, hk, pl.ds(k0, tk), :], qx_ref.at[hk],
                                  _with_ones(v_ref[0, hk, :, pl.ds(k0, tk)]),
                                  bias_ref, stats[0].at[hk], stats[1].at[hk], ncol, tq)
        return chains

    prev1 = jnp.maximum(t - 1, 0)
    bias1 = jnp.where(t >= 1, 1, 3)

    nb = ovt_ref.shape[0]
    sub = SUBLANE
    ci = lax.broadcasted_iota(jnp.int32, (1, ncol), 1)
    col_ok = q0 + ci % tq >= CMP_LEN - 1
    jb = lax.broadcasted_iota(jnp.int32, (nb, tq), 0)
    cur = (q0 + lax.broadcasted_iota(jnp.int32, (nb, tq), 1)) // SLC_LEN
    forced = jnp.logical_or(jb == 0, jnp.logical_or(jb == cur, jb == cur - 1))
    jsub = lax.broadcasted_iota(jnp.int32, (sub, tq), 0)
    for hk in HK:
        s = _dot(kc_ref[0, hk], qpad_ref[hk]) + bc_ref[hk, 0]
        mx = jnp.max(s, axis=0, keepdims=True)
        p = jnp.exp2(s - mx)
        lsum = jnp.sum(p, axis=0, keepdims=True)
        pcb = (p * jnp.where(col_ok, 1.0 / lsum, 0.0)).astype(bf)
        both = _dot(jnp.concatenate([vc_ref[0, hk], ovt_ref[...]], axis=0), pcb)
        oc_ref[hk] = both[0:NSA_DK]
        imp = both[NSA_DK:, 0:tq]
        for g in range(1, G):
            imp = imp + both[NSA_DK:, g * tq:(g + 1) * tq]
        imp_ref[hk] = jnp.where(forced, FORCE, jnp.where(jb <= cur, imp, NEG))

    win = (mw_ref, accw_ref)
    _flash_init(*win)
    _flash_chains(tile_chains(kw_ref, vw_ref, qpad_ref, jnp.maximum(t - 2, 0), jnp.where(t >= 2, 2, 3), win)
                  + tile_chains(kw_ref, vw_ref, qpad_ref, prev1, bias1, win)
                  + tile_chains(kw_ref, vw_ref, qpad_ref, t, 0, win))

    tiles_per_group = 16 * SLC_LEN // tq
    n_groups = nb // 16

    def write_mask(hk, selb_rows):
        selb = jnp.concatenate(selb_rows, axis=0).astype(bf)
        for g in range(G):
            qaug_ref[hk, NSA_DK:, g * tq:(g + 1) * tq] = selb

    def select(n_act):
        tail = [jnp.full((nb - n_act, tq), NEG, jnp.float32)] if n_act < nb else []
        for hk in HK:
            imp = imp_ref[hk, 0:n_act, :]
            if n_act <= SLC_TOPN:
                write_mask(hk, [jnp.where(imp > 0.5 * NEG, 0.0, NEG)] + tail)
                continue
            slabs = [imp[r0 * sub:(r0 + 1) * sub] for r0 in range(n_act // sub)]
            ranks = [jnp.zeros((sub, tq), jnp.int32) for _ in slabs]
            for jp in range(n_act):
                rowv = imp[jp:jp + 1, :]
                for r0, slab in enumerate(slabs):
                    lo = r0 * sub
                    if lo > jp:
                        one = jnp.where(rowv >= slab, 1, 0)
                    elif lo + sub - 1 <= jp:
                        one = jnp.where(rowv > slab, 1, 0)
                    else:
                        one = jnp.where(jsub + lo > jp, jnp.where(rowv >= slab, 1, 0),
                                        jnp.where(rowv > slab, 1, 0))
                    ranks[r0] = ranks[r0] + one
            write_mask(hk, [jnp.where(r < SLC_TOPN, 0.0, NEG) for r in ranks] + tail)

    for grp in range(n_groups):
        pl.when(t // tiles_per_group == grp)(functools.partial(select, 16 * (grp + 1)))

    slc = (m_ref, acc_ref)
    _flash_init(*slc)
    n_far = jnp.maximum(t - 1, 0)

    done = 0
    size = NSA_FAR_GROUP
    while size >= 1:
        trips = (n_far - done) // size

        def far(c, carry, size=size, done=done):
            chains = []
            for j in range(size):
                chains += tile_chains(ks_ref, vs_ref, qaug_ref, done + c * size + j, None, slc)
            _flash_chains(chains)
            return carry

        lax.fori_loop(0, trips, far, 0)
        done = done + trips * size
        size //= 2
    _flash_chains(tile_chains(ks_ref, vs_ref, qaug_ref, prev1, bias1, slc)
                  + tile_chains(ks_ref, vs_ref, qaug_ref, t, 0, slc))

    for hk in HK:
        gate = gate_ref[0, hk]
        comb = []
        for g in range(G):
            cols = slice(g * tq, (g + 1) * tq)
            gr = lambda br: gate[g * 3 + br:g * 3 + br + 1, :]
            w_s = gr(1) / acc_ref[hk, NSA_DK:NSA_DK + 1, cols]
            w_w = gr(2) / accw_ref[hk, NSA_DK:NSA_DK + 1, cols]
            comb.append(gr(0) * oc_ref[hk, :, cols] + w_s * acc_ref[hk, 0:NSA_DK, cols]
                        + w_w * accw_ref[hk, 0:NSA_DK, cols])
        width = G * NSA_DK
        o_ref[0, :, hk * width:(hk + 1) * width] = jnp.transpose(jnp.concatenate(comb, axis=0))


def _nsa(q_t, ks, vs_t, kw, vw_t, kc, vc_t, gate_t, bias_c, tz, ovt):
    B, _, S, _ = ks.shape
    tq, tk = NSA_TQ, NSA_TK
    nq = S // tq
    nc = kc.shape[2]
    m = NSA_GROUP * tq
    hkv = NSA_KV_HEADS
    tok = lambda: pl.BlockSpec((1, hkv, S, LANE), lambda b, t: (b, 0, 0, 0))
    feat = lambda: pl.BlockSpec((1, hkv, NSA_DK, S), lambda b, t: (b, 0, 0, 0))
    return pl.pallas_call(
        functools.partial(_nsa_kernel, tq=tq, tk=tk),
        grid=(B, nq),
        in_specs=[pl.BlockSpec((1, NSA_HEADS, NSA_DK, tq), lambda b, t: (b, 0, 0, t)),
                  tok(), feat(), tok(), feat(),
                  pl.BlockSpec((1, hkv, nc, LANE), lambda b, t: (b, 0, 0, 0)),
                  pl.BlockSpec((1, hkv, NSA_DK, nc), lambda b, t: (b, 0, 0, 0)),
                  pl.BlockSpec((1, hkv, GATE_ROWS, tq), lambda b, t: (b, 0, 0, t)),
                  pl.BlockSpec((hkv, 1, nc, m), lambda b, t: (0, t, 0, 0)),
                  _const_spec(tz.shape),
                  _const_spec(ovt.shape)],
        out_specs=pl.BlockSpec((1, tq, NSA_HEADS * NSA_DK), lambda b, t: (b, t, 0)),
        out_shape=jax.ShapeDtypeStruct((B, S, NSA_HEADS * NSA_DK), jnp.float32),
        scratch_shapes=[pltpu.VMEM((hkv, NSA_DK + SLC_ROWS, m), jnp.bfloat16),
                        pltpu.VMEM((hkv, NSA_DK + SLC_ROWS, m), jnp.bfloat16),
                        pltpu.VMEM((hkv, 1, m), jnp.float32),
                        pltpu.VMEM((hkv, NSA_DK + ONES_ROWS, m), jnp.float32),
                        pltpu.VMEM((hkv, 1, m), jnp.float32),
                        pltpu.VMEM((hkv, NSA_DK + ONES_ROWS, m), jnp.float32),
                        pltpu.VMEM((hkv, NSA_DK, m), jnp.float32),
                        pltpu.VMEM((hkv, SLC_ROWS, tq), jnp.float32)],
        compiler_params=_params(2), name="nsa",
    )(q_t, ks, vs_t, kw, vw_t, kc, vc_t, gate_t, bias_c, tz, ovt)


def _post_kernel(x_ref, om_ref, on_ref, gm_ref, gn_ref, wo_ref, gf_ref, wg_ref, wu_ref, wd_ref,
                 gfin_ref, o_ref):
    half = om_ref.shape[2]
    mix_m = _rms(om_ref[0], gm_ref[...]).astype(jnp.bfloat16)
    mix_n = _rms(on_ref[0], gn_ref[...]).astype(jnp.bfloat16)
    h = x_ref[0] + _dot(mix_m, wo_ref[0:half, :]) + _dot(mix_n, wo_ref[half:2 * half, :])
    f = _rms(h, gf_ref[...]).astype(jnp.bfloat16)
    a = _dot(f, wg_ref[...])
    act = (a * jax.nn.sigmoid(a) * _dot(f, wu_ref[...])).astype(jnp.bfloat16)
    h = h + _dot(act, wd_ref[...])
    o_ref[0] = _rms(h, gfin_ref[...])


def _post(x, o_mla, o_nsa, gm, gn, w_out, gf, wg, wu, wd, gfin):
    B, S, D = x.shape
    tm = min(POST_TM, S)
    tok = lambda w: pl.BlockSpec((1, tm, w), lambda b, i: (b, i, 0))
    consts = (gm, gn, w_out, gf, wg, wu, wd, gfin)
    return pl.pallas_call(
        _post_kernel, grid=(B, S // tm),
        in_specs=[tok(D), tok(o_mla.shape[2]), tok(o_nsa.shape[2])] + [_const_spec(c.shape) for c in consts],
        out_specs=tok(D), out_shape=jax.ShapeDtypeStruct((B, S, D), jnp.float32),
        compiler_params=_params(2), name="post",
    )(x, o_mla, o_nsa, *consts)


def _rope_tables(S):
    pos = jnp.arange(S, dtype=jnp.float32)
    inv = ROPE_THETA ** (-jnp.arange(0, MLA_ROPE, 2, dtype=jnp.float32) / MLA_ROPE)
    ang = pos[:, None] * inv[None, :]
    cos, sin = jnp.cos(ang), jnp.sin(ang)
    cos2 = jnp.concatenate([cos, cos], axis=-1)
    sin2 = jnp.concatenate([sin, sin], axis=-1)
    scale = (MLA_NOPE + MLA_ROPE) ** -0.5 * LOG2E
    zq = jnp.zeros((S, LANE - MLA_NOPE - MLA_ROPE), jnp.float32)
    cosq = jnp.concatenate([jnp.ones((S, MLA_NOPE), jnp.float32), cos2, zq], axis=-1) * scale
    sinq = jnp.concatenate([jnp.zeros((S, MLA_NOPE), jnp.float32), sin2, zq], axis=-1) * scale
    zk0 = jnp.zeros((S, MLA_NOPE), jnp.float32)
    cosk = jnp.concatenate([zk0, cos2, zq], axis=-1)
    sink = jnp.concatenate([zk0, sin2, zq], axis=-1)
    return cosq.T, sinq.T, cosk, sink


def _overlap_t(S):
    n_cmp = (S - CMP_LEN) // CMP_STRIDE + 1
    n_slc = S // SLC_LEN
    assert n_slc <= SLC_ROWS
    cs = np.arange(n_cmp) * CMP_STRIDE
    ss = np.arange(n_slc) * SLC_LEN
    ov = np.maximum(0, np.minimum(cs[:, None] + CMP_LEN, ss[None, :] + SLC_LEN)
                    - np.maximum(cs[:, None], ss[None, :])).astype(np.float32) / CMP_STRIDE
    out = np.zeros((SLC_ROWS, S // CMP_STRIDE), np.float32)
    out[:n_slc, :n_cmp] = ov.T
    return jnp.asarray(out, jnp.bfloat16)


def kernel(x, norm_mix_g, w_in, mla_q_norm_g, mla_w_uq, mla_kv_norm_g, mla_w_ukv, nsa_cmp_pos_k, nsa_cmp_w1_k, nsa_cmp_w2_k, nsa_cmp_pos_v, nsa_cmp_w1_v, nsa_cmp_w2_v, t5_table, out_norm_mla_g, out_norm_nsa_g, w_out, norm_ffn_g, w_gate, w_up, w_down, final_norm_g):
    B, S, D = x.shape
    assert w_in.shape[0] == 1
    assert D == D_MODEL and S % NSA_TQ == 0 and S % CMP_STRIDE == 0
    bf = jnp.bfloat16
    l = 0
    cosq_t, sinq_t, cosk, sink = _rope_tables(S)
    bias_c, tz = _bias_tables(t5_table, S)
    ovt = _overlap_t(S)
    nc = S // CMP_STRIDE
    cw = CMP_STRIDE * NSA_DK
    row = lambda v: v.reshape(1, -1)

    w_row = _w_in_row_layout().apply(w_in[l])
    w_col = _w_in_col_layout().apply(w_in[l]).T
    w_q2 = _w_uq_layout().apply(mla_w_uq[l]).T
    w_k = _w_uk_layout().apply(mla_w_ukv[l])
    w_v = _w_uv_layout().apply(mla_w_ukv[l]).T
    (q_mla, k_mla, v_mla, q_nsa, kv_cmp, k_slc, v_slc, k_win, v_win, gates) = _proj(
        x, row(norm_mix_g[l]), w_row, w_col, row(mla_q_norm_g[l]), w_q2, row(mla_kv_norm_g[l]), w_k, w_v,
        cosq_t, sinq_t, cosk, sink)

    def per_offset(pos_l, w1_l):
        pos2 = jnp.tile(pos_l, (1, NSA_KV_HEADS)).reshape(CMP_LEN, 1, NSA_KV_HEADS * NSA_DK)
        w = w1_l.reshape(CMP_LEN, NSA_DK, CMP_HIDDEN)
        z = jnp.zeros_like(w)
        return pos2, jnp.concatenate([jnp.concatenate([w, z], axis=2), jnp.concatenate([z, w], axis=2)], axis=1)

    pos_k, w1_k = per_offset(nsa_cmp_pos_k[l], nsa_cmp_w1_k[l])
    pos_v, w1_v = per_offset(nsa_cmp_pos_v[l], nsa_cmp_w1_v[l])
    pos = jnp.stack([pos_k, pos_v])
    w1 = jnp.stack([w1_k, w1_v]).astype(bf)
    kvc = kv_cmp
    w2k = jnp.concatenate([nsa_cmp_w2_k[l], jnp.zeros((CMP_HIDDEN, LANE - NSA_DK), jnp.float32)],
                          axis=1).astype(bf)
    w2vt = nsa_cmp_w2_v[l].T.astype(bf)
    k_cmp, v_cmp = _compress(kvc, pos, w1, w2k, w2vt)

    o_mla = _mla(q_mla, k_mla, v_mla)
    o_nsa = _nsa(q_nsa, k_slc, v_slc, k_win, v_win, k_cmp, v_cmp, gates, bias_c, tz, ovt)

    return _post(x, o_mla, o_nsa, row(out_norm_mla_g[l]), row(out_norm_nsa_g[l]), w_out[l].astype(bf),
                 row(norm_ffn_g[l]), w_gate[l].astype(bf), w_up[l].astype(bf), w_down[l].astype(bf),
                 row(final_norm_g))
```

```python
import functools
import math

import numpy as np
import jax
import jax.numpy as jnp
from jax import lax
from jax.experimental import pallas as pl
from jax.experimental.pallas import tpu as pltpu

D_MODEL = 1024
MLA_HEADS = 8
MLA_NOPE = 64
MLA_ROPE = 32
MLA_V = 64
MLA_Q_LORA = 256
MLA_KV_LORA = 128
ROPE_THETA = 10000.0
NSA_HEADS = 8
NSA_KV_HEADS = 2
NSA_GROUP = 4
NSA_DK = 64
CMP_LEN = 32
CMP_STRIDE = 16
CMP_HIDDEN = 128
SLC_LEN = 64
SLC_TOPN = 16
WINDOW = 512
T5_BUCKETS = 32
T5_MAX_DIST = 128
D_FF = 2816
EPS = 1e-6
NEG = -1e30
FORCE = 1e30

LANE = 128
SUBLANE = 8
SLC_ROWS = 64
GATE_ROWS = 16
ONES_ROWS = 16
LOG2E = math.log2(math.e)

PROJ_TM = 512
MLA_TQ = 1024
MLA_TK = 256
MLA_FAR_TK = 256
MLA_TRIP_KEYS = 1024
MLA_CW = 256
FLASH_LOOKAHEAD = 6
NSA_TQ = 256
NSA_TK = 256
NSA_FAR_GROUP = 4
POST_TM = 512
VMEM_LIMIT = 56 * 1024 * 1024

C_CQ = 0
C_CKV = 256
C_MISC = 384
C_KVCMP = 512
C_KSLC = 768
C_KWIN = 1024
D_ROW = 1280
R_QNSA = 0
R_VSLC = 512
R_VWIN = 640
R_GATE = 768
D_COL = 800


def _dot(a, b):
    return jnp.dot(a, b, preferred_element_type=jnp.float32)


def _dot_nt(a, b):
    return lax.dot_general(a, b, (((1,), (1,)), ((), ())), preferred_element_type=jnp.float32)


def _rms(x, g):
    return x * lax.rsqrt(jnp.mean(x * x, axis=-1, keepdims=True) + EPS) * g


def _const_spec(shape):
    nd = len(shape)
    return pl.BlockSpec(shape, lambda *_: (0,) * nd, pipeline_mode=pl.Buffered(1))


def _params(n_axes):
    return pltpu.CompilerParams(dimension_semantics=("arbitrary",) * n_axes,
                                vmem_limit_bytes=VMEM_LIMIT)


def _in_offsets():
    o_krope = MLA_Q_LORA + MLA_KV_LORA
    o_qnsa = o_krope + MLA_ROPE
    o_cmp = o_qnsa + NSA_HEADS * NSA_DK
    o_slc = o_cmp + 2 * NSA_KV_HEADS * NSA_DK
    o_win = o_slc + 2 * NSA_KV_HEADS * NSA_DK
    o_gate = o_win + 2 * NSA_KV_HEADS * NSA_DK
    return o_krope, o_qnsa, o_cmp, o_slc, o_win, o_gate


class _Layout:
    def __init__(self, n):
        self.src = np.zeros((n,), np.int32)
        self.mul = np.zeros((n,), np.float32)

    def put(self, dst, s, n, m=1.0):
        self.src[dst:dst + n] = np.arange(s, s + n)
        self.mul[dst:dst + n] = m

    def apply(self, w):
        return (w[:, self.src] * self.mul[None, :]).astype(jnp.bfloat16)


def _w_in_row_layout():
    o_krope, _, o_cmp, o_slc, o_win, _ = _in_offsets()
    lay = _Layout(D_ROW)
    lay.put(C_CQ, 0, MLA_Q_LORA)
    lay.put(C_CKV, MLA_Q_LORA, MLA_KV_LORA)
    half = MLA_ROPE // 2
    lay.put(C_MISC + MLA_NOPE, o_krope, MLA_ROPE)
    lay.put(C_MISC + MLA_NOPE + MLA_ROPE, o_krope + half, half, -1.0)
    lay.put(C_MISC + MLA_NOPE + MLA_ROPE + half, o_krope, half, 1.0)
    lay.put(C_KVCMP, o_cmp, 2 * NSA_KV_HEADS * NSA_DK)
    for hk in range(NSA_KV_HEADS):
        lay.put(C_KSLC + hk * LANE, o_slc + hk * NSA_DK, NSA_DK)
        lay.put(C_KWIN + hk * LANE, o_win + hk * NSA_DK, NSA_DK)
    return lay


def _w_in_col_layout():
    _, o_qnsa, _, o_slc, o_win, o_gate = _in_offsets()
    lay = _Layout(D_COL)
    lay.put(R_QNSA, o_qnsa, NSA_HEADS * NSA_DK, NSA_DK ** -0.5)
    lay.put(R_VSLC, o_slc + NSA_KV_HEADS * NSA_DK, NSA_KV_HEADS * NSA_DK)
    lay.put(R_VWIN, o_win + NSA_KV_HEADS * NSA_DK, NSA_KV_HEADS * NSA_DK)
    for hk in range(NSA_KV_HEADS):
        lay.put(R_GATE + hk * GATE_ROWS, o_gate + hk * NSA_GROUP * 3, NSA_GROUP * 3)
    return lay


def _w_uq_layout():
    lay = _Layout(MLA_HEADS * LANE)
    dq = MLA_NOPE + MLA_ROPE
    for h in range(MLA_HEADS):
        lay.put(h * LANE, h * dq, dq)
    return lay


def _w_uk_layout():
    lay = _Layout(MLA_HEADS * LANE)
    per = MLA_NOPE + MLA_V
    for h in range(MLA_HEADS):
        lay.put(h * LANE, h * per, MLA_NOPE)
    return lay


def _w_uv_layout():
    lay = _Layout(MLA_HEADS * MLA_V)
    per = MLA_NOPE + MLA_V
    for h in range(MLA_HEADS):
        lay.put(h * MLA_V, h * per + MLA_NOPE, MLA_V)
    return lay


def _t5_thresholds(max_dist):
    n = np.arange(max_dist, dtype=np.int64)
    max_exact = T5_BUCKETS // 2
    nf = np.maximum(n, 1).astype(np.float32)
    val = (np.log(nf / np.float32(max_exact)) / np.float32(math.log(T5_MAX_DIST / max_exact))
           * np.float32(T5_BUCKETS - max_exact))
    large = np.minimum(max_exact + val.astype(np.int32), T5_BUCKETS - 1)
    bucket = np.where(n < max_exact, n, large)
    assert np.all(np.diff(bucket) >= 0)
    frac = np.abs(val[max_exact + 1:T5_MAX_DIST] - np.round(val[max_exact + 1:T5_MAX_DIST]))
    assert frac.min() > 1e-3
    return [int(np.argmax(bucket >= b)) for b in range(T5_BUCKETS)]


def _proj_kernel(x_ref, gmix_ref, wrow_ref, wcol_ref, gq_ref, wq_ref, gkv_ref, wk_ref, wv_ref,
                 cq_ref, sq_ref, ck_ref, sk_ref,
                 qm_ref, km_ref, vm_ref, qn_ref, kvc_ref, ks_ref, vs_ref, kw_ref, vw_ref, gate_ref):
    tm = x_ref.shape[1]
    bf = jnp.bfloat16
    xn = _rms(x_ref[0], gmix_ref[...]).astype(bf)
    u = _dot(xn, wrow_ref[...])
    ut = _dot_nt(wcol_ref[...], xn)

    cqn = _rms(u[:, C_CQ:C_CQ + MLA_Q_LORA], gq_ref[...]).astype(bf)
    qq = _dot_nt(wq_ref[...], cqn)
    cosq = cq_ref[...]
    sinq = sq_ref[...]
    r0, r1, r2 = MLA_NOPE, MLA_NOPE + MLA_ROPE // 2, MLA_NOPE + MLA_ROPE
    for h in range(MLA_HEADS):
        q = qq[h * LANE:(h + 1) * LANE]
        rot = jnp.concatenate([q[0:r0], -q[r1:r2], q[r0:r1], q[r2:]], axis=0)
        qm_ref[0, h] = (q * cosq + rot * sinq).astype(bf)

    ckvn = _rms(u[:, C_CKV:C_CKV + MLA_KV_LORA], gkv_ref[...]).astype(bf)
    kk = _dot(ckvn, wk_ref[...])
    misc = u[:, C_MISC:C_MISC + LANE]
    kpe = misc * ck_ref[...] + pltpu.roll(misc, LANE - MLA_ROPE, axis=1) * sk_ref[...]
    for h in range(MLA_HEADS):
        km_ref[0, h] = (kk[:, h * LANE:(h + 1) * LANE] + kpe).astype(bf)
    vv = _dot_nt(wv_ref[...], ckvn)
    for h in range(MLA_HEADS):
        vm_ref[0, h] = vv[h * MLA_V:(h + 1) * MLA_V].astype(bf)

    for h in range(NSA_HEADS):
        qn_ref[0, h] = (ut[R_QNSA + h * NSA_DK:R_QNSA + (h + 1) * NSA_DK] * LOG2E).astype(bf)
    for kv in range(2):
        kvc_ref[0, kv] = u[:, C_KVCMP + kv * LANE:C_KVCMP + (kv + 1) * LANE]
    pos = pl.program_id(1) * tm + lax.broadcasted_iota(jnp.int32, (tm, LANE), 0)
    lane = lax.broadcasted_iota(jnp.int32, (tm, LANE), 1)
    onehot = jnp.where(lane - (LANE - SLC_ROWS) == pos // SLC_LEN, 1.0, 0.0)
    for hk in range(NSA_KV_HEADS):
        ks_ref[0, hk] = (u[:, C_KSLC + hk * LANE:C_KSLC + (hk + 1) * LANE] + onehot).astype(bf)
        kw_ref[0, hk] = u[:, C_KWIN + hk * LANE:C_KWIN + (hk + 1) * LANE].astype(bf)
        vs_ref[0, hk] = ut[R_VSLC + hk * NSA_DK:R_VSLC + (hk + 1) * NSA_DK].astype(bf)
        vw_ref[0, hk] = ut[R_VWIN + hk * NSA_DK:R_VWIN + (hk + 1) * NSA_DK].astype(bf)
        gate_ref[0, hk] = jax.nn.sigmoid(ut[R_GATE + hk * GATE_ROWS:R_GATE + (hk + 1) * GATE_ROWS])


def _proj(x, gmix, w_row, w_col, gq, w_q2, gkv, w_k, w_v, cosq_t, sinq_t, cosk, sink):
    B, S, D = x.shape
    tm = min(PROJ_TM, S)
    bf = jnp.bfloat16
    grid = (B, S // tm)
    tok_major = lambda n: pl.BlockSpec((1, n, tm, LANE), lambda b, i: (b, 0, i, 0))
    feat_major = lambda n, d: pl.BlockSpec((1, n, d, tm), lambda b, i: (b, 0, 0, i))
    out_shape = (
        jax.ShapeDtypeStruct((B, MLA_HEADS, LANE, S), bf),
        jax.ShapeDtypeStruct((B, MLA_HEADS, S, LANE), bf),
        jax.ShapeDtypeStruct((B, MLA_HEADS, MLA_V, S), bf),
        jax.ShapeDtypeStruct((B, NSA_HEADS, NSA_DK, S), bf),
        jax.ShapeDtypeStruct((B, 2, S, LANE), jnp.float32),
        jax.ShapeDtypeStruct((B, NSA_KV_HEADS, S, LANE), bf),
        jax.ShapeDtypeStruct((B, NSA_KV_HEADS, NSA_DK, S), bf),
        jax.ShapeDtypeStruct((B, NSA_KV_HEADS, S, LANE), bf),
        jax.ShapeDtypeStruct((B, NSA_KV_HEADS, NSA_DK, S), bf),
        jax.ShapeDtypeStruct((B, NSA_KV_HEADS, GATE_ROWS, S), jnp.float32),
    )
    out_specs = (
        feat_major(MLA_HEADS, LANE), tok_major(MLA_HEADS), feat_major(MLA_HEADS, MLA_V),
        feat_major(NSA_HEADS, NSA_DK),
        tok_major(2),
        tok_major(NSA_KV_HEADS), feat_major(NSA_KV_HEADS, NSA_DK),
        tok_major(NSA_KV_HEADS), feat_major(NSA_KV_HEADS, NSA_DK),
        feat_major(NSA_KV_HEADS, GATE_ROWS),
    )
    consts = (gmix, w_row, w_col, gq, w_q2, gkv, w_k, w_v)
    in_specs = ([pl.BlockSpec((1, tm, D), lambda b, i: (b, i, 0))] + [_const_spec(c.shape) for c in consts]
                + [pl.BlockSpec((LANE, tm), lambda b, i: (0, i)), pl.BlockSpec((LANE, tm), lambda b, i: (0, i)),
                   pl.BlockSpec((tm, LANE), lambda b, i: (i, 0)), pl.BlockSpec((tm, LANE), lambda b, i: (i, 0))])
    return pl.pallas_call(
        _proj_kernel, grid=grid, in_specs=in_specs, out_specs=out_specs, out_shape=out_shape,
        compiler_params=_params(2), name="proj",
    )(x, *consts, cosq_t, sinq_t, cosk, sink)


def _compress_kernel(c_ref, pos_ref, w1_ref, w2k_ref, w2vt_ref, kc_ref, vc_ref):
    nc = kc_ref.shape[2]
    bf = jnp.bfloat16
    hw = NSA_KV_HEADS * NSA_DK
    for kv in range(2):
        halves = []
        for half in range(CMP_LEN // CMP_STRIDE):
            acc = None
            for r in range(CMP_STRIDE):
                l = half * CMP_STRIDE + r
                x = c_ref[0, kv, pl.ds(r, nc, stride=CMP_STRIDE), :]
                part = _dot((x + pos_ref[kv, l]).astype(bf), w1_ref[kv, l])
                acc = part if acc is None else acc + part
            halves.append(acc)
        hid = halves[0] + pltpu.roll(halves[1], nc - 1, axis=0)
        act = jax.nn.gelu(hid).astype(bf)
        for hk in range(NSA_KV_HEADS):
            act_h = act[:, hk * CMP_HIDDEN:(hk + 1) * CMP_HIDDEN]
            if kv == 0:
                kc_ref[0, hk] = _dot(act_h, w2k_ref[...]).astype(bf)
            else:
                vc_ref[0, hk] = _dot_nt(w2vt_ref[...], act_h).astype(bf)


def _compress(kvc, pos, w1, w2k, w2vt):
    B, _, S, width = kvc.shape
    nc = S // CMP_STRIDE
    return pl.pallas_call(
        _compress_kernel, grid=(B,),
        in_specs=[pl.BlockSpec((1, 2, S, width), lambda b: (b, 0, 0, 0)),
                  _const_spec(pos.shape), _const_spec(w1.shape), _const_spec(w2k.shape),
                  _const_spec(w2vt.shape)],
        out_specs=(pl.BlockSpec((1, NSA_KV_HEADS, nc, LANE), lambda b: (b, 0, 0, 0)),
                   pl.BlockSpec((1, NSA_KV_HEADS, NSA_DK, nc), lambda b: (b, 0, 0, 0))),
        out_shape=(jax.ShapeDtypeStruct((B, NSA_KV_HEADS, nc, LANE), jnp.bfloat16),
                   jax.ShapeDtypeStruct((B, NSA_KV_HEADS, NSA_DK, nc), jnp.bfloat16)),
        compiler_params=_params(1), name="compress",
    )(kvc, pos, w1, w2k, w2vt)


def _t5_lookup(dist, tab_ref, head, thr):
    val = jnp.full(dist.shape, tab_ref[0, head], jnp.float32)
    for b in range(1, T5_BUCKETS):
        val = jnp.where(dist >= thr[b], tab_ref[b, head], val)
    return (val - tab_ref[T5_BUCKETS - 1, head]) * LOG2E


def _bias_cmp_kernel(tab_ref, out_ref, *, thr, tq, n_cmp):
    hk = pl.program_id(0)
    t = pl.program_id(1)
    nc = out_ref.shape[2]
    per_tile = tq // CMP_STRIDE
    band = per_tile + (T5_MAX_DIST + CMP_LEN) // CMP_STRIDE + SUBLANE
    band = min(-(-band // SUBLANE) * SUBLANE, nc)
    start = jnp.clip(t * per_tile - (band - per_tile), 0, nc - band)
    start = pl.multiple_of(start // SUBLANE * SUBLANE, SUBLANE)
    n_all = lax.broadcasted_iota(jnp.int32, (nc, NSA_GROUP * tq), 0)
    out_ref[0, 0] = jnp.where(n_all < start, 0.0, NEG)
    n = start + lax.broadcasted_iota(jnp.int32, (band, tq), 0)
    i = lax.broadcasted_iota(jnp.int32, (band, tq), 1)
    dist = t * tq + i - (n * CMP_STRIDE + CMP_LEN - 1)
    ok = jnp.logical_and(dist >= 0, n < n_cmp)
    for g in range(NSA_GROUP):
        val = _t5_lookup(dist, tab_ref, hk * NSA_GROUP + g, thr)
        out_ref[0, 0, pl.ds(start, band), g * tq:(g + 1) * tq] = jnp.where(ok, val, NEG)


def _bias_tile_kernel(tab_ref, out_ref, *, thr, tq):
    hk = pl.program_id(0)
    tk = out_ref.shape[2]
    j = lax.broadcasted_iota(jnp.int32, (tk, tq), 0)
    i = lax.broadcasted_iota(jnp.int32, (tk, tq), 1)
    for g in range(NSA_GROUP):
        head = hk * NSA_GROUP + g
        cols = slice(g * tq, (g + 1) * tq)
        d0 = i - j
        out_ref[0, 0, :, cols] = jnp.where(d0 >= 0, _t5_lookup(d0, tab_ref, head, thr), NEG)
        out_ref[0, 1, :, cols] = _t5_lookup(d0 + tk, tab_ref, head, thr)
        out_ref[0, 2, :, cols] = jnp.where(j > i, 0.0, NEG)
        out_ref[0, 3, :, cols] = jnp.full((tk, tq), NEG, jnp.float32)


def _bias_tables(t5_table, S):
    thr = _t5_thresholds(S)
    tq, tk = NSA_TQ, NSA_TK
    assert tq == tk and WINDOW == 2 * tk and tk >= T5_MAX_DIST
    nq = S // tq
    nc = S // CMP_STRIDE
    n_cmp = (S - CMP_LEN) // CMP_STRIDE + 1
    m = NSA_GROUP * tq
    smem = pl.BlockSpec(memory_space=pltpu.SMEM)
    bias_c = pl.pallas_call(
        functools.partial(_bias_cmp_kernel, thr=thr, tq=tq, n_cmp=n_cmp),
        grid=(NSA_KV_HEADS, nq), in_specs=[smem],
        out_specs=pl.BlockSpec((1, 1, nc, m), lambda h, t: (h, t, 0, 0)),
        out_shape=jax.ShapeDtypeStruct((NSA_KV_HEADS, nq, nc, m), jnp.float32),
        compiler_params=_params(2), name="bias_cmp",
    )(t5_table)
    tz = pl.pallas_call(
        functools.partial(_bias_tile_kernel, thr=thr, tq=tq),
        grid=(NSA_KV_HEADS,), in_specs=[smem],
        out_specs=pl.BlockSpec((1, 4, tk, m), lambda h: (h, 0, 0, 0)),
        out_shape=jax.ShapeDtypeStruct((NSA_KV_HEADS, 4, tk, m), jnp.float32),
        compiler_params=_params(1), name="bias_tile",
    )(t5_table)
    return bias_c, tz


def _flash_chains(chains, side_work=()):
    def scores(chain):
        k, q_t, _, bias_t = chain[:4]
        s = _dot(k, q_t)
        return s if bias_t is None else s + bias_t

    side_work = list(side_work)
    per_chain = -(-len(side_work) // len(chains))
    pending = [scores(c) for c in chains[:FLASH_LOOKAHEAD]]
    for i, chain in enumerate(chains):
        for thunk in side_work[i * per_chain:(i + 1) * per_chain]:
            thunk()
        s = pending.pop(0)
        if i + FLASH_LOOKAHEAD < len(chains):
            pending.append(scores(chains[i + FLASH_LOOKAHEAD]))
        v_aug, m_ref, acc_ref = chain[2], chain[4], chain[5]
        m_prev = m_ref[...]
        m_new = jnp.maximum(m_prev, jnp.max(s, axis=0, keepdims=True))
        alpha = jnp.exp2(m_prev - m_new)
        p = jnp.exp2(s - m_new).astype(jnp.bfloat16)
        acc_ref[...] = alpha * acc_ref[...] + _dot(v_aug, p)
        m_ref[...] = m_new


def _with_ones(v_t):
    tk = v_t.shape[1]
    row = lax.broadcasted_iota(jnp.int32, (ONES_ROWS, tk), 0)
    return jnp.concatenate([v_t, jnp.where(row == 0, 1.0, 0.0).astype(v_t.dtype)], axis=0)


def _col_chains(k, q_ref, v_aug, bias_ref, m_ref, acc_ref, ncol, width):
    chains = []
    for c0 in range(0, ncol, width):
        cols = slice(c0, c0 + width)
        chains.append((k, q_ref[:, cols], v_aug, None if bias_ref is None else bias_ref[:, cols],
                       m_ref.at[:, cols], acc_ref.at[:, cols]))
    return chains


def _flash_init(m_ref, acc_ref):
    m_ref[...] = jnp.full(m_ref.shape, -jnp.inf, jnp.float32)
    acc_ref[...] = jnp.zeros(acc_ref.shape, jnp.float32)


def _flash_result(acc_ref, dv):
    return acc_ref[0:dv, :] * (1.0 / acc_ref[dv:dv + 1, :])


def _mla_kernel(q_ref, k_ref, v_ref, o_ref, m_ref, acc_ref, *, tq, tk, cw):
    S = k_ref.shape[2]
    nqt = S // tq
    kk = lax.broadcasted_iota(jnp.int32, (tk, cw), 0)
    qq = lax.broadcasted_iota(jnp.int32, (tk, cw), 1)

    def q_tile(qt, carry):
        q0 = pl.multiple_of(qt * tq, tq)
        for e in range(2):
            _flash_init(m_ref.at[e], acc_ref.at[e])

        def step(k0, key_off, rows=tk):
            chains = []
            for e in range(2):
                k = k_ref[0, e, pl.ds(k0, rows), :]
                v_aug = _with_ones(v_ref[0, e, :, pl.ds(k0, rows)])
                for c0 in range(0, tq, cw):
                    bias = None
                    if key_off is not None:
                        if key_off > c0 + cw - 1:
                            continue
                        if key_off + tk - 1 > c0:
                            bias = jnp.where(kk + key_off <= qq + c0, 0.0, NEG)
                    cols = slice(c0, c0 + cw)
                    chains.append((k, q_ref[0, e, :, pl.ds(pl.multiple_of(q0 + c0, cw), cw)], v_aug, bias,
                                   m_ref.at[e, :, cols], acc_ref.at[e, :, cols]))
            return chains

        per = MLA_TRIP_KEYS // MLA_FAR_TK

        def far(c, carry2):
            chains = []
            for j in range(per):
                chains += step(pl.multiple_of(c * MLA_TRIP_KEYS + j * MLA_FAR_TK, MLA_FAR_TK), None, MLA_FAR_TK)
            _flash_chains(chains)
            return carry2

        lax.fori_loop(0, qt * (tq // MLA_TRIP_KEYS), far, 0)
        chains = []
        for j in range(tq // tk):
            chains += step(pl.multiple_of(q0 + j * tk, tk), j * tk)
        _flash_chains(chains)
        o_t = jnp.concatenate([_flash_result(acc_ref.at[e], MLA_V) for e in range(2)], axis=0)
        o_ref[0, pl.ds(q0, tq), :] = jnp.transpose(o_t)
        return carry

    lax.fori_loop(0, nqt, q_tile, 0)


def _mla(q_t, k, v_t):
    B, H, S, _ = k.shape
    tq = min(MLA_TQ, S)
    tk = min(MLA_TK, tq)
    cw = min(MLA_CW, tq)
    return pl.pallas_call(
        functools.partial(_mla_kernel, tq=tq, tk=tk, cw=cw),
        grid=(B, H // 2),
        in_specs=[pl.BlockSpec((1, 2, LANE, S), lambda b, p: (b, p, 0, 0)),
                  pl.BlockSpec((1, 2, S, LANE), lambda b, p: (b, p, 0, 0)),
                  pl.BlockSpec((1, 2, MLA_V, S), lambda b, p: (b, p, 0, 0))],
        out_specs=pl.BlockSpec((1, S, 2 * MLA_V), lambda b, p: (b, 0, p)),
        out_shape=jax.ShapeDtypeStruct((B, S, H * MLA_V), jnp.float32),
        scratch_shapes=[pltpu.VMEM((2, 1, tq), jnp.float32),
                        pltpu.VMEM((2, MLA_V + ONES_ROWS, tq), jnp.float32)],
        compiler_params=_params(2), name="mla",
    )(q_t, k, v_t)


def _nsa_kernel(q_ref, ks_ref, vs_ref, kw_ref, vw_ref, kc_ref, vc_ref, gate_ref, bc_ref, tz_ref, ovt_ref,
                o_ref, qaug_ref, qpad_ref, m_ref, acc_ref, mw_ref, accw_ref, oc_ref, imp_ref, *, tq, tk):
    t = pl.program_id(1)
    G = NSA_GROUP
    HK = range(NSA_KV_HEADS)
    ncol = G * tq
    bf = jnp.bfloat16
    q0 = t * tq
    for hk in HK:
        for g in range(G):
            qpad_ref[hk, 0:NSA_DK, g * tq:(g + 1) * tq] = q_ref[0, hk * G + g]
            qaug_ref[hk, 0:NSA_DK, g * tq:(g + 1) * tq] = q_ref[0, hk * G + g]
        qpad_ref[hk, NSA_DK:, :] = jnp.zeros((SLC_ROWS, ncol), bf)

    def tile_chains(k_ref, v_ref, qx_ref, c, bias_idx, stats):
        k0 = pl.multiple_of(c * tk, tk)
        chains = []
        for hk in HK:
            bias_ref = None if bias_idx is None else tz_ref.at[hk, bias_idx]
            chains += _col_chains(k_ref[0, hk, pl.ds(k0, tk), :], qx_ref.at[hk],
                                  _with_ones(v_ref[0, hk, :, pl.ds(k0, tk)]),
                                  bias_ref, stats[0].at[hk], stats[1].at[hk], ncol, tq)
        return chains

    prev1 = jnp.maximum(t - 1, 0)
    bias1 = jnp.where(t >= 1, 1, 3)

    nb = ovt_ref.shape[0]
    sub = SUBLANE
    ci = lax.broadcasted_iota(jnp.int32, (1, ncol), 1)
    col_ok = q0 + ci % tq >= CMP_LEN - 1
    jb = lax.broadcasted_iota(jnp.int32, (nb, tq), 0)
    cur = (q0 + lax.broadcasted_iota(jnp.int32, (nb, tq), 1)) // SLC_LEN
    forced = jnp.logical_or(jb == 0, jnp.logical_or(jb == cur, jb == cur - 1))
    jsub = lax.broadcasted_iota(jnp.int32, (sub, tq), 0)
    for hk in HK:
        s = _dot(kc_ref[0, hk], qpad_ref[hk]) + bc_ref[hk, 0]
        mx = jnp.max(s, axis=0, keepdims=True)
        p = jnp.exp2(s - mx)
        lsum = jnp.sum(p, axis=0, keepdims=True)
        pcb = (p * jnp.where(col_ok, 1.0 / lsum, 0.0)).astype(bf)
        both = _dot(jnp.concatenate([vc_ref[0, hk], ovt_ref[...]], axis=0), pcb)
        oc_ref[hk] = both[0:NSA_DK]
        imp = both[NSA_DK:, 0:tq]
        for g in range(1, G):
            imp = imp + both[NSA_DK:, g * tq:(g + 1) * tq]
        imp_ref[hk] = jnp.where(forced, FORCE, jnp.where(jb <= cur, imp, NEG))

    win = (mw_ref, accw_ref)
    _flash_init(*win)
    _flash_chains(tile_chains(kw_ref, vw_ref, qpad_ref, jnp.maximum(t - 2, 0), jnp.where(t >= 2, 2, 3), win)
                  + tile_chains(kw_ref, vw_ref, qpad_ref, prev1, bias1, win)
                  + tile_chains(kw_ref, vw_ref, qpad_ref, t, 0, win))

    tiles_per_group = 16 * SLC_LEN // tq
    n_groups = nb // 16

    def write_mask(hk, selb_rows):
        selb = jnp.concatenate(selb_rows, axis=0).astype(bf)
        for g in range(G):
            qaug_ref[hk, NSA_DK:, g * tq:(g + 1) * tq] = selb

    def select(n_act):
        tail = [jnp.full((nb - n_act, tq), NEG, jnp.float32)] if n_act < nb else []
        for hk in HK:
            imp = imp_ref[hk, 0:n_act, :]
            if n_act <= SLC_TOPN:
                write_mask(hk, [jnp.where(imp > 0.5 * NEG, 0.0, NEG)] + tail)
                continue
            slabs = [imp[r0 * sub:(r0 + 1) * sub] for r0 in range(n_act // sub)]
            ranks = [jnp.zeros((sub, tq), jnp.int32) for _ in slabs]
            for jp in range(n_act):
                rowv = imp[jp:jp + 1, :]
                for r0, slab in enumerate(slabs):
                    lo = r0 * sub
                    if lo > jp:
                        one = jnp.where(rowv >= slab, 1, 0)
                    elif lo + sub - 1 <= jp:
                        one = jnp.where(rowv > slab, 1, 0)
                    else:
                        one = jnp.where(jsub + lo > jp, jnp.where(rowv >= slab, 1, 0),
                                        jnp.where(rowv > slab, 1, 0))
                    ranks[r0] = ranks[r0] + one
            write_mask(hk, [jnp.where(r < SLC_TOPN, 0.0, NEG) for r in ranks] + tail)

    for grp in range(n_groups):
        pl.when(t // tiles_per_group == grp)(functools.partial(select, 16 * (grp + 1)))

    slc = (m_ref, acc_ref)
    _flash_init(*slc)
    n_far = jnp.maximum(t - 1, 0)

    done = 0
    size = NSA_FAR_GROUP
    while size >= 1:
        trips = (n_far - done) // size

        def far(c, carry, size=size, done=done):
            chains = []
            for j in range(size):
                chains += tile_chains(ks_ref, vs_ref, qaug_ref, done + c * size + j, None, slc)
            _flash_chains(chains)
            return carry

        lax.fori_loop(0, trips, far, 0)
        done = done + trips * size
        size //= 2
    _flash_chains(tile_chains(ks_ref, vs_ref, qaug_ref, prev1, bias1, slc)
                  + tile_chains(ks_ref, vs_ref, qaug_ref, t, 0, slc))

    for hk in HK:
        gate = gate_ref[0, hk]
        comb = []
        for g in range(G):
            cols = slice(g * tq, (g + 1) * tq)
            gr = lambda br: gate[g * 3 + br:g * 3 + br + 1, :]
            w_s = gr(1) / acc_ref[hk, NSA_DK:NSA_DK + 1, cols]
            w_w = gr(2) / accw_ref[hk, NSA_DK:NSA_DK + 1, cols]
            comb.append(gr(0) * oc_ref[hk, :, cols] + w_s * acc_ref[hk, 0:NSA_DK, cols]
                        + w_w * accw_ref[hk, 0:NSA_DK, cols])
        width = G * NSA_DK
        o_ref[0, :, hk * width:(hk + 1) * width] = jnp.transpose(jnp.concatenate(comb, axis=0))


def _nsa(q_t, ks, vs_t, kw, vw_t, kc, vc_t, gate_t, bias_c, tz, ovt):
    B, _, S, _ = ks.shape
    tq, tk = NSA_TQ, NSA_TK
    nq = S // tq
    nc = kc.shape[2]
    m = NSA_GROUP * tq
    hkv = NSA_KV_HEADS
    tok = lambda: pl.BlockSpec((1, hkv, S, LANE), lambda b, t: (b, 0, 0, 0))
    feat = lambda: pl.BlockSpec((1, hkv, NSA_DK, S), lambda b, t: (b, 0, 0, 0))
    return pl.pallas_call(
        functools.partial(_nsa_kernel, tq=tq, tk=tk),
        grid=(B, nq),
        in_specs=[pl.BlockSpec((1, NSA_HEADS, NSA_DK, tq), lambda b, t: (b, 0, 0, t)),
                  tok(), feat(), tok(), feat(),
                  pl.BlockSpec((1, hkv, nc, LANE), lambda b, t: (b, 0, 0, 0)),
                  pl.BlockSpec((1, hkv, NSA_DK, nc), lambda b, t: (b, 0, 0, 0)),
                  pl.BlockSpec((1, hkv, GATE_ROWS, tq), lambda b, t: (b, 0, 0, t)),
                  pl.BlockSpec((hkv, 1, nc, m), lambda b, t: (0, t, 0, 0)),
                  _const_spec(tz.shape),
                  _const_spec(ovt.shape)],
        out_specs=pl.BlockSpec((1, tq, NSA_HEADS * NSA_DK), lambda b, t: (b, t, 0)),
        out_shape=jax.ShapeDtypeStruct((B, S, NSA_HEADS * NSA_DK), jnp.float32),
        scratch_shapes=[pltpu.VMEM((hkv, NSA_DK + SLC_ROWS, m), jnp.bfloat16),
                        pltpu.VMEM((hkv, NSA_DK + SLC_ROWS, m), jnp.bfloat16),
                        pltpu.VMEM((hkv, 1, m), jnp.float32),
                        pltpu.VMEM((hkv, NSA_DK + ONES_ROWS, m), jnp.float32),
                        pltpu.VMEM((hkv, 1, m), jnp.float32),
                        pltpu.VMEM((hkv, NSA_DK + ONES_ROWS, m), jnp.float32),
                        pltpu.VMEM((hkv, NSA_DK, m), jnp.float32),
                        pltpu.VMEM((hkv, SLC_ROWS, tq), jnp.float32)],
        compiler_params=_params(2), name="nsa",
    )(q_t, ks, vs_t, kw, vw_t, kc, vc_t, gate_t, bias_c, tz, ovt)


def _post_kernel(x_ref, om_ref, on_ref, gm_ref, gn_ref, wo_ref, gf_ref, wg_ref, wu_ref, wd_ref,
                 gfin_ref, o_ref):
    half = om_ref.shape[2]
    mix_m = _rms(om_ref[0], gm_ref[...]).astype(jnp.bfloat16)
    mix_n = _rms(on_ref[0], gn_ref[...]).astype(jnp.bfloat16)
    h = x_ref[0] + _dot(mix_m, wo_ref[0:half, :]) + _dot(mix_n, wo_ref[half:2 * half, :])
    f = _rms(h, gf_ref[...]).astype(jnp.bfloat16)
    a = _dot(f, wg_ref[...])
    act = (a * jax.nn.sigmoid(a) * _dot(f, wu_ref[...])).astype(jnp.bfloat16)
    h = h + _dot(act, wd_ref[...])
    o_ref[0] = _rms(h, gfin_ref[...])


def _post(x, o_mla, o_nsa, gm, gn, w_out, gf, wg, wu, wd, gfin):
    B, S, D = x.shape
    tm = min(POST_TM, S)
    tok = lambda w: pl.BlockSpec((1, tm, w), lambda b, i: (b, i, 0))
    consts = (gm, gn, w_out, gf, wg, wu, wd, gfin)
    return pl.pallas_call(
        _post_kernel, grid=(B, S // tm),
        in_specs=[tok(D), tok(o_mla.shape[2]), tok(o_nsa.shape[2])] + [_const_spec(c.shape) for c in consts],
        out_specs=tok(D), out_shape=jax.ShapeDtypeStruct((B, S, D), jnp.float32),
        compiler_params=_params(2), name="post",
    )(x, o_mla, o_nsa, *consts)


def _rope_tables(S):
    pos = jnp.arange(S, dtype=jnp.float32)
    inv = ROPE_THETA ** (-jnp.arange(0, MLA_ROPE, 2, dtype=jnp.float32) / MLA_ROPE)
    ang = pos[:, None] * inv[None, :]
    cos, sin = jnp.cos(ang), jnp.sin(ang)
    cos2 = jnp.concatenate([cos, cos], axis=-1)
    sin2 = jnp.concatenate([sin, sin], axis=-1)
    scale = (MLA_NOPE + MLA_ROPE) ** -0.5 * LOG2E
    zq = jnp.zeros((S, LANE - MLA_NOPE - MLA_ROPE), jnp.float32)
    cosq = jnp.concatenate([jnp.ones((S, MLA_NOPE), jnp.float32), cos2, zq], axis=-1) * scale
    sinq = jnp.concatenate([jnp.zeros((S, MLA_NOPE), jnp.float32), sin2, zq], axis=-1) * scale
    zk0 = jnp.zeros((S, MLA_NOPE), jnp.float32)
    cosk = jnp.concatenate([zk0, cos2, zq], axis=-1)
    sink = jnp.concatenate([zk0, sin2, zq], axis=-1)
    return cosq.T, sinq.T, cosk, sink


def _overlap_t(S):
    n_cmp = (S - CMP_LEN) // CMP_STRIDE + 1
    n_slc = S // SLC_LEN
    assert n_slc <= SLC_ROWS
    cs = np.arange(n_cmp) * CMP_STRIDE
    ss = np.arange(n_slc) * SLC_LEN
    ov = np.maximum(0, np.minimum(cs[:, None] + CMP_LEN, ss[None, :] + SLC_LEN)
                    - np.maximum(cs[:, None], ss[None, :])).astype(np.float32) / CMP_STRIDE
    out = np.zeros((SLC_ROWS, S // CMP_STRIDE), np.float32)
    out[:n_slc, :n_cmp] = ov.T
    return jnp.asarray(out, jnp.bfloat16)


def kernel(x, norm_mix_g, w_in, mla_q_norm_g, mla_w_uq, mla_kv_norm_g, mla_w_ukv, nsa_cmp_pos_k, nsa_cmp_w1_k, nsa_cmp_w2_k, nsa_cmp_pos_v, nsa_cmp_w1_v, nsa_cmp_w2_v, t5_table, out_norm_mla_g, out_norm_nsa_g, w_out, norm_ffn_g, w_gate, w_up, w_down, final_norm_g):
    B, S, D = x.shape
    assert w_in.shape[0] == 1
    assert D == D_MODEL and S % NSA_TQ == 0 and S % CMP_STRIDE == 0
    bf = jnp.bfloat16
    l = 0
    cosq_t, sinq_t, cosk, sink = _rope_tables(S)
    bias_c, tz = _bias_tables(t5_table, S)
    ovt = _overlap_t(S)
    nc = S // CMP_STRIDE
    cw = CMP_STRIDE * NSA_DK
    row = lambda v: v.reshape(1, -1)

    w_row = _w_in_row_layout().apply(w_in[l])
    w_col = _w_in_col_layout().apply(w_in[l]).T
    w_q2 = _w_uq_layout().apply(mla_w_uq[l]).T
    w_k = _w_uk_layout().apply(mla_w_ukv[l])
    w_v = _w_uv_layout().apply(mla_w_ukv[l]).T
    (q_mla, k_mla, v_mla, q_nsa, kv_cmp, k_slc, v_slc, k_win, v_win, gates) = _proj(
        x, row(norm_mix_g[l]), w_row, w_col, row(mla_q_norm_g[l]), w_q2, row(mla_kv_norm_g[l]), w_k, w_v,
        cosq_t, sinq_t, cosk, sink)

    def per_offset(pos_l, w1_l):
        pos2 = jnp.tile(pos_l, (1, NSA_KV_HEADS)).reshape(CMP_LEN, 1, NSA_KV_HEADS * NSA_DK)
        w = w1_l.reshape(CMP_LEN, NSA_DK, CMP_HIDDEN)
        z = jnp.zeros_like(w)
        return pos2, jnp.concatenate([jnp.concatenate([w, z], axis=2), jnp.concatenate([z, w], axis=2)], axis=1)

    pos_k, w1_k = per_offset(nsa_cmp_pos_k[l], nsa_cmp_w1_k[l])
    pos_v, w1_v = per_offset(nsa_cmp_pos_v[l], nsa_cmp_w1_v[l])
    pos = jnp.stack([pos_k, pos_v])
    w1 = jnp.stack([w1_k, w1_v]).astype(bf)
    kvc = kv_cmp
    w2k = jnp.concatenate([nsa_cmp_w2_k[l], jnp.zeros((CMP_HIDDEN, LANE - NSA_DK), jnp.float32)],
                          axis=1).astype(bf)
    w2vt = nsa_cmp_w2_v[l].T.astype(bf)
    k_cmp, v_cmp = _compress(kvc, pos, w1, w2k, w2vt)

    o_mla = _mla(q_mla, k_mla, v_mla)
    o_nsa = _nsa(q_nsa, k_slc, v_slc, k_win, v_win, k_cmp, v_cmp, gates, bias_c, tz, ovt)

    return _post(x, o_mla, o_nsa, row(out_norm_mla_g[l]), row(out_norm_nsa_g[l]), w_out[l].astype(bf),
                 row(norm_ffn_g[l]), w_gate[l].astype(bf), w_up[l].astype(bf), w_down[l].astype(bf),
                 row(final_norm_g))
```

```python
import functools
import math

import numpy as np
import jax
import jax.numpy as jnp
from jax import lax
from jax.experimental import pallas as pl
from jax.experimental.pallas import tpu as pltpu

D_MODEL = 1024
MLA_HEADS = 8
MLA_NOPE = 64
MLA_ROPE = 32
MLA_V = 64
MLA_Q_LORA = 256
MLA_KV_LORA = 128
ROPE_THETA = 10000.0
NSA_HEADS = 8
NSA_KV_HEADS = 2
NSA_GROUP = 4
NSA_DK = 64
CMP_LEN = 32
CMP_STRIDE = 16
CMP_HIDDEN = 128
SLC_LEN = 64
SLC_TOPN = 16
WINDOW = 512
T5_BUCKETS = 32
T5_MAX_DIST = 128
D_FF = 2816
EPS = 1e-6
NEG = -1e30
FORCE = 1e30

LANE = 128
SUBLANE = 8
SLC_ROWS = 64
GATE_ROWS = 16
ONES_ROWS = 16
LOG2E = math.log2(math.e)

PROJ_TM = 512
MLA_TQ = 2048
MLA_TK = 256
MLA_FAR_TK = 256
MLA_TRIP_KEYS = 1024
MLA_CW = 256
FLASH_LOOKAHEAD = 6
NSA_TQ = 256
NSA_TK = 256
NSA_FAR_GROUP = 8
POST_TM = 512
VMEM_LIMIT = 56 * 1024 * 1024

C_CQ = 0
C_CKV = 256
C_MISC = 384
C_KVCMP = 512
C_KSLC = 768
C_KWIN = 1024
D_ROW = 1280
R_QNSA = 0
R_VSLC = 512
R_VWIN = 640
R_GATE = 768
D_COL = 800


def _dot(a, b):
    return jnp.dot(a, b, preferred_element_type=jnp.float32)


def _dot_nt(a, b):
    return lax.dot_general(a, b, (((1,), (1,)), ((), ())), preferred_element_type=jnp.float32)


def _rms(x, g):
    return x * lax.rsqrt(jnp.mean(x * x, axis=-1, keepdims=True) + EPS) * g


def _const_spec(shape):
    nd = len(shape)
    return pl.BlockSpec(shape, lambda *_: (0,) * nd, pipeline_mode=pl.Buffered(1))


def _params(n_axes):
    return pltpu.CompilerParams(dimension_semantics=("arbitrary",) * n_axes,
                                vmem_limit_bytes=VMEM_LIMIT)


def _in_offsets():
    o_krope = MLA_Q_LORA + MLA_KV_LORA
    o_qnsa = o_krope + MLA_ROPE
    o_cmp = o_qnsa + NSA_HEADS * NSA_DK
    o_slc = o_cmp + 2 * NSA_KV_HEADS * NSA_DK
    o_win = o_slc + 2 * NSA_KV_HEADS * NSA_DK
    o_gate = o_win + 2 * NSA_KV_HEADS * NSA_DK
    return o_krope, o_qnsa, o_cmp, o_slc, o_win, o_gate


class _Layout:
    def __init__(self, n):
        self.src = np.zeros((n,), np.int32)
        self.mul = np.zeros((n,), np.float32)

    def put(self, dst, s, n, m=1.0):
        self.src[dst:dst + n] = np.arange(s, s + n)
        self.mul[dst:dst + n] = m

    def apply(self, w):
        return (w[:, self.src] * self.mul[None, :]).astype(jnp.bfloat16)


def _w_in_row_layout():
    o_krope, _, o_cmp, o_slc, o_win, _ = _in_offsets()
    lay = _Layout(D_ROW)
    lay.put(C_CQ, 0, MLA_Q_LORA)
    lay.put(C_CKV, MLA_Q_LORA, MLA_KV_LORA)
    half = MLA_ROPE // 2
    lay.put(C_MISC + MLA_NOPE, o_krope, MLA_ROPE)
    lay.put(C_MISC + MLA_NOPE + MLA_ROPE, o_krope + half, half, -1.0)
    lay.put(C_MISC + MLA_NOPE + MLA_ROPE + half, o_krope, half, 1.0)
    lay.put(C_KVCMP, o_cmp, 2 * NSA_KV_HEADS * NSA_DK)
    for hk in range(NSA_KV_HEADS):
        lay.put(C_KSLC + hk * LANE, o_slc + hk * NSA_DK, NSA_DK)
        lay.put(C_KWIN + hk * LANE, o_win + hk * NSA_DK, NSA_DK)
    return lay


def _w_in_col_layout():
    _, o_qnsa, _, o_slc, o_win, o_gate = _in_offsets()
    lay = _Layout(D_COL)
    lay.put(R_QNSA, o_qnsa, NSA_HEADS * NSA_DK, NSA_DK ** -0.5)
    lay.put(R_VSLC, o_slc + NSA_KV_HEADS * NSA_DK, NSA_KV_HEADS * NSA_DK)
    lay.put(R_VWIN, o_win + NSA_KV_HEADS * NSA_DK, NSA_KV_HEADS * NSA_DK)
    for hk in range(NSA_KV_HEADS):
        lay.put(R_GATE + hk * GATE_ROWS, o_gate + hk * NSA_GROUP * 3, NSA_GROUP * 3)
    return lay


def _w_uq_layout():
    lay = _Layout(MLA_HEADS * LANE)
    dq = MLA_NOPE + MLA_ROPE
    for h in range(MLA_HEADS):
        lay.put(h * LANE, h * dq, dq)
    return lay


def _w_uk_layout():
    lay = _Layout(MLA_HEADS * LANE)
    per = MLA_NOPE + MLA_V
    for h in range(MLA_HEADS):
        lay.put(h * LANE, h * per, MLA_NOPE)
    return lay


def _w_uv_layout():
    lay = _Layout(MLA_HEADS * MLA_V)
    per = MLA_NOPE + MLA_V
    for h in range(MLA_HEADS):
        lay.put(h * MLA_V, h * per + MLA_NOPE, MLA_V)
    return lay


def _t5_thresholds(max_dist):
    n = np.arange(max_dist, dtype=np.int64)
    max_exact = T5_BUCKETS // 2
    nf = np.maximum(n, 1).astype(np.float32)
    val = (np.log(nf / np.float32(max_exact)) / np.float32(math.log(T5_MAX_DIST / max_exact))
           * np.float32(T5_BUCKETS - max_exact))
    large = np.minimum(max_exact + val.astype(np.int32), T5_BUCKETS - 1)
    bucket = np.where(n < max_exact, n, large)
    assert np.all(np.diff(bucket) >= 0)
    frac = np.abs(val[max_exact + 1:T5_MAX_DIST] - np.round(val[max_exact + 1:T5_MAX_DIST]))
    assert frac.min() > 1e-3
    return [int(np.argmax(bucket >= b)) for b in range(T5_BUCKETS)]


def _proj_kernel(x_ref, gmix_ref, wrow_ref, wcol_ref, gq_ref, wq_ref, gkv_ref, wk_ref, wv_ref,
                 cq_ref, sq_ref, ck_ref, sk_ref,
                 qm_ref, km_ref, vm_ref, qn_ref, kvc_ref, ks_ref, vs_ref, kw_ref, vw_ref, gate_ref):
    tm = x_ref.shape[1]
    bf = jnp.bfloat16
    xn = _rms(x_ref[0], gmix_ref[...]).astype(bf)
    u = _dot(xn, wrow_ref[...])
    ut = _dot_nt(wcol_ref[...], xn)

    cqn = _rms(u[:, C_CQ:C_CQ + MLA_Q_LORA], gq_ref[...]).astype(bf)
    qq = _dot_nt(wq_ref[...], cqn)
    cosq = cq_ref[...]
    sinq = sq_ref[...]
    r0, r1, r2 = MLA_NOPE, MLA_NOPE + MLA_ROPE // 2, MLA_NOPE + MLA_ROPE
    for h in range(MLA_HEADS):
        q = qq[h * LANE:(h + 1) * LANE]
        rot = jnp.concatenate([q[0:r0], -q[r1:r2], q[r0:r1], q[r2:]], axis=0)
        qm_ref[0, h] = (q * cosq + rot * sinq).astype(bf)

    ckvn = _rms(u[:, C_CKV:C_CKV + MLA_KV_LORA], gkv_ref[...]).astype(bf)
    kk = _dot(ckvn, wk_ref[...])
    misc = u[:, C_MISC:C_MISC + LANE]
    kpe = misc * ck_ref[...] + pltpu.roll(misc, LANE - MLA_ROPE, axis=1) * sk_ref[...]
    for h in range(MLA_HEADS):
        km_ref[0, h] = (kk[:, h * LANE:(h + 1) * LANE] + kpe).astype(bf)
    vv = _dot_nt(wv_ref[...], ckvn)
    for h in range(MLA_HEADS):
        vm_ref[0, h] = vv[h * MLA_V:(h + 1) * MLA_V].astype(bf)

    for h in range(NSA_HEADS):
        qn_ref[0, h] = (ut[R_QNSA + h * NSA_DK:R_QNSA + (h + 1) * NSA_DK] * LOG2E).astype(bf)
    for kv in range(2):
        kvc_ref[0, kv] = u[:, C_KVCMP + kv * LANE:C_KVCMP + (kv + 1) * LANE]
    pos = pl.program_id(1) * tm + lax.broadcasted_iota(jnp.int32, (tm, LANE), 0)
    lane = lax.broadcasted_iota(jnp.int32, (tm, LANE), 1)
    onehot = jnp.where(lane - (LANE - SLC_ROWS) == pos // SLC_LEN, 1.0, 0.0)
    for hk in range(NSA_KV_HEADS):
        ks_ref[0, hk] = (u[:, C_KSLC + hk * LANE:C_KSLC + (hk + 1) * LANE] + onehot).astype(bf)
        kw_ref[0, hk] = u[:, C_KWIN + hk * LANE:C_KWIN + (hk + 1) * LANE].astype(bf)
        vs_ref[0, hk] = ut[R_VSLC + hk * NSA_DK:R_VSLC + (hk + 1) * NSA_DK].astype(bf)
        vw_ref[0, hk] = ut[R_VWIN + hk * NSA_DK:R_VWIN + (hk + 1) * NSA_DK].astype(bf)
        gate_ref[0, hk] = jax.nn.sigmoid(ut[R_GATE + hk * GATE_ROWS:R_GATE + (hk + 1) * GATE_ROWS])


def _proj(x, gmix, w_row, w_col, gq, w_q2, gkv, w_k, w_v, cosq_t, sinq_t, cosk, sink):
    B, S, D = x.shape
    tm = min(PROJ_TM, S)
    bf = jnp.bfloat16
    grid = (B, S // tm)
    tok_major = lambda n: pl.BlockSpec((1, n, tm, LANE), lambda b, i: (b, 0, i, 0))
    feat_major = lambda n, d: pl.BlockSpec((1, n, d, tm), lambda b, i: (b, 0, 0, i))
    out_shape = (
        jax.ShapeDtypeStruct((B, MLA_HEADS, LANE, S), bf),
        jax.ShapeDtypeStruct((B, MLA_HEADS, S, LANE), bf),
        jax.ShapeDtypeStruct((B, MLA_HEADS, MLA_V, S), bf),
        jax.ShapeDtypeStruct((B, NSA_HEADS, NSA_DK, S), bf),
        jax.ShapeDtypeStruct((B, 2, S, LANE), jnp.float32),
        jax.ShapeDtypeStruct((B, NSA_KV_HEADS, S, LANE), bf),
        jax.ShapeDtypeStruct((B, NSA_KV_HEADS, NSA_DK, S), bf),
        jax.ShapeDtypeStruct((B, NSA_KV_HEADS, S, LANE), bf),
        jax.ShapeDtypeStruct((B, NSA_KV_HEADS, NSA_DK, S), bf),
        jax.ShapeDtypeStruct((B, NSA_KV_HEADS, GATE_ROWS, S), jnp.float32),
    )
    out_specs = (
        feat_major(MLA_HEADS, LANE), tok_major(MLA_HEADS), feat_major(MLA_HEADS, MLA_V),
        feat_major(NSA_HEADS, NSA_DK),
        tok_major(2),
        tok_major(NSA_KV_HEADS), feat_major(NSA_KV_HEADS, NSA_DK),
        tok_major(NSA_KV_HEADS), feat_major(NSA_KV_HEADS, NSA_DK),
        feat_major(NSA_KV_HEADS, GATE_ROWS),
    )
    consts = (gmix, w_row, w_col, gq, w_q2, gkv, w_k, w_v)
    in_specs = ([pl.BlockSpec((1, tm, D), lambda b, i: (b, i, 0))] + [_const_spec(c.shape) for c in consts]
                + [pl.BlockSpec((LANE, tm), lambda b, i: (0, i)), pl.BlockSpec((LANE, tm), lambda b, i: (0, i)),
                   pl.BlockSpec((tm, LANE), lambda b, i: (i, 0)), pl.BlockSpec((tm, LANE), lambda b, i: (i, 0))])
    return pl.pallas_call(
        _proj_kernel, grid=grid, in_specs=in_specs, out_specs=out_specs, out_shape=out_shape,
        compiler_params=_params(2), name="proj",
    )(x, *consts, cosq_t, sinq_t, cosk, sink)


def _compress_kernel(c_ref, pos_ref, w1_ref, w2k_ref, w2vt_ref, kc_ref, vc_ref):
    nc = kc_ref.shape[2]
    bf = jnp.bfloat16
    hw = NSA_KV_HEADS * NSA_DK
    for kv in range(2):
        halves = []
        for half in range(CMP_LEN // CMP_STRIDE):
            acc = None
            for r in range(CMP_STRIDE):
                l = half * CMP_STRIDE + r
                x = c_ref[0, kv, pl.ds(r, nc, stride=CMP_STRIDE), :]
                part = _dot((x + pos_ref[kv, l]).astype(bf), w1_ref[kv, l])
                acc = part if acc is None else acc + part
            halves.append(acc)
        hid = halves[0] + pltpu.roll(halves[1], nc - 1, axis=0)
        act = jax.nn.gelu(hid).astype(bf)
        for hk in range(NSA_KV_HEADS):
            act_h = act[:, hk * CMP_HIDDEN:(hk + 1) * CMP_HIDDEN]
            if kv == 0:
                kc_ref[0, hk] = _dot(act_h, w2k_ref[...]).astype(bf)
            else:
                vc_ref[0, hk] = _dot_nt(w2vt_ref[...], act_h).astype(bf)


def _compress(kvc, pos, w1, w2k, w2vt):
    B, _, S, width = kvc.shape
    nc = S // CMP_STRIDE
    return pl.pallas_call(
        _compress_kernel, grid=(B,),
        in_specs=[pl.BlockSpec((1, 2, S, width), lambda b: (b, 0, 0, 0)),
                  _const_spec(pos.shape), _const_spec(w1.shape), _const_spec(w2k.shape),
                  _const_spec(w2vt.shape)],
        out_specs=(pl.BlockSpec((1, NSA_KV_HEADS, nc, LANE), lambda b: (b, 0, 0, 0)),
                   pl.BlockSpec((1, NSA_KV_HEADS, NSA_DK, nc), lambda b: (b, 0, 0, 0))),
        out_shape=(jax.ShapeDtypeStruct((B, NSA_KV_HEADS, nc, LANE), jnp.bfloat16),
                   jax.ShapeDtypeStruct((B, NSA_KV_HEADS, NSA_DK, nc), jnp.bfloat16)),
        compiler_params=_params(1), name="compress",
    )(kvc, pos, w1, w2k, w2vt)


def _t5_lookup(dist, tab_ref, head, thr):
    val = jnp.full(dist.shape, tab_ref[0, head], jnp.float32)
    for b in range(1, T5_BUCKETS):
        val = jnp.where(dist >= thr[b], tab_ref[b, head], val)
    return (val - tab_ref[T5_BUCKETS - 1, head]) * LOG2E


def _bias_cmp_kernel(tab_ref, out_ref, *, thr, tq, n_cmp):
    hk = pl.program_id(0)
    t = pl.program_id(1)
    nc = out_ref.shape[2]
    per_tile = tq // CMP_STRIDE
    band = per_tile + (T5_MAX_DIST + CMP_LEN) // CMP_STRIDE + SUBLANE
    band = min(-(-band // SUBLANE) * SUBLANE, nc)
    start = jnp.clip(t * per_tile - (band - per_tile), 0, nc - band)
    start = pl.multiple_of(start // SUBLANE * SUBLANE, SUBLANE)
    n_all = lax.broadcasted_iota(jnp.int32, (nc, NSA_GROUP * tq), 0)
    out_ref[0, 0] = jnp.where(n_all < start, 0.0, NEG)
    n = start + lax.broadcasted_iota(jnp.int32, (band, tq), 0)
    i = lax.broadcasted_iota(jnp.int32, (band, tq), 1)
    dist = t * tq + i - (n * CMP_STRIDE + CMP_LEN - 1)
    ok = jnp.logical_and(dist >= 0, n < n_cmp)
    for g in range(NSA_GROUP):
        val = _t5_lookup(dist, tab_ref, hk * NSA_GROUP + g, thr)
        out_ref[0, 0, pl.ds(start, band), g * tq:(g + 1) * tq] = jnp.where(ok, val, NEG)


def _bias_tile_kernel(tab_ref, out_ref, *, thr, tq):
    hk = pl.program_id(0)
    tk = out_ref.shape[2]
    j = lax.broadcasted_iota(jnp.int32, (tk, tq), 0)
    i = lax.broadcasted_iota(jnp.int32, (tk, tq), 1)
    for g in range(NSA_GROUP):
        head = hk * NSA_GROUP + g
        cols = slice(g * tq, (g + 1) * tq)
        d0 = i - j
        out_ref[0, 0, :, cols] = jnp.where(d0 >= 0, _t5_lookup(d0, tab_ref, head, thr), NEG)
        out_ref[0, 1, :, cols] = _t5_lookup(d0 + tk, tab_ref, head, thr)
        out_ref[0, 2, :, cols] = jnp.where(j > i, 0.0, NEG)
        out_ref[0, 3, :, cols] = jnp.full((tk, tq), NEG, jnp.float32)


def _bias_tables(t5_table, S):
    thr = _t5_thresholds(S)
    tq, tk = NSA_TQ, NSA_TK
    assert tq == tk and WINDOW == 2 * tk and tk >= T5_MAX_DIST
    nq = S // tq
    nc = S // CMP_STRIDE
    n_cmp = (S - CMP_LEN) // CMP_STRIDE + 1
    m = NSA_GROUP * tq
    smem = pl.BlockSpec(memory_space=pltpu.SMEM)
    bias_c = pl.pallas_call(
        functools.partial(_bias_cmp_kernel, thr=thr, tq=tq, n_cmp=n_cmp),
        grid=(NSA_KV_HEADS, nq), in_specs=[smem],
        out_specs=pl.BlockSpec((1, 1, nc, m), lambda h, t: (h, t, 0, 0)),
        out_shape=jax.ShapeDtypeStruct((NSA_KV_HEADS, nq, nc, m), jnp.float32),
        compiler_params=_params(2), name="bias_cmp",
    )(t5_table)
    tz = pl.pallas_call(
        functools.partial(_bias_tile_kernel, thr=thr, tq=tq),
        grid=(NSA_KV_HEADS,), in_specs=[smem],
        out_specs=pl.BlockSpec((1, 4, tk, m), lambda h: (h, 0, 0, 0)),
        out_shape=jax.ShapeDtypeStruct((NSA_KV_HEADS, 4, tk, m), jnp.float32),
        compiler_params=_params(1), name="bias_tile",
    )(t5_table)
    return bias_c, tz


def _flash_chains(chains, side_work=()):
    def scores(chain):
        k, q_t, _, bias_t = chain[:4]
        s = _dot(k, q_t)
        return s if bias_t is None else s + bias_t

    side_work = list(side_work)
    per_chain = -(-len(side_work) // len(chains))
    pending = [scores(c) for c in chains[:FLASH_LOOKAHEAD]]
    for i, chain in enumerate(chains):
        for thunk in side_work[i * per_chain:(i + 1) * per_chain]:
            thunk()
        s = pending.pop(0)
        if i + FLASH_LOOKAHEAD < len(chains):
            pending.append(scores(chains[i + FLASH_LOOKAHEAD]))
        v_aug, m_ref, acc_ref = chain[2], chain[4], chain[5]
        m_prev = m_ref[...]
        m_new = jnp.maximum(m_prev, jnp.max(s, axis=0, keepdims=True))
        alpha = jnp.exp2(m_prev - m_new)
        p = jnp.exp2(s - m_new).astype(jnp.bfloat16)
        acc_ref[...] = alpha * acc_ref[...] + _dot(v_aug, p)
        m_ref[...] = m_new


def _with_ones(v_t):
    tk = v_t.shape[1]
    row = lax.broadcasted_iota(jnp.int32, (ONES_ROWS, tk), 0)
    return jnp.concatenate([v_t, jnp.where(row == 0, 1.0, 0.0).astype(v_t.dtype)], axis=0)


def _col_chains(k, q_ref, v_aug, bias_ref, m_ref, acc_ref, ncol, width):
    chains = []
    for c0 in range(0, ncol, width):
        cols = slice(c0, c0 + width)
        chains.append((k, q_ref[:, cols], v_aug, None if bias_ref is None else bias_ref[:, cols],
                       m_ref.at[:, cols], acc_ref.at[:, cols]))
    return chains


def _flash_init(m_ref, acc_ref):
    m_ref[...] = jnp.full(m_ref.shape, -jnp.inf, jnp.float32)
    acc_ref[...] = jnp.zeros(acc_ref.shape, jnp.float32)


def _flash_result(acc_ref, dv):
    return acc_ref[0:dv, :] * (1.0 / acc_ref[dv:dv + 1, :])


def _mla_kernel(q_ref, k_ref, v_ref, o_ref, m_ref, acc_ref, *, tq, tk, cw):
    S = k_ref.shape[2]
    nqt = S // tq
    kk = lax.broadcasted_iota(jnp.int32, (tk, cw), 0)
    qq = lax.broadcasted_iota(jnp.int32, (tk, cw), 1)

    def q_tile(qt, carry):
        q0 = pl.multiple_of(qt * tq, tq)
        for e in range(2):
            _flash_init(m_ref.at[e], acc_ref.at[e])

        def step(k0, key_off, rows=tk):
            chains = []
            for e in range(2):
                k = k_ref[0, e, pl.ds(k0, rows), :]
                v_aug = _with_ones(v_ref[0, e, :, pl.ds(k0, rows)])
                for c0 in range(0, tq, cw):
                    bias = None
                    if key_off is not None:
                        if key_off > c0 + cw - 1:
                            continue
                        if key_off + tk - 1 > c0:
                            bias = jnp.where(kk + key_off <= qq + c0, 0.0, NEG)
                    cols = slice(c0, c0 + cw)
                    chains.append((k, q_ref[0, e, :, pl.ds(pl.multiple_of(q0 + c0, cw), cw)], v_aug, bias,
                                   m_ref.at[e, :, cols], acc_ref.at[e, :, cols]))
            return chains

        per = MLA_TRIP_KEYS // MLA_FAR_TK

        def far(c, carry2):
            chains = []
            for j in range(per):
                chains += step(pl.multiple_of(c * MLA_TRIP_KEYS + j * MLA_FAR_TK, MLA_FAR_TK), None, MLA_FAR_TK)
            _flash_chains(chains)
            return carry2

        lax.fori_loop(0, qt * (tq // MLA_TRIP_KEYS), far, 0)
        chains = []
        for j in range(tq // tk):
            chains += step(pl.multiple_of(q0 + j * tk, tk), j * tk)
        _flash_chains(chains)
        o_t = jnp.concatenate([_flash_result(acc_ref.at[e], MLA_V) for e in range(2)], axis=0)
        o_ref[0, pl.ds(q0, tq), :] = jnp.transpose(o_t)
        return carry

    lax.fori_loop(0, nqt, q_tile, 0)


def _mla(q_t, k, v_t):
    B, H, S, _ = k.shape
    tq = min(MLA_TQ, S)
    tk = min(MLA_TK, tq)
    cw = min(MLA_CW, tq)
    return pl.pallas_call(
        functools.partial(_mla_kernel, tq=tq, tk=tk, cw=cw),
        grid=(B, H // 2),
        in_specs=[pl.BlockSpec((1, 2, LANE, S), lambda b, p: (b, p, 0, 0)),
                  pl.BlockSpec((1, 2, S, LANE), lambda b, p: (b, p, 0, 0)),
                  pl.BlockSpec((1, 2, MLA_V, S), lambda b, p: (b, p, 0, 0))],
        out_specs=pl.BlockSpec((1, S, 2 * MLA_V), lambda b, p: (b, 0, p)),
        out_shape=jax.ShapeDtypeStruct((B, S, H * MLA_V), jnp.float32),
        scratch_shapes=[pltpu.VMEM((2, 1, tq), jnp.float32),
                        pltpu.VMEM((2, MLA_V + ONES_ROWS, tq), jnp.float32)],
        compiler_params=_params(2), name="mla",
    )(q_t, k, v_t)


def _nsa_kernel(q_ref, ks_ref, vs_ref, kw_ref, vw_ref, kc_ref, vc_ref, gate_ref, bc_ref, tz_ref, ovt_ref,
                o_ref, qaug_ref, qpad_ref, m_ref, acc_ref, mw_ref, accw_ref, oc_ref, imp_ref, *, tq, tk):
    t = pl.program_id(1)
    G = NSA_GROUP
    HK = range(NSA_KV_HEADS)
    ncol = G * tq
    bf = jnp.bfloat16
    q0 = t * tq
    for hk in HK:
        for g in range(G):
            qpad_ref[hk, 0:NSA_DK, g * tq:(g + 1) * tq] = q_ref[0, hk * G + g]
            qaug_ref[hk, 0:NSA_DK, g * tq:(g + 1) * tq] = q_ref[0, hk * G + g]
        qpad_ref[hk, NSA_DK:, :] = jnp.zeros((SLC_ROWS, ncol), bf)

    def tile_chains(k_ref, v_ref, qx_ref, c, bias_idx, stats):
        k0 = pl.multiple_of(c * tk, tk)
        chains = []
        for hk in HK:
            bias_ref = None if bias_idx is None else tz_ref.at[hk, bias_idx]
            chains += _col_chains(k_ref[0, hk, pl.ds(k0, tk), :], qx_ref.at[hk],
                                  _with_ones(v_ref[0, hk, :, pl.ds(k0, tk)]),
                                  bias_ref, stats[0].at[hk], stats[1].at[hk], ncol, tq)
        return chains

    prev1 = jnp.maximum(t - 1, 0)
    bias1 = jnp.where(t >= 1, 1, 3)

    nb = ovt_ref.shape[0]
    sub = SUBLANE
    ci = lax.broadcasted_iota(jnp.int32, (1, ncol), 1)
    col_ok = q0 + ci % tq >= CMP_LEN - 1
    jb = lax.broadcasted_iota(jnp.int32, (nb, tq), 0)
    cur = (q0 + lax.broadcasted_iota(jnp.int32, (nb, tq), 1)) // SLC_LEN
    forced = jnp.logical_or(jb == 0, jnp.logical_or(jb == cur, jb == cur - 1))
    jsub = lax.broadcasted_iota(jnp.int32, (sub, tq), 0)
    for hk in HK:
        s = _dot(kc_ref[0, hk], qpad_ref[hk]) + bc_ref[hk, 0]
        mx = jnp.max(s, axis=0, keepdims=True)
        p = jnp.exp2(s - mx)
        lsum = jnp.sum(p, axis=0, keepdims=True)
        pcb = (p * jnp.where(col_ok, 1.0 / lsum, 0.0)).astype(bf)
        both = _dot(jnp.concatenate([vc_ref[0, hk], ovt_ref[...]], axis=0), pcb)
        oc_ref[hk] = both[0:NSA_DK]
        imp = both[NSA_DK:, 0:tq]
        for g in range(1, G):
            imp = imp + both[NSA_DK:, g * tq:(g + 1) * tq]
        imp_ref[hk] = jnp.where(forced, FORCE, jnp.where(jb <= cur, imp, NEG))

    win = (mw_ref, accw_ref)
    _flash_init(*win)
    _flash_chains(tile_chains(kw_ref, vw_ref, qpad_ref, jnp.maximum(t - 2, 0), jnp.where(t >= 2, 2, 3), win)
                  + tile_chains(kw_ref, vw_ref, qpad_ref, prev1, bias1, win)
                  + tile_chains(kw_ref, vw_ref, qpad_ref, t, 0, win))

    tiles_per_group = 16 * SLC_LEN // tq
    n_groups = nb // 16

    def write_mask(hk, selb_rows):
        selb = jnp.concatenate(selb_rows, axis=0).astype(bf)
        for g in range(G):
            qaug_ref[hk, NSA_DK:, g * tq:(g + 1) * tq] = selb

    def select(n_act):
        tail = [jnp.full((nb - n_act, tq), NEG, jnp.float32)] if n_act < nb else []
        for hk in HK:
            imp = imp_ref[hk, 0:n_act, :]
            if n_act <= SLC_TOPN:
                write_mask(hk, [jnp.where(imp > 0.5 * NEG, 0.0, NEG)] + tail)
                continue
            slabs = [imp[r0 * sub:(r0 + 1) * sub] for r0 in range(n_act // sub)]
            ranks = [jnp.zeros((sub, tq), jnp.int32) for _ in slabs]
            for jp in range(n_act):
                rowv = imp[jp:jp + 1, :]
                for r0, slab in enumerate(slabs):
                    lo = r0 * sub
                    if lo > jp:
                        one = jnp.where(rowv >= slab, 1, 0)
                    elif lo + sub - 1 <= jp:
                        one = jnp.where(rowv > slab, 1, 0)
                    else:
                        one = jnp.where(jsub + lo > jp, jnp.where(rowv >= slab, 1, 0),
                                        jnp.where(rowv > slab, 1, 0))
                    ranks[r0] = ranks[r0] + one
            write_mask(hk, [jnp.where(r < SLC_TOPN, 0.0, NEG) for r in ranks] + tail)

    for grp in range(n_groups):
        pl.when(t // tiles_per_group == grp)(functools.partial(select, 16 * (grp + 1)))

    slc = (m_ref, acc_ref)
    _flash_init(*slc)
    n_far = jnp.maximum(t - 1, 0)

    done = 0
    size = NSA_FAR_GROUP
    while size >= 1:
        trips = (n_far - done) // size

        def far(c, carry, size=size, done=done):
            chains = []
            for j in range(size):
                chains += tile_chains(ks_ref, vs_ref, qaug_ref, done + c * size + j, None, slc)
            _flash_chains(chains)
            return carry

        lax.fori_loop(0, trips, far, 0)
        done = done + trips * size
        size //= 2
    _flash_chains(tile_chains(ks_ref, vs_ref, qaug_ref, prev1, bias1, slc)
                  + tile_chains(ks_ref, vs_ref, qaug_ref, t, 0, slc))

    for hk in HK:
        gate = gate_ref[0, hk]
        comb = []
        for g in range(G):
            cols = slice(g * tq, (g + 1) * tq)
            gr = lambda br: gate[g * 3 + br:g * 3 + br + 1, :]
            w_s = gr(1) / acc_ref[hk, NSA_DK:NSA_DK + 1, cols]
            w_w = gr(2) / accw_ref[hk, NSA_DK:NSA_DK + 1, cols]
            comb.append(gr(0) * oc_ref[hk, :, cols] + w_s * acc_ref[hk, 0:NSA_DK, cols]
                        + w_w * accw_ref[hk, 0:NSA_DK, cols])
        width = G * NSA_DK
        o_ref[0, :, hk * width:(hk + 1) * width] = jnp.transpose(jnp.concatenate(comb, axis=0))


def _nsa(q_t, ks, vs_t, kw, vw_t, kc, vc_t, gate_t, bias_c, tz, ovt):
    B, _, S, _ = ks.shape
    tq, tk = NSA_TQ, NSA_TK
    nq = S // tq
    nc = kc.shape[2]
    m = NSA_GROUP * tq
    hkv = NSA_KV_HEADS
    tok = lambda: pl.BlockSpec((1, hkv, S, LANE), lambda b, t: (b, 0, 0, 0))
    feat = lambda: pl.BlockSpec((1, hkv, NSA_DK, S), lambda b, t: (b, 0, 0, 0))
    return pl.pallas_call(
        functools.partial(_nsa_kernel, tq=tq, tk=tk),
        grid=(B, nq),
        in_specs=[pl.BlockSpec((1, NSA_HEADS, NSA_DK, tq), lambda b, t: (b, 0, 0, t)),
                  tok(), feat(), tok(), feat(),
                  pl.BlockSpec((1, hkv, nc, LANE), lambda b, t: (b, 0, 0, 0)),
                  pl.BlockSpec((1, hkv, NSA_DK, nc), lambda b, t: (b, 0, 0, 0)),
                  pl.BlockSpec((1, hkv, GATE_ROWS, tq), lambda b, t: (b, 0, 0, t)),
                  pl.BlockSpec((hkv, 1, nc, m), lambda b, t: (0, t, 0, 0)),
                  _const_spec(tz.shape),
                  _const_spec(ovt.shape)],
        out_specs=pl.BlockSpec((1, tq, NSA_HEADS * NSA_DK), lambda b, t: (b, t, 0)),
        out_shape=jax.ShapeDtypeStruct((B, S, NSA_HEADS * NSA_DK), jnp.float32),
        scratch_shapes=[pltpu.VMEM((hkv, NSA_DK + SLC_ROWS, m), jnp.bfloat16),
                        pltpu.VMEM((hkv, NSA_DK + SLC_ROWS, m), jnp.bfloat16),
                        pltpu.VMEM((hkv, 1, m), jnp.float32),
                        pltpu.VMEM((hkv, NSA_DK + ONES_ROWS, m), jnp.float32),
                        pltpu.VMEM((hkv, 1, m), jnp.float32),
                        pltpu.VMEM((hkv, NSA_DK + ONES_ROWS, m), jnp.float32),
                        pltpu.VMEM((hkv, NSA_DK, m), jnp.float32),
                        pltpu.VMEM((hkv, SLC_ROWS, tq), jnp.float32)],
        compiler_params=_params(2), name="nsa",
    )(q_t, ks, vs_t, kw, vw_t, kc, vc_t, gate_t, bias_c, tz, ovt)


def _post_kernel(x_ref, om_ref, on_ref, gm_ref, gn_ref, wo_ref, gf_ref, wg_ref, wu_ref, wd_ref,
                 gfin_ref, o_ref):
    half = om_ref.shape[2]
    mix_m = _rms(om_ref[0], gm_ref[...]).astype(jnp.bfloat16)
    mix_n = _rms(on_ref[0], gn_ref[...]).astype(jnp.bfloat16)
    h = x_ref[0] + _dot(mix_m, wo_ref[0:half, :]) + _dot(mix_n, wo_ref[half:2 * half, :])
    f = _rms(h, gf_ref[...]).astype(jnp.bfloat16)
    a = _dot(f, wg_ref[...])
    act = (a * jax.nn.sigmoid(a) * _dot(f, wu_ref[...])).astype(jnp.bfloat16)
    h = h + _dot(act, wd_ref[...])
    o_ref[0] = _rms(h, gfin_ref[...])


def _post(x, o_mla, o_nsa, gm, gn, w_out, gf, wg, wu, wd, gfin):
    B, S, D = x.shape
    tm = min(POST_TM, S)
    tok = lambda w: pl.BlockSpec((1, tm, w), lambda b, i: (b, i, 0))
    consts = (gm, gn, w_out, gf, wg, wu, wd, gfin)
    return pl.pallas_call(
        _post_kernel, grid=(B, S // tm),
        in_specs=[tok(D), tok(o_mla.shape[2]), tok(o_nsa.shape[2])] + [_const_spec(c.shape) for c in consts],
        out_specs=tok(D), out_shape=jax.ShapeDtypeStruct((B, S, D), jnp.float32),
        compiler_params=_params(2), name="post",
    )(x, o_mla, o_nsa, *consts)


def _rope_tables(S):
    pos = jnp.arange(S, dtype=jnp.float32)
    inv = ROPE_THETA ** (-jnp.arange(0, MLA_ROPE, 2, dtype=jnp.float32) / MLA_ROPE)
    ang = pos[:, None] * inv[None, :]
    cos, sin = jnp.cos(ang), jnp.sin(ang)
    cos2 = jnp.concatenate([cos, cos], axis=-1)
    sin2 = jnp.concatenate([sin, sin], axis=-1)
    scale = (MLA_NOPE + MLA_ROPE) ** -0.5 * LOG2E
    zq = jnp.zeros((S, LANE - MLA_NOPE - MLA_ROPE), jnp.float32)
    cosq = jnp.concatenate([jnp.ones((S, MLA_NOPE), jnp.float32), cos2, zq], axis=-1) * scale
    sinq = jnp.concatenate([jnp.zeros((S, MLA_NOPE), jnp.float32), sin2, zq], axis=-1) * scale
    zk0 = jnp.zeros((S, MLA_NOPE), jnp.float32)
    cosk = jnp.concatenate([zk0, cos2, zq], axis=-1)
    sink = jnp.concatenate([zk0, sin2, zq], axis=-1)
    return cosq.T, sinq.T, cosk, sink


def _overlap_t(S):
    n_cmp = (S - CMP_LEN) // CMP_STRIDE + 1
    n_slc = S // SLC_LEN
    assert n_slc <= SLC_ROWS
    cs = np.arange(n_cmp) * CMP_STRIDE
    ss = np.arange(n_slc) * SLC_LEN
    ov = np.maximum(0, np.minimum(cs[:, None] + CMP_LEN, ss[None, :] + SLC_LEN)
                    - np.maximum(cs[:, None], ss[None, :])).astype(np.float32) / CMP_STRIDE
    out = np.zeros((SLC_ROWS, S // CMP_STRIDE), np.float32)
    out[:n_slc, :n_cmp] = ov.T
    return jnp.asarray(out, jnp.bfloat16)


def kernel(x, norm_mix_g, w_in, mla_q_norm_g, mla_w_uq, mla_kv_norm_g, mla_w_ukv, nsa_cmp_pos_k, nsa_cmp_w1_k, nsa_cmp_w2_k, nsa_cmp_pos_v, nsa_cmp_w1_v, nsa_cmp_w2_v, t5_table, out_norm_mla_g, out_norm_nsa_g, w_out, norm_ffn_g, w_gate, w_up, w_down, final_norm_g):
    B, S, D = x.shape
    assert w_in.shape[0] == 1
    assert D == D_MODEL and S % NSA_TQ == 0 and S % CMP_STRIDE == 0
    bf = jnp.bfloat16
    l = 0
    cosq_t, sinq_t, cosk, sink = _rope_tables(S)
    bias_c, tz = _bias_tables(t5_table, S)
    ovt = _overlap_t(S)
    nc = S // CMP_STRIDE
    cw = CMP_STRIDE * NSA_DK
    row = lambda v: v.reshape(1, -1)

    w_row = _w_in_row_layout().apply(w_in[l])
    w_col = _w_in_col_layout().apply(w_in[l]).T
    w_q2 = _w_uq_layout().apply(mla_w_uq[l]).T
    w_k = _w_uk_layout().apply(mla_w_ukv[l])
    w_v = _w_uv_layout().apply(mla_w_ukv[l]).T
    (q_mla, k_mla, v_mla, q_nsa, kv_cmp, k_slc, v_slc, k_win, v_win, gates) = _proj(
        x, row(norm_mix_g[l]), w_row, w_col, row(mla_q_norm_g[l]), w_q2, row(mla_kv_norm_g[l]), w_k, w_v,
        cosq_t, sinq_t, cosk, sink)

    def per_offset(pos_l, w1_l):
        pos2 = jnp.tile(pos_l, (1, NSA_KV_HEADS)).reshape(CMP_LEN, 1, NSA_KV_HEADS * NSA_DK)
        w = w1_l.reshape(CMP_LEN, NSA_DK, CMP_HIDDEN)
        z = jnp.zeros_like(w)
        return pos2, jnp.concatenate([jnp.concatenate([w, z], axis=2), jnp.concatenate([z, w], axis=2)], axis=1)

    pos_k, w1_k = per_offset(nsa_cmp_pos_k[l], nsa_cmp_w1_k[l])
    pos_v, w1_v = per_offset(nsa_cmp_pos_v[l], nsa_cmp_w1_v[l])
    pos = jnp.stack([pos_k, pos_v])
    w1 = jnp.stack([w1_k, w1_v]).astype(bf)
    kvc = kv_cmp
    w2k = jnp.concatenate([nsa_cmp_w2_k[l], jnp.zeros((CMP_HIDDEN, LANE - NSA_DK), jnp.float32)],
                          axis=1).astype(bf)
    w2vt = nsa_cmp_w2_v[l].T.astype(bf)
    k_cmp, v_cmp = _compress(kvc, pos, w1, w2k, w2vt)

    o_mla = _mla(q_mla, k_mla, v_mla)
    o_nsa = _nsa(q_nsa, k_slc, v_slc, k_win, v_win, k_cmp, v_cmp, gates, bias_c, tz, ovt)

    return _post(x, o_mla, o_nsa, row(out_norm_mla_g[l]), row(out_norm_nsa_g[l]), w_out[l].astype(bf),
                 row(norm_ffn_g[l]), w_gate[l].astype(bf), w_up[l].astype(bf), w_down[l].astype(bf),
                 row(final_norm_g))
```

```python
import functools
import math

import numpy as np
import jax
import jax.numpy as jnp
from jax import lax
from jax.experimental import pallas as pl
from jax.experimental.pallas import tpu as pltpu

D_MODEL = 1024
MLA_HEADS = 8
MLA_NOPE = 64
MLA_ROPE = 32
MLA_V = 64
MLA_Q_LORA = 256
MLA_KV_LORA = 128
ROPE_THETA = 10000.0
NSA_HEADS = 8
NSA_KV_HEADS = 2
NSA_GROUP = 4
NSA_DK = 64
CMP_LEN = 32
CMP_STRIDE = 16
CMP_HIDDEN = 128
SLC_LEN = 64
SLC_TOPN = 16
WINDOW = 512
T5_BUCKETS = 32
T5_MAX_DIST = 128
D_FF = 2816
EPS = 1e-6
NEG = -1e30
FORCE = 1e30

LANE = 128
SUBLANE = 8
SLC_ROWS = 64
GATE_ROWS = 16
ONES_ROWS = 16
LOG2E = math.log2(math.e)

PROJ_TM = 512
MLA_TQ = 4096
MLA_TK = 256
MLA_FAR_TK = 256
MLA_TRIP_KEYS = 1024
MLA_CW = 256
FLASH_LOOKAHEAD = 6
NSA_TQ = 256
NSA_TK = 256
NSA_FAR_GROUP = 8
POST_TM = 512
VMEM_LIMIT = 56 * 1024 * 1024

C_CQ = 0
C_CKV = 256
C_MISC = 384
C_KVCMP = 512
C_KSLC = 768
C_KWIN = 1024
D_ROW = 1280
R_QNSA = 0
R_VSLC = 512
R_VWIN = 640
R_GATE = 768
D_COL = 800


def _dot(a, b):
    return jnp.dot(a, b, preferred_element_type=jnp.float32)


def _dot_nt(a, b):
    return lax.dot_general(a, b, (((1,), (1,)), ((), ())), preferred_element_type=jnp.float32)


def _rms(x, g):
    return x * lax.rsqrt(jnp.mean(x * x, axis=-1, keepdims=True) + EPS) * g


def _const_spec(shape):
    nd = len(shape)
    return pl.BlockSpec(shape, lambda *_: (0,) * nd, pipeline_mode=pl.Buffered(1))


def _params(n_axes):
    return pltpu.CompilerParams(dimension_semantics=("arbitrary",) * n_axes,
                                vmem_limit_bytes=VMEM_LIMIT)


def _in_offsets():
    o_krope = MLA_Q_LORA + MLA_KV_LORA
    o_qnsa = o_krope + MLA_ROPE
    o_cmp = o_qnsa + NSA_HEADS * NSA_DK
    o_slc = o_cmp + 2 * NSA_KV_HEADS * NSA_DK
    o_win = o_slc + 2 * NSA_KV_HEADS * NSA_DK
    o_gate = o_win + 2 * NSA_KV_HEADS * NSA_DK
    return o_krope, o_qnsa, o_cmp, o_slc, o_win, o_gate


class _Layout:
    def __init__(self, n):
        self.src = np.zeros((n,), np.int32)
        self.mul = np.zeros((n,), np.float32)

    def put(self, dst, s, n, m=1.0):
        self.src[dst:dst + n] = np.arange(s, s + n)
        self.mul[dst:dst + n] = m

    def apply(self, w):
        return (w[:, self.src] * self.mul[None, :]).astype(jnp.bfloat16)


def _w_in_row_layout():
    o_krope, _, o_cmp, o_slc, o_win, _ = _in_offsets()
    lay = _Layout(D_ROW)
    lay.put(C_CQ, 0, MLA_Q_LORA)
    lay.put(C_CKV, MLA_Q_LORA, MLA_KV_LORA)
    half = MLA_ROPE // 2
    lay.put(C_MISC + MLA_NOPE, o_krope, MLA_ROPE)
    lay.put(C_MISC + MLA_NOPE + MLA_ROPE, o_krope + half, half, -1.0)
    lay.put(C_MISC + MLA_NOPE + MLA_ROPE + half, o_krope, half, 1.0)
    lay.put(C_KVCMP, o_cmp, 2 * NSA_KV_HEADS * NSA_DK)
    for hk in range(NSA_KV_HEADS):
        lay.put(C_KSLC + hk * LANE, o_slc + hk * NSA_DK, NSA_DK)
        lay.put(C_KWIN + hk * LANE, o_win + hk * NSA_DK, NSA_DK)
    return lay


def _w_in_col_layout():
    _, o_qnsa, _, o_slc, o_win, o_gate = _in_offsets()
    lay = _Layout(D_COL)
    lay.put(R_QNSA, o_qnsa, NSA_HEADS * NSA_DK, NSA_DK ** -0.5)
    lay.put(R_VSLC, o_slc + NSA_KV_HEADS * NSA_DK, NSA_KV_HEADS * NSA_DK)
    lay.put(R_VWIN, o_win + NSA_KV_HEADS * NSA_DK, NSA_KV_HEADS * NSA_DK)
    for hk in range(NSA_KV_HEADS):
        lay.put(R_GATE + hk * GATE_ROWS, o_gate + hk * NSA_GROUP * 3, NSA_GROUP * 3)
    return lay


def _w_uq_layout():
    lay = _Layout(MLA_HEADS * LANE)
    dq = MLA_NOPE + MLA_ROPE
    for h in range(MLA_HEADS):
        lay.put(h * LANE, h * dq, dq)
    return lay


def _w_uk_layout():
    lay = _Layout(MLA_HEADS * LANE)
    per = MLA_NOPE + MLA_V
    for h in range(MLA_HEADS):
        lay.put(h * LANE, h * per, MLA_NOPE)
    return lay


def _w_uv_layout():
    lay = _Layout(MLA_HEADS * MLA_V)
    per = MLA_NOPE + MLA_V
    for h in range(MLA_HEADS):
        lay.put(h * MLA_V, h * per + MLA_NOPE, MLA_V)
    return lay


def _t5_thresholds(max_dist):
    n = np.arange(max_dist, dtype=np.int64)
    max_exact = T5_BUCKETS // 2
    nf = np.maximum(n, 1).astype(np.float32)
    val = (np.log(nf / np.float32(max_exact)) / np.float32(math.log(T5_MAX_DIST / max_exact))
           * np.float32(T5_BUCKETS - max_exact))
    large = np.minimum(max_exact + val.astype(np.int32), T5_BUCKETS - 1)
    bucket = np.where(n < max_exact, n, large)
    assert np.all(np.diff(bucket) >= 0)
    frac = np.abs(val[max_exact + 1:T5_MAX_DIST] - np.round(val[max_exact + 1:T5_MAX_DIST]))
    assert frac.min() > 1e-3
    return [int(np.argmax(bucket >= b)) for b in range(T5_BUCKETS)]


def _proj_kernel(x_ref, gmix_ref, wrow_ref, wcol_ref, gq_ref, wq_ref, gkv_ref, wk_ref, wv_ref,
                 cq_ref, sq_ref, ck_ref, sk_ref,
                 qm_ref, km_ref, vm_ref, qn_ref, kvc_ref, ks_ref, vs_ref, kw_ref, vw_ref, gate_ref):
    tm = x_ref.shape[1]
    bf = jnp.bfloat16
    xn = _rms(x_ref[0], gmix_ref[...]).astype(bf)
    u = _dot(xn, wrow_ref[...])
    ut = _dot_nt(wcol_ref[...], xn)

    cqn = _rms(u[:, C_CQ:C_CQ + MLA_Q_LORA], gq_ref[...]).astype(bf)
    qq = _dot_nt(wq_ref[...], cqn)
    cosq = cq_ref[...]
    sinq = sq_ref[...]
    r0, r1, r2 = MLA_NOPE, MLA_NOPE + MLA_ROPE // 2, MLA_NOPE + MLA_ROPE
    for h in range(MLA_HEADS):
        q = qq[h * LANE:(h + 1) * LANE]
        rot = jnp.concatenate([q[0:r0], -q[r1:r2], q[r0:r1], q[r2:]], axis=0)
        qm_ref[0, h] = (q * cosq + rot * sinq).astype(bf)

    ckvn = _rms(u[:, C_CKV:C_CKV + MLA_KV_LORA], gkv_ref[...]).astype(bf)
    kk = _dot(ckvn, wk_ref[...])
    misc = u[:, C_MISC:C_MISC + LANE]
    kpe = misc * ck_ref[...] + pltpu.roll(misc, LANE - MLA_ROPE, axis=1) * sk_ref[...]
    for h in range(MLA_HEADS):
        km_ref[0, h] = (kk[:, h * LANE:(h + 1) * LANE] + kpe).astype(bf)
    vv = _dot_nt(wv_ref[...], ckvn)
    for h in range(MLA_HEADS):
        vm_ref[0, h] = vv[h * MLA_V:(h + 1) * MLA_V].astype(bf)

    for h in range(NSA_HEADS):
        qn_ref[0, h] = (ut[R_QNSA + h * NSA_DK:R_QNSA + (h + 1) * NSA_DK] * LOG2E).astype(bf)
    for kv in range(2):
        kvc_ref[0, kv] = u[:, C_KVCMP + kv * LANE:C_KVCMP + (kv + 1) * LANE]
    pos = pl.program_id(1) * tm + lax.broadcasted_iota(jnp.int32, (tm, LANE), 0)
    lane = lax.broadcasted_iota(jnp.int32, (tm, LANE), 1)
    onehot = jnp.where(lane - (LANE - SLC_ROWS) == pos // SLC_LEN, 1.0, 0.0)
    for hk in range(NSA_KV_HEADS):
        ks_ref[0, hk] = (u[:, C_KSLC + hk * LANE:C_KSLC + (hk + 1) * LANE] + onehot).astype(bf)
        kw_ref[0, hk] = u[:, C_KWIN + hk * LANE:C_KWIN + (hk + 1) * LANE].astype(bf)
        vs_ref[0, hk] = ut[R_VSLC + hk * NSA_DK:R_VSLC + (hk + 1) * NSA_DK].astype(bf)
        vw_ref[0, hk] = ut[R_VWIN + hk * NSA_DK:R_VWIN + (hk + 1) * NSA_DK].astype(bf)
        gate_ref[0, hk] = jax.nn.sigmoid(ut[R_GATE + hk * GATE_ROWS:R_GATE + (hk + 1) * GATE_ROWS])


def _proj(x, gmix, w_row, w_col, gq, w_q2, gkv, w_k, w_v, cosq_t, sinq_t, cosk, sink):
    B, S, D = x.shape
    tm = min(PROJ_TM, S)
    bf = jnp.bfloat16
    grid = (B, S // tm)
    tok_major = lambda n: pl.BlockSpec((1, n, tm, LANE), lambda b, i: (b, 0, i, 0))
    feat_major = lambda n, d: pl.BlockSpec((1, n, d, tm), lambda b, i: (b, 0, 0, i))
    out_shape = (
        jax.ShapeDtypeStruct((B, MLA_HEADS, LANE, S), bf),
        jax.ShapeDtypeStruct((B, MLA_HEADS, S, LANE), bf),
        jax.ShapeDtypeStruct((B, MLA_HEADS, MLA_V, S), bf),
        jax.ShapeDtypeStruct((B, NSA_HEADS, NSA_DK, S), bf),
        jax.ShapeDtypeStruct((B, 2, S, LANE), jnp.float32),
        jax.ShapeDtypeStruct((B, NSA_KV_HEADS, S, LANE), bf),
        jax.ShapeDtypeStruct((B, NSA_KV_HEADS, NSA_DK, S), bf),
        jax.ShapeDtypeStruct((B, NSA_KV_HEADS, S, LANE), bf),
        jax.ShapeDtypeStruct((B, NSA_KV_HEADS, NSA_DK, S), bf),
        jax.ShapeDtypeStruct((B, NSA_KV_HEADS, GATE_ROWS, S), jnp.float32),
    )
    out_specs = (
        feat_major(MLA_HEADS, LANE), tok_major(MLA_HEADS), feat_major(MLA_HEADS, MLA_V),
        feat_major(NSA_HEADS, NSA_DK),
        tok_major(2),
        tok_major(NSA_KV_HEADS), feat_major(NSA_KV_HEADS, NSA_DK),
        tok_major(NSA_KV_HEADS), feat_major(NSA_KV_HEADS, NSA_DK),
        feat_major(NSA_KV_HEADS, GATE_ROWS),
    )
    consts = (gmix, w_row, w_col, gq, w_q2, gkv, w_k, w_v)
    in_specs = ([pl.BlockSpec((1, tm, D), lambda b, i: (b, i, 0))] + [_const_spec(c.shape) for c in consts]
                + [pl.BlockSpec((LANE, tm), lambda b, i: (0, i)), pl.BlockSpec((LANE, tm), lambda b, i: (0, i)),
                   pl.BlockSpec((tm, LANE), lambda b, i: (i, 0)), pl.BlockSpec((tm, LANE), lambda b, i: (i, 0))])
    return pl.pallas_call(
        _proj_kernel, grid=grid, in_specs=in_specs, out_specs=out_specs, out_shape=out_shape,
        compiler_params=_params(2), name="proj",
    )(x, *consts, cosq_t, sinq_t, cosk, sink)


def _compress_kernel(c_ref, pos_ref, w1_ref, w2k_ref, w2vt_ref, kc_ref, vc_ref):
    nc = kc_ref.shape[2]
    bf = jnp.bfloat16
    hw = NSA_KV_HEADS * NSA_DK
    for kv in range(2):
        halves = []
        for half in range(CMP_LEN // CMP_STRIDE):
            acc = None
            for r in range(CMP_STRIDE):
                l = half * CMP_STRIDE + r
                x = c_ref[0, kv, pl.ds(r, nc, stride=CMP_STRIDE), :]
                part = _dot((x + pos_ref[kv, l]).astype(bf), w1_ref[kv, l])
                acc = part if acc is None else acc + part
            halves.append(acc)
        hid = halves[0] + pltpu.roll(halves[1], nc - 1, axis=0)
        act = jax.nn.gelu(hid).astype(bf)
        for hk in range(NSA_KV_HEADS):
            act_h = act[:, hk * CMP_HIDDEN:(hk + 1) * CMP_HIDDEN]
            if kv == 0:
                kc_ref[0, hk] = _dot(act_h, w2k_ref[...]).astype(bf)
            else:
                vc_ref[0, hk] = _dot_nt(w2vt_ref[...], act_h).astype(bf)


def _compress(kvc, pos, w1, w2k, w2vt):
    B, _, S, width = kvc.shape
    nc = S // CMP_STRIDE
    return pl.pallas_call(
        _compress_kernel, grid=(B,),
        in_specs=[pl.BlockSpec((1, 2, S, width), lambda b: (b, 0, 0, 0)),
                  _const_spec(pos.shape), _const_spec(w1.shape), _const_spec(w2k.shape),
                  _const_spec(w2vt.shape)],
        out_specs=(pl.BlockSpec((1, NSA_KV_HEADS, nc, LANE), lambda b: (b, 0, 0, 0)),
                   pl.BlockSpec((1, NSA_KV_HEADS, NSA_DK, nc), lambda b: (b, 0, 0, 0))),
        out_shape=(jax.ShapeDtypeStruct((B, NSA_KV_HEADS, nc, LANE), jnp.bfloat16),
                   jax.ShapeDtypeStruct((B, NSA_KV_HEADS, NSA_DK, nc), jnp.bfloat16)),
        compiler_params=_params(1), name="compress",
    )(kvc, pos, w1, w2k, w2vt)


def _t5_lookup(dist, tab_ref, head, thr):
    val = jnp.full(dist.shape, tab_ref[0, head], jnp.float32)
    for b in range(1, T5_BUCKETS):
        val = jnp.where(dist >= thr[b], tab_ref[b, head], val)
    return (val - tab_ref[T5_BUCKETS - 1, head]) * LOG2E


def _bias_cmp_kernel(tab_ref, out_ref, *, thr, tq, n_cmp):
    hk = pl.program_id(0)
    t = pl.program_id(1)
    nc = out_ref.shape[2]
    per_tile = tq // CMP_STRIDE
    band = per_tile + (T5_MAX_DIST + CMP_LEN) // CMP_STRIDE + SUBLANE
    band = min(-(-band // SUBLANE) * SUBLANE, nc)
    start = jnp.clip(t * per_tile - (band - per_tile), 0, nc - band)
    start = pl.multiple_of(start // SUBLANE * SUBLANE, SUBLANE)
    n_all = lax.broadcasted_iota(jnp.int32, (nc, NSA_GROUP * tq), 0)
    out_ref[0, 0] = jnp.where(n_all < start, 0.0, NEG)
    n = start + lax.broadcasted_iota(jnp.int32, (band, tq), 0)
    i = lax.broadcasted_iota(jnp.int32, (band, tq), 1)
    dist = t * tq + i - (n * CMP_STRIDE + CMP_LEN - 1)
    ok = jnp.logical_and(dist >= 0, n < n_cmp)
    for g in range(NSA_GROUP):
        val = _t5_lookup(dist, tab_ref, hk * NSA_GROUP + g, thr)
        out_ref[0, 0, pl.ds(start, band), g * tq:(g + 1) * tq] = jnp.where(ok, val, NEG)


def _bias_tile_kernel(tab_ref, out_ref, *, thr, tq):
    hk = pl.program_id(0)
    tk = out_ref.shape[2]
    j = lax.broadcasted_iota(jnp.int32, (tk, tq), 0)
    i = lax.broadcasted_iota(jnp.int32, (tk, tq), 1)
    for g in range(NSA_GROUP):
        head = hk * NSA_GROUP + g
        cols = slice(g * tq, (g + 1) * tq)
        d0 = i - j
        out_ref[0, 0, :, cols] = jnp.where(d0 >= 0, _t5_lookup(d0, tab_ref, head, thr), NEG)
        out_ref[0, 1, :, cols] = _t5_lookup(d0 + tk, tab_ref, head, thr)
        out_ref[0, 2, :, cols] = jnp.where(j > i, 0.0, NEG)
        out_ref[0, 3, :, cols] = jnp.full((tk, tq), NEG, jnp.float32)


def _bias_tables(t5_table, S):
    thr = _t5_thresholds(S)
    tq, tk = NSA_TQ, NSA_TK
    assert tq == tk and WINDOW == 2 * tk and tk >= T5_MAX_DIST
    nq = S // tq
    nc = S // CMP_STRIDE
    n_cmp = (S - CMP_LEN) // CMP_STRIDE + 1
    m = NSA_GROUP * tq
    smem = pl.BlockSpec(memory_space=pltpu.SMEM)
    bias_c = pl.pallas_call(
        functools.partial(_bias_cmp_kernel, thr=thr, tq=tq, n_cmp=n_cmp),
        grid=(NSA_KV_HEADS, nq), in_specs=[smem],
        out_specs=pl.BlockSpec((1, 1, nc, m), lambda h, t: (h, t, 0, 0)),
        out_shape=jax.ShapeDtypeStruct((NSA_KV_HEADS, nq, nc, m), jnp.float32),
        compiler_params=_params(2), name="bias_cmp",
    )(t5_table)
    tz = pl.pallas_call(
        functools.partial(_bias_tile_kernel, thr=thr, tq=tq),
        grid=(NSA_KV_HEADS,), in_specs=[smem],
        out_specs=pl.BlockSpec((1, 4, tk, m), lambda h: (h, 0, 0, 0)),
        out_shape=jax.ShapeDtypeStruct((NSA_KV_HEADS, 4, tk, m), jnp.float32),
        compiler_params=_params(1), name="bias_tile",
    )(t5_table)
    return bias_c, tz


def _pipeline(stages):
    pending = [issue() for issue, _ in stages[:FLASH_LOOKAHEAD]]
    for i, (_, consume) in enumerate(stages):
        s = pending.pop(0)
        if i + FLASH_LOOKAHEAD < len(stages):
            pending.append(stages[i + FLASH_LOOKAHEAD][0]())
        consume(s)


def _flash_stage(chain):
    k, q_t, v_aug, bias_t, m_ref, acc_ref = chain

    def issue():
        s = _dot(k, q_t)
        return s if bias_t is None else s + bias_t

    def consume(s):
        m_prev = m_ref[...]
        m_new = jnp.maximum(m_prev, jnp.max(s, axis=0, keepdims=True))
        alpha = jnp.exp2(m_prev - m_new)
        p = jnp.exp2(s - m_new).astype(jnp.bfloat16)
        acc_ref[...] = alpha * acc_ref[...] + _dot(v_aug, p)
        m_ref[...] = m_new

    return issue, consume


def _flash_chains(chains):
    _pipeline([_flash_stage(c) for c in chains])


def _with_ones(v_t):
    tk = v_t.shape[1]
    row = lax.broadcasted_iota(jnp.int32, (ONES_ROWS, tk), 0)
    return jnp.concatenate([v_t, jnp.where(row == 0, 1.0, 0.0).astype(v_t.dtype)], axis=0)


def _col_chains(k, q_ref, v_aug, bias_ref, m_ref, acc_ref, ncol, width):
    chains = []
    for c0 in range(0, ncol, width):
        cols = slice(c0, c0 + width)
        chains.append((k, q_ref[:, cols], v_aug, None if bias_ref is None else bias_ref[:, cols],
                       m_ref.at[:, cols], acc_ref.at[:, cols]))
    return chains


def _flash_init(m_ref, acc_ref):
    m_ref[...] = jnp.full(m_ref.shape, -jnp.inf, jnp.float32)
    acc_ref[...] = jnp.zeros(acc_ref.shape, jnp.float32)


def _flash_result(acc_ref, dv):
    return acc_ref[0:dv, :] * (1.0 / acc_ref[dv:dv + 1, :])


def _mla_kernel(q_ref, k_ref, v_ref, o_ref, m_ref, acc_ref, *, tq, tk, cw):
    S = k_ref.shape[2]
    nqt = S // tq
    kk = lax.broadcasted_iota(jnp.int32, (tk, cw), 0)
    qq = lax.broadcasted_iota(jnp.int32, (tk, cw), 1)

    def q_tile(qt, carry):
        q0 = pl.multiple_of(qt * tq, tq)
        for e in range(2):
            _flash_init(m_ref.at[e], acc_ref.at[e])

        def step(k0, key_off, rows=tk):
            chains = []
            for e in range(2):
                k = k_ref[0, e, pl.ds(k0, rows), :]
                v_aug = _with_ones(v_ref[0, e, :, pl.ds(k0, rows)])
                for c0 in range(0, tq, cw):
                    bias = None
                    if key_off is not None:
                        if key_off > c0 + cw - 1:
                            continue
                        if key_off + tk - 1 > c0:
                            bias = jnp.where(kk + key_off <= qq + c0, 0.0, NEG)
                    cols = slice(c0, c0 + cw)
                    chains.append((k, q_ref[0, e, :, pl.ds(pl.multiple_of(q0 + c0, cw), cw)], v_aug, bias,
                                   m_ref.at[e, :, cols], acc_ref.at[e, :, cols]))
            return chains

        per = MLA_TRIP_KEYS // MLA_FAR_TK

        def far(c, carry2):
            chains = []
            for j in range(per):
                chains += step(pl.multiple_of(c * MLA_TRIP_KEYS + j * MLA_FAR_TK, MLA_FAR_TK), None, MLA_FAR_TK)
            _flash_chains(chains)
            return carry2

        lax.fori_loop(0, qt * (tq // MLA_TRIP_KEYS), far, 0)
        chains = []
        for j in range(tq // tk):
            chains += step(pl.multiple_of(q0 + j * tk, tk), j * tk)
        _flash_chains(chains)
        o_t = jnp.concatenate([_flash_result(acc_ref.at[e], MLA_V) for e in range(2)], axis=0)
        o_ref[0, pl.ds(q0, tq), :] = jnp.transpose(o_t)
        return carry

    lax.fori_loop(0, nqt, q_tile, 0)


def _mla(q_t, k, v_t):
    B, H, S, _ = k.shape
    tq = min(MLA_TQ, S)
    tk = min(MLA_TK, tq)
    cw = min(MLA_CW, tq)
    return pl.pallas_call(
        functools.partial(_mla_kernel, tq=tq, tk=tk, cw=cw),
        grid=(B, H // 2),
        in_specs=[pl.BlockSpec((1, 2, LANE, S), lambda b, p: (b, p, 0, 0)),
                  pl.BlockSpec((1, 2, S, LANE), lambda b, p: (b, p, 0, 0)),
                  pl.BlockSpec((1, 2, MLA_V, S), lambda b, p: (b, p, 0, 0))],
        out_specs=pl.BlockSpec((1, S, 2 * MLA_V), lambda b, p: (b, 0, p)),
        out_shape=jax.ShapeDtypeStruct((B, S, H * MLA_V), jnp.float32),
        scratch_shapes=[pltpu.VMEM((2, 1, tq), jnp.float32),
                        pltpu.VMEM((2, MLA_V + ONES_ROWS, tq), jnp.float32)],
        compiler_params=_params(2), name="mla",
    )(q_t, k, v_t)


def _nsa_kernel(q_ref, ks_ref, vs_ref, kw_ref, vw_ref, kc_ref, vc_ref, gate_ref, bc_ref, tz_ref, ovt_ref,
                o_ref, qaug_ref, qpad_ref, m_ref, acc_ref, mw_ref, accw_ref, oc_ref, imp_ref, *, tq, tk):
    t = pl.program_id(1)
    G = NSA_GROUP
    HK = range(NSA_KV_HEADS)
    ncol = G * tq
    bf = jnp.bfloat16
    q0 = t * tq
    for hk in HK:
        for g in range(G):
            qpad_ref[hk, 0:NSA_DK, g * tq:(g + 1) * tq] = q_ref[0, hk * G + g]
            qaug_ref[hk, 0:NSA_DK, g * tq:(g + 1) * tq] = q_ref[0, hk * G + g]
        qpad_ref[hk, NSA_DK:, :] = jnp.zeros((SLC_ROWS, ncol), bf)

    def tile_chains(k_ref, v_ref, qx_ref, c, bias_idx, stats):
        k0 = pl.multiple_of(c * tk, tk)
        chains = []
        for hk in HK:
            bias_ref = None if bias_idx is None else tz_ref.at[hk, bias_idx]
            chains += _col_chains(k_ref[0, hk, pl.ds(k0, tk), :], qx_ref.at[hk],
                                  _with_ones(v_ref[0, hk, :, pl.ds(k0, tk)]),
                                  bias_ref, stats[0].at[hk], stats[1].at[hk], ncol, tq)
        return chains

    prev1 = jnp.maximum(t - 1, 0)
    bias1 = jnp.where(t >= 1, 1, 3)

    nb = ovt_ref.shape[0]
    sub = SUBLANE
    col_ok = q0 + lax.broadcasted_iota(jnp.int32, (1, tq), 1) >= CMP_LEN - 1
    jb = lax.broadcasted_iota(jnp.int32, (nb, tq), 0)
    cur = (q0 + lax.broadcasted_iota(jnp.int32, (nb, tq), 1)) // SLC_LEN
    forced = jnp.logical_or(jb == 0, jnp.logical_or(jb == cur, jb == cur - 1))
    jsub = lax.broadcasted_iota(jnp.int32, (sub, tq), 0)
    imp_sum = [None] * NSA_KV_HEADS

    def cmp_stage(hk, g):
        cols = slice(g * tq, (g + 1) * tq)

        def issue():
            return _dot(kc_ref[0, hk], qpad_ref[hk, :, cols]) + bc_ref[hk, 0, :, cols]

        def consume(s):
            mx = jnp.max(s, axis=0, keepdims=True)
            p = jnp.exp2(s - mx)
            lsum = jnp.sum(p, axis=0, keepdims=True)
            pcb = (p * jnp.where(col_ok, 1.0 / lsum, 0.0)).astype(bf)
            both = _dot(jnp.concatenate([vc_ref[0, hk], ovt_ref[...]], axis=0), pcb)
            oc_ref[hk, :, cols] = both[0:NSA_DK]
            imp_sum[hk] = both[NSA_DK:] if imp_sum[hk] is None else imp_sum[hk] + both[NSA_DK:]

        return issue, consume

    win = (mw_ref, accw_ref)
    _flash_init(*win)
    win_stages = [_flash_stage(c) for c in
                  tile_chains(kw_ref, vw_ref, qpad_ref, jnp.maximum(t - 2, 0), jnp.where(t >= 2, 2, 3), win)
                  + tile_chains(kw_ref, vw_ref, qpad_ref, prev1, bias1, win)
                  + tile_chains(kw_ref, vw_ref, qpad_ref, t, 0, win)]
    cmp_stages = [cmp_stage(hk, g) for g in range(G) for hk in HK]
    every = len(win_stages) // len(cmp_stages)
    stages = []
    for i, st in enumerate(cmp_stages):
        stages += [st] + win_stages[i * every:(i + 1) * every]
    _pipeline(stages + win_stages[len(cmp_stages) * every:])
    for hk in HK:
        imp_ref[hk] = jnp.where(forced, FORCE, jnp.where(jb <= cur, imp_sum[hk], NEG))

    tiles_per_group = 16 * SLC_LEN // tq
    n_groups = nb // 16

    def write_mask(hk, selb_rows):
        selb = jnp.concatenate(selb_rows, axis=0).astype(bf)
        for g in range(G):
            qaug_ref[hk, NSA_DK:, g * tq:(g + 1) * tq] = selb

    def select(n_act):
        tail = [jnp.full((nb - n_act, tq), NEG, jnp.float32)] if n_act < nb else []
        for hk in HK:
            imp = imp_ref[hk, 0:n_act, :]
            if n_act <= SLC_TOPN:
                write_mask(hk, [jnp.where(imp > 0.5 * NEG, 0.0, NEG)] + tail)
                continue
            slabs = [imp[r0 * sub:(r0 + 1) * sub] for r0 in range(n_act // sub)]
            ranks = [jnp.zeros((sub, tq), jnp.int32) for _ in slabs]
            for jp in range(n_act):
                rowv = imp[jp:jp + 1, :]
                for r0, slab in enumerate(slabs):
                    lo = r0 * sub
                    if lo > jp:
                        one = jnp.where(rowv >= slab, 1, 0)
                    elif lo + sub - 1 <= jp:
                        one = jnp.where(rowv > slab, 1, 0)
                    else:
                        one = jnp.where(jsub + lo > jp, jnp.where(rowv >= slab, 1, 0),
                                        jnp.where(rowv > slab, 1, 0))
                    ranks[r0] = ranks[r0] + one
            write_mask(hk, [jnp.where(r < SLC_TOPN, 0.0, NEG) for r in ranks] + tail)

    for grp in range(n_groups):
        pl.when(t // tiles_per_group == grp)(functools.partial(select, 16 * (grp + 1)))

    slc = (m_ref, acc_ref)
    _flash_init(*slc)
    n_far = jnp.maximum(t - 1, 0)

    done = 0
    size = NSA_FAR_GROUP
    while size >= 1:
        trips = (n_far - done) // size

        def far(c, carry, size=size, done=done):
            chains = []
            for j in range(size):
                chains += tile_chains(ks_ref, vs_ref, qaug_ref, done + c * size + j, None, slc)
            _flash_chains(chains)
            return carry

        lax.fori_loop(0, trips, far, 0)
        done = done + trips * size
        size //= 2
    _flash_chains(tile_chains(ks_ref, vs_ref, qaug_ref, prev1, bias1, slc)
                  + tile_chains(ks_ref, vs_ref, qaug_ref, t, 0, slc))

    for hk in HK:
        gate = gate_ref[0, hk]
        comb = []
        for g in range(G):
            cols = slice(g * tq, (g + 1) * tq)
            gr = lambda br: gate[g * 3 + br:g * 3 + br + 1, :]
            w_s = gr(1) / acc_ref[hk, NSA_DK:NSA_DK + 1, cols]
            w_w = gr(2) / accw_ref[hk, NSA_DK:NSA_DK + 1, cols]
            comb.append(gr(0) * oc_ref[hk, :, cols] + w_s * acc_ref[hk, 0:NSA_DK, cols]
                        + w_w * accw_ref[hk, 0:NSA_DK, cols])
        width = G * NSA_DK
        o_ref[0, :, hk * width:(hk + 1) * width] = jnp.transpose(jnp.concatenate(comb, axis=0))


def _nsa(q_t, ks, vs_t, kw, vw_t, kc, vc_t, gate_t, bias_c, tz, ovt):
    B, _, S, _ = ks.shape
    tq, tk = NSA_TQ, NSA_TK
    nq = S // tq
    nc = kc.shape[2]
    m = NSA_GROUP * tq
    hkv = NSA_KV_HEADS
    tok = lambda: pl.BlockSpec((1, hkv, S, LANE), lambda b, t: (b, 0, 0, 0))
    feat = lambda: pl.BlockSpec((1, hkv, NSA_DK, S), lambda b, t: (b, 0, 0, 0))
    return pl.pallas_call(
        functools.partial(_nsa_kernel, tq=tq, tk=tk),
        grid=(B, nq),
        in_specs=[pl.BlockSpec((1, NSA_HEADS, NSA_DK, tq), lambda b, t: (b, 0, 0, t)),
                  tok(), feat(), tok(), feat(),
                  pl.BlockSpec((1, hkv, nc, LANE), lambda b, t: (b, 0, 0, 0)),
                  pl.BlockSpec((1, hkv, NSA_DK, nc), lambda b, t: (b, 0, 0, 0)),
                  pl.BlockSpec((1, hkv, GATE_ROWS, tq), lambda b, t: (b, 0, 0, t)),
                  pl.BlockSpec((hkv, 1, nc, m), lambda b, t: (0, t, 0, 0)),
                  _const_spec(tz.shape),
                  _const_spec(ovt.shape)],
        out_specs=pl.BlockSpec((1, tq, NSA_HEADS * NSA_DK), lambda b, t: (b, t, 0)),
        out_shape=jax.ShapeDtypeStruct((B, S, NSA_HEADS * NSA_DK), jnp.float32),
        scratch_shapes=[pltpu.VMEM((hkv, NSA_DK + SLC_ROWS, m), jnp.bfloat16),
                        pltpu.VMEM((hkv, NSA_DK + SLC_ROWS, m), jnp.bfloat16),
                        pltpu.VMEM((hkv, 1, m), jnp.float32),
                        pltpu.VMEM((hkv, NSA_DK + ONES_ROWS, m), jnp.float32),
                        pltpu.VMEM((hkv, 1, m), jnp.float32),
                        pltpu.VMEM((hkv, NSA_DK + ONES_ROWS, m), jnp.float32),
                        pltpu.VMEM((hkv, NSA_DK, m), jnp.float32),
                        pltpu.VMEM((hkv, SLC_ROWS, tq), jnp.float32)],
        compiler_params=_params(2), name="nsa",
    )(q_t, ks, vs_t, kw, vw_t, kc, vc_t, gate_t, bias_c, tz, ovt)


def _post_kernel(x_ref, om_ref, on_ref, gm_ref, gn_ref, wo_ref, gf_ref, wg_ref, wu_ref, wd_ref,
                 gfin_ref, o_ref):
    half = om_ref.shape[2]
    mix_m = _rms(om_ref[0], gm_ref[...]).astype(jnp.bfloat16)
    mix_n = _rms(on_ref[0], gn_ref[...]).astype(jnp.bfloat16)
    h = x_ref[0] + _dot(mix_m, wo_ref[0:half, :]) + _dot(mix_n, wo_ref[half:2 * half, :])
    f = _rms(h, gf_ref[...]).astype(jnp.bfloat16)
    a = _dot(f, wg_ref[...])
    act = (a * jax.nn.sigmoid(a) * _dot(f, wu_ref[...])).astype(jnp.bfloat16)
    h = h + _dot(act, wd_ref[...])
    o_ref[0] = _rms(h, gfin_ref[...])


def _post(x, o_mla, o_nsa, gm, gn, w_out, gf, wg, wu, wd, gfin):
    B, S, D = x.shape
    tm = min(POST_TM, S)
    tok = lambda w: pl.BlockSpec((1, tm, w), lambda b, i: (b, i, 0))
    consts = (gm, gn, w_out, gf, wg, wu, wd, gfin)
    return pl.pallas_call(
        _post_kernel, grid=(B, S // tm),
        in_specs=[tok(D), tok(o_mla.shape[2]), tok(o_nsa.shape[2])] + [_const_spec(c.shape) for c in consts],
        out_specs=tok(D), out_shape=jax.ShapeDtypeStruct((B, S, D), jnp.float32),
        compiler_params=_params(2), name="post",
    )(x, o_mla, o_nsa, *consts)


def _rope_tables(S):
    pos = jnp.arange(S, dtype=jnp.float32)
    inv = ROPE_THETA ** (-jnp.arange(0, MLA_ROPE, 2, dtype=jnp.float32) / MLA_ROPE)
    ang = pos[:, None] * inv[None, :]
    cos, sin = jnp.cos(ang), jnp.sin(ang)
    cos2 = jnp.concatenate([cos, cos], axis=-1)
    sin2 = jnp.concatenate([sin, sin], axis=-1)
    scale = (MLA_NOPE + MLA_ROPE) ** -0.5 * LOG2E
    zq = jnp.zeros((S, LANE - MLA_NOPE - MLA_ROPE), jnp.float32)
    cosq = jnp.concatenate([jnp.ones((S, MLA_NOPE), jnp.float32), cos2, zq], axis=-1) * scale
    sinq = jnp.concatenate([jnp.zeros((S, MLA_NOPE), jnp.float32), sin2, zq], axis=-1) * scale
    zk0 = jnp.zeros((S, MLA_NOPE), jnp.float32)
    cosk = jnp.concatenate([zk0, cos2, zq], axis=-1)
    sink = jnp.concatenate([zk0, sin2, zq], axis=-1)
    return cosq.T, sinq.T, cosk, sink


def _overlap_t(S):
    n_cmp = (S - CMP_LEN) // CMP_STRIDE + 1
    n_slc = S // SLC_LEN
    assert n_slc <= SLC_ROWS
    cs = np.arange(n_cmp) * CMP_STRIDE
    ss = np.arange(n_slc) * SLC_LEN
    ov = np.maximum(0, np.minimum(cs[:, None] + CMP_LEN, ss[None, :] + SLC_LEN)
                    - np.maximum(cs[:, None], ss[None, :])).astype(np.float32) / CMP_STRIDE
    out = np.zeros((SLC_ROWS, S // CMP_STRIDE), np.float32)
    out[:n_slc, :n_cmp] = ov.T
    return jnp.asarray(out, jnp.bfloat16)


def kernel(x, norm_mix_g, w_in, mla_q_norm_g, mla_w_uq, mla_kv_norm_g, mla_w_ukv, nsa_cmp_pos_k, nsa_cmp_w1_k, nsa_cmp_w2_k, nsa_cmp_pos_v, nsa_cmp_w1_v, nsa_cmp_w2_v, t5_table, out_norm_mla_g, out_norm_nsa_g, w_out, norm_ffn_g, w_gate, w_up, w_down, final_norm_g):
    B, S, D = x.shape
    assert w_in.shape[0] == 1
    assert D == D_MODEL and S % NSA_TQ == 0 and S % CMP_STRIDE == 0
    bf = jnp.bfloat16
    l = 0
    cosq_t, sinq_t, cosk, sink = _rope_tables(S)
    bias_c, tz = _bias_tables(t5_table, S)
    ovt = _overlap_t(S)
    nc = S // CMP_STRIDE
    cw = CMP_STRIDE * NSA_DK
    row = lambda v: v.reshape(1, -1)

    w_row = _w_in_row_layout().apply(w_in[l])
    w_col = _w_in_col_layout().apply(w_in[l]).T
    w_q2 = _w_uq_layout().apply(mla_w_uq[l]).T
    w_k = _w_uk_layout().apply(mla_w_ukv[l])
    w_v = _w_uv_layout().apply(mla_w_ukv[l]).T
    (q_mla, k_mla, v_mla, q_nsa, kv_cmp, k_slc, v_slc, k_win, v_win, gates) = _proj(
        x, row(norm_mix_g[l]), w_row, w_col, row(mla_q_norm_g[l]), w_q2, row(mla_kv_norm_g[l]), w_k, w_v,
        cosq_t, sinq_t, cosk, sink)

    def per_offset(pos_l, w1_l):
        pos2 = jnp.tile(pos_l, (1, NSA_KV_HEADS)).reshape(CMP_LEN, 1, NSA_KV_HEADS * NSA_DK)
        w = w1_l.reshape(CMP_LEN, NSA_DK, CMP_HIDDEN)
        z = jnp.zeros_like(w)
        return pos2, jnp.concatenate([jnp.concatenate([w, z], axis=2), jnp.concatenate([z, w], axis=2)], axis=1)

    pos_k, w1_k = per_offset(nsa_cmp_pos_k[l], nsa_cmp_w1_k[l])
    pos_v, w1_v = per_offset(nsa_cmp_pos_v[l], nsa_cmp_w1_v[l])
    pos = jnp.stack([pos_k, pos_v])
    w1 = jnp.stack([w1_k, w1_v]).astype(bf)
    kvc = kv_cmp
    w2k = jnp.concatenate([nsa_cmp_w2_k[l], jnp.zeros((CMP_HIDDEN, LANE - NSA_DK), jnp.float32)],
                          axis=1).astype(bf)
    w2vt = nsa_cmp_w2_v[l].T.astype(bf)
    k_cmp, v_cmp = _compress(kvc, pos, w1, w2k, w2vt)

    o_mla = _mla(q_mla, k_mla, v_mla)
    o_nsa = _nsa(q_nsa, k_slc, v_slc, k_win, v_win, k_cmp, v_cmp, gates, bias_c, tz, ovt)

    return _post(x, o_mla, o_nsa, row(out_norm_mla_g[l]), row(out_norm_nsa_g[l]), w_out[l].astype(bf),
                 row(norm_ffn_g[l]), w_gate[l].astype(bf), w_up[l].astype(bf), w_down[l].astype(bf),
                 row(final_norm_g))
```

```python
import functools
import math

import numpy as np
import jax
import jax.numpy as jnp
from jax import lax
from jax.experimental import pallas as pl
from jax.experimental.pallas import tpu as pltpu

D_MODEL = 1024
MLA_HEADS = 8
MLA_NOPE = 64
MLA_ROPE = 32
MLA_V = 64
MLA_Q_LORA = 256
MLA_KV_LORA = 128
ROPE_THETA = 10000.0
NSA_HEADS = 8
NSA_KV_HEADS = 2
NSA_GROUP = 4
NSA_DK = 64
CMP_LEN = 32
CMP_STRIDE = 16
CMP_HIDDEN = 128
SLC_LEN = 64
SLC_TOPN = 16
WINDOW = 512
T5_BUCKETS = 32
T5_MAX_DIST = 128
D_FF = 2816
EPS = 1e-6
NEG = -1e30
FORCE = 1e30

LANE = 128
SUBLANE = 8
SLC_ROWS = 64
GATE_ROWS = 16
ONES_ROWS = 16
LOG2E = math.log2(math.e)

PROJ_TM = 512
MLA_TK = 256
MLA_CW = 256
FLASH_LOOKAHEAD = 6
NSA_TQ = 256
NSA_TK = 256
NSA_FAR_GROUP = 8
POST_TM = 512
V7X_VMEM_BYTES = 64 * 1024 * 1024
VMEM_LIMIT = V7X_VMEM_BYTES * 7 // 8

C_CQ = 0
C_CKV = 256
C_MISC = 384
C_KVCMP = 512
C_KSLC = 768
C_KWIN = 1024
D_ROW = 1280
R_QNSA = 0
R_VSLC = 512
R_VWIN = 640
R_GATE = 768
D_COL = 800


def _dot(a, b):
    return jnp.dot(a, b, preferred_element_type=jnp.float32)


def _dot_nt(a, b):
    return lax.dot_general(a, b, (((1,), (1,)), ((), ())), preferred_element_type=jnp.float32)


def _rms(x, g):
    return x * lax.rsqrt(jnp.mean(x * x, axis=-1, keepdims=True) + EPS) * g


def _const_spec(shape):
    nd = len(shape)
    return pl.BlockSpec(shape, lambda *_: (0,) * nd, pipeline_mode=pl.Buffered(1))


def _params(n_axes):
    return pltpu.CompilerParams(dimension_semantics=("arbitrary",) * n_axes,
                                vmem_limit_bytes=VMEM_LIMIT)


def _in_offsets():
    o_krope = MLA_Q_LORA + MLA_KV_LORA
    o_qnsa = o_krope + MLA_ROPE
    o_cmp = o_qnsa + NSA_HEADS * NSA_DK
    o_slc = o_cmp + 2 * NSA_KV_HEADS * NSA_DK
    o_win = o_slc + 2 * NSA_KV_HEADS * NSA_DK
    o_gate = o_win + 2 * NSA_KV_HEADS * NSA_DK
    return o_krope, o_qnsa, o_cmp, o_slc, o_win, o_gate


class _Layout:
    def __init__(self, n):
        self.src = np.zeros((n,), np.int32)
        self.mul = np.zeros((n,), np.float32)

    def put(self, dst, s, n, m=1.0):
        self.src[dst:dst + n] = np.arange(s, s + n)
        self.mul[dst:dst + n] = m

    def apply(self, w):
        return (w[:, self.src] * self.mul[None, :]).astype(jnp.bfloat16)


def _w_in_row_layout():
    o_krope, _, o_cmp, o_slc, o_win, _ = _in_offsets()
    lay = _Layout(D_ROW)
    lay.put(C_CQ, 0, MLA_Q_LORA)
    lay.put(C_CKV, MLA_Q_LORA, MLA_KV_LORA)
    half = MLA_ROPE // 2
    lay.put(C_MISC + MLA_NOPE, o_krope, MLA_ROPE)
    lay.put(C_MISC + MLA_NOPE + MLA_ROPE, o_krope + half, half, -1.0)
    lay.put(C_MISC + MLA_NOPE + MLA_ROPE + half, o_krope, half, 1.0)
    lay.put(C_KVCMP, o_cmp, 2 * NSA_KV_HEADS * NSA_DK)
    for hk in range(NSA_KV_HEADS):
        lay.put(C_KSLC + hk * LANE, o_slc + hk * NSA_DK, NSA_DK)
        lay.put(C_KWIN + hk * LANE, o_win + hk * NSA_DK, NSA_DK)
    return lay


def _w_in_col_layout():
    _, o_qnsa, _, o_slc, o_win, o_gate = _in_offsets()
    lay = _Layout(D_COL)
    lay.put(R_QNSA, o_qnsa, NSA_HEADS * NSA_DK, NSA_DK ** -0.5)
    lay.put(R_VSLC, o_slc + NSA_KV_HEADS * NSA_DK, NSA_KV_HEADS * NSA_DK)
    lay.put(R_VWIN, o_win + NSA_KV_HEADS * NSA_DK, NSA_KV_HEADS * NSA_DK)
    for hk in range(NSA_KV_HEADS):
        lay.put(R_GATE + hk * GATE_ROWS, o_gate + hk * NSA_GROUP * 3, NSA_GROUP * 3)
    return lay


def _w_uq_layout():
    lay = _Layout(MLA_HEADS * LANE)
    dq = MLA_NOPE + MLA_ROPE
    for h in range(MLA_HEADS):
        lay.put(h * LANE, h * dq, dq)
    return lay


def _w_uk_layout():
    lay = _Layout(MLA_HEADS * LANE)
    per = MLA_NOPE + MLA_V
    for h in range(MLA_HEADS):
        lay.put(h * LANE, h * per, MLA_NOPE)
    return lay


def _w_uv_layout():
    lay = _Layout(MLA_HEADS * MLA_V)
    per = MLA_NOPE + MLA_V
    for h in range(MLA_HEADS):
        lay.put(h * MLA_V, h * per + MLA_NOPE, MLA_V)
    return lay


def _t5_thresholds(max_dist):
    n = np.arange(max_dist, dtype=np.int64)
    max_exact = T5_BUCKETS // 2
    nf = np.maximum(n, 1).astype(np.float32)
    val = (np.log(nf / np.float32(max_exact)) / np.float32(math.log(T5_MAX_DIST / max_exact))
           * np.float32(T5_BUCKETS - max_exact))
    large = np.minimum(max_exact + val.astype(np.int32), T5_BUCKETS - 1)
    bucket = np.where(n < max_exact, n, large)
    assert np.all(np.diff(bucket) >= 0)
    frac = np.abs(val[max_exact + 1:T5_MAX_DIST] - np.round(val[max_exact + 1:T5_MAX_DIST]))
    assert frac.min() > 1e-3
    return [int(np.argmax(bucket >= b)) for b in range(T5_BUCKETS)]


def _proj_kernel(x_ref, gmix_ref, wrow_ref, wcol_ref, gq_ref, wq_ref, gkv_ref, wk_ref, wv_ref,
                 cq_ref, sq_ref, ck_ref, sk_ref,
                 qm_ref, km_ref, vm_ref, qn_ref, kvc_ref, ks_ref, vs_ref, kw_ref, vw_ref, gate_ref):
    tm = x_ref.shape[1]
    bf = jnp.bfloat16
    xn = _rms(x_ref[0], gmix_ref[...]).astype(bf)
    u = _dot(xn, wrow_ref[...])
    ut = _dot_nt(wcol_ref[...], xn)

    cqn = _rms(u[:, C_CQ:C_CQ + MLA_Q_LORA], gq_ref[...]).astype(bf)
    qq = _dot_nt(wq_ref[...], cqn)
    cosq = cq_ref[...]
    sinq = sq_ref[...]
    r0, r1, r2 = MLA_NOPE, MLA_NOPE + MLA_ROPE // 2, MLA_NOPE + MLA_ROPE
    for h in range(MLA_HEADS):
        q = qq[h * LANE:(h + 1) * LANE]
        rot = jnp.concatenate([q[0:r0], -q[r1:r2], q[r0:r1], q[r2:]], axis=0)
        qm_ref[0, h] = (q * cosq + rot * sinq).astype(bf)

    ckvn = _rms(u[:, C_CKV:C_CKV + MLA_KV_LORA], gkv_ref[...]).astype(bf)
    kk = _dot(ckvn, wk_ref[...])
    misc = u[:, C_MISC:C_MISC + LANE]
    kpe = misc * ck_ref[...] + pltpu.roll(misc, LANE - MLA_ROPE, axis=1) * sk_ref[...]
    for h in range(MLA_HEADS):
        km_ref[0, h] = (kk[:, h * LANE:(h + 1) * LANE] + kpe).astype(bf)
    vv = _dot_nt(wv_ref[...], ckvn)
    for h in range(MLA_HEADS):
        vm_ref[0, h] = vv[h * MLA_V:(h + 1) * MLA_V].astype(bf)

    for h in range(NSA_HEADS):
        qn_ref[0, h] = (ut[R_QNSA + h * NSA_DK:R_QNSA + (h + 1) * NSA_DK] * LOG2E).astype(bf)
    for kv in range(2):
        kvc_ref[0, kv] = u[:, C_KVCMP + kv * LANE:C_KVCMP + (kv + 1) * LANE]
    pos = pl.program_id(1) * tm + lax.broadcasted_iota(jnp.int32, (tm, LANE), 0)
    lane = lax.broadcasted_iota(jnp.int32, (tm, LANE), 1)
    onehot = jnp.where(lane - (LANE - SLC_ROWS) == pos // SLC_LEN, 1.0, 0.0)
    for hk in range(NSA_KV_HEADS):
        ks_ref[0, hk] = (u[:, C_KSLC + hk * LANE:C_KSLC + (hk + 1) * LANE] + onehot).astype(bf)
        kw_ref[0, hk] = u[:, C_KWIN + hk * LANE:C_KWIN + (hk + 1) * LANE].astype(bf)
        vs_ref[0, hk] = ut[R_VSLC + hk * NSA_DK:R_VSLC + (hk + 1) * NSA_DK].astype(bf)
        vw_ref[0, hk] = ut[R_VWIN + hk * NSA_DK:R_VWIN + (hk + 1) * NSA_DK].astype(bf)
        gate_ref[0, hk] = jax.nn.sigmoid(ut[R_GATE + hk * GATE_ROWS:R_GATE + (hk + 1) * GATE_ROWS])


def _proj(x, gmix, w_row, w_col, gq, w_q2, gkv, w_k, w_v, cosq_t, sinq_t, cosk, sink):
    B, S, D = x.shape
    tm = min(PROJ_TM, S)
    bf = jnp.bfloat16
    grid = (B, S // tm)
    tok_major = lambda n: pl.BlockSpec((1, n, tm, LANE), lambda b, i: (b, 0, i, 0))
    feat_major = lambda n, d: pl.BlockSpec((1, n, d, tm), lambda b, i: (b, 0, 0, i))
    out_shape = (
        jax.ShapeDtypeStruct((B, MLA_HEADS, LANE, S), bf),
        jax.ShapeDtypeStruct((B, MLA_HEADS, S, LANE), bf),
        jax.ShapeDtypeStruct((B, MLA_HEADS, MLA_V, S), bf),
        jax.ShapeDtypeStruct((B, NSA_HEADS, NSA_DK, S), bf),
        jax.ShapeDtypeStruct((B, 2, S, LANE), jnp.float32),
        jax.ShapeDtypeStruct((B, NSA_KV_HEADS, S, LANE), bf),
        jax.ShapeDtypeStruct((B, NSA_KV_HEADS, NSA_DK, S), bf),
        jax.ShapeDtypeStruct((B, NSA_KV_HEADS, S, LANE), bf),
        jax.ShapeDtypeStruct((B, NSA_KV_HEADS, NSA_DK, S), bf),
        jax.ShapeDtypeStruct((B, NSA_KV_HEADS, GATE_ROWS, S), jnp.float32),
    )
    out_specs = (
        feat_major(MLA_HEADS, LANE), tok_major(MLA_HEADS), feat_major(MLA_HEADS, MLA_V),
        feat_major(NSA_HEADS, NSA_DK),
        tok_major(2),
        tok_major(NSA_KV_HEADS), feat_major(NSA_KV_HEADS, NSA_DK),
        tok_major(NSA_KV_HEADS), feat_major(NSA_KV_HEADS, NSA_DK),
        feat_major(NSA_KV_HEADS, GATE_ROWS),
    )
    consts = (gmix, w_row, w_col, gq, w_q2, gkv, w_k, w_v)
    in_specs = ([pl.BlockSpec((1, tm, D), lambda b, i: (b, i, 0))] + [_const_spec(c.shape) for c in consts]
                + [pl.BlockSpec((LANE, tm), lambda b, i: (0, i)), pl.BlockSpec((LANE, tm), lambda b, i: (0, i)),
                   pl.BlockSpec((tm, LANE), lambda b, i: (i, 0)), pl.BlockSpec((tm, LANE), lambda b, i: (i, 0))])
    return pl.pallas_call(
        _proj_kernel, grid=grid, in_specs=in_specs, out_specs=out_specs, out_shape=out_shape,
        compiler_params=_params(2), name="proj",
    )(x, *consts, cosq_t, sinq_t, cosk, sink)


def _compress_kernel(c_ref, pos_ref, w1_ref, w2k_ref, w2vt_ref, kc_ref, vc_ref):
    nc = kc_ref.shape[2]
    bf = jnp.bfloat16
    hw = NSA_KV_HEADS * NSA_DK
    for kv in range(2):
        halves = []
        for half in range(CMP_LEN // CMP_STRIDE):
            acc = None
            for r in range(CMP_STRIDE):
                l = half * CMP_STRIDE + r
                x = c_ref[0, kv, pl.ds(r, nc, stride=CMP_STRIDE), :]
                part = _dot((x + pos_ref[kv, l]).astype(bf), w1_ref[kv, l])
                acc = part if acc is None else acc + part
            halves.append(acc)
        hid = halves[0] + pltpu.roll(halves[1], nc - 1, axis=0)
        act = jax.nn.gelu(hid).astype(bf)
        for hk in range(NSA_KV_HEADS):
            act_h = act[:, hk * CMP_HIDDEN:(hk + 1) * CMP_HIDDEN]
            if kv == 0:
                kc_ref[0, hk] = _dot(act_h, w2k_ref[...]).astype(bf)
            else:
                vc_ref[0, hk] = _dot_nt(w2vt_ref[...], act_h).astype(bf)


def _compress(kvc, pos, w1, w2k, w2vt):
    B, _, S, width = kvc.shape
    nc = S // CMP_STRIDE
    return pl.pallas_call(
        _compress_kernel, grid=(B,),
        in_specs=[pl.BlockSpec((1, 2, S, width), lambda b: (b, 0, 0, 0)),
                  _const_spec(pos.shape), _const_spec(w1.shape), _const_spec(w2k.shape),
                  _const_spec(w2vt.shape)],
        out_specs=(pl.BlockSpec((1, NSA_KV_HEADS, nc, LANE), lambda b: (b, 0, 0, 0)),
                   pl.BlockSpec((1, NSA_KV_HEADS, NSA_DK, nc), lambda b: (b, 0, 0, 0))),
        out_shape=(jax.ShapeDtypeStruct((B, NSA_KV_HEADS, nc, LANE), jnp.bfloat16),
                   jax.ShapeDtypeStruct((B, NSA_KV_HEADS, NSA_DK, nc), jnp.bfloat16)),
        compiler_params=_params(1), name="compress",
    )(kvc, pos, w1, w2k, w2vt)


def _t5_lookup(dist, tab_ref, head, thr):
    val = jnp.full(dist.shape, tab_ref[0, head], jnp.float32)
    for b in range(1, T5_BUCKETS):
        val = jnp.where(dist >= thr[b], tab_ref[b, head], val)
    return (val - tab_ref[T5_BUCKETS - 1, head]) * LOG2E


def _bias_cmp_kernel(tab_ref, out_ref, *, thr, tq, n_cmp):
    hk = pl.program_id(0)
    t = pl.program_id(1)
    nc = out_ref.shape[2]
    per_tile = tq // CMP_STRIDE
    band = per_tile + (T5_MAX_DIST + CMP_LEN) // CMP_STRIDE + SUBLANE
    band = min(-(-band // SUBLANE) * SUBLANE, nc)
    start = jnp.clip(t * per_tile - (band - per_tile), 0, nc - band)
    start = pl.multiple_of(start // SUBLANE * SUBLANE, SUBLANE)
    n_all = lax.broadcasted_iota(jnp.int32, (nc, NSA_GROUP * tq), 0)
    out_ref[0, 0] = jnp.where(n_all < start, 0.0, NEG)
    n = start + lax.broadcasted_iota(jnp.int32, (band, tq), 0)
    i = lax.broadcasted_iota(jnp.int32, (band, tq), 1)
    dist = t * tq + i - (n * CMP_STRIDE + CMP_LEN - 1)
    ok = jnp.logical_and(dist >= 0, n < n_cmp)
    for g in range(NSA_GROUP):
        val = _t5_lookup(dist, tab_ref, hk * NSA_GROUP + g, thr)
        out_ref[0, 0, pl.ds(start, band), g * tq:(g + 1) * tq] = jnp.where(ok, val, NEG)


def _bias_tile_kernel(tab_ref, out_ref, *, thr, tq):
    hk = pl.program_id(0)
    tk = out_ref.shape[2]
    j = lax.broadcasted_iota(jnp.int32, (tk, tq), 0)
    i = lax.broadcasted_iota(jnp.int32, (tk, tq), 1)
    for g in range(NSA_GROUP):
        head = hk * NSA_GROUP + g
        cols = slice(g * tq, (g + 1) * tq)
        d0 = i - j
        out_ref[0, 0, :, cols] = jnp.where(d0 >= 0, _t5_lookup(d0, tab_ref, head, thr), NEG)
        out_ref[0, 1, :, cols] = _t5_lookup(d0 + tk, tab_ref, head, thr)
        out_ref[0, 2, :, cols] = jnp.where(j > i, 0.0, NEG)
        out_ref[0, 3, :, cols] = jnp.full((tk, tq), NEG, jnp.float32)


def _bias_tables(t5_table, S):
    thr = _t5_thresholds(S)
    tq, tk = NSA_TQ, NSA_TK
    assert tq == tk and WINDOW == 2 * tk and tk >= T5_MAX_DIST
    nq = S // tq
    nc = S // CMP_STRIDE
    n_cmp = (S - CMP_LEN) // CMP_STRIDE + 1
    m = NSA_GROUP * tq
    smem = pl.BlockSpec(memory_space=pltpu.SMEM)
    bias_c = pl.pallas_call(
        functools.partial(_bias_cmp_kernel, thr=thr, tq=tq, n_cmp=n_cmp),
        grid=(NSA_KV_HEADS, nq), in_specs=[smem],
        out_specs=pl.BlockSpec((1, 1, nc, m), lambda h, t: (h, t, 0, 0)),
        out_shape=jax.ShapeDtypeStruct((NSA_KV_HEADS, nq, nc, m), jnp.float32),
        compiler_params=_params(2), name="bias_cmp",
    )(t5_table)
    tz = pl.pallas_call(
        functools.partial(_bias_tile_kernel, thr=thr, tq=tq),
        grid=(NSA_KV_HEADS,), in_specs=[smem],
        out_specs=pl.BlockSpec((1, 4, tk, m), lambda h: (h, 0, 0, 0)),
        out_shape=jax.ShapeDtypeStruct((NSA_KV_HEADS, 4, tk, m), jnp.float32),
        compiler_params=_params(1), name="bias_tile",
    )(t5_table)
    return bias_c, tz


def _pipeline(stages):
    pending = [issue() for issue, _ in stages[:FLASH_LOOKAHEAD]]
    for i, (_, consume) in enumerate(stages):
        s = pending.pop(0)
        if i + FLASH_LOOKAHEAD < len(stages):
            pending.append(stages[i + FLASH_LOOKAHEAD][0]())
        consume(s)


def _flash_stage(chain):
    k, q_t, v_aug, bias_t, m_ref, acc_ref = chain

    def issue():
        s = _dot(k, q_t)
        return s if bias_t is None else s + bias_t

    def consume(s):
        m_prev = m_ref[...]
        m_new = jnp.maximum(m_prev, jnp.max(s, axis=0, keepdims=True))
        alpha = jnp.exp2(m_prev - m_new)
        p = jnp.exp2(s - m_new).astype(jnp.bfloat16)
        acc_ref[...] = alpha * acc_ref[...] + _dot(v_aug, p)
        m_ref[...] = m_new

    return issue, consume


def _flash_chains(chains):
    _pipeline([_flash_stage(c) for c in chains])


def _with_ones(v_t):
    tk = v_t.shape[1]
    row = lax.broadcasted_iota(jnp.int32, (ONES_ROWS, tk), 0)
    return jnp.concatenate([v_t, jnp.where(row == 0, 1.0, 0.0).astype(v_t.dtype)], axis=0)


def _col_chains(k, q_ref, v_aug, bias_ref, m_ref, acc_ref, ncol, width):
    chains = []
    for c0 in range(0, ncol, width):
        cols = slice(c0, c0 + width)
        chains.append((k, q_ref[:, cols], v_aug, None if bias_ref is None else bias_ref[:, cols],
                       m_ref.at[:, cols], acc_ref.at[:, cols]))
    return chains


def _flash_init(m_ref, acc_ref):
    m_ref[...] = jnp.full(m_ref.shape, -jnp.inf, jnp.float32)
    acc_ref[...] = jnp.zeros(acc_ref.shape, jnp.float32)


def _flash_result(acc_ref, dv):
    return acc_ref[0:dv, :] * (1.0 / acc_ref[dv:dv + 1, :])


def _mla_kernel(q_ref, k_ref, v_ref, o_ref, m_ref, acc_ref, *, tk, cw):
    S = k_ref.shape[2]
    kk = lax.broadcasted_iota(jnp.int32, (tk, cw), 0)
    qq = lax.broadcasted_iota(jnp.int32, (tk, cw), 1)
    for e in range(2):
        _flash_init(m_ref.at[e], acc_ref.at[e])
    chains = []
    for k0 in range(0, S, tk):
        for e in range(2):
            k = k_ref[0, e, k0:k0 + tk, :]
            v_aug = _with_ones(v_ref[0, e, :, k0:k0 + tk])
            for c0 in range(0, S, cw):
                if k0 > c0 + cw - 1:
                    continue
                bias = jnp.where(kk + k0 <= qq + c0, 0.0, NEG) if k0 + tk - 1 > c0 else None
                cols = slice(c0, c0 + cw)
                chains.append((k, q_ref[0, e, :, cols], v_aug, bias,
                               m_ref.at[e, :, cols], acc_ref.at[e, :, cols]))
    _flash_chains(chains)
    o_t = jnp.concatenate([_flash_result(acc_ref.at[e], MLA_V) for e in range(2)], axis=0)
    o_ref[0] = jnp.transpose(o_t)


def _mla(q_t, k, v_t):
    B, H, S, _ = k.shape
    tq = S
    tk = min(MLA_TK, S)
    cw = min(MLA_CW, S)
    return pl.pallas_call(
        functools.partial(_mla_kernel, tk=tk, cw=cw),
        grid=(B, H // 2),
        in_specs=[pl.BlockSpec((1, 2, LANE, S), lambda b, p: (b, p, 0, 0)),
                  pl.BlockSpec((1, 2, S, LANE), lambda b, p: (b, p, 0, 0)),
                  pl.BlockSpec((1, 2, MLA_V, S), lambda b, p: (b, p, 0, 0))],
        out_specs=pl.BlockSpec((1, S, 2 * MLA_V), lambda b, p: (b, 0, p)),
        out_shape=jax.ShapeDtypeStruct((B, S, H * MLA_V), jnp.float32),
        scratch_shapes=[pltpu.VMEM((2, 1, tq), jnp.float32),
                        pltpu.VMEM((2, MLA_V + ONES_ROWS, tq), jnp.float32)],
        compiler_params=_params(2), name="mla",
    )(q_t, k, v_t)


def _nsa_kernel(q_ref, ks_ref, vs_ref, kw_ref, vw_ref, kc_ref, vc_ref, gate_ref, bc_ref, tz_ref, ovt_ref,
                o_ref, qaug_ref, qpad_ref, m_ref, acc_ref, mw_ref, accw_ref, oc_ref, imp_ref, *, tq, tk):
    t = pl.program_id(1)
    G = NSA_GROUP
    HK = range(NSA_KV_HEADS)
    ncol = G * tq
    bf = jnp.bfloat16
    q0 = t * tq
    for hk in HK:
        for g in range(G):
            qpad_ref[hk, 0:NSA_DK, g * tq:(g + 1) * tq] = q_ref[0, hk * G + g]
            qaug_ref[hk, 0:NSA_DK, g * tq:(g + 1) * tq] = q_ref[0, hk * G + g]
        qpad_ref[hk, NSA_DK:, :] = jnp.zeros((SLC_ROWS, ncol), bf)

    def tile_chains(k_ref, v_ref, qx_ref, c, bias_idx, stats):
        k0 = pl.multiple_of(c * tk, tk)
        chains = []
        for hk in HK:
            bias_ref = None if bias_idx is None else tz_ref.at[hk, bias_idx]
            chains += _col_chains(k_ref[0, hk, pl.ds(k0, tk), :], qx_ref.at[hk],
                                  _with_ones(v_ref[0, hk, :, pl.ds(k0, tk)]),
                                  bias_ref, stats[0].at[hk], stats[1].at[hk], ncol, tq)
        return chains

    prev1 = jnp.maximum(t - 1, 0)
    bias1 = jnp.where(t >= 1, 1, 3)

    nb = ovt_ref.shape[0]
    sub = SUBLANE
    col_ok = q0 + lax.broadcasted_iota(jnp.int32, (1, tq), 1) >= CMP_LEN - 1
    jb = lax.broadcasted_iota(jnp.int32, (nb, tq), 0)
    cur = (q0 + lax.broadcasted_iota(jnp.int32, (nb, tq), 1)) // SLC_LEN
    forced = jnp.logical_or(jb == 0, jnp.logical_or(jb == cur, jb == cur - 1))
    jsub = lax.broadcasted_iota(jnp.int32, (sub, tq), 0)
    imp_sum = [None] * NSA_KV_HEADS

    def cmp_stage(hk, g):
        cols = slice(g * tq, (g + 1) * tq)

        def issue():
            return _dot(kc_ref[0, hk], qpad_ref[hk, :, cols]) + bc_ref[hk, 0, :, cols]

        def consume(s):
            mx = jnp.max(s, axis=0, keepdims=True)
            p = jnp.exp2(s - mx)
            lsum = jnp.sum(p, axis=0, keepdims=True)
            pcb = (p * jnp.where(col_ok, 1.0 / lsum, 0.0)).astype(bf)
            both = _dot(jnp.concatenate([vc_ref[0, hk], ovt_ref[...]], axis=0), pcb)
            oc_ref[hk, :, cols] = both[0:NSA_DK]
            imp_sum[hk] = both[NSA_DK:] if imp_sum[hk] is None else imp_sum[hk] + both[NSA_DK:]

        return issue, consume

    win = (mw_ref, accw_ref)
    _flash_init(*win)
    win_stages = [_flash_stage(c) for c in
                  tile_chains(kw_ref, vw_ref, qpad_ref, jnp.maximum(t - 2, 0), jnp.where(t >= 2, 2, 3), win)
                  + tile_chains(kw_ref, vw_ref, qpad_ref, prev1, bias1, win)
                  + tile_chains(kw_ref, vw_ref, qpad_ref, t, 0, win)]
    cmp_stages = [cmp_stage(hk, g) for g in range(G) for hk in HK]
    every = len(win_stages) // len(cmp_stages)
    stages = []
    for i, st in enumerate(cmp_stages):
        stages += [st] + win_stages[i * every:(i + 1) * every]
    _pipeline(stages + win_stages[len(cmp_stages) * every:])
    for hk in HK:
        imp_ref[hk] = jnp.where(forced, FORCE, jnp.where(jb <= cur, imp_sum[hk], NEG))

    tiles_per_group = 16 * SLC_LEN // tq
    n_groups = nb // 16

    def write_mask(hk, selb_rows):
        selb = jnp.concatenate(selb_rows, axis=0).astype(bf)
        for g in range(G):
            qaug_ref[hk, NSA_DK:, g * tq:(g + 1) * tq] = selb

    def select(n_act):
        tail = [jnp.full((nb - n_act, tq), NEG, jnp.float32)] if n_act < nb else []
        for hk in HK:
            imp = imp_ref[hk, 0:n_act, :]
            if n_act <= SLC_TOPN:
                write_mask(hk, [jnp.where(imp > 0.5 * NEG, 0.0, NEG)] + tail)
                continue
            slabs = [imp[r0 * sub:(r0 + 1) * sub] for r0 in range(n_act // sub)]
            ranks = [jnp.zeros((sub, tq), jnp.int32) for _ in slabs]
            for jp in range(n_act):
                rowv = imp[jp:jp + 1, :]
                for r0, slab in enumerate(slabs):
                    lo = r0 * sub
                    if lo > jp:
                        one = jnp.where(rowv >= slab, 1, 0)
                    elif lo + sub - 1 <= jp:
                        one = jnp.where(rowv > slab, 1, 0)
                    else:
                        one = jnp.where(jsub + lo > jp, jnp.where(rowv >= slab, 1, 0),
                                        jnp.where(rowv > slab, 1, 0))
                    ranks[r0] = ranks[r0] + one
            write_mask(hk, [jnp.where(r < SLC_TOPN, 0.0, NEG) for r in ranks] + tail)

    for grp in range(n_groups):
        pl.when(t // tiles_per_group == grp)(functools.partial(select, 16 * (grp + 1)))

    slc = (m_ref, acc_ref)
    _flash_init(*slc)
    n_far = jnp.maximum(t - 1, 0)

    done = 0
    size = NSA_FAR_GROUP
    while size >= 1:
        trips = (n_far - done) // size

        def far(c, carry, size=size, done=done):
            chains = []
            for j in range(size):
                chains += tile_chains(ks_ref, vs_ref, qaug_ref, done + c * size + j, None, slc)
            _flash_chains(chains)
            return carry

        lax.fori_loop(0, trips, far, 0)
        done = done + trips * size
        size //= 2
    _flash_chains(tile_chains(ks_ref, vs_ref, qaug_ref, prev1, bias1, slc)
                  + tile_chains(ks_ref, vs_ref, qaug_ref, t, 0, slc))

    for hk in HK:
        gate = gate_ref[0, hk]
        comb = []
        for g in range(G):
            cols = slice(g * tq, (g + 1) * tq)
            gr = lambda br: gate[g * 3 + br:g * 3 + br + 1, :]
            w_s = gr(1) / acc_ref[hk, NSA_DK:NSA_DK + 1, cols]
            w_w = gr(2) / accw_ref[hk, NSA_DK:NSA_DK + 1, cols]
            comb.append(gr(0) * oc_ref[hk, :, cols] + w_s * acc_ref[hk, 0:NSA_DK, cols]
                        + w_w * accw_ref[hk, 0:NSA_DK, cols])
        width = G * NSA_DK
        o_ref[0, :, hk * width:(hk + 1) * width] = jnp.transpose(jnp.concatenate(comb, axis=0))


def _nsa(q_t, ks, vs_t, kw, vw_t, kc, vc_t, gate_t, bias_c, tz, ovt):
    B, _, S, _ = ks.shape
    tq, tk = NSA_TQ, NSA_TK
    nq = S // tq
    nc = kc.shape[2]
    m = NSA_GROUP * tq
    hkv = NSA_KV_HEADS
    tok = lambda: pl.BlockSpec((1, hkv, S, LANE), lambda b, t: (b, 0, 0, 0))
    feat = lambda: pl.BlockSpec((1, hkv, NSA_DK, S), lambda b, t: (b, 0, 0, 0))
    return pl.pallas_call(
        functools.partial(_nsa_kernel, tq=tq, tk=tk),
        grid=(B, nq),
        in_specs=[pl.BlockSpec((1, NSA_HEADS, NSA_DK, tq), lambda b, t: (b, 0, 0, t)),
                  tok(), feat(), tok(), feat(),
                  pl.BlockSpec((1, hkv, nc, LANE), lambda b, t: (b, 0, 0, 0)),
                  pl.BlockSpec((1, hkv, NSA_DK, nc), lambda b, t: (b, 0, 0, 0)),
                  pl.BlockSpec((1, hkv, GATE_ROWS, tq), lambda b, t: (b, 0, 0, t)),
                  pl.BlockSpec((hkv, 1, nc, m), lambda b, t: (0, t, 0, 0)),
                  _const_spec(tz.shape),
                  _const_spec(ovt.shape)],
        out_specs=pl.BlockSpec((1, tq, NSA_HEADS * NSA_DK), lambda b, t: (b, t, 0)),
        out_shape=jax.ShapeDtypeStruct((B, S, NSA_HEADS * NSA_DK), jnp.float32),
        scratch_shapes=[pltpu.VMEM((hkv, NSA_DK + SLC_ROWS, m), jnp.bfloat16),
                        pltpu.VMEM((hkv, NSA_DK + SLC_ROWS, m), jnp.bfloat16),
                        pltpu.VMEM((hkv, 1, m), jnp.float32),
                        pltpu.VMEM((hkv, NSA_DK + ONES_ROWS, m), jnp.float32),
                        pltpu.VMEM((hkv, 1, m), jnp.float32),
                        pltpu.VMEM((hkv, NSA_DK + ONES_ROWS, m), jnp.float32),
                        pltpu.VMEM((hkv, NSA_DK, m), jnp.float32),
                        pltpu.VMEM((hkv, SLC_ROWS, tq), jnp.float32)],
        compiler_params=_params(2), name="nsa",
    )(q_t, ks, vs_t, kw, vw_t, kc, vc_t, gate_t, bias_c, tz, ovt)


def _post_kernel(x_ref, om_ref, on_ref, gm_ref, gn_ref, wo_ref, gf_ref, wg_ref, wu_ref, wd_ref,
                 gfin_ref, o_ref):
    half = om_ref.shape[2]
    mix_m = _rms(om_ref[0], gm_ref[...]).astype(jnp.bfloat16)
    mix_n = _rms(on_ref[0], gn_ref[...]).astype(jnp.bfloat16)
    h = x_ref[0] + _dot(mix_m, wo_ref[0:half, :]) + _dot(mix_n, wo_ref[half:2 * half, :])
    f = _rms(h, gf_ref[...]).astype(jnp.bfloat16)
    a = _dot(f, wg_ref[...])
    act = (a * jax.nn.sigmoid(a) * _dot(f, wu_ref[...])).astype(jnp.bfloat16)
    h = h + _dot(act, wd_ref[...])
    o_ref[0] = _rms(h, gfin_ref[...])


def _post(x, o_mla, o_nsa, gm, gn, w_out, gf, wg, wu, wd, gfin):
    B, S, D = x.shape
    tm = min(POST_TM, S)
    tok = lambda w: pl.BlockSpec((1, tm, w), lambda b, i: (b, i, 0))
    consts = (gm, gn, w_out, gf, wg, wu, wd, gfin)
    return pl.pallas_call(
        _post_kernel, grid=(B, S // tm),
        in_specs=[tok(D), tok(o_mla.shape[2]), tok(o_nsa.shape[2])] + [_const_spec(c.shape) for c in consts],
        out_specs=tok(D), out_shape=jax.ShapeDtypeStruct((B, S, D), jnp.float32),
        compiler_params=_params(2), name="post",
    )(x, o_mla, o_nsa, *consts)


def _rope_tables(S):
    pos = jnp.arange(S, dtype=jnp.float32)
    inv = ROPE_THETA ** (-jnp.arange(0, MLA_ROPE, 2, dtype=jnp.float32) / MLA_ROPE)
    ang = pos[:, None] * inv[None, :]
    cos, sin = jnp.cos(ang), jnp.sin(ang)
    cos2 = jnp.concatenate([cos, cos], axis=-1)
    sin2 = jnp.concatenate([sin, sin], axis=-1)
    scale = (MLA_NOPE + MLA_ROPE) ** -0.5 * LOG2E
    zq = jnp.zeros((S, LANE - MLA_NOPE - MLA_ROPE), jnp.float32)
    cosq = jnp.concatenate([jnp.ones((S, MLA_NOPE), jnp.float32), cos2, zq], axis=-1) * scale
    sinq = jnp.concatenate([jnp.zeros((S, MLA_NOPE), jnp.float32), sin2, zq], axis=-1) * scale
    zk0 = jnp.zeros((S, MLA_NOPE), jnp.float32)
    cosk = jnp.concatenate([zk0, cos2, zq], axis=-1)
    sink = jnp.concatenate([zk0, sin2, zq], axis=-1)
    return cosq.T, sinq.T, cosk, sink


def _overlap_t(S):
    n_cmp = (S - CMP_LEN) // CMP_STRIDE + 1
    n_slc = S // SLC_LEN
    assert n_slc <= SLC_ROWS
    cs = np.arange(n_cmp) * CMP_STRIDE
    ss = np.arange(n_slc) * SLC_LEN
    ov = np.maximum(0, np.minimum(cs[:, None] + CMP_LEN, ss[None, :] + SLC_LEN)
                    - np.maximum(cs[:, None], ss[None, :])).astype(np.float32) / CMP_STRIDE
    out = np.zeros((SLC_ROWS, S // CMP_STRIDE), np.float32)
    out[:n_slc, :n_cmp] = ov.T
    return jnp.asarray(out, jnp.bfloat16)


def kernel(x, norm_mix_g, w_in, mla_q_norm_g, mla_w_uq, mla_kv_norm_g, mla_w_ukv, nsa_cmp_pos_k, nsa_cmp_w1_k, nsa_cmp_w2_k, nsa_cmp_pos_v, nsa_cmp_w1_v, nsa_cmp_w2_v, t5_table, out_norm_mla_g, out_norm_nsa_g, w_out, norm_ffn_g, w_gate, w_up, w_down, final_norm_g):
    B, S, D = x.shape
    assert w_in.shape[0] == 1
    assert D == D_MODEL and S % NSA_TQ == 0 and S % CMP_STRIDE == 0
    bf = jnp.bfloat16
    l = 0
    cosq_t, sinq_t, cosk, sink = _rope_tables(S)
    bias_c, tz = _bias_tables(t5_table, S)
    ovt = _overlap_t(S)
    row = lambda v: v.reshape(1, -1)

    w_row = _w_in_row_layout().apply(w_in[l])
    w_col = _w_in_col_layout().apply(w_in[l]).T
    w_q2 = _w_uq_layout().apply(mla_w_uq[l]).T
    w_k = _w_uk_layout().apply(mla_w_ukv[l])
    w_v = _w_uv_layout().apply(mla_w_ukv[l]).T
    (q_mla, k_mla, v_mla, q_nsa, kv_cmp, k_slc, v_slc, k_win, v_win, gates) = _proj(
        x, row(norm_mix_g[l]), w_row, w_col, row(mla_q_norm_g[l]), w_q2, row(mla_kv_norm_g[l]), w_k, w_v,
        cosq_t, sinq_t, cosk, sink)

    def per_offset(pos_l, w1_l):
        pos2 = jnp.tile(pos_l, (1, NSA_KV_HEADS)).reshape(CMP_LEN, 1, NSA_KV_HEADS * NSA_DK)
        w = w1_l.reshape(CMP_LEN, NSA_DK, CMP_HIDDEN)
        z = jnp.zeros_like(w)
        return pos2, jnp.concatenate([jnp.concatenate([w, z], axis=2), jnp.concatenate([z, w], axis=2)], axis=1)

    pos_k, w1_k = per_offset(nsa_cmp_pos_k[l], nsa_cmp_w1_k[l])
    pos_v, w1_v = per_offset(nsa_cmp_pos_v[l], nsa_cmp_w1_v[l])
    pos = jnp.stack([pos_k, pos_v])
    w1 = jnp.stack([w1_k, w1_v]).astype(bf)
    w2k =jnp.concatenate([nsa_cmp_w2_k[l], jnp.zeros((CMP_HIDDEN, LANE - NSA_DK), jnp.float32)],
                          axis=1).astype(bf)
    w2vt = nsa_cmp_w2_v[l].T.astype(bf)
    k_cmp, v_cmp = _compress(kv_cmp, pos, w1, w2k, w2vt)

    o_mla = _mla(q_mla, k_mla, v_mla)
    o_nsa = _nsa(q_nsa, k_slc, v_slc, k_win, v_win, k_cmp, v_cmp, gates, bias_c, tz, ovt)

    return _post(x, o_mla, o_nsa, row(out_norm_mla_g[l]), row(out_norm_nsa_g[l]), w_out[l].astype(bf),
                 row(norm_ffn_g[l]), w_gate[l].astype(bf), w_up[l].astype(bf), w_down[l].astype(bf),
                 row(final_norm_g))
```

```python
import functools
import math

import numpy as np
import jax
import jax.numpy as jnp
from jax import lax
from jax.experimental import pallas as pl
from jax.experimental.pallas import tpu as pltpu

D_MODEL = 1024
MLA_HEADS = 8
MLA_NOPE = 64
MLA_ROPE = 32
MLA_V = 64
MLA_Q_LORA = 256
MLA_KV_LORA = 128
ROPE_THETA = 10000.0
NSA_HEADS = 8
NSA_KV_HEADS = 2
NSA_GROUP = 4
NSA_DK = 64
CMP_LEN = 32
CMP_STRIDE = 16
CMP_HIDDEN = 128
SLC_LEN = 64
SLC_TOPN = 16
WINDOW = 512
T5_BUCKETS = 32
T5_MAX_DIST = 128
D_FF = 2816
EPS = 1e-6
NEG = -1e30
FORCE = 1e30

LANE = 128
SUBLANE = 8
SLC_ROWS = 64
GATE_ROWS = 16
ONES_ROWS = 16
LOG2E = math.log2(math.e)

PROJ_TM = 512
MLA_TK = 256
MLA_CW = 256
FLASH_LOOKAHEAD = 6
NSA_TQ = 256
NSA_TK = 256
NSA_FAR_GROUP = 8
POST_TM = 512
V7X_VMEM_BYTES = 64 * 1024 * 1024
VMEM_LIMIT = V7X_VMEM_BYTES * 7 // 8

C_CQ = 0
C_CKV = 256
C_MISC = 384
C_KVCMP = 512
C_KSLC = 768
C_KWIN = 1024
D_ROW = 1280
R_QNSA = 0
R_VSLC = 512
R_VWIN = 640
R_GATE = 768
D_COL = 800


def _dot(a, b):
    return jnp.dot(a, b, preferred_element_type=jnp.float32)


def _dot_nt(a, b):
    return lax.dot_general(a, b, (((1,), (1,)), ((), ())), preferred_element_type=jnp.float32)


def _rms(x, g):
    return x * lax.rsqrt(jnp.mean(x * x, axis=-1, keepdims=True) + EPS) * g


def _const_spec(shape):
    nd = len(shape)
    return pl.BlockSpec(shape, lambda *_: (0,) * nd, pipeline_mode=pl.Buffered(1))


def _params(n_axes):
    return pltpu.CompilerParams(dimension_semantics=("arbitrary",) * n_axes,
                                vmem_limit_bytes=VMEM_LIMIT)


def _in_offsets():
    o_krope = MLA_Q_LORA + MLA_KV_LORA
    o_qnsa = o_krope + MLA_ROPE
    o_cmp = o_qnsa + NSA_HEADS * NSA_DK
    o_slc = o_cmp + 2 * NSA_KV_HEADS * NSA_DK
    o_win = o_slc + 2 * NSA_KV_HEADS * NSA_DK
    o_gate = o_win + 2 * NSA_KV_HEADS * NSA_DK
    return o_krope, o_qnsa, o_cmp, o_slc, o_win, o_gate


class _Layout:
    def __init__(self, n):
        self.src = np.zeros((n,), np.int32)
        self.mul = np.zeros((n,), np.float32)

    def put(self, dst, s, n, m=1.0):
        self.src[dst:dst + n] = np.arange(s, s + n)
        self.mul[dst:dst + n] = m

    def apply(self, w):
        return (w[:, self.src] * self.mul[None, :]).astype(jnp.bfloat16)


def _w_in_row_layout():
    o_krope, _, o_cmp, o_slc, o_win, _ = _in_offsets()
    lay = _Layout(D_ROW)
    lay.put(C_CQ, 0, MLA_Q_LORA)
    lay.put(C_CKV, MLA_Q_LORA, MLA_KV_LORA)
    half = MLA_ROPE // 2
    lay.put(C_MISC + MLA_NOPE, o_krope, MLA_ROPE)
    lay.put(C_MISC + MLA_NOPE + MLA_ROPE, o_krope + half, half, -1.0)
    lay.put(C_MISC + MLA_NOPE + MLA_ROPE + half, o_krope, half, 1.0)
    lay.put(C_KVCMP, o_cmp, 2 * NSA_KV_HEADS * NSA_DK)
    for hk in range(NSA_KV_HEADS):
        lay.put(C_KSLC + hk * LANE, o_slc + hk * NSA_DK, NSA_DK)
        lay.put(C_KWIN + hk * LANE, o_win + hk * NSA_DK, NSA_DK)
    return lay


def _w_in_col_layout():
    _, o_qnsa, _, o_slc, o_win, o_gate = _in_offsets()
    lay = _Layout(D_COL)
    lay.put(R_QNSA, o_qnsa, NSA_HEADS * NSA_DK, NSA_DK ** -0.5)
    lay.put(R_VSLC, o_slc + NSA_KV_HEADS * NSA_DK, NSA_KV_HEADS * NSA_DK)
    lay.put(R_VWIN, o_win + NSA_KV_HEADS * NSA_DK, NSA_KV_HEADS * NSA_DK)
    for hk in range(NSA_KV_HEADS):
        lay.put(R_GATE + hk * GATE_ROWS, o_gate + hk * NSA_GROUP * 3, NSA_GROUP * 3)
    return lay


def _w_uq_layout():
    lay = _Layout(MLA_HEADS * LANE)
    dq = MLA_NOPE + MLA_ROPE
    for h in range(MLA_HEADS):
        lay.put(h * LANE, h * dq, dq)
    return lay


def _w_uk_layout():
    lay = _Layout(MLA_HEADS * LANE)
    per = MLA_NOPE + MLA_V
    for h in range(MLA_HEADS):
        lay.put(h * LANE, h * per, MLA_NOPE)
    return lay


def _w_uv_layout():
    lay = _Layout(MLA_HEADS * MLA_V)
    per = MLA_NOPE + MLA_V
    for h in range(MLA_HEADS):
        lay.put(h * MLA_V, h * per + MLA_NOPE, MLA_V)
    return lay


def _t5_thresholds(max_dist):
    n = np.arange(max_dist, dtype=np.int64)
    max_exact = T5_BUCKETS // 2
    nf = np.maximum(n, 1).astype(np.float32)
    val = (np.log(nf / np.float32(max_exact)) / np.float32(math.log(T5_MAX_DIST / max_exact))
           * np.float32(T5_BUCKETS - max_exact))
    large = np.minimum(max_exact + val.astype(np.int32), T5_BUCKETS - 1)
    bucket = np.where(n < max_exact, n, large)
    assert np.all(np.diff(bucket) >= 0)
    frac = np.abs(val[max_exact + 1:T5_MAX_DIST] - np.round(val[max_exact + 1:T5_MAX_DIST]))
    assert frac.min() > 1e-3
    return [int(np.argmax(bucket >= b)) for b in range(T5_BUCKETS)]


def _proj_kernel(x_ref, gmix_ref, wrow_ref, wcol_ref, gq_ref, wq_ref, gkv_ref, wk_ref, wv_ref,
                 cq_ref, sq_ref, ck_ref, sk_ref,
                 qm_ref, km_ref, vm_ref, qn_ref, kvc_ref, ks_ref, vs_ref, kw_ref, vw_ref, gate_ref):
    tm = x_ref.shape[1]
    bf = jnp.bfloat16
    xn = _rms(x_ref[0], gmix_ref[...]).astype(bf)
    u = _dot(xn, wrow_ref[...])
    ut = _dot_nt(wcol_ref[...], xn)

    cqn = _rms(u[:, C_CQ:C_CQ + MLA_Q_LORA], gq_ref[...]).astype(bf)
    qq = _dot_nt(wq_ref[...], cqn)
    cosq = cq_ref[...]
    sinq = sq_ref[...]
    r0, r1, r2 = MLA_NOPE, MLA_NOPE + MLA_ROPE // 2, MLA_NOPE + MLA_ROPE
    for h in range(MLA_HEADS):
        q = qq[h * LANE:(h + 1) * LANE]
        rot = jnp.concatenate([q[0:r0], -q[r1:r2], q[r0:r1], q[r2:]], axis=0)
        qm_ref[0, h] = (q * cosq + rot * sinq).astype(bf)

    ckvn = _rms(u[:, C_CKV:C_CKV + MLA_KV_LORA], gkv_ref[...]).astype(bf)
    kk = _dot(ckvn, wk_ref[...])
    misc = u[:, C_MISC:C_MISC + LANE]
    kpe = misc * ck_ref[...] + pltpu.roll(misc, LANE - MLA_ROPE, axis=1) * sk_ref[...]
    for h in range(MLA_HEADS):
        km_ref[0, h] = (kk[:, h * LANE:(h + 1) * LANE] + kpe).astype(bf)
    vv = _dot_nt(wv_ref[...], ckvn)
    for h in range(MLA_HEADS):
        vm_ref[0, h] = vv[h * MLA_V:(h + 1) * MLA_V].astype(bf)

    for h in range(NSA_HEADS):
        qn_ref[0, h] = (ut[R_QNSA + h * NSA_DK:R_QNSA + (h + 1) * NSA_DK] * LOG2E).astype(bf)
    for kv in range(2):
        kvc_ref[0, kv] = u[:, C_KVCMP + kv * LANE:C_KVCMP + (kv + 1) * LANE]
    pos = pl.program_id(1) * tm + lax.broadcasted_iota(jnp.int32, (tm, LANE), 0)
    lane = lax.broadcasted_iota(jnp.int32, (tm, LANE), 1)
    onehot = jnp.where(lane - (LANE - SLC_ROWS) == pos // SLC_LEN, 1.0, 0.0)
    for hk in range(NSA_KV_HEADS):
        ks_ref[0, hk] = (u[:, C_KSLC + hk * LANE:C_KSLC + (hk + 1) * LANE] + onehot).astype(bf)
        kw_ref[0, hk] = u[:, C_KWIN + hk * LANE:C_KWIN + (hk + 1) * LANE].astype(bf)
        vs_ref[0, hk] = ut[R_VSLC + hk * NSA_DK:R_VSLC + (hk + 1) * NSA_DK].astype(bf)
        vw_ref[0, hk] = ut[R_VWIN + hk * NSA_DK:R_VWIN + (hk + 1) * NSA_DK].astype(bf)
        gate_ref[0, hk] = jax.nn.sigmoid(ut[R_GATE + hk * GATE_ROWS:R_GATE + (hk + 1) * GATE_ROWS])


def _proj(x, gmix, w_row, w_col, gq, w_q2, gkv, w_k, w_v, cosq_t, sinq_t, cosk, sink):
    B, S, D = x.shape
    tm = min(PROJ_TM, S)
    bf = jnp.bfloat16
    grid = (B, S // tm)
    tok_major = lambda n: pl.BlockSpec((1, n, tm, LANE), lambda b, i: (b, 0, i, 0))
    feat_major = lambda n, d: pl.BlockSpec((1, n, d, tm), lambda b, i: (b, 0, 0, i))
    out_shape = (
        jax.ShapeDtypeStruct((B, MLA_HEADS, LANE, S), bf),
        jax.ShapeDtypeStruct((B, MLA_HEADS, S, LANE), bf),
        jax.ShapeDtypeStruct((B, MLA_HEADS, MLA_V, S), bf),
        jax.ShapeDtypeStruct((B, NSA_HEADS, NSA_DK, S), bf),
        jax.ShapeDtypeStruct((B, 2, S, LANE), jnp.float32),
        jax.ShapeDtypeStruct((B, NSA_KV_HEADS, S, LANE), bf),
        jax.ShapeDtypeStruct((B, NSA_KV_HEADS, NSA_DK, S), bf),
        jax.ShapeDtypeStruct((B, NSA_KV_HEADS, S, LANE), bf),
        jax.ShapeDtypeStruct((B, NSA_KV_HEADS, NSA_DK, S), bf),
        jax.ShapeDtypeStruct((B, NSA_KV_HEADS, GATE_ROWS, S), jnp.float32),
    )
    out_specs = (
        feat_major(MLA_HEADS, LANE), tok_major(MLA_HEADS), feat_major(MLA_HEADS, MLA_V),
        feat_major(NSA_HEADS, NSA_DK),
        tok_major(2),
        tok_major(NSA_KV_HEADS), feat_major(NSA_KV_HEADS, NSA_DK),
        tok_major(NSA_KV_HEADS), feat_major(NSA_KV_HEADS, NSA_DK),
        feat_major(NSA_KV_HEADS, GATE_ROWS),
    )
    consts = (gmix, w_row, w_col, gq, w_q2, gkv, w_k, w_v)
    in_specs = ([pl.BlockSpec((1, tm, D), lambda b, i: (b, i, 0))] + [_const_spec(c.shape) for c in consts]
                + [pl.BlockSpec((LANE, tm), lambda b, i: (0, i)), pl.BlockSpec((LANE, tm), lambda b, i: (0, i)),
                   pl.BlockSpec((tm, LANE), lambda b, i: (i, 0)), pl.BlockSpec((tm, LANE), lambda b, i: (i, 0))])
    return pl.pallas_call(
        _proj_kernel, grid=grid, in_specs=in_specs, out_specs=out_specs, out_shape=out_shape,
        compiler_params=_params(2), name="proj",
    )(x, *consts, cosq_t, sinq_t, cosk, sink)


def _compress_kernel(c_ref, pos_ref, w1_ref, w2k_ref, w2vt_ref, kc_ref, vc_ref):
    nc = kc_ref.shape[2]
    bf = jnp.bfloat16
    hw = NSA_KV_HEADS * NSA_DK
    for kv in range(2):
        halves = []
        for half in range(CMP_LEN // CMP_STRIDE):
            acc = None
            for r in range(CMP_STRIDE):
                l = half * CMP_STRIDE + r
                x = c_ref[0, kv, pl.ds(r, nc, stride=CMP_STRIDE), :]
                part = _dot((x + pos_ref[kv, l]).astype(bf), w1_ref[kv, l])
                acc = part if acc is None else acc + part
            halves.append(acc)
        hid = halves[0] + pltpu.roll(halves[1], nc - 1, axis=0)
        act = jax.nn.gelu(hid).astype(bf)
        for hk in range(NSA_KV_HEADS):
            act_h = act[:, hk * CMP_HIDDEN:(hk + 1) * CMP_HIDDEN]
            if kv == 0:
                kc_ref[0, hk] = _dot(act_h, w2k_ref[...]).astype(bf)
            else:
                vc_ref[0, hk] = _dot_nt(w2vt_ref[...], act_h).astype(bf)


def _compress(kvc, pos, w1, w2k, w2vt):
    B, _, S, width = kvc.shape
    nc = S // CMP_STRIDE
    return pl.pallas_call(
        _compress_kernel, grid=(B,),
        in_specs=[pl.BlockSpec((1, 2, S, width), lambda b: (b, 0, 0, 0)),
                  _const_spec(pos.shape), _const_spec(w1.shape), _const_spec(w2k.shape),
                  _const_spec(w2vt.shape)],
        out_specs=(pl.BlockSpec((1, NSA_KV_HEADS, nc, LANE), lambda b: (b, 0, 0, 0)),
                   pl.BlockSpec((1, NSA_KV_HEADS, NSA_DK, nc), lambda b: (b, 0, 0, 0))),
        out_shape=(jax.ShapeDtypeStruct((B, NSA_KV_HEADS, nc, LANE), jnp.bfloat16),
                   jax.ShapeDtypeStruct((B, NSA_KV_HEADS, NSA_DK, nc), jnp.bfloat16)),
        compiler_params=_params(1), name="compress",
    )(kvc, pos, w1, w2k, w2vt)


def _t5_lookup(dist, tab_ref, head, thr):
    val = jnp.full(dist.shape, tab_ref[0, head], jnp.float32)
    for b in range(1, T5_BUCKETS):
        val = jnp.where(dist >= thr[b], tab_ref[b, head], val)
    return (val - tab_ref[T5_BUCKETS - 1, head]) * LOG2E


def _bias_cmp_kernel(tab_ref, out_ref, *, thr, tq, n_cmp):
    hk = pl.program_id(0)
    t = pl.program_id(1)
    nc = out_ref.shape[2]
    per_tile = tq // CMP_STRIDE
    band = per_tile + (T5_MAX_DIST + CMP_LEN) // CMP_STRIDE + SUBLANE
    band = min(-(-band // SUBLANE) * SUBLANE, nc)
    start = jnp.clip(t * per_tile - (band - per_tile), 0, nc - band)
    start = pl.multiple_of(start // SUBLANE * SUBLANE, SUBLANE)
    n_all = lax.broadcasted_iota(jnp.int32, (nc, NSA_GROUP * tq), 0)
    out_ref[0, 0] = jnp.where(n_all < start, 0.0, NEG)
    n = start + lax.broadcasted_iota(jnp.int32, (band, tq), 0)
    i = lax.broadcasted_iota(jnp.int32, (band, tq), 1)
    dist = t * tq + i - (n * CMP_STRIDE + CMP_LEN - 1)
    ok = jnp.logical_and(dist >= 0, n < n_cmp)
    for g in range(NSA_GROUP):
        val = _t5_lookup(dist, tab_ref, hk * NSA_GROUP + g, thr)
        out_ref[0, 0, pl.ds(start, band), g * tq:(g + 1) * tq] = jnp.where(ok, val, NEG)


def _bias_tile_kernel(tab_ref, out_ref, *, thr, tq):
    hk = pl.program_id(0)
    tk = out_ref.shape[2]
    j = lax.broadcasted_iota(jnp.int32, (tk, tq), 0)
    i = lax.broadcasted_iota(jnp.int32, (tk, tq), 1)
    for g in range(NSA_GROUP):
        head = hk * NSA_GROUP + g
        cols = slice(g * tq, (g + 1) * tq)
        d0 = i - j
        out_ref[0, 0, :, cols] = jnp.where(d0 >= 0, _t5_lookup(d0, tab_ref, head, thr), NEG)
        out_ref[0, 1, :, cols] = _t5_lookup(d0 + tk, tab_ref, head, thr)
        out_ref[0, 2, :, cols] = jnp.where(j > i, 0.0, NEG)
        out_ref[0, 3, :, cols] = jnp.full((tk, tq), NEG, jnp.float32)


def _bias_tables(t5_table, S):
    thr = _t5_thresholds(S)
    tq, tk = NSA_TQ, NSA_TK
    assert tq == tk and WINDOW == 2 * tk and tk >= T5_MAX_DIST
    nq = S // tq
    nc = S // CMP_STRIDE
    n_cmp = (S - CMP_LEN) // CMP_STRIDE + 1
    m = NSA_GROUP * tq
    smem = pl.BlockSpec(memory_space=pltpu.SMEM)
    bias_c = pl.pallas_call(
        functools.partial(_bias_cmp_kernel, thr=thr, tq=tq, n_cmp=n_cmp),
        grid=(NSA_KV_HEADS, nq), in_specs=[smem],
        out_specs=pl.BlockSpec((1, 1, nc, m), lambda h, t: (h, t, 0, 0)),
        out_shape=jax.ShapeDtypeStruct((NSA_KV_HEADS, nq, nc, m), jnp.float32),
        compiler_params=_params(2), name="bias_cmp",
    )(t5_table)
    tz = pl.pallas_call(
        functools.partial(_bias_tile_kernel, thr=thr, tq=tq),
        grid=(NSA_KV_HEADS,), in_specs=[smem],
        out_specs=pl.BlockSpec((1, 4, tk, m), lambda h: (h, 0, 0, 0)),
        out_shape=jax.ShapeDtypeStruct((NSA_KV_HEADS, 4, tk, m), jnp.float32),
        compiler_params=_params(1), name="bias_tile",
    )(t5_table)
    return bias_c, tz


def _pipeline(stages):
    pending = [issue() for issue, _ in stages[:FLASH_LOOKAHEAD]]
    for i, (_, consume) in enumerate(stages):
        s = pending.pop(0)
        if i + FLASH_LOOKAHEAD < len(stages):
            pending.append(stages[i + FLASH_LOOKAHEAD][0]())
        consume(s)


def _flash_stage(chain):
    k, q_t, v_aug, bias_t, m_ref, acc_ref = chain

    def issue():
        s = _dot(k, q_t)
        return s if bias_t is None else s + bias_t

    def consume(s):
        m_prev = m_ref[...]
        m_new = jnp.maximum(m_prev, jnp.max(s, axis=0, keepdims=True))
        alpha = jnp.exp2(m_prev - m_new)
        p = jnp.exp2(s - m_new).astype(jnp.bfloat16)
        acc_ref[...] = alpha * acc_ref[...] + _dot(v_aug, p)
        m_ref[...] = m_new

    return issue, consume


def _flash_chains(chains):
    _pipeline([_flash_stage(c) for c in chains])


def _with_ones(v_t):
    tk = v_t.shape[1]
    row = lax.broadcasted_iota(jnp.int32, (ONES_ROWS, tk), 0)
    return jnp.concatenate([v_t, jnp.where(row == 0, 1.0, 0.0).astype(v_t.dtype)], axis=0)


def _col_chains(k, q_ref, v_aug, bias_ref, m_ref, acc_ref, ncol, width):
    chains = []
    for c0 in range(0, ncol, width):
        cols = slice(c0, c0 + width)
        chains.append((k, q_ref[:, cols], v_aug, None if bias_ref is None else bias_ref[:, cols],
                       m_ref.at[:, cols], acc_ref.at[:, cols]))
    return chains


def _flash_init(m_ref, acc_ref):
    m_ref[...] = jnp.full(m_ref.shape, -jnp.inf, jnp.float32)
    acc_ref[...] = jnp.zeros(acc_ref.shape, jnp.float32)


def _flash_result(acc_ref, dv):
    return acc_ref[0:dv, :] * (1.0 / acc_ref[dv:dv + 1, :])


def _mla_kernel(q_ref, k_ref, v_ref, o_ref, m_ref, acc_ref, *, tk, cw):
    S = k_ref.shape[2]
    kk = lax.broadcasted_iota(jnp.int32, (tk, cw), 0)
    qq = lax.broadcasted_iota(jnp.int32, (tk, cw), 1)
    for e in range(2):
        _flash_init(m_ref.at[e], acc_ref.at[e])
    chains = []
    for k0 in range(0, S, tk):
        for e in range(2):
            k = k_ref[0, e, k0:k0 + tk, :]
            v_aug = _with_ones(v_ref[0, e, :, k0:k0 + tk])
            for c0 in range(0, S, cw):
                if k0 > c0 + cw - 1:
                    continue
                bias = jnp.where(kk + k0 <= qq + c0, 0.0, NEG) if k0 + tk - 1 > c0 else None
                cols = slice(c0, c0 + cw)
                chains.append((k, q_ref[0, e, :, cols], v_aug, bias,
                               m_ref.at[e, :, cols], acc_ref.at[e, :, cols]))
    _flash_chains(chains)
    o_t = jnp.concatenate([_flash_result(acc_ref.at[e], MLA_V) for e in range(2)], axis=0)
    o_ref[0] = jnp.transpose(o_t)


def _mla(q_t, k, v_t):
    B, H, S, _ = k.shape
    tq = S
    tk = min(MLA_TK, S)
    cw = min(MLA_CW, S)
    return pl.pallas_call(
        functools.partial(_mla_kernel, tk=tk, cw=cw),
        grid=(B, H // 2),
        in_specs=[pl.BlockSpec((1, 2, LANE, S), lambda b, p: (b, p, 0, 0)),
                  pl.BlockSpec((1, 2, S, LANE), lambda b, p: (b, p, 0, 0)),
                  pl.BlockSpec((1, 2, MLA_V, S), lambda b, p: (b, p, 0, 0))],
        out_specs=pl.BlockSpec((1, S, 2 * MLA_V), lambda b, p: (b, 0, p)),
        out_shape=jax.ShapeDtypeStruct((B, S, H * MLA_V), jnp.float32),
        scratch_shapes=[pltpu.VMEM((2, 1, tq), jnp.float32),
                        pltpu.VMEM((2, MLA_V + ONES_ROWS, tq), jnp.float32)],
        compiler_params=_params(2), name="mla",
    )(q_t, k, v_t)


def _nsa_kernel(q_ref, ks_ref, vs_ref, kw_ref, vw_ref, kc_ref, vc_ref, gate_ref, bc_ref, tz_ref, ovt_ref,
                o_ref, qaug_ref, qpad_ref, m_ref, acc_ref, mw_ref, accw_ref, oc_ref, imp_ref, *, tq, tk):
    t = pl.program_id(1)
    G = NSA_GROUP
    HK = range(NSA_KV_HEADS)
    ncol = G * tq
    bf = jnp.bfloat16
    q0 = t * tq
    for hk in HK:
        for g in range(G):
            qpad_ref[hk, 0:NSA_DK, g * tq:(g + 1) * tq] = q_ref[0, hk * G + g]
            qaug_ref[hk, 0:NSA_DK, g * tq:(g + 1) * tq] = q_ref[0, hk * G + g]
        qpad_ref[hk, NSA_DK:, :] = jnp.zeros((SLC_ROWS, ncol), bf)

    def tile_chains(k_ref, v_ref, qx_ref, c, bias_idx, stats):
        k0 = pl.multiple_of(c * tk, tk)
        chains = []
        for hk in HK:
            bias_ref = None if bias_idx is None else tz_ref.at[hk, bias_idx]
            chains += _col_chains(k_ref[0, hk, pl.ds(k0, tk), :], qx_ref.at[hk],
                                  _with_ones(v_ref[0, hk, :, pl.ds(k0, tk)]),
                                  bias_ref, stats[0].at[hk], stats[1].at[hk], ncol, tq)
        return chains

    prev1 = jnp.maximum(t - 1, 0)
    bias1 = jnp.where(t >= 1, 1, 3)

    nb = ovt_ref.shape[0]
    sub = SUBLANE
    col_ok = q0 + lax.broadcasted_iota(jnp.int32, (1, tq), 1) >= CMP_LEN - 1
    jb = lax.broadcasted_iota(jnp.int32, (nb, tq), 0)
    cur = (q0 + lax.broadcasted_iota(jnp.int32, (nb, tq), 1)) // SLC_LEN
    forced = jnp.logical_or(jb == 0, jnp.logical_or(jb == cur, jb == cur - 1))
    jsub = lax.broadcasted_iota(jnp.int32, (sub, tq), 0)
    imp_sum = [None] * NSA_KV_HEADS

    def cmp_stage(hk, g):
        cols = slice(g * tq, (g + 1) * tq)

        def issue():
            return _dot(kc_ref[0, hk], qpad_ref[hk, :, cols]) + bc_ref[hk, 0, :, cols]

        def consume(s):
            mx = jnp.max(s, axis=0, keepdims=True)
            p = jnp.exp2(s - mx)
            lsum = jnp.sum(p, axis=0, keepdims=True)
            pcb = (p * jnp.where(col_ok, 1.0 / lsum, 0.0)).astype(bf)
            both = _dot(jnp.concatenate([vc_ref[0, hk], ovt_ref[...]], axis=0), pcb)
            oc_ref[hk, :, cols] = both[0:NSA_DK]
            imp_sum[hk] = both[NSA_DK:] if imp_sum[hk] is None else imp_sum[hk] + both[NSA_DK:]

        return issue, consume

    win = (mw_ref, accw_ref)
    _flash_init(*win)
    win_stages = [_flash_stage(c) for c in
                  tile_chains(kw_ref, vw_ref, qpad_ref, jnp.maximum(t - 2, 0), jnp.where(t >= 2, 2, 3), win)
                  + tile_chains(kw_ref, vw_ref, qpad_ref, prev1, bias1, win)
                  + tile_chains(kw_ref, vw_ref, qpad_ref, t, 0, win)]
    cmp_stages = [cmp_stage(hk, g) for g in range(G) for hk in HK]
    every = len(win_stages) // len(cmp_stages)
    stages = []
    for i, st in enumerate(cmp_stages):
        stages += [st] + win_stages[i * every:(i + 1) * every]
    _pipeline(stages + win_stages[len(cmp_stages) * every:])
    for hk in HK:
        imp_ref[hk] = jnp.where(forced, FORCE, jnp.where(jb <= cur, imp_sum[hk], NEG))

    tiles_per_group = 16 * SLC_LEN // tq
    n_groups = nb // 16

    def write_mask(hk, selb_rows):
        selb = jnp.concatenate(selb_rows, axis=0).astype(bf)
        for g in range(G):
            qaug_ref[hk, NSA_DK:, g * tq:(g + 1) * tq] = selb

    def select(n_act):
        tail = [jnp.full((nb - n_act, tq), NEG, jnp.float32)] if n_act < nb else []
        for hk in HK:
            imp = imp_ref[hk, 0:n_act, :]
            if n_act <= SLC_TOPN:
                write_mask(hk, [jnp.where(imp > 0.5 * NEG, 0.0, NEG)] + tail)
                continue
            slabs = [imp[r0 * sub:(r0 + 1) * sub] for r0 in range(n_act // sub)]
            ranks = [jnp.zeros((sub, tq), jnp.int32) for _ in slabs]
            for jp in range(n_act):
                rowv = imp[jp:jp + 1, :]
                for r0, slab in enumerate(slabs):
                    lo = r0 * sub
                    if lo > jp:
                        one = jnp.where(rowv >= slab, 1, 0)
                    elif lo + sub - 1 <= jp:
                        one = jnp.where(rowv > slab, 1, 0)
                    else:
                        one = jnp.where(jsub + lo > jp, jnp.where(rowv >= slab, 1, 0),
                                        jnp.where(rowv > slab, 1, 0))
                    ranks[r0] = ranks[r0] + one
            write_mask(hk, [jnp.where(r < SLC_TOPN, 0.0, NEG) for r in ranks] + tail)

    for grp in range(n_groups):
        pl.when(t // tiles_per_group == grp)(functools.partial(select, 16 * (grp + 1)))

    slc = (m_ref, acc_ref)
    _flash_init(*slc)
    n_far = jnp.maximum(t - 1, 0)

    def far(c, carry):
        chains = []
        for j in range(NSA_FAR_GROUP):
            chains += tile_chains(ks_ref, vs_ref, qaug_ref, c * NSA_FAR_GROUP + j, None, slc)
        _flash_chains(chains)
        return carry

    lax.fori_loop(0, n_far // NSA_FAR_GROUP, far, 0)
    rest0 = n_far // NSA_FAR_GROUP * NSA_FAR_GROUP

    def tail(n_rest):
        chains = []
        for j in range(n_rest):
            chains += tile_chains(ks_ref, vs_ref, qaug_ref, rest0 + j, None, slc)
        _flash_chains(chains + tile_chains(ks_ref, vs_ref, qaug_ref, prev1, bias1, slc)
                      + tile_chains(ks_ref, vs_ref, qaug_ref, t, 0, slc))

    for n_rest in range(NSA_FAR_GROUP):
        pl.when(n_far - rest0 == n_rest)(functools.partial(tail, n_rest))

    for hk in HK:
        gate = gate_ref[0, hk]
        comb = []
        for g in range(G):
            cols = slice(g * tq, (g + 1) * tq)
            gr = lambda br: gate[g * 3 + br:g * 3 + br + 1, :]
            w_s = gr(1) / acc_ref[hk, NSA_DK:NSA_DK + 1, cols]
            w_w = gr(2) / accw_ref[hk, NSA_DK:NSA_DK + 1, cols]
            comb.append(gr(0) * oc_ref[hk, :, cols] + w_s * acc_ref[hk, 0:NSA_DK, cols]
                        + w_w * accw_ref[hk, 0:NSA_DK, cols])
        width = G * NSA_DK
        o_ref[0, :, hk * width:(hk + 1) * width] = jnp.transpose(jnp.concatenate(comb, axis=0))


def _nsa(q_t, ks, vs_t, kw, vw_t, kc, vc_t, gate_t, bias_c, tz, ovt):
    B, _, S, _ = ks.shape
    tq, tk = NSA_TQ, NSA_TK
    nq = S // tq
    nc = kc.shape[2]
    m = NSA_GROUP * tq
    hkv = NSA_KV_HEADS
    tok = lambda: pl.BlockSpec((1, hkv, S, LANE), lambda b, t: (b, 0, 0, 0))
    feat = lambda: pl.BlockSpec((1, hkv, NSA_DK, S), lambda b, t: (b, 0, 0, 0))
    return pl.pallas_call(
        functools.partial(_nsa_kernel, tq=tq, tk=tk),
        grid=(B, nq),
        in_specs=[pl.BlockSpec((1, NSA_HEADS, NSA_DK, tq), lambda b, t: (b, 0, 0, t)),
                  tok(), feat(), tok(), feat(),
                  pl.BlockSpec((1, hkv, nc, LANE), lambda b, t: (b, 0, 0, 0)),
                  pl.BlockSpec((1, hkv, NSA_DK, nc), lambda b, t: (b, 0, 0, 0)),
                  pl.BlockSpec((1, hkv, GATE_ROWS, tq), lambda b, t: (b, 0, 0, t)),
                  pl.BlockSpec((hkv, 1, nc, m), lambda b, t: (0, t, 0, 0)),
                  _const_spec(tz.shape),
                  _const_spec(ovt.shape)],
        out_specs=pl.BlockSpec((1, tq, NSA_HEADS * NSA_DK), lambda b, t: (b, t, 0)),
        out_shape=jax.ShapeDtypeStruct((B, S, NSA_HEADS * NSA_DK), jnp.float32),
        scratch_shapes=[pltpu.VMEM((hkv, NSA_DK + SLC_ROWS, m), jnp.bfloat16),
                        pltpu.VMEM((hkv, NSA_DK + SLC_ROWS, m), jnp.bfloat16),
                        pltpu.VMEM((hkv, 1, m), jnp.float32),
                        pltpu.VMEM((hkv, NSA_DK + ONES_ROWS, m), jnp.float32),
                        pltpu.VMEM((hkv, 1, m), jnp.float32),
                        pltpu.VMEM((hkv, NSA_DK + ONES_ROWS, m), jnp.float32),
                        pltpu.VMEM((hkv, NSA_DK, m), jnp.float32),
                        pltpu.VMEM((hkv, SLC_ROWS, tq), jnp.float32)],
        compiler_params=_params(2), name="nsa",
    )(q_t, ks, vs_t, kw, vw_t, kc, vc_t, gate_t, bias_c, tz, ovt)


def _post_kernel(x_ref, om_ref, on_ref, gm_ref, gn_ref, wo_ref, gf_ref, wg_ref, wu_ref, wd_ref,
                 gfin_ref, o_ref):
    half = om_ref.shape[2]
    mix_m = _rms(om_ref[0], gm_ref[...]).astype(jnp.bfloat16)
    mix_n = _rms(on_ref[0], gn_ref[...]).astype(jnp.bfloat16)
    h = x_ref[0] + _dot(mix_m, wo_ref[0:half, :]) + _dot(mix_n, wo_ref[half:2 * half, :])
    f = _rms(h, gf_ref[...]).astype(jnp.bfloat16)
    a = _dot(f, wg_ref[...])
    act = (a * jax.nn.sigmoid(a) * _dot(f, wu_ref[...])).astype(jnp.bfloat16)
    h = h + _dot(act, wd_ref[...])
    o_ref[0] = _rms(h, gfin_ref[...])


def _post(x, o_mla, o_nsa, gm, gn, w_out, gf, wg, wu, wd, gfin):
    B, S, D = x.shape
    tm = min(POST_TM, S)
    tok = lambda w: pl.BlockSpec((1, tm, w), lambda b, i: (b, i, 0))
    consts = (gm, gn, w_out, gf, wg, wu, wd, gfin)
    return pl.pallas_call(
        _post_kernel, grid=(B, S // tm),
        in_specs=[tok(D), tok(o_mla.shape[2]), tok(o_nsa.shape[2])] + [_const_spec(c.shape) for c in consts],
        out_specs=tok(D), out_shape=jax.ShapeDtypeStruct((B, S, D), jnp.float32),
        compiler_params=_params(2), name="post",
    )(x, o_mla, o_nsa, *consts)


def _rope_tables(S):
    pos = jnp.arange(S, dtype=jnp.float32)
    inv = ROPE_THETA ** (-jnp.arange(0, MLA_ROPE, 2, dtype=jnp.float32) / MLA_ROPE)
    ang = pos[:, None] * inv[None, :]
    cos, sin = jnp.cos(ang), jnp.sin(ang)
    cos2 = jnp.concatenate([cos, cos], axis=-1)
    sin2 = jnp.concatenate([sin, sin], axis=-1)
    scale = (MLA_NOPE + MLA_ROPE) ** -0.5 * LOG2E
    zq = jnp.zeros((S, LANE - MLA_NOPE - MLA_ROPE), jnp.float32)
    cosq = jnp.concatenate([jnp.ones((S, MLA_NOPE), jnp.float32), cos2, zq], axis=-1) * scale
    sinq = jnp.concatenate([jnp.zeros((S, MLA_NOPE), jnp.float32), sin2, zq], axis=-1) * scale
    zk0 = jnp.zeros((S, MLA_NOPE), jnp.float32)
    cosk = jnp.concatenate([zk0, cos2, zq], axis=-1)
    sink = jnp.concatenate([zk0, sin2, zq], axis=-1)
    return cosq.T, sinq.T, cosk, sink


def _overlap_t(S):
    n_cmp = (S - CMP_LEN) // CMP_STRIDE + 1
    n_slc = S // SLC_LEN
    assert n_slc <= SLC_ROWS
    cs = np.arange(n_cmp) * CMP_STRIDE
    ss = np.arange(n_slc) * SLC_LEN
    ov = np.maximum(0, np.minimum(cs[:, None] + CMP_LEN, ss[None, :] + SLC_LEN)
                    - np.maximum(cs[:, None], ss[None, :])).astype(np.float32) / CMP_STRIDE
    out = np.zeros((SLC_ROWS, S // CMP_STRIDE), np.float32)
    out[:n_slc, :n_cmp] = ov.T
    return jnp.asarray(out, jnp.bfloat16)


def kernel(x, norm_mix_g, w_in, mla_q_norm_g, mla_w_uq, mla_kv_norm_g, mla_w_ukv, nsa_cmp_pos_k, nsa_cmp_w1_k, nsa_cmp_w2_k, nsa_cmp_pos_v, nsa_cmp_w1_v, nsa_cmp_w2_v, t5_table, out_norm_mla_g, out_norm_nsa_g, w_out, norm_ffn_g, w_gate, w_up, w_down, final_norm_g):
    B, S, D = x.shape
    assert w_in.shape[0] == 1
    assert D == D_MODEL and S % NSA_TQ == 0 and S % CMP_STRIDE == 0
    bf = jnp.bfloat16
    l = 0
    cosq_t, sinq_t, cosk, sink = _rope_tables(S)
    bias_c, tz = _bias_tables(t5_table, S)
    ovt = _overlap_t(S)
    row = lambda v: v.reshape(1, -1)

    w_row = _w_in_row_layout().apply(w_in[l])
    w_col = _w_in_col_layout().apply(w_in[l]).T
    w_q2 = _w_uq_layout().apply(mla_w_uq[l]).T
    w_k = _w_uk_layout().apply(mla_w_ukv[l])
    w_v = _w_uv_layout().apply(mla_w_ukv[l]).T
    (q_mla, k_mla, v_mla, q_nsa, kv_cmp, k_slc, v_slc, k_win, v_win, gates) = _proj(
        x, row(norm_mix_g[l]), w_row, w_col, row(mla_q_norm_g[l]), w_q2, row(mla_kv_norm_g[l]), w_k, w_v,
        cosq_t, sinq_t, cosk, sink)

    def per_offset(pos_l, w1_l):
        pos2 = jnp.tile(pos_l, (1, NSA_KV_HEADS)).reshape(CMP_LEN, 1, NSA_KV_HEADS * NSA_DK)
        w = w1_l.reshape(CMP_LEN, NSA_DK, CMP_HIDDEN)
        z = jnp.zeros_like(w)
        return pos2, jnp.concatenate([jnp.concatenate([w, z], axis=2), jnp.concatenate([z, w], axis=2)], axis=1)

    pos_k, w1_k = per_offset(nsa_cmp_pos_k[l], nsa_cmp_w1_k[l])
    pos_v, w1_v = per_offset(nsa_cmp_pos_v[l], nsa_cmp_w1_v[l])
    pos = jnp.stack([pos_k, pos_v])
    w1 = jnp.stack([w1_k, w1_v]).astype(bf)
    w2k =jnp.concatenate([nsa_cmp_w2_k[l], jnp.zeros((CMP_HIDDEN, LANE - NSA_DK), jnp.float32)],
                          axis=1).astype(bf)
    w2vt = nsa_cmp_w2_v[l].T.astype(bf)
    k_cmp, v_cmp = _compress(kv_cmp, pos, w1, w2k, w2vt)

    o_mla = _mla(q_mla, k_mla, v_mla)
    o_nsa = _nsa(q_nsa, k_slc, v_slc, k_win, v_win, k_cmp, v_cmp, gates, bias_c, tz, ovt)

    return _post(x, o_mla, o_nsa, row(out_norm_mla_g[l]), row(out_norm_nsa_g[l]), w_out[l].astype(bf),
                 row(norm_ffn_g[l]), w_gate[l].astype(bf), w_up[l].astype(bf), w_down[l].astype(bf),
                 row(final_norm_g))
```

```python
import functools
import math

import numpy as np
import jax
import jax.numpy as jnp
from jax import lax
from jax.experimental import pallas as pl
from jax.experimental.pallas import tpu as pltpu

D_MODEL = 1024
MLA_HEADS = 8
MLA_NOPE = 64
MLA_ROPE = 32
MLA_V = 64
MLA_Q_LORA = 256
MLA_KV_LORA = 128
ROPE_THETA = 10000.0
NSA_HEADS = 8
NSA_KV_HEADS = 2
NSA_GROUP = 4
NSA_DK = 64
CMP_LEN = 32
CMP_STRIDE = 16
CMP_HIDDEN = 128
SLC_LEN = 64
SLC_TOPN = 16
WINDOW = 512
T5_BUCKETS = 32
T5_MAX_DIST = 128
D_FF = 2816
EPS = 1e-6
NEG = -1e30
FORCE = 1e30

LANE = 128
SUBLANE = 8
SLC_ROWS = 64
GATE_ROWS = 16
ONES_ROWS = 16
LOG2E = math.log2(math.e)

PROJ_TM = 512
MLA_TK = 256
MLA_CW = 256
FLASH_LOOKAHEAD = 6
NSA_TQ = 256
NSA_TK = 256
SEL_GROUP = 8
NSA_FAR_GROUP = 8
POST_TM = 512
V7X_VMEM_BYTES = 64 * 1024 * 1024
VMEM_LIMIT = V7X_VMEM_BYTES * 7 // 8

C_CQ = 0
C_CKV = 256
C_MISC = 384
C_KVCMP = 512
C_KSLC = 768
C_KWIN = 1024
D_ROW = 1280
R_QNSA = 0
R_VSLC = 512
R_VWIN = 640
R_GATE = 768
D_COL = 800


def _dot(a, b):
    return jnp.dot(a, b, preferred_element_type=jnp.float32)


def _dot_nt(a, b):
    return lax.dot_general(a, b, (((1,), (1,)), ((), ())), preferred_element_type=jnp.float32)


def _rms(x, g):
    return x * lax.rsqrt(jnp.mean(x * x, axis=-1, keepdims=True) + EPS) * g


def _const_spec(shape):
    nd = len(shape)
    return pl.BlockSpec(shape, lambda *_: (0,) * nd, pipeline_mode=pl.Buffered(1))


def _params(n_axes):
    return pltpu.CompilerParams(dimension_semantics=("arbitrary",) * n_axes,
                                vmem_limit_bytes=VMEM_LIMIT)


def _in_offsets():
    o_krope = MLA_Q_LORA + MLA_KV_LORA
    o_qnsa = o_krope + MLA_ROPE
    o_cmp = o_qnsa + NSA_HEADS * NSA_DK
    o_slc = o_cmp + 2 * NSA_KV_HEADS * NSA_DK
    o_win = o_slc + 2 * NSA_KV_HEADS * NSA_DK
    o_gate = o_win + 2 * NSA_KV_HEADS * NSA_DK
    return o_krope, o_qnsa, o_cmp, o_slc, o_win, o_gate


class _Layout:
    def __init__(self, n):
        self.src = np.zeros((n,), np.int32)
        self.mul = np.zeros((n,), np.float32)

    def put(self, dst, s, n, m=1.0):
        self.src[dst:dst + n] = np.arange(s, s + n)
        self.mul[dst:dst + n] = m

    def apply(self, w):
        return (w[:, self.src] * self.mul[None, :]).astype(jnp.bfloat16)


def _w_in_row_layout():
    o_krope, _, o_cmp, o_slc, o_win, _ = _in_offsets()
    lay = _Layout(D_ROW)
    lay.put(C_CQ, 0, MLA_Q_LORA)
    lay.put(C_CKV, MLA_Q_LORA, MLA_KV_LORA)
    half = MLA_ROPE // 2
    lay.put(C_MISC + MLA_NOPE, o_krope, MLA_ROPE)
    lay.put(C_MISC + MLA_NOPE + MLA_ROPE, o_krope + half, half, -1.0)
    lay.put(C_MISC + MLA_NOPE + MLA_ROPE + half, o_krope, half, 1.0)
    lay.put(C_KVCMP, o_cmp, 2 * NSA_KV_HEADS * NSA_DK)
    for hk in range(NSA_KV_HEADS):
        lay.put(C_KSLC + hk * LANE, o_slc + hk * NSA_DK, NSA_DK)
        lay.put(C_KWIN + hk * LANE, o_win + hk * NSA_DK, NSA_DK)
    return lay


def _w_in_col_layout():
    _, o_qnsa, _, o_slc, o_win, o_gate = _in_offsets()
    lay = _Layout(D_COL)
    lay.put(R_QNSA, o_qnsa, NSA_HEADS * NSA_DK, NSA_DK ** -0.5)
    lay.put(R_VSLC, o_slc + NSA_KV_HEADS * NSA_DK, NSA_KV_HEADS * NSA_DK)
    lay.put(R_VWIN, o_win + NSA_KV_HEADS * NSA_DK, NSA_KV_HEADS * NSA_DK)
    for hk in range(NSA_KV_HEADS):
        lay.put(R_GATE + hk * GATE_ROWS, o_gate + hk * NSA_GROUP * 3, NSA_GROUP * 3)
    return lay


def _w_uq_layout():
    lay = _Layout(MLA_HEADS * LANE)
    dq = MLA_NOPE + MLA_ROPE
    for h in range(MLA_HEADS):
        lay.put(h * LANE, h * dq, dq)
    return lay


def _w_uk_layout():
    lay = _Layout(MLA_HEADS * LANE)
    per = MLA_NOPE + MLA_V
    for h in range(MLA_HEADS):
        lay.put(h * LANE, h * per, MLA_NOPE)
    return lay


def _w_uv_layout():
    lay = _Layout(MLA_HEADS * MLA_V)
    per = MLA_NOPE + MLA_V
    for h in range(MLA_HEADS):
        lay.put(h * MLA_V, h * per + MLA_NOPE, MLA_V)
    return lay


def _t5_thresholds(max_dist):
    n = np.arange(max_dist, dtype=np.int64)
    max_exact = T5_BUCKETS // 2
    nf = np.maximum(n, 1).astype(np.float32)
    val = (np.log(nf / np.float32(max_exact)) / np.float32(math.log(T5_MAX_DIST / max_exact))
           * np.float32(T5_BUCKETS - max_exact))
    large = np.minimum(max_exact + val.astype(np.int32), T5_BUCKETS - 1)
    bucket = np.where(n < max_exact, n, large)
    assert np.all(np.diff(bucket) >= 0)
    frac = np.abs(val[max_exact + 1:T5_MAX_DIST] - np.round(val[max_exact + 1:T5_MAX_DIST]))
    assert frac.min() > 1e-3
    return [int(np.argmax(bucket >= b)) for b in range(T5_BUCKETS)]


def _proj_kernel(x_ref, gmix_ref, wrow_ref, wcol_ref, gq_ref, wq_ref, gkv_ref, wk_ref, wv_ref,
                 cq_ref, sq_ref, ck_ref, sk_ref,
                 qm_ref, km_ref, vm_ref, qn_ref, kvc_ref, ks_ref, vs_ref, kw_ref, vw_ref, gate_ref):
    tm = x_ref.shape[1]
    bf = jnp.bfloat16
    xn = _rms(x_ref[0], gmix_ref[...]).astype(bf)
    u = _dot(xn, wrow_ref[...])
    ut = _dot_nt(wcol_ref[...], xn)

    cqn = _rms(u[:, C_CQ:C_CQ + MLA_Q_LORA], gq_ref[...]).astype(bf)
    qq = _dot_nt(wq_ref[...], cqn)
    cosq = cq_ref[...]
    sinq = sq_ref[...]
    r0, r1, r2 = MLA_NOPE, MLA_NOPE + MLA_ROPE // 2, MLA_NOPE + MLA_ROPE
    for h in range(MLA_HEADS):
        q = qq[h * LANE:(h + 1) * LANE]
        rot = jnp.concatenate([q[0:r0], -q[r1:r2], q[r0:r1], q[r2:]], axis=0)
        qm_ref[0, h] = (q * cosq + rot * sinq).astype(bf)

    ckvn = _rms(u[:, C_CKV:C_CKV + MLA_KV_LORA], gkv_ref[...]).astype(bf)
    kk = _dot(ckvn, wk_ref[...])
    misc = u[:, C_MISC:C_MISC + LANE]
    kpe = misc * ck_ref[...] + pltpu.roll(misc, LANE - MLA_ROPE, axis=1) * sk_ref[...]
    for h in range(MLA_HEADS):
        km_ref[0, h] = (kk[:, h * LANE:(h + 1) * LANE] + kpe).astype(bf)
    vv = _dot_nt(wv_ref[...], ckvn)
    for h in range(MLA_HEADS):
        vm_ref[0, h] = vv[h * MLA_V:(h + 1) * MLA_V].astype(bf)

    for h in range(NSA_HEADS):
        qn_ref[0, h] = (ut[R_QNSA + h * NSA_DK:R_QNSA + (h + 1) * NSA_DK] * LOG2E).astype(bf)
    for kv in range(2):
        kvc_ref[0, kv] = u[:, C_KVCMP + kv * LANE:C_KVCMP + (kv + 1) * LANE]
    pos = pl.program_id(1) * tm + lax.broadcasted_iota(jnp.int32, (tm, LANE), 0)
    lane = lax.broadcasted_iota(jnp.int32, (tm, LANE), 1)
    onehot = jnp.where(lane - (LANE - SLC_ROWS) == pos // SLC_LEN, 1.0, 0.0)
    for hk in range(NSA_KV_HEADS):
        ks_ref[0, hk] = (u[:, C_KSLC + hk * LANE:C_KSLC + (hk + 1) * LANE] + onehot).astype(bf)
        kw_ref[0, hk] = u[:, C_KWIN + hk * LANE:C_KWIN + (hk + 1) * LANE].astype(bf)
        vs_ref[0, hk] = ut[R_VSLC + hk * NSA_DK:R_VSLC + (hk + 1) * NSA_DK].astype(bf)
        vw_ref[0, hk] = ut[R_VWIN + hk * NSA_DK:R_VWIN + (hk + 1) * NSA_DK].astype(bf)
        gate_ref[0, hk] = jax.nn.sigmoid(ut[R_GATE + hk * GATE_ROWS:R_GATE + (hk + 1) * GATE_ROWS])


def _proj(x, gmix, w_row, w_col, gq, w_q2, gkv, w_k, w_v, cosq_t, sinq_t, cosk, sink):
    B, S, D = x.shape
    tm = min(PROJ_TM, S)
    bf = jnp.bfloat16
    grid = (B, S // tm)
    tok_major = lambda n: pl.BlockSpec((1, n, tm, LANE), lambda b, i: (b, 0, i, 0))
    feat_major = lambda n, d: pl.BlockSpec((1, n, d, tm), lambda b, i: (b, 0, 0, i))
    out_shape = (
        jax.ShapeDtypeStruct((B, MLA_HEADS, LANE, S), bf),
        jax.ShapeDtypeStruct((B, MLA_HEADS, S, LANE), bf),
        jax.ShapeDtypeStruct((B, MLA_HEADS, MLA_V, S), bf),
        jax.ShapeDtypeStruct((B, NSA_HEADS, NSA_DK, S), bf),
        jax.ShapeDtypeStruct((B, 2, S, LANE), jnp.float32),
        jax.ShapeDtypeStruct((B, NSA_KV_HEADS, S, LANE), bf),
        jax.ShapeDtypeStruct((B, NSA_KV_HEADS, NSA_DK, S), bf),
        jax.ShapeDtypeStruct((B, NSA_KV_HEADS, S, LANE), bf),
        jax.ShapeDtypeStruct((B, NSA_KV_HEADS, NSA_DK, S), bf),
        jax.ShapeDtypeStruct((B, NSA_KV_HEADS, GATE_ROWS, S), jnp.float32),
    )
    out_specs = (
        feat_major(MLA_HEADS, LANE), tok_major(MLA_HEADS), feat_major(MLA_HEADS, MLA_V),
        feat_major(NSA_HEADS, NSA_DK),
        tok_major(2),
        tok_major(NSA_KV_HEADS), feat_major(NSA_KV_HEADS, NSA_DK),
        tok_major(NSA_KV_HEADS), feat_major(NSA_KV_HEADS, NSA_DK),
        feat_major(NSA_KV_HEADS, GATE_ROWS),
    )
    consts = (gmix, w_row, w_col, gq, w_q2, gkv, w_k, w_v)
    in_specs = ([pl.BlockSpec((1, tm, D), lambda b, i: (b, i, 0))] + [_const_spec(c.shape) for c in consts]
                + [pl.BlockSpec((LANE, tm), lambda b, i: (0, i)), pl.BlockSpec((LANE, tm), lambda b, i: (0, i)),
                   pl.BlockSpec((tm, LANE), lambda b, i: (i, 0)), pl.BlockSpec((tm, LANE), lambda b, i: (i, 0))])
    return pl.pallas_call(
        _proj_kernel, grid=grid, in_specs=in_specs, out_specs=out_specs, out_shape=out_shape,
        compiler_params=_params(2), name="proj",
    )(x, *consts, cosq_t, sinq_t, cosk, sink)


def _compress_kernel(c_ref, pos_ref, w1_ref, w2k_ref, w2vt_ref, kc_ref, vc_ref):
    nc = kc_ref.shape[2]
    bf = jnp.bfloat16
    hw = NSA_KV_HEADS * NSA_DK
    for kv in range(2):
        halves = []
        for half in range(CMP_LEN // CMP_STRIDE):
            acc = None
            for r in range(CMP_STRIDE):
                l = half * CMP_STRIDE + r
                x = c_ref[0, kv, pl.ds(r, nc, stride=CMP_STRIDE), :]
                part = _dot((x + pos_ref[kv, l]).astype(bf), w1_ref[kv, l])
                acc = part if acc is None else acc + part
            halves.append(acc)
        hid = halves[0] + pltpu.roll(halves[1], nc - 1, axis=0)
        act = jax.nn.gelu(hid).astype(bf)
        for hk in range(NSA_KV_HEADS):
            act_h = act[:, hk * CMP_HIDDEN:(hk + 1) * CMP_HIDDEN]
            if kv == 0:
                kc_ref[0, hk] = _dot(act_h, w2k_ref[...]).astype(bf)
            else:
                vc_ref[0, hk] = _dot_nt(w2vt_ref[...], act_h).astype(bf)


def _compress(kvc, pos, w1, w2k, w2vt):
    B, _, S, width = kvc.shape
    nc = S // CMP_STRIDE
    return pl.pallas_call(
        _compress_kernel, grid=(B,),
        in_specs=[pl.BlockSpec((1, 2, S, width), lambda b: (b, 0, 0, 0)),
                  _const_spec(pos.shape), _const_spec(w1.shape), _const_spec(w2k.shape),
                  _const_spec(w2vt.shape)],
        out_specs=(pl.BlockSpec((1, NSA_KV_HEADS, nc, LANE), lambda b: (b, 0, 0, 0)),
                   pl.BlockSpec((1, NSA_KV_HEADS, NSA_DK, nc), lambda b: (b, 0, 0, 0))),
        out_shape=(jax.ShapeDtypeStruct((B, NSA_KV_HEADS, nc, LANE), jnp.bfloat16),
                   jax.ShapeDtypeStruct((B, NSA_KV_HEADS, NSA_DK, nc), jnp.bfloat16)),
        compiler_params=_params(1), name="compress",
    )(kvc, pos, w1, w2k, w2vt)


def _t5_lookup(dist, tab_ref, head, thr):
    val = jnp.full(dist.shape, tab_ref[0, head], jnp.float32)
    for b in range(1, T5_BUCKETS):
        val = jnp.where(dist >= thr[b], tab_ref[b, head], val)
    return (val - tab_ref[T5_BUCKETS - 1, head]) * LOG2E


def _bias_cmp_kernel(tab_ref, out_ref, *, thr, tq, n_cmp):
    hk = pl.program_id(0)
    t = pl.program_id(1)
    nc = out_ref.shape[2]
    per_tile = tq // CMP_STRIDE
    band = per_tile + (T5_MAX_DIST + CMP_LEN) // CMP_STRIDE + SUBLANE
    band = min(-(-band // SUBLANE) * SUBLANE, nc)
    start = jnp.clip(t * per_tile - (band - per_tile), 0, nc - band)
    start = pl.multiple_of(start // SUBLANE * SUBLANE, SUBLANE)
    n_all = lax.broadcasted_iota(jnp.int32, (nc, NSA_GROUP * tq), 0)
    out_ref[0, 0] = jnp.where(n_all < start, 0.0, NEG)
    n = start + lax.broadcasted_iota(jnp.int32, (band, tq), 0)
    i = lax.broadcasted_iota(jnp.int32, (band, tq), 1)
    dist = t * tq + i - (n * CMP_STRIDE + CMP_LEN - 1)
    ok = jnp.logical_and(dist >= 0, n < n_cmp)
    for g in range(NSA_GROUP):
        val = _t5_lookup(dist, tab_ref, hk * NSA_GROUP + g, thr)
        out_ref[0, 0, pl.ds(start, band), g * tq:(g + 1) * tq] = jnp.where(ok, val, NEG)


def _bias_tile_kernel(tab_ref, out_ref, *, thr, tq):
    hk = pl.program_id(0)
    tk = out_ref.shape[2]
    j = lax.broadcasted_iota(jnp.int32, (tk, tq), 0)
    i = lax.broadcasted_iota(jnp.int32, (tk, tq), 1)
    for g in range(NSA_GROUP):
        head = hk * NSA_GROUP + g
        cols = slice(g * tq, (g + 1) * tq)
        d0 = i - j
        out_ref[0, 0, :, cols] = jnp.where(d0 >= 0, _t5_lookup(d0, tab_ref, head, thr), NEG)
        out_ref[0, 1, :, cols] = _t5_lookup(d0 + tk, tab_ref, head, thr)
        out_ref[0, 2, :, cols] = jnp.where(j > i, 0.0, NEG)
        out_ref[0, 3, :, cols] = jnp.full((tk, tq), NEG, jnp.float32)


def _bias_tables(t5_table, S):
    thr = _t5_thresholds(S)
    tq, tk = NSA_TQ, NSA_TK
    assert tq == tk and WINDOW == 2 * tk and tk >= T5_MAX_DIST
    nq = S // tq
    nc = S // CMP_STRIDE
    n_cmp = (S - CMP_LEN) // CMP_STRIDE + 1
    m = NSA_GROUP * tq
    smem = pl.BlockSpec(memory_space=pltpu.SMEM)
    bias_c = pl.pallas_call(
        functools.partial(_bias_cmp_kernel, thr=thr, tq=tq, n_cmp=n_cmp),
        grid=(NSA_KV_HEADS, nq), in_specs=[smem],
        out_specs=pl.BlockSpec((1, 1, nc, m), lambda h, t: (h, t, 0, 0)),
        out_shape=jax.ShapeDtypeStruct((NSA_KV_HEADS, nq, nc, m), jnp.float32),
        compiler_params=_params(2), name="bias_cmp",
    )(t5_table)
    tz = pl.pallas_call(
        functools.partial(_bias_tile_kernel, thr=thr, tq=tq),
        grid=(NSA_KV_HEADS,), in_specs=[smem],
        out_specs=pl.BlockSpec((1, 4, tk, m), lambda h: (h, 0, 0, 0)),
        out_shape=jax.ShapeDtypeStruct((NSA_KV_HEADS, 4, tk, m), jnp.float32),
        compiler_params=_params(1), name="bias_tile",
    )(t5_table)
    return bias_c, tz


def _pipeline(stages):
    pending = [issue() for issue, _ in stages[:FLASH_LOOKAHEAD]]
    for i, (_, consume) in enumerate(stages):
        s = pending.pop(0)
        if i + FLASH_LOOKAHEAD < len(stages):
            pending.append(stages[i + FLASH_LOOKAHEAD][0]())
        consume(s)


def _flash_stage(chain):
    k, q_t, v_aug, bias_t, m_ref, acc_ref = chain

    def issue():
        s = _dot(k, q_t)
        return s if bias_t is None else s + bias_t

    def consume(s):
        m_prev = m_ref[...]
        m_new = jnp.maximum(m_prev, jnp.max(s, axis=0, keepdims=True))
        alpha = jnp.exp2(m_prev - m_new)
        p = jnp.exp2(s - m_new).astype(jnp.bfloat16)
        acc_ref[...] = alpha * acc_ref[...] + _dot(v_aug, p)
        m_ref[...] = m_new

    return issue, consume


def _flash_chains(chains):
    _pipeline([_flash_stage(c) for c in chains])


def _with_ones(v_t):
    tk = v_t.shape[1]
    row = lax.broadcasted_iota(jnp.int32, (ONES_ROWS, tk), 0)
    return jnp.concatenate([v_t, jnp.where(row == 0, 1.0, 0.0).astype(v_t.dtype)], axis=0)


def _col_chains(k, q_ref, v_aug, bias_ref, m_ref, acc_ref, ncol, width):
    chains = []
    for c0 in range(0, ncol, width):
        cols = slice(c0, c0 + width)
        chains.append((k, q_ref[:, cols], v_aug, None if bias_ref is None else bias_ref[:, cols],
                       m_ref.at[:, cols], acc_ref.at[:, cols]))
    return chains


def _flash_init(m_ref, acc_ref):
    m_ref[...] = jnp.full(m_ref.shape, -jnp.inf, jnp.float32)
    acc_ref[...] = jnp.zeros(acc_ref.shape, jnp.float32)


def _flash_result(acc_ref, dv):
    return acc_ref[0:dv, :] * (1.0 / acc_ref[dv:dv + 1, :])


def _mla_kernel(q_ref, k_ref, v_ref, o_ref, m_ref, acc_ref, *, tk, cw):
    S = k_ref.shape[2]
    kk = lax.broadcasted_iota(jnp.int32, (tk, cw), 0)
    qq = lax.broadcasted_iota(jnp.int32, (tk, cw), 1)
    for e in range(2):
        _flash_init(m_ref.at[e], acc_ref.at[e])
    chains = []
    for k0 in range(0, S, tk):
        for e in range(2):
            k = k_ref[0, e, k0:k0 + tk, :]
            v_aug = _with_ones(v_ref[0, e, :, k0:k0 + tk])
            for c0 in range(0, S, cw):
                if k0 > c0 + cw - 1:
                    continue
                bias = jnp.where(kk + k0 <= qq + c0, 0.0, NEG) if k0 + tk - 1 > c0 else None
                cols = slice(c0, c0 + cw)
                chains.append((k, q_ref[0, e, :, cols], v_aug, bias,
                               m_ref.at[e, :, cols], acc_ref.at[e, :, cols]))
    _flash_chains(chains)
    o_t = jnp.concatenate([_flash_result(acc_ref.at[e], MLA_V) for e in range(2)], axis=0)
    o_ref[0] = jnp.transpose(o_t)


def _mla(q_t, k, v_t):
    B, H, S, _ = k.shape
    tq = S
    tk = min(MLA_TK, S)
    cw = min(MLA_CW, S)
    return pl.pallas_call(
        functools.partial(_mla_kernel, tk=tk, cw=cw),
        grid=(B, H // 2),
        in_specs=[pl.BlockSpec((1, 2, LANE, S), lambda b, p: (b, p, 0, 0)),
                  pl.BlockSpec((1, 2, S, LANE), lambda b, p: (b, p, 0, 0)),
                  pl.BlockSpec((1, 2, MLA_V, S), lambda b, p: (b, p, 0, 0))],
        out_specs=pl.BlockSpec((1, S, 2 * MLA_V), lambda b, p: (b, 0, p)),
        out_shape=jax.ShapeDtypeStruct((B, S, H * MLA_V), jnp.float32),
        scratch_shapes=[pltpu.VMEM((2, 1, tq), jnp.float32),
                        pltpu.VMEM((2, MLA_V + ONES_ROWS, tq), jnp.float32)],
        compiler_params=_params(2), name="mla",
    )(q_t, k, v_t)


def _nsa_kernel(q_ref, ks_ref, vs_ref, kw_ref, vw_ref, kc_ref, vc_ref, gate_ref, bc_ref, tz_ref, ovt_ref,
                o_ref, qaug_ref, qpad_ref, m_ref, acc_ref, mw_ref, accw_ref, oc_ref, imp_ref, *, tq, tk):
    t = pl.program_id(1)
    G = NSA_GROUP
    HK = range(NSA_KV_HEADS)
    ncol = G * tq
    bf = jnp.bfloat16
    q0 = t * tq
    for hk in HK:
        for g in range(G):
            qpad_ref[hk, 0:NSA_DK, g * tq:(g + 1) * tq] = q_ref[0, hk * G + g]
            qaug_ref[hk, 0:NSA_DK, g * tq:(g + 1) * tq] = q_ref[0, hk * G + g]
        qpad_ref[hk, NSA_DK:, :] = jnp.zeros((SLC_ROWS, ncol), bf)

    def tile_chains(k_ref, v_ref, qx_ref, c, bias_idx, stats):
        k0 = pl.multiple_of(c * tk, tk)
        chains = []
        for hk in HK:
            bias_ref = None if bias_idx is None else tz_ref.at[hk, bias_idx]
            chains += _col_chains(k_ref[0, hk, pl.ds(k0, tk), :], qx_ref.at[hk],
                                  _with_ones(v_ref[0, hk, :, pl.ds(k0, tk)]),
                                  bias_ref, stats[0].at[hk], stats[1].at[hk], ncol, tq)
        return chains

    prev1 = jnp.maximum(t - 1, 0)
    bias1 = jnp.where(t >= 1, 1, 3)

    nb = ovt_ref.shape[0]
    sub = SUBLANE
    col_ok = q0 + lax.broadcasted_iota(jnp.int32, (1, tq), 1) >= CMP_LEN - 1
    jb = lax.broadcasted_iota(jnp.int32, (nb, tq), 0)
    cur = (q0 + lax.broadcasted_iota(jnp.int32, (nb, tq), 1)) // SLC_LEN
    forced = jnp.logical_or(jb == 0, jnp.logical_or(jb == cur, jb == cur - 1))
    jsub = lax.broadcasted_iota(jnp.int32, (sub, tq), 0)
    imp_sum = [None] * NSA_KV_HEADS

    def cmp_stage(hk, g):
        cols = slice(g * tq, (g + 1) * tq)

        def issue():
            return _dot(kc_ref[0, hk], qpad_ref[hk, :, cols]) + bc_ref[hk, 0, :, cols]

        def consume(s):
            mx = jnp.max(s, axis=0, keepdims=True)
            p = jnp.exp2(s - mx)
            lsum = jnp.sum(p, axis=0, keepdims=True)
            pcb = (p * jnp.where(col_ok, 1.0 / lsum, 0.0)).astype(bf)
            both = _dot(jnp.concatenate([vc_ref[0, hk], ovt_ref[...]], axis=0), pcb)
            oc_ref[hk, :, cols] = both[0:NSA_DK]
            imp_sum[hk] = both[NSA_DK:] if imp_sum[hk] is None else imp_sum[hk] + both[NSA_DK:]

        return issue, consume

    win = (mw_ref, accw_ref)
    _flash_init(*win)
    win_stages = [_flash_stage(c) for c in
                  tile_chains(kw_ref, vw_ref, qpad_ref, jnp.maximum(t - 2, 0), jnp.where(t >= 2, 2, 3), win)
                  + tile_chains(kw_ref, vw_ref, qpad_ref, prev1, bias1, win)
                  + tile_chains(kw_ref, vw_ref, qpad_ref, t, 0, win)]
    cmp_stages = [cmp_stage(hk, g) for g in range(G) for hk in HK]
    every = len(win_stages) // len(cmp_stages)
    stages = []
    for i, st in enumerate(cmp_stages):
        stages += [st] + win_stages[i * every:(i + 1) * every]
    _pipeline(stages + win_stages[len(cmp_stages) * every:])
    for hk in HK:
        imp_ref[hk] = jnp.where(forced, FORCE, jnp.where(jb <= cur, imp_sum[hk], NEG))

    tiles_per_group = SEL_GROUP * SLC_LEN // tq
    n_groups = nb // SEL_GROUP

    def write_mask(hk, selb_rows):
        selb = jnp.concatenate(selb_rows, axis=0).astype(bf)
        for g in range(G):
            qaug_ref[hk, NSA_DK:, g * tq:(g + 1) * tq] = selb

    def select(n_act):
        tail = [jnp.full((nb - n_act, tq), NEG, jnp.float32)] if n_act < nb else []
        for hk in HK:
            imp = imp_ref[hk, 0:n_act, :]
            if n_act <= SLC_TOPN:
                write_mask(hk, [jnp.where(imp > 0.5 * NEG, 0.0, NEG)] + tail)
                continue
            slabs = [imp[r0 * sub:(r0 + 1) * sub] for r0 in range(n_act // sub)]
            ranks = [jnp.zeros((sub, tq), jnp.int32) for _ in slabs]
            for jp in range(n_act):
                rowv = imp[jp:jp + 1, :]
                for r0, slab in enumerate(slabs):
                    lo = r0 * sub
                    if lo > jp:
                        one = jnp.where(rowv >= slab, 1, 0)
                    elif lo + sub - 1 <= jp:
                        one = jnp.where(rowv > slab, 1, 0)
                    else:
                        one = jnp.where(jsub + lo > jp, jnp.where(rowv >= slab, 1, 0),
                                        jnp.where(rowv > slab, 1, 0))
                    ranks[r0] = ranks[r0] + one
            write_mask(hk, [jnp.where(r < SLC_TOPN, 0.0, NEG) for r in ranks] + tail)

    for grp in range(n_groups):
        pl.when(t // tiles_per_group == grp)(functools.partial(select, SEL_GROUP * (grp + 1)))

    slc = (m_ref, acc_ref)
    _flash_init(*slc)
    n_far = jnp.maximum(t - 1, 0)

    def far(c, carry):
        chains = []
        for j in range(NSA_FAR_GROUP):
            chains += tile_chains(ks_ref, vs_ref, qaug_ref, c * NSA_FAR_GROUP + j, None, slc)
        _flash_chains(chains)
        return carry

    lax.fori_loop(0, n_far // NSA_FAR_GROUP, far, 0)
    rest0 = n_far // NSA_FAR_GROUP * NSA_FAR_GROUP

    def combine():
        for hk in HK:
            gate = gate_ref[0, hk]
            comb = []
            for g in range(G):
                cols = slice(g * tq, (g + 1) * tq)
                gr = lambda br: gate[g * 3 + br:g * 3 + br + 1, :]
                w_s = gr(1) / acc_ref[hk, NSA_DK:NSA_DK + 1, cols]
                w_w = gr(2) / accw_ref[hk, NSA_DK:NSA_DK + 1, cols]
                comb.append(gr(0) * oc_ref[hk, :, cols] + w_s * acc_ref[hk, 0:NSA_DK, cols]
                            + w_w * accw_ref[hk, 0:NSA_DK, cols])
            width = G * NSA_DK
            o_ref[0, :, hk * width:(hk + 1) * width] = jnp.transpose(jnp.concatenate(comb, axis=0))

    def tail(n_rest):
        chains = []
        for j in range(n_rest):
            chains += tile_chains(ks_ref, vs_ref, qaug_ref, rest0 + j, None, slc)
        _flash_chains(chains + tile_chains(ks_ref, vs_ref, qaug_ref, prev1, bias1, slc)
                      + tile_chains(ks_ref, vs_ref, qaug_ref, t, 0, slc))
        combine()

    for n_rest in range(NSA_FAR_GROUP):
        pl.when(n_far - rest0 == n_rest)(functools.partial(tail, n_rest))


def _nsa(q_t, ks, vs_t, kw, vw_t, kc, vc_t, gate_t, bias_c, tz, ovt):
    B, _, S, _ = ks.shape
    tq, tk = NSA_TQ, NSA_TK
    nq = S // tq
    nc = kc.shape[2]
    m = NSA_GROUP * tq
    hkv = NSA_KV_HEADS
    tok = lambda: pl.BlockSpec((1, hkv, S, LANE), lambda b, t: (b, 0, 0, 0))
    feat = lambda: pl.BlockSpec((1, hkv, NSA_DK, S), lambda b, t: (b, 0, 0, 0))
    return pl.pallas_call(
        functools.partial(_nsa_kernel, tq=tq, tk=tk),
        grid=(B, nq),
        in_specs=[pl.BlockSpec((1, NSA_HEADS, NSA_DK, tq), lambda b, t: (b, 0, 0, t)),
                  tok(), feat(), tok(), feat(),
                  pl.BlockSpec((1, hkv, nc, LANE), lambda b, t: (b, 0, 0, 0)),
                  pl.BlockSpec((1, hkv, NSA_DK, nc), lambda b, t: (b, 0, 0, 0)),
                  pl.BlockSpec((1, hkv, GATE_ROWS, tq), lambda b, t: (b, 0, 0, t)),
                  pl.BlockSpec((hkv, 1, nc, m), lambda b, t: (0, t, 0, 0)),
                  _const_spec(tz.shape),
                  _const_spec(ovt.shape)],
        out_specs=pl.BlockSpec((1, tq, NSA_HEADS * NSA_DK), lambda b, t: (b, t, 0)),
        out_shape=jax.ShapeDtypeStruct((B, S, NSA_HEADS * NSA_DK), jnp.float32),
        scratch_shapes=[pltpu.VMEM((hkv, NSA_DK + SLC_ROWS, m), jnp.bfloat16),
                        pltpu.VMEM((hkv, NSA_DK + SLC_ROWS, m), jnp.bfloat16),
                        pltpu.VMEM((hkv, 1, m), jnp.float32),
                        pltpu.VMEM((hkv, NSA_DK + ONES_ROWS, m), jnp.float32),
                        pltpu.VMEM((hkv, 1, m), jnp.float32),
                        pltpu.VMEM((hkv, NSA_DK + ONES_ROWS, m), jnp.float32),
                        pltpu.VMEM((hkv, NSA_DK, m), jnp.float32),
                        pltpu.VMEM((hkv, SLC_ROWS, tq), jnp.float32)],
        compiler_params=_params(2), name="nsa",
    )(q_t, ks, vs_t, kw, vw_t, kc, vc_t, gate_t, bias_c, tz, ovt)


def _post_kernel(x_ref, om_ref, on_ref, gm_ref, gn_ref, wo_ref, gf_ref, wg_ref, wu_ref, wd_ref,
                 gfin_ref, o_ref):
    half = om_ref.shape[2]
    mix_m = _rms(om_ref[0], gm_ref[...]).astype(jnp.bfloat16)
    mix_n = _rms(on_ref[0], gn_ref[...]).astype(jnp.bfloat16)
    h = x_ref[0] + _dot(mix_m, wo_ref[0:half, :]) + _dot(mix_n, wo_ref[half:2 * half, :])
    f = _rms(h, gf_ref[...]).astype(jnp.bfloat16)
    a = _dot(f, wg_ref[...])
    act = (a * jax.nn.sigmoid(a) * _dot(f, wu_ref[...])).astype(jnp.bfloat16)
    h = h + _dot(act, wd_ref[...])
    o_ref[0] = _rms(h, gfin_ref[...])


def _post(x, o_mla, o_nsa, gm, gn, w_out, gf, wg, wu, wd, gfin):
    B, S, D = x.shape
    tm = min(POST_TM, S)
    tok = lambda w: pl.BlockSpec((1, tm, w), lambda b, i: (b, i, 0))
    consts = (gm, gn, w_out, gf, wg, wu, wd, gfin)
    return pl.pallas_call(
        _post_kernel, grid=(B, S // tm),
        in_specs=[tok(D), tok(o_mla.shape[2]), tok(o_nsa.shape[2])] + [_const_spec(c.shape) for c in consts],
        out_specs=tok(D), out_shape=jax.ShapeDtypeStruct((B, S, D), jnp.float32),
        compiler_params=_params(2), name="post",
    )(x, o_mla, o_nsa, *consts)


def _rope_tables(S):
    pos = jnp.arange(S, dtype=jnp.float32)
    inv = ROPE_THETA ** (-jnp.arange(0, MLA_ROPE, 2, dtype=jnp.float32) / MLA_ROPE)
    ang = pos[:, None] * inv[None, :]
    cos, sin = jnp.cos(ang), jnp.sin(ang)
    cos2 = jnp.concatenate([cos, cos], axis=-1)
    sin2 = jnp.concatenate([sin, sin], axis=-1)
    scale = (MLA_NOPE + MLA_ROPE) ** -0.5 * LOG2E
    zq = jnp.zeros((S, LANE - MLA_NOPE - MLA_ROPE), jnp.float32)
    cosq = jnp.concatenate([jnp.ones((S, MLA_NOPE), jnp.float32), cos2, zq], axis=-1) * scale
    sinq = jnp.concatenate([jnp.zeros((S, MLA_NOPE), jnp.float32), sin2, zq], axis=-1) * scale
    zk0 = jnp.zeros((S, MLA_NOPE), jnp.float32)
    cosk = jnp.concatenate([zk0, cos2, zq], axis=-1)
    sink = jnp.concatenate([zk0, sin2, zq], axis=-1)
    return cosq.T, sinq.T, cosk, sink


def _overlap_t(S):
    n_cmp = (S - CMP_LEN) // CMP_STRIDE + 1
    n_slc = S // SLC_LEN
    assert n_slc <= SLC_ROWS
    cs = np.arange(n_cmp) * CMP_STRIDE
    ss = np.arange(n_slc) * SLC_LEN
    ov = np.maximum(0, np.minimum(cs[:, None] + CMP_LEN, ss[None, :] + SLC_LEN)
                    - np.maximum(cs[:, None], ss[None, :])).astype(np.float32) / CMP_STRIDE
    out = np.zeros((SLC_ROWS, S // CMP_STRIDE), np.float32)
    out[:n_slc, :n_cmp] = ov.T
    return jnp.asarray(out, jnp.bfloat16)


def kernel(x, norm_mix_g, w_in, mla_q_norm_g, mla_w_uq, mla_kv_norm_g, mla_w_ukv, nsa_cmp_pos_k, nsa_cmp_w1_k, nsa_cmp_w2_k, nsa_cmp_pos_v, nsa_cmp_w1_v, nsa_cmp_w2_v, t5_table, out_norm_mla_g, out_norm_nsa_g, w_out, norm_ffn_g, w_gate, w_up, w_down, final_norm_g):
    B, S, D = x.shape
    assert w_in.shape[0] == 1
    assert D == D_MODEL and S % NSA_TQ == 0 and S % CMP_STRIDE == 0
    bf = jnp.bfloat16
    l = 0
    cosq_t, sinq_t, cosk, sink = _rope_tables(S)
    bias_c, tz = _bias_tables(t5_table, S)
    ovt = _overlap_t(S)
    row = lambda v: v.reshape(1, -1)

    w_row = _w_in_row_layout().apply(w_in[l])
    w_col = _w_in_col_layout().apply(w_in[l]).T
    w_q2 = _w_uq_layout().apply(mla_w_uq[l]).T
    w_k = _w_uk_layout().apply(mla_w_ukv[l])
    w_v = _w_uv_layout().apply(mla_w_ukv[l]).T
    (q_mla, k_mla, v_mla, q_nsa, kv_cmp, k_slc, v_slc, k_win, v_win, gates) = _proj(
        x, row(norm_mix_g[l]), w_row, w_col, row(mla_q_norm_g[l]), w_q2, row(mla_kv_norm_g[l]), w_k, w_v,
        cosq_t, sinq_t, cosk, sink)

    def per_offset(pos_l, w1_l):
        pos2 = jnp.tile(pos_l, (1, NSA_KV_HEADS)).reshape(CMP_LEN, 1, NSA_KV_HEADS * NSA_DK)
        w = w1_l.reshape(CMP_LEN, NSA_DK, CMP_HIDDEN)
        z = jnp.zeros_like(w)
        return pos2, jnp.concatenate([jnp.concatenate([w, z], axis=2), jnp.concatenate([z, w], axis=2)], axis=1)

    pos_k, w1_k = per_offset(nsa_cmp_pos_k[l], nsa_cmp_w1_k[l])
    pos_v, w1_v = per_offset(nsa_cmp_pos_v[l], nsa_cmp_w1_v[l])
    pos = jnp.stack([pos_k, pos_v])
    w1 = jnp.stack([w1_k, w1_v]).astype(bf)
    w2k =jnp.concatenate([nsa_cmp_w2_k[l], jnp.zeros((CMP_HIDDEN, LANE - NSA_DK), jnp.float32)],
                          axis=1).astype(bf)
    w2vt = nsa_cmp_w2_v[l].T.astype(bf)
    k_cmp, v_cmp = _compress(kv_cmp, pos, w1, w2k, w2vt)

    o_mla = _mla(q_mla, k_mla, v_mla)
    o_nsa = _nsa(q_nsa, k_slc, v_slc, k_win, v_win, k_cmp, v_cmp, gates, bias_c, tz, ovt)

    return _post(x, o_mla, o_nsa, row(out_norm_mla_g[l]), row(out_norm_nsa_g[l]), w_out[l].astype(bf),
                 row(norm_ffn_g[l]), w_gate[l].astype(bf), w_up[l].astype(bf), w_down[l].astype(bf),
                 row(final_norm_g))
```

```python
import functools
import math

import numpy as np
import jax
import jax.numpy as jnp
from jax import lax
from jax.experimental import pallas as pl
from jax.experimental.pallas import tpu as pltpu

D_MODEL = 1024
MLA_HEADS = 8
MLA_NOPE = 64
MLA_ROPE = 32
MLA_V = 64
MLA_Q_LORA = 256
MLA_KV_LORA = 128
ROPE_THETA = 10000.0
NSA_HEADS = 8
NSA_KV_HEADS = 2
NSA_GROUP = 4
NSA_DK = 64
CMP_LEN = 32
CMP_STRIDE = 16
CMP_HIDDEN = 128
SLC_LEN = 64
SLC_TOPN = 16
WINDOW = 512
T5_BUCKETS = 32
T5_MAX_DIST = 128
D_FF = 2816
EPS = 1e-6
NEG = -1e30
FORCE = 1e30

LANE = 128
SUBLANE = 8
SLC_ROWS = 64
GATE_ROWS = 16
ONES_ROWS = 16
LOG2E = math.log2(math.e)

PROJ_TM = 512
MLA_TK = 256
MLA_CW = 256
FLASH_LOOKAHEAD = 6
NSA_TQ = 256
NSA_TK = 256
SEL_GROUP = 8
NSA_FAR_GROUP = 8
POST_TM = 512
V7X_VMEM_BYTES = 64 * 1024 * 1024
VMEM_LIMIT = V7X_VMEM_BYTES * 7 // 8

C_CQ = 0
C_CKV = 256
C_MISC = 384
C_KVCMP = 512
C_KSLC = 768
C_KWIN = 1024
D_ROW = 1280
R_QNSA = 0
R_VSLC = 512
R_VWIN = 640
R_GATE = 768
D_COL = 800


def _dot(a, b):
    return jnp.dot(a, b, preferred_element_type=jnp.float32)


def _dot_nt(a, b):
    return lax.dot_general(a, b, (((1,), (1,)), ((), ())), preferred_element_type=jnp.float32)


def _rms(x, g):
    return x * lax.rsqrt(jnp.mean(x * x, axis=-1, keepdims=True) + EPS) * g


def _const_spec(shape):
    nd = len(shape)
    return pl.BlockSpec(shape, lambda *_: (0,) * nd, pipeline_mode=pl.Buffered(1))


def _params(n_axes):
    return pltpu.CompilerParams(dimension_semantics=("arbitrary",) * n_axes,
                                vmem_limit_bytes=VMEM_LIMIT)


def _in_offsets():
    o_krope = MLA_Q_LORA + MLA_KV_LORA
    o_qnsa = o_krope + MLA_ROPE
    o_cmp = o_qnsa + NSA_HEADS * NSA_DK
    o_slc = o_cmp + 2 * NSA_KV_HEADS * NSA_DK
    o_win = o_slc + 2 * NSA_KV_HEADS * NSA_DK
    o_gate = o_win + 2 * NSA_KV_HEADS * NSA_DK
    return o_krope, o_qnsa, o_cmp, o_slc, o_win, o_gate


class _Layout:
    def __init__(self, n):
        self.src = np.zeros((n,), np.int32)
        self.mul = np.zeros((n,), np.float32)

    def put(self, dst, s, n, m=1.0):
        self.src[dst:dst + n] = np.arange(s, s + n)
        self.mul[dst:dst + n] = m

    def apply(self, w):
        return (w[:, self.src] * self.mul[None, :]).astype(jnp.bfloat16)


def _w_in_row_layout():
    o_krope, _, o_cmp, o_slc, o_win, _ = _in_offsets()
    lay = _Layout(D_ROW)
    lay.put(C_CQ, 0, MLA_Q_LORA)
    lay.put(C_CKV, MLA_Q_LORA, MLA_KV_LORA)
    half = MLA_ROPE // 2
    lay.put(C_MISC + MLA_NOPE, o_krope, MLA_ROPE)
    lay.put(C_MISC + MLA_NOPE + MLA_ROPE, o_krope + half, half, -1.0)
    lay.put(C_MISC + MLA_NOPE + MLA_ROPE + half, o_krope, half, 1.0)
    lay.put(C_KVCMP, o_cmp, 2 * NSA_KV_HEADS * NSA_DK)
    for hk in range(NSA_KV_HEADS):
        lay.put(C_KSLC + hk * LANE, o_slc + hk * NSA_DK, NSA_DK)
        lay.put(C_KWIN + hk * LANE, o_win + hk * NSA_DK, NSA_DK)
    return lay


def _w_in_col_layout():
    _, o_qnsa, _, o_slc, o_win, o_gate = _in_offsets()
    lay = _Layout(D_COL)
    lay.put(R_QNSA, o_qnsa, NSA_HEADS * NSA_DK, NSA_DK ** -0.5)
    lay.put(R_VSLC, o_slc + NSA_KV_HEADS * NSA_DK, NSA_KV_HEADS * NSA_DK)
    lay.put(R_VWIN, o_win + NSA_KV_HEADS * NSA_DK, NSA_KV_HEADS * NSA_DK)
    for hk in range(NSA_KV_HEADS):
        lay.put(R_GATE + hk * GATE_ROWS, o_gate + hk * NSA_GROUP * 3, NSA_GROUP * 3)
    return lay


def _w_uq_layout():
    lay = _Layout(MLA_HEADS * LANE)
    dq = MLA_NOPE + MLA_ROPE
    for h in range(MLA_HEADS):
        lay.put(h * LANE, h * dq, dq)
    return lay


def _w_uk_layout():
    lay = _Layout(MLA_HEADS * LANE)
    per = MLA_NOPE + MLA_V
    for h in range(MLA_HEADS):
        lay.put(h * LANE, h * per, MLA_NOPE)
    return lay


def _w_uv_layout():
    lay = _Layout(MLA_HEADS * MLA_V)
    per = MLA_NOPE + MLA_V
    for h in range(MLA_HEADS):
        lay.put(h * MLA_V, h * per + MLA_NOPE, MLA_V)
    return lay


def _t5_thresholds(max_dist):
    n = np.arange(max_dist, dtype=np.int64)
    max_exact = T5_BUCKETS // 2
    nf = np.maximum(n, 1).astype(np.float32)
    val = (np.log(nf / np.float32(max_exact)) / np.float32(math.log(T5_MAX_DIST / max_exact))
           * np.float32(T5_BUCKETS - max_exact))
    large = np.minimum(max_exact + val.astype(np.int32), T5_BUCKETS - 1)
    bucket = np.where(n < max_exact, n, large)
    assert np.all(np.diff(bucket) >= 0)
    frac = np.abs(val[max_exact + 1:T5_MAX_DIST] - np.round(val[max_exact + 1:T5_MAX_DIST]))
    assert frac.min() > 1e-3
    return [int(np.argmax(bucket >= b)) for b in range(T5_BUCKETS)]


def _proj_kernel(x_ref, gmix_ref, wrow_ref, wcol_ref, gq_ref, wq_ref, gkv_ref, wk_ref, wv_ref,
                 cq_ref, sq_ref, ck_ref, sk_ref,
                 qm_ref, km_ref, vm_ref, qn_ref, kvc_ref, ks_ref, vs_ref, kw_ref, vw_ref, gate_ref):
    tm = x_ref.shape[1]
    bf = jnp.bfloat16
    xn = _rms(x_ref[0], gmix_ref[...]).astype(bf)
    u = _dot(xn, wrow_ref[...])
    ut = _dot_nt(wcol_ref[...], xn)

    cqn = _rms(u[:, C_CQ:C_CQ + MLA_Q_LORA], gq_ref[...]).astype(bf)
    qq = _dot_nt(wq_ref[...], cqn)
    cosq = cq_ref[...]
    sinq = sq_ref[...]
    r0, r1, r2 = MLA_NOPE, MLA_NOPE + MLA_ROPE // 2, MLA_NOPE + MLA_ROPE
    for h in range(MLA_HEADS):
        q = qq[h * LANE:(h + 1) * LANE]
        rot = jnp.concatenate([q[0:r0], -q[r1:r2], q[r0:r1], q[r2:]], axis=0)
        qm_ref[0, h] = (q * cosq + rot * sinq).astype(bf)

    ckvn = _rms(u[:, C_CKV:C_CKV + MLA_KV_LORA], gkv_ref[...]).astype(bf)
    kk = _dot(ckvn, wk_ref[...])
    misc = u[:, C_MISC:C_MISC + LANE]
    kpe = misc * ck_ref[...] + pltpu.roll(misc, LANE - MLA_ROPE, axis=1) * sk_ref[...]
    for h in range(MLA_HEADS):
        km_ref[0, h] = (kk[:, h * LANE:(h + 1) * LANE] + kpe).astype(bf)
    vv = _dot_nt(wv_ref[...], ckvn)
    for h in range(MLA_HEADS):
        vm_ref[0, h] = vv[h * MLA_V:(h + 1) * MLA_V].astype(bf)

    for h in range(NSA_HEADS):
        qn_ref[0, h] = (ut[R_QNSA + h * NSA_DK:R_QNSA + (h + 1) * NSA_DK] * LOG2E).astype(bf)
    for kv in range(2):
        kvc_ref[0, kv] = u[:, C_KVCMP + kv * LANE:C_KVCMP + (kv + 1) * LANE]
    pos = pl.program_id(1) * tm + lax.broadcasted_iota(jnp.int32, (tm, LANE), 0)
    lane = lax.broadcasted_iota(jnp.int32, (tm, LANE), 1)
    onehot = jnp.where(lane - (LANE - SLC_ROWS) == pos // SLC_LEN, 1.0, 0.0)
    for hk in range(NSA_KV_HEADS):
        ks_ref[0, hk] = (u[:, C_KSLC + hk * LANE:C_KSLC + (hk + 1) * LANE] + onehot).astype(bf)
        kw_ref[0, hk] = u[:, C_KWIN + hk * LANE:C_KWIN + (hk + 1) * LANE].astype(bf)
        vs_ref[0, hk] = ut[R_VSLC + hk * NSA_DK:R_VSLC + (hk + 1) * NSA_DK].astype(bf)
        vw_ref[0, hk] = ut[R_VWIN + hk * NSA_DK:R_VWIN + (hk + 1) * NSA_DK].astype(bf)
        gate_ref[0, hk] = jax.nn.sigmoid(ut[R_GATE + hk * GATE_ROWS:R_GATE + (hk + 1) * GATE_ROWS])


def _proj(x, gmix, w_row, w_col, gq, w_q2, gkv, w_k, w_v, cosq_t, sinq_t, cosk, sink):
    B, S, D = x.shape
    tm = min(PROJ_TM, S)
    bf = jnp.bfloat16
    grid = (B, S // tm)
    tok_major = lambda n: pl.BlockSpec((1, n, tm, LANE), lambda b, i: (b, 0, i, 0))
    feat_major = lambda n, d: pl.BlockSpec((1, n, d, tm), lambda b, i: (b, 0, 0, i))
    out_shape = (
        jax.ShapeDtypeStruct((B, MLA_HEADS, LANE, S), bf),
        jax.ShapeDtypeStruct((B, MLA_HEADS, S, LANE), bf),
        jax.ShapeDtypeStruct((B, MLA_HEADS, MLA_V, S), bf),
        jax.ShapeDtypeStruct((B, NSA_HEADS, NSA_DK, S), bf),
        jax.ShapeDtypeStruct((B, 2, S, LANE), jnp.float32),
        jax.ShapeDtypeStruct((B, NSA_KV_HEADS, S, LANE), bf),
        jax.ShapeDtypeStruct((B, NSA_KV_HEADS, NSA_DK, S), bf),
        jax.ShapeDtypeStruct((B, NSA_KV_HEADS, S, LANE), bf),
        jax.ShapeDtypeStruct((B, NSA_KV_HEADS, NSA_DK, S), bf),
        jax.ShapeDtypeStruct((B, NSA_KV_HEADS, GATE_ROWS, S), jnp.float32),
    )
    out_specs = (
        feat_major(MLA_HEADS, LANE), tok_major(MLA_HEADS), feat_major(MLA_HEADS, MLA_V),
        feat_major(NSA_HEADS, NSA_DK),
        tok_major(2),
        tok_major(NSA_KV_HEADS), feat_major(NSA_KV_HEADS, NSA_DK),
        tok_major(NSA_KV_HEADS), feat_major(NSA_KV_HEADS, NSA_DK),
        feat_major(NSA_KV_HEADS, GATE_ROWS),
    )
    consts = (gmix, w_row, w_col, gq, w_q2, gkv, w_k, w_v)
    in_specs = ([pl.BlockSpec((1, tm, D), lambda b, i: (b, i, 0))] + [_const_spec(c.shape) for c in consts]
                + [pl.BlockSpec((LANE, tm), lambda b, i: (0, i)), pl.BlockSpec((LANE, tm), lambda b, i: (0, i)),
                   pl.BlockSpec((tm, LANE), lambda b, i: (i, 0)), pl.BlockSpec((tm, LANE), lambda b, i: (i, 0))])
    return pl.pallas_call(
        _proj_kernel, grid=grid, in_specs=in_specs, out_specs=out_specs, out_shape=out_shape,
        compiler_params=_params(2), name="proj",
    )(x, *consts, cosq_t, sinq_t, cosk, sink)


def _compress_kernel(c_ref, pos_ref, w1_ref, w2k_ref, w2vt_ref, kc_ref, vc_ref):
    nc = kc_ref.shape[2]
    bf = jnp.bfloat16
    hw = NSA_KV_HEADS * NSA_DK
    for kv in range(2):
        halves = []
        for half in range(CMP_LEN // CMP_STRIDE):
            acc = None
            for r in range(CMP_STRIDE):
                l = half * CMP_STRIDE + r
                x = c_ref[0, kv, pl.ds(r, nc, stride=CMP_STRIDE), :]
                part = _dot((x + pos_ref[kv, l]).astype(bf), w1_ref[kv, l])
                acc = part if acc is None else acc + part
            halves.append(acc)
        hid = halves[0] + pltpu.roll(halves[1], nc - 1, axis=0)
        act = jax.nn.gelu(hid).astype(bf)
        for hk in range(NSA_KV_HEADS):
            act_h = act[:, hk * CMP_HIDDEN:(hk + 1) * CMP_HIDDEN]
            if kv == 0:
                kc_ref[0, hk] = _dot(act_h, w2k_ref[...]).astype(bf)
            else:
                vc_ref[0, hk] = _dot_nt(w2vt_ref[...], act_h).astype(bf)


def _compress(kvc, pos, w1, w2k, w2vt):
    B, _, S, width = kvc.shape
    nc = S // CMP_STRIDE
    return pl.pallas_call(
        _compress_kernel, grid=(B,),
        in_specs=[pl.BlockSpec((1, 2, S, width), lambda b: (b, 0, 0, 0)),
                  _const_spec(pos.shape), _const_spec(w1.shape), _const_spec(w2k.shape),
                  _const_spec(w2vt.shape)],
        out_specs=(pl.BlockSpec((1, NSA_KV_HEADS, nc, LANE), lambda b: (b, 0, 0, 0)),
                   pl.BlockSpec((1, NSA_KV_HEADS, NSA_DK, nc), lambda b: (b, 0, 0, 0))),
        out_shape=(jax.ShapeDtypeStruct((B, NSA_KV_HEADS, nc, LANE), jnp.bfloat16),
                   jax.ShapeDtypeStruct((B, NSA_KV_HEADS, NSA_DK, nc), jnp.bfloat16)),
        compiler_params=_params(1), name="compress",
    )(kvc, pos, w1, w2k, w2vt)


def _t5_lookup(dist, tab_ref, head, thr):
    val = jnp.full(dist.shape, tab_ref[0, head], jnp.float32)
    for b in range(1, T5_BUCKETS):
        val = jnp.where(dist >= thr[b], tab_ref[b, head], val)
    return (val - tab_ref[T5_BUCKETS - 1, head]) * LOG2E


def _bias_cmp_kernel(tab_ref, out_ref, *, thr, tq, n_cmp):
    hk = pl.program_id(0)
    t = pl.program_id(1)
    nc = out_ref.shape[2]
    per_tile = tq // CMP_STRIDE
    band = per_tile + (T5_MAX_DIST + CMP_LEN) // CMP_STRIDE + SUBLANE
    band = min(-(-band // SUBLANE) * SUBLANE, nc)
    start = jnp.clip(t * per_tile - (band - per_tile), 0, nc - band)
    start = pl.multiple_of(start // SUBLANE * SUBLANE, SUBLANE)
    n_all = lax.broadcasted_iota(jnp.int32, (nc, NSA_GROUP * tq), 0)
    out_ref[0, 0] = jnp.where(n_all < start, 0.0, NEG)
    n = start + lax.broadcasted_iota(jnp.int32, (band, tq), 0)
    i = lax.broadcasted_iota(jnp.int32, (band, tq), 1)
    dist = t * tq + i - (n * CMP_STRIDE + CMP_LEN - 1)
    ok = jnp.logical_and(dist >= 0, n < n_cmp)
    for g in range(NSA_GROUP):
        val = _t5_lookup(dist, tab_ref, hk * NSA_GROUP + g, thr)
        out_ref[0, 0, pl.ds(start, band), g * tq:(g + 1) * tq] = jnp.where(ok, val, NEG)


def _bias_tile_kernel(tab_ref, out_ref, *, thr, tq):
    hk = pl.program_id(0)
    tk = out_ref.shape[2]
    j = lax.broadcasted_iota(jnp.int32, (tk, tq), 0)
    i = lax.broadcasted_iota(jnp.int32, (tk, tq), 1)
    for g in range(NSA_GROUP):
        head = hk * NSA_GROUP + g
        cols = slice(g * tq, (g + 1) * tq)
        d0 = i - j
        out_ref[0, 0, :, cols] = jnp.where(d0 >= 0, _t5_lookup(d0, tab_ref, head, thr), NEG)
        out_ref[0, 1, :, cols] = _t5_lookup(d0 + tk, tab_ref, head, thr)
        out_ref[0, 2, :, cols] = jnp.where(j > i, 0.0, NEG)
        out_ref[0, 3, :, cols] = jnp.full((tk, tq), NEG, jnp.float32)


def _bias_tables(t5_table, S):
    thr = _t5_thresholds(S)
    tq, tk = NSA_TQ, NSA_TK
    assert tq == tk and WINDOW == 2 * tk and tk >= T5_MAX_DIST
    nq = S // tq
    nc = S // CMP_STRIDE
    n_cmp = (S - CMP_LEN) // CMP_STRIDE + 1
    m = NSA_GROUP * tq
    smem = pl.BlockSpec(memory_space=pltpu.SMEM)
    bias_c = pl.pallas_call(
        functools.partial(_bias_cmp_kernel, thr=thr, tq=tq, n_cmp=n_cmp),
        grid=(NSA_KV_HEADS, nq), in_specs=[smem],
        out_specs=pl.BlockSpec((1, 1, nc, m), lambda h, t: (h, t, 0, 0)),
        out_shape=jax.ShapeDtypeStruct((NSA_KV_HEADS, nq, nc, m), jnp.float32),
        compiler_params=_params(2), name="bias_cmp",
    )(t5_table)
    tz = pl.pallas_call(
        functools.partial(_bias_tile_kernel, thr=thr, tq=tq),
        grid=(NSA_KV_HEADS,), in_specs=[smem],
        out_specs=pl.BlockSpec((1, 4, tk, m), lambda h: (h, 0, 0, 0)),
        out_shape=jax.ShapeDtypeStruct((NSA_KV_HEADS, 4, tk, m), jnp.float32),
        compiler_params=_params(1), name="bias_tile",
    )(t5_table)
    return bias_c, tz


def _pipeline(stages):
    pending = [issue() for issue, _ in stages[:FLASH_LOOKAHEAD]]
    for i, (_, consume) in enumerate(stages):
        s = pending.pop(0)
        if i + FLASH_LOOKAHEAD < len(stages):
            pending.append(stages[i + FLASH_LOOKAHEAD][0]())
        consume(s)


def _flash_stage(chain):
    k, q_t, v_aug, bias_t, m_ref, acc_ref = chain

    def issue():
        s = _dot(k, q_t)
        return s if bias_t is None else s + bias_t

    def consume(s):
        m_prev = m_ref[...]
        m_new = jnp.maximum(m_prev, jnp.max(s, axis=0, keepdims=True))
        alpha = jnp.exp2(m_prev - m_new)
        p = jnp.exp2(s - m_new).astype(jnp.bfloat16)
        acc_ref[...] = alpha * acc_ref[...] + _dot(v_aug, p)
        m_ref[...] = m_new

    return issue, consume


def _flash_chains(chains):
    _pipeline([_flash_stage(c) for c in chains])


def _with_ones(v_t):
    tk = v_t.shape[1]
    row = lax.broadcasted_iota(jnp.int32, (ONES_ROWS, tk), 0)
    return jnp.concatenate([v_t, jnp.where(row == 0, 1.0, 0.0).astype(v_t.dtype)], axis=0)


def _col_chains(k, q_ref, v_aug, bias_ref, m_ref, acc_ref, ncol, width):
    chains = []
    for c0 in range(0, ncol, width):
        cols = slice(c0, c0 + width)
        chains.append((k, q_ref[:, cols], v_aug, None if bias_ref is None else bias_ref[:, cols],
                       m_ref.at[:, cols], acc_ref.at[:, cols]))
    return chains


def _flash_init(m_ref, acc_ref):
    m_ref[...] = jnp.full(m_ref.shape, -jnp.inf, jnp.float32)
    acc_ref[...] = jnp.zeros(acc_ref.shape, jnp.float32)


def _flash_result(acc_ref, dv):
    return acc_ref[0:dv, :] * (1.0 / acc_ref[dv:dv + 1, :])


def _mla_kernel(q_ref, k_ref, v_ref, o_ref, m_ref, acc_ref, *, tk, cw):
    S = k_ref.shape[2]
    assert tk == cw
    kk = lax.broadcasted_iota(jnp.int32, (tk, cw), 0)
    qq = lax.broadcasted_iota(jnp.int32, (tk, cw), 1)
    causal = jnp.where(kk <= qq, 0.0, NEG)
    for e in range(2):
        _flash_init(m_ref.at[e], acc_ref.at[e])
    chains = []
    for k0 in range(0, S, tk):
        for e in range(2):
            k = k_ref[0, e, k0:k0 + tk, :]
            v_aug = _with_ones(v_ref[0, e, :, k0:k0 + tk])
            for c0 in range(0, S, cw):
                if k0 > c0 + cw - 1:
                    continue
                bias = causal if k0 + tk - 1 > c0 else None
                cols = slice(c0, c0 + cw)
                chains.append((k, q_ref[0, e, :, cols], v_aug, bias,
                               m_ref.at[e, :, cols], acc_ref.at[e, :, cols]))
    _flash_chains(chains)
    o_t = jnp.concatenate([_flash_result(acc_ref.at[e], MLA_V) for e in range(2)], axis=0)
    o_ref[0] = jnp.transpose(o_t)


def _mla(q_t, k, v_t):
    B, H, S, _ = k.shape
    tq = S
    tk = min(MLA_TK, S)
    cw = min(MLA_CW, S)
    return pl.pallas_call(
        functools.partial(_mla_kernel, tk=tk, cw=cw),
        grid=(B, H // 2),
        in_specs=[pl.BlockSpec((1, 2, LANE, S), lambda b, p: (b, p, 0, 0)),
                  pl.BlockSpec((1, 2, S, LANE), lambda b, p: (b, p, 0, 0)),
                  pl.BlockSpec((1, 2, MLA_V, S), lambda b, p: (b, p, 0, 0))],
        out_specs=pl.BlockSpec((1, S, 2 * MLA_V), lambda b, p: (b, 0, p)),
        out_shape=jax.ShapeDtypeStruct((B, S, H * MLA_V), jnp.float32),
        scratch_shapes=[pltpu.VMEM((2, 1, tq), jnp.float32),
                        pltpu.VMEM((2, MLA_V + ONES_ROWS, tq), jnp.float32)],
        compiler_params=_params(2), name="mla",
    )(q_t, k, v_t)


def _nsa_kernel(q_ref, ks_ref, vs_ref, kw_ref, vw_ref, kc_ref, vc_ref, gate_ref, bc_ref, tz_ref, ovt_ref,
                o_ref, qaug_ref, qpad_ref, m_ref, acc_ref, mw_ref, accw_ref, oc_ref, imp_ref, *, tq, tk):
    t = pl.program_id(1)
    G = NSA_GROUP
    HK = range(NSA_KV_HEADS)
    ncol = G * tq
    bf = jnp.bfloat16
    q0 = t * tq
    for hk in HK:
        for g in range(G):
            qpad_ref[hk, 0:NSA_DK, g * tq:(g + 1) * tq] = q_ref[0, hk * G + g]
            qaug_ref[hk, 0:NSA_DK, g * tq:(g + 1) * tq] = q_ref[0, hk * G + g]
        qpad_ref[hk, NSA_DK:, :] = jnp.zeros((SLC_ROWS, ncol), bf)

    def tile_chains(k_ref, v_ref, qx_ref, c, bias_idx, stats):
        k0 = pl.multiple_of(c * tk, tk)
        chains = []
        for hk in HK:
            bias_ref = None if bias_idx is None else tz_ref.at[hk, bias_idx]
            chains += _col_chains(k_ref[0, hk, pl.ds(k0, tk), :], qx_ref.at[hk],
                                  _with_ones(v_ref[0, hk, :, pl.ds(k0, tk)]),
                                  bias_ref, stats[0].at[hk], stats[1].at[hk], ncol, tq)
        return chains

    prev1 = jnp.maximum(t - 1, 0)
    bias1 = jnp.where(t >= 1, 1, 3)

    nb = ovt_ref.shape[0]
    sub = SUBLANE
    col_ok = q0 + lax.broadcasted_iota(jnp.int32, (1, tq), 1) >= CMP_LEN - 1
    jb = lax.broadcasted_iota(jnp.int32, (nb, tq), 0)
    cur = (q0 + lax.broadcasted_iota(jnp.int32, (nb, tq), 1)) // SLC_LEN
    forced = jnp.logical_or(jb == 0, jnp.logical_or(jb == cur, jb == cur - 1))
    jsub = lax.broadcasted_iota(jnp.int32, (sub, tq), 0)
    imp_sum = [None] * NSA_KV_HEADS

    def cmp_stage(hk, g):
        cols = slice(g * tq, (g + 1) * tq)

        def issue():
            return _dot(kc_ref[0, hk], qpad_ref[hk, :, cols]) + bc_ref[hk, 0, :, cols]

        def consume(s):
            mx = jnp.max(s, axis=0, keepdims=True)
            p = jnp.exp2(s - mx)
            lsum = jnp.sum(p, axis=0, keepdims=True)
            pcb = (p * jnp.where(col_ok, 1.0 / lsum, 0.0)).astype(bf)
            both = _dot(jnp.concatenate([vc_ref[0, hk], ovt_ref[...]], axis=0), pcb)
            oc_ref[hk, :, cols] = both[0:NSA_DK]
            imp_sum[hk] = both[NSA_DK:] if imp_sum[hk] is None else imp_sum[hk] + both[NSA_DK:]

        return issue, consume

    win = (mw_ref, accw_ref)
    _flash_init(*win)
    win_stages = [_flash_stage(c) for c in
                  tile_chains(kw_ref, vw_ref, qpad_ref, jnp.maximum(t - 2, 0), jnp.where(t >= 2, 2, 3), win)
                  + tile_chains(kw_ref, vw_ref, qpad_ref, prev1, bias1, win)
                  + tile_chains(kw_ref, vw_ref, qpad_ref, t, 0, win)]
    cmp_stages = [cmp_stage(hk, g) for g in range(G) for hk in HK]
    every = len(win_stages) // len(cmp_stages)
    stages = []
    for i, st in enumerate(cmp_stages):
        stages += [st] + win_stages[i * every:(i + 1) * every]
    _pipeline(stages + win_stages[len(cmp_stages) * every:])
    for hk in HK:
        imp_ref[hk] = jnp.where(forced, FORCE, jnp.where(jb <= cur, imp_sum[hk], NEG))

    tiles_per_group = SEL_GROUP * SLC_LEN // tq
    n_groups = nb // SEL_GROUP

    def write_mask(hk, selb_rows):
        selb = jnp.concatenate(selb_rows, axis=0).astype(bf)
        for g in range(G):
            qaug_ref[hk, NSA_DK:, g * tq:(g + 1) * tq] = selb

    def select(n_act):
        tail = [jnp.full((nb - n_act, tq), NEG, jnp.float32)] if n_act < nb else []
        for hk in HK:
            imp = imp_ref[hk, 0:n_act, :]
            if n_act <= SLC_TOPN:
                write_mask(hk, [jnp.where(imp > 0.5 * NEG, 0.0, NEG)] + tail)
                continue
            slabs = [imp[r0 * sub:(r0 + 1) * sub] for r0 in range(n_act // sub)]
            ranks = [jnp.zeros((sub, tq), jnp.int32) for _ in slabs]
            for jp in range(n_act):
                rowv = imp[jp:jp + 1, :]
                for r0, slab in enumerate(slabs):
                    lo = r0 * sub
                    if lo > jp:
                        one = jnp.where(rowv >= slab, 1, 0)
                    elif lo + sub - 1 <= jp:
                        one = jnp.where(rowv > slab, 1, 0)
                    else:
                        one = jnp.where(jsub + lo > jp, jnp.where(rowv >= slab, 1, 0),
                                        jnp.where(rowv > slab, 1, 0))
                    ranks[r0] = ranks[r0] + one
            write_mask(hk, [jnp.where(r < SLC_TOPN, 0.0, NEG) for r in ranks] + tail)

    for grp in range(n_groups):
        pl.when(t // tiles_per_group == grp)(functools.partial(select, SEL_GROUP * (grp + 1)))

    slc = (m_ref, acc_ref)
    _flash_init(*slc)
    n_far = jnp.maximum(t - 1, 0)

    def far(c, carry):
        chains = []
        for j in range(NSA_FAR_GROUP):
            chains += tile_chains(ks_ref, vs_ref, qaug_ref, c * NSA_FAR_GROUP + j, None, slc)
        _flash_chains(chains)
        return carry

    lax.fori_loop(0, n_far // NSA_FAR_GROUP, far, 0)
    rest0 = n_far // NSA_FAR_GROUP * NSA_FAR_GROUP

    def combine():
        for hk in HK:
            gate = gate_ref[0, hk]
            comb = []
            for g in range(G):
                cols = slice(g * tq, (g + 1) * tq)
                gr = lambda br: gate[g * 3 + br:g * 3 + br + 1, :]
                w_s = gr(1) / acc_ref[hk, NSA_DK:NSA_DK + 1, cols]
                w_w = gr(2) / accw_ref[hk, NSA_DK:NSA_DK + 1, cols]
                comb.append(gr(0) * oc_ref[hk, :, cols] + w_s * acc_ref[hk, 0:NSA_DK, cols]
                            + w_w * accw_ref[hk, 0:NSA_DK, cols])
            width = G * NSA_DK
            o_ref[0, :, hk * width:(hk + 1) * width] = jnp.transpose(jnp.concatenate(comb, axis=0))

    def tail(n_rest):
        chains = []
        for j in range(n_rest):
            chains += tile_chains(ks_ref, vs_ref, qaug_ref, rest0 + j, None, slc)
        _flash_chains(chains + tile_chains(ks_ref, vs_ref, qaug_ref, prev1, bias1, slc)
                      + tile_chains(ks_ref, vs_ref, qaug_ref, t, 0, slc))
        combine()

    for n_rest in range(NSA_FAR_GROUP):
        pl.when(n_far - rest0 == n_rest)(functools.partial(tail, n_rest))


def _nsa(q_t, ks, vs_t, kw, vw_t, kc, vc_t, gate_t, bias_c, tz, ovt):
    B, _, S, _ = ks.shape
    tq, tk = NSA_TQ, NSA_TK
    nq = S // tq
    nc = kc.shape[2]
    m = NSA_GROUP * tq
    hkv = NSA_KV_HEADS
    tok = lambda: pl.BlockSpec((1, hkv, S, LANE), lambda b, t: (b, 0, 0, 0))
    feat = lambda: pl.BlockSpec((1, hkv, NSA_DK, S), lambda b, t: (b, 0, 0, 0))
    return pl.pallas_call(
        functools.partial(_nsa_kernel, tq=tq, tk=tk),
        grid=(B, nq),
        in_specs=[pl.BlockSpec((1, NSA_HEADS, NSA_DK, tq), lambda b, t: (b, 0, 0, t)),
                  tok(), feat(), tok(), feat(),
                  pl.BlockSpec((1, hkv, nc, LANE), lambda b, t: (b, 0, 0, 0)),
                  pl.BlockSpec((1, hkv, NSA_DK, nc), lambda b, t: (b, 0, 0, 0)),
                  pl.BlockSpec((1, hkv, GATE_ROWS, tq), lambda b, t: (b, 0, 0, t)),
                  pl.BlockSpec((hkv, 1, nc, m), lambda b, t: (0, t, 0, 0)),
                  _const_spec(tz.shape),
                  _const_spec(ovt.shape)],
        out_specs=pl.BlockSpec((1, tq, NSA_HEADS * NSA_DK), lambda b, t: (b, t, 0)),
        out_shape=jax.ShapeDtypeStruct((B, S, NSA_HEADS * NSA_DK), jnp.float32),
        scratch_shapes=[pltpu.VMEM((hkv, NSA_DK + SLC_ROWS, m), jnp.bfloat16),
                        pltpu.VMEM((hkv, NSA_DK + SLC_ROWS, m), jnp.bfloat16),
                        pltpu.VMEM((hkv, 1, m), jnp.float32),
                        pltpu.VMEM((hkv, NSA_DK + ONES_ROWS, m), jnp.float32),
                        pltpu.VMEM((hkv, 1, m), jnp.float32),
                        pltpu.VMEM((hkv, NSA_DK + ONES_ROWS, m), jnp.float32),
                        pltpu.VMEM((hkv, NSA_DK, m), jnp.float32),
                        pltpu.VMEM((hkv, SLC_ROWS, tq), jnp.float32)],
        compiler_params=_params(2), name="nsa",
    )(q_t, ks, vs_t, kw, vw_t, kc, vc_t, gate_t, bias_c, tz, ovt)


def _post_kernel(x_ref, om_ref, on_ref, gm_ref, gn_ref, wo_ref, gf_ref, wg_ref, wu_ref, wd_ref,
                 gfin_ref, o_ref):
    half = om_ref.shape[2]
    mix_m = _rms(om_ref[0], gm_ref[...]).astype(jnp.bfloat16)
    mix_n = _rms(on_ref[0], gn_ref[...]).astype(jnp.bfloat16)
    h = x_ref[0] + _dot(mix_m, wo_ref[0:half, :]) + _dot(mix_n, wo_ref[half:2 * half, :])
    f = _rms(h, gf_ref[...]).astype(jnp.bfloat16)
    a = _dot(f, wg_ref[...])
    act = (a * jax.nn.sigmoid(a) * _dot(f, wu_ref[...])).astype(jnp.bfloat16)
    h = h + _dot(act, wd_ref[...])
    o_ref[0] = _rms(h, gfin_ref[...])


def _post(x, o_mla, o_nsa, gm, gn, w_out, gf, wg, wu, wd, gfin):
    B, S, D = x.shape
    tm = min(POST_TM, S)
    tok = lambda w: pl.BlockSpec((1, tm, w), lambda b, i: (b, i, 0))
    consts = (gm, gn, w_out, gf, wg, wu, wd, gfin)
    return pl.pallas_call(
        _post_kernel, grid=(B, S // tm),
        in_specs=[tok(D), tok(o_mla.shape[2]), tok(o_nsa.shape[2])] + [_const_spec(c.shape) for c in consts],
        out_specs=tok(D), out_shape=jax.ShapeDtypeStruct((B, S, D), jnp.float32),
        compiler_params=_params(2), name="post",
    )(x, o_mla, o_nsa, *consts)


def _rope_tables(S):
    pos = jnp.arange(S, dtype=jnp.float32)
    inv = ROPE_THETA ** (-jnp.arange(0, MLA_ROPE, 2, dtype=jnp.float32) / MLA_ROPE)
    ang = pos[:, None] * inv[None, :]
    cos, sin = jnp.cos(ang), jnp.sin(ang)
    cos2 = jnp.concatenate([cos, cos], axis=-1)
    sin2 = jnp.concatenate([sin, sin], axis=-1)
    scale = (MLA_NOPE + MLA_ROPE) ** -0.5 * LOG2E
    zq = jnp.zeros((S, LANE - MLA_NOPE - MLA_ROPE), jnp.float32)
    cosq = jnp.concatenate([jnp.ones((S, MLA_NOPE), jnp.float32), cos2, zq], axis=-1) * scale
    sinq = jnp.concatenate([jnp.zeros((S, MLA_NOPE), jnp.float32), sin2, zq], axis=-1) * scale
    zk0 = jnp.zeros((S, MLA_NOPE), jnp.float32)
    cosk = jnp.concatenate([zk0, cos2, zq], axis=-1)
    sink = jnp.concatenate([zk0, sin2, zq], axis=-1)
    return cosq.T, sinq.T, cosk, sink


def _overlap_t(S):
    n_cmp = (S - CMP_LEN) // CMP_STRIDE + 1
    n_slc = S // SLC_LEN
    assert n_slc <= SLC_ROWS
    cs = np.arange(n_cmp) * CMP_STRIDE
    ss = np.arange(n_slc) * SLC_LEN
    ov = np.maximum(0, np.minimum(cs[:, None] + CMP_LEN, ss[None, :] + SLC_LEN)
                    - np.maximum(cs[:, None], ss[None, :])).astype(np.float32) / CMP_STRIDE
    out = np.zeros((SLC_ROWS, S // CMP_STRIDE), np.float32)
    out[:n_slc, :n_cmp] = ov.T
    return jnp.asarray(out, jnp.bfloat16)


def kernel(x, norm_mix_g, w_in, mla_q_norm_g, mla_w_uq, mla_kv_norm_g, mla_w_ukv, nsa_cmp_pos_k, nsa_cmp_w1_k, nsa_cmp_w2_k, nsa_cmp_pos_v, nsa_cmp_w1_v, nsa_cmp_w2_v, t5_table, out_norm_mla_g, out_norm_nsa_g, w_out, norm_ffn_g, w_gate, w_up, w_down, final_norm_g):
    B, S, D = x.shape
    assert w_in.shape[0] == 1
    assert D == D_MODEL and S % NSA_TQ == 0 and S % CMP_STRIDE == 0
    bf = jnp.bfloat16
    l = 0
    cosq_t, sinq_t, cosk, sink = _rope_tables(S)
    bias_c, tz = _bias_tables(t5_table, S)
    ovt = _overlap_t(S)
    row = lambda v: v.reshape(1, -1)

    w_row = _w_in_row_layout().apply(w_in[l])
    w_col = _w_in_col_layout().apply(w_in[l]).T
    w_q2 = _w_uq_layout().apply(mla_w_uq[l]).T
    w_k = _w_uk_layout().apply(mla_w_ukv[l])
    w_v = _w_uv_layout().apply(mla_w_ukv[l]).T
    (q_mla, k_mla, v_mla, q_nsa, kv_cmp, k_slc, v_slc, k_win, v_win, gates) = _proj(
        x, row(norm_mix_g[l]), w_row, w_col, row(mla_q_norm_g[l]), w_q2, row(mla_kv_norm_g[l]), w_k, w_v,
        cosq_t, sinq_t, cosk, sink)

    def per_offset(pos_l, w1_l):
        pos2 = jnp.tile(pos_l, (1, NSA_KV_HEADS)).reshape(CMP_LEN, 1, NSA_KV_HEADS * NSA_DK)
        w = w1_l.reshape(CMP_LEN, NSA_DK, CMP_HIDDEN)
        z = jnp.zeros_like(w)
        return pos2, jnp.concatenate([jnp.concatenate([w, z], axis=2), jnp.concatenate([z, w], axis=2)], axis=1)

    pos_k, w1_k = per_offset(nsa_cmp_pos_k[l], nsa_cmp_w1_k[l])
    pos_v, w1_v = per_offset(nsa_cmp_pos_v[l], nsa_cmp_w1_v[l])
    pos = jnp.stack([pos_k, pos_v])
    w1 = jnp.stack([w1_k, w1_v]).astype(bf)
    w2k =jnp.concatenate([nsa_cmp_w2_k[l], jnp.zeros((CMP_HIDDEN, LANE - NSA_DK), jnp.float32)],
                          axis=1).astype(bf)
    w2vt = nsa_cmp_w2_v[l].T.astype(bf)
    k_cmp, v_cmp = _compress(kv_cmp, pos, w1, w2k, w2vt)

    o_mla = _mla(q_mla, k_mla, v_mla)
    o_nsa = _nsa(q_nsa, k_slc, v_slc, k_win, v_win, k_cmp, v_cmp, gates, bias_c, tz, ovt)

    return _post(x, o_mla, o_nsa, row(out_norm_mla_g[l]), row(out_norm_nsa_g[l]), w_out[l].astype(bf),
                 row(norm_ffn_g[l]), w_gate[l].astype(bf), w_up[l].astype(bf), w_down[l].astype(bf),
                 row(final_norm_g))
```

```python
import functools
import math

import numpy as np
import jax
import jax.numpy as jnp
from jax import lax
from jax.experimental import pallas as pl
from jax.experimental.pallas import tpu as pltpu

D_MODEL = 1024
MLA_HEADS = 8
MLA_NOPE = 64
MLA_ROPE = 32
MLA_V = 64
MLA_Q_LORA = 256
MLA_KV_LORA = 128
ROPE_THETA = 10000.0
NSA_HEADS = 8
NSA_KV_HEADS = 2
NSA_GROUP = 4
NSA_DK = 64
CMP_LEN = 32
CMP_STRIDE = 16
CMP_HIDDEN = 128
SLC_LEN = 64
SLC_TOPN = 16
WINDOW = 512
T5_BUCKETS = 32
T5_MAX_DIST = 128
D_FF = 2816
EPS = 1e-6
NEG = -1e30
FORCE = 1e30

LANE = 128
SUBLANE = 8
SLC_ROWS = 64
GATE_ROWS = 16
ONES_ROWS = 16
LOG2E = math.log2(math.e)

PROJ_TM = 512
MLA_TK = 256
MLA_CW = 256
FLASH_LOOKAHEAD = 6
NSA_TQ = 256
NSA_TK = 256
SEL_GROUP = 8
NSA_FAR_GROUP = 8
POST_TM = 512
V7X_VMEM_BYTES = 64 * 1024 * 1024
VMEM_LIMIT = V7X_VMEM_BYTES * 7 // 8

C_CQ = 0
C_CKV = 256
C_MISC = 384
C_KVCMP = 512
C_KSLC = 768
C_KWIN = 1024
D_ROW = 1280
R_QNSA = 0
R_VSLC = 512
R_VWIN = 640
R_GATE = 768
D_COL = 800


def _dot(a, b):
    return jnp.dot(a, b, preferred_element_type=jnp.float32)


def _dot_nt(a, b):
    return lax.dot_general(a, b, (((1,), (1,)), ((), ())), preferred_element_type=jnp.float32)


def _rms(x, g):
    return x * lax.rsqrt(jnp.mean(x * x, axis=-1, keepdims=True) + EPS) * g


def _const_spec(shape):
    nd = len(shape)
    return pl.BlockSpec(shape, lambda *_: (0,) * nd, pipeline_mode=pl.Buffered(1))


def _params(n_axes):
    return pltpu.CompilerParams(dimension_semantics=("parallel",) * n_axes,
                                vmem_limit_bytes=VMEM_LIMIT)


def _in_offsets():
    o_krope = MLA_Q_LORA + MLA_KV_LORA
    o_qnsa = o_krope + MLA_ROPE
    o_cmp = o_qnsa + NSA_HEADS * NSA_DK
    o_slc = o_cmp + 2 * NSA_KV_HEADS * NSA_DK
    o_win = o_slc + 2 * NSA_KV_HEADS * NSA_DK
    o_gate = o_win + 2 * NSA_KV_HEADS * NSA_DK
    return o_krope, o_qnsa, o_cmp, o_slc, o_win, o_gate


class _Layout:
    def __init__(self, n):
        self.src = np.zeros((n,), np.int32)
        self.mul = np.zeros((n,), np.float32)

    def put(self, dst, s, n, m=1.0):
        self.src[dst:dst + n] = np.arange(s, s + n)
        self.mul[dst:dst + n] = m

    def apply(self, w):
        return (w[:, self.src] * self.mul[None, :]).astype(jnp.bfloat16)


def _w_in_row_layout():
    o_krope, _, o_cmp, o_slc, o_win, _ = _in_offsets()
    lay = _Layout(D_ROW)
    lay.put(C_CQ, 0, MLA_Q_LORA)
    lay.put(C_CKV, MLA_Q_LORA, MLA_KV_LORA)
    half = MLA_ROPE // 2
    lay.put(C_MISC + MLA_NOPE, o_krope, MLA_ROPE)
    lay.put(C_MISC + MLA_NOPE + MLA_ROPE, o_krope + half, half, -1.0)
    lay.put(C_MISC + MLA_NOPE + MLA_ROPE + half, o_krope, half, 1.0)
    lay.put(C_KVCMP, o_cmp, 2 * NSA_KV_HEADS * NSA_DK)
    for hk in range(NSA_KV_HEADS):
        lay.put(C_KSLC + hk * LANE, o_slc + hk * NSA_DK, NSA_DK)
        lay.put(C_KWIN + hk * LANE, o_win + hk * NSA_DK, NSA_DK)
    return lay


def _w_in_col_layout():
    _, o_qnsa, _, o_slc, o_win, o_gate = _in_offsets()
    lay = _Layout(D_COL)
    lay.put(R_QNSA, o_qnsa, NSA_HEADS * NSA_DK, NSA_DK ** -0.5)
    lay.put(R_VSLC, o_slc + NSA_KV_HEADS * NSA_DK, NSA_KV_HEADS * NSA_DK)
    lay.put(R_VWIN, o_win + NSA_KV_HEADS * NSA_DK, NSA_KV_HEADS * NSA_DK)
    for hk in range(NSA_KV_HEADS):
        lay.put(R_GATE + hk * GATE_ROWS, o_gate + hk * NSA_GROUP * 3, NSA_GROUP * 3)
    return lay


def _w_uq_layout():
    lay = _Layout(MLA_HEADS * LANE)
    dq = MLA_NOPE + MLA_ROPE
    for h in range(MLA_HEADS):
        lay.put(h * LANE, h * dq, dq)
    return lay


def _w_uk_layout():
    lay = _Layout(MLA_HEADS * LANE)
    per = MLA_NOPE + MLA_V
    for h in range(MLA_HEADS):
        lay.put(h * LANE, h * per, MLA_NOPE)
    return lay


def _w_uv_layout():
    lay = _Layout(MLA_HEADS * MLA_V)
    per = MLA_NOPE + MLA_V
    for h in range(MLA_HEADS):
        lay.put(h * MLA_V, h * per + MLA_NOPE, MLA_V)
    return lay


def _t5_thresholds(max_dist):
    n = np.arange(max_dist, dtype=np.int64)
    max_exact = T5_BUCKETS // 2
    nf = np.maximum(n, 1).astype(np.float32)
    val = (np.log(nf / np.float32(max_exact)) / np.float32(math.log(T5_MAX_DIST / max_exact))
           * np.float32(T5_BUCKETS - max_exact))
    large = np.minimum(max_exact + val.astype(np.int32), T5_BUCKETS - 1)
    bucket = np.where(n < max_exact, n, large)
    assert np.all(np.diff(bucket) >= 0)
    frac = np.abs(val[max_exact + 1:T5_MAX_DIST] - np.round(val[max_exact + 1:T5_MAX_DIST]))
    assert frac.min() > 1e-3
    return [int(np.argmax(bucket >= b)) for b in range(T5_BUCKETS)]


def _proj_kernel(x_ref, gmix_ref, wrow_ref, wcol_ref, gq_ref, wq_ref, gkv_ref, wk_ref, wv_ref,
                 cq_ref, sq_ref, ck_ref, sk_ref,
                 qm_ref, km_ref, vm_ref, qn_ref, kvc_ref, ks_ref, vs_ref, kw_ref, vw_ref, gate_ref):
    tm = x_ref.shape[1]
    bf = jnp.bfloat16
    xn = _rms(x_ref[0], gmix_ref[...]).astype(bf)
    u = _dot(xn, wrow_ref[...])
    ut = _dot_nt(wcol_ref[...], xn)

    cqn = _rms(u[:, C_CQ:C_CQ + MLA_Q_LORA], gq_ref[...]).astype(bf)
    qq = _dot_nt(wq_ref[...], cqn)
    cosq = cq_ref[...]
    sinq = sq_ref[...]
    r0, r1, r2 = MLA_NOPE, MLA_NOPE + MLA_ROPE // 2, MLA_NOPE + MLA_ROPE
    for h in range(MLA_HEADS):
        q = qq[h * LANE:(h + 1) * LANE]
        rot = jnp.concatenate([q[0:r0], -q[r1:r2], q[r0:r1], q[r2:]], axis=0)
        qm_ref[0, h] = (q * cosq + rot * sinq).astype(bf)

    ckvn = _rms(u[:, C_CKV:C_CKV + MLA_KV_LORA], gkv_ref[...]).astype(bf)
    kk = _dot(ckvn, wk_ref[...])
    misc = u[:, C_MISC:C_MISC + LANE]
    kpe = misc * ck_ref[...] + pltpu.roll(misc, LANE - MLA_ROPE, axis=1) * sk_ref[...]
    for h in range(MLA_HEADS):
        km_ref[0, h] = (kk[:, h * LANE:(h + 1) * LANE] + kpe).astype(bf)
    vv = _dot_nt(wv_ref[...], ckvn)
    for h in range(MLA_HEADS):
        vm_ref[0, h] = vv[h * MLA_V:(h + 1) * MLA_V].astype(bf)

    for h in range(NSA_HEADS):
        qn_ref[0, h] = (ut[R_QNSA + h * NSA_DK:R_QNSA + (h + 1) * NSA_DK] * LOG2E).astype(bf)
    for kv in range(2):
        kvc_ref[0, kv] = u[:, C_KVCMP + kv * LANE:C_KVCMP + (kv + 1) * LANE]
    pos = pl.program_id(1) * tm + lax.broadcasted_iota(jnp.int32, (tm, LANE), 0)
    lane = lax.broadcasted_iota(jnp.int32, (tm, LANE), 1)
    onehot = jnp.where(lane - (LANE - SLC_ROWS) == pos // SLC_LEN, 1.0, 0.0)
    for hk in range(NSA_KV_HEADS):
        ks_ref[0, hk] = (u[:, C_KSLC + hk * LANE:C_KSLC + (hk + 1) * LANE] + onehot).astype(bf)
        kw_ref[0, hk] = u[:, C_KWIN + hk * LANE:C_KWIN + (hk + 1) * LANE].astype(bf)
        vs_ref[0, hk] = ut[R_VSLC + hk * NSA_DK:R_VSLC + (hk + 1) * NSA_DK].astype(bf)
        vw_ref[0, hk] = ut[R_VWIN + hk * NSA_DK:R_VWIN + (hk + 1) * NSA_DK].astype(bf)
        gate_ref[0, hk] = jax.nn.sigmoid(ut[R_GATE + hk * GATE_ROWS:R_GATE + (hk + 1) * GATE_ROWS])


def _proj(x, gmix, w_row, w_col, gq, w_q2, gkv, w_k, w_v, cosq_t, sinq_t, cosk, sink):
    B, S, D = x.shape
    tm = min(PROJ_TM, S)
    bf = jnp.bfloat16
    grid = (B, S // tm)
    tok_major = lambda n: pl.BlockSpec((1, n, tm, LANE), lambda b, i: (b, 0, i, 0))
    feat_major = lambda n, d: pl.BlockSpec((1, n, d, tm), lambda b, i: (b, 0, 0, i))
    out_shape = (
        jax.ShapeDtypeStruct((B, MLA_HEADS, LANE, S), bf),
        jax.ShapeDtypeStruct((B, MLA_HEADS, S, LANE), bf),
        jax.ShapeDtypeStruct((B, MLA_HEADS, MLA_V, S), bf),
        jax.ShapeDtypeStruct((B, NSA_HEADS, NSA_DK, S), bf),
        jax.ShapeDtypeStruct((B, 2, S, LANE), jnp.float32),
        jax.ShapeDtypeStruct((B, NSA_KV_HEADS, S, LANE), bf),
        jax.ShapeDtypeStruct((B, NSA_KV_HEADS, NSA_DK, S), bf),
        jax.ShapeDtypeStruct((B, NSA_KV_HEADS, S, LANE), bf),
        jax.ShapeDtypeStruct((B, NSA_KV_HEADS, NSA_DK, S), bf),
        jax.ShapeDtypeStruct((B, NSA_KV_HEADS, GATE_ROWS, S), jnp.float32),
    )
    out_specs = (
        feat_major(MLA_HEADS, LANE), tok_major(MLA_HEADS), feat_major(MLA_HEADS, MLA_V),
        feat_major(NSA_HEADS, NSA_DK),
        tok_major(2),
        tok_major(NSA_KV_HEADS), feat_major(NSA_KV_HEADS, NSA_DK),
        tok_major(NSA_KV_HEADS), feat_major(NSA_KV_HEADS, NSA_DK),
        feat_major(NSA_KV_HEADS, GATE_ROWS),
    )
    consts = (gmix, w_row, w_col, gq, w_q2, gkv, w_k, w_v)
    in_specs = ([pl.BlockSpec((1, tm, D), lambda b, i: (b, i, 0))] + [_const_spec(c.shape) for c in consts]
                + [pl.BlockSpec((LANE, tm), lambda b, i: (0, i)), pl.BlockSpec((LANE, tm), lambda b, i: (0, i)),
                   pl.BlockSpec((tm, LANE), lambda b, i: (i, 0)), pl.BlockSpec((tm, LANE), lambda b, i: (i, 0))])
    return pl.pallas_call(
        _proj_kernel, grid=grid, in_specs=in_specs, out_specs=out_specs, out_shape=out_shape,
        compiler_params=_params(2), name="proj",
    )(x, *consts, cosq_t, sinq_t, cosk, sink)


def _compress_kernel(c_ref, pos_ref, w1_ref, w2k_ref, w2vt_ref, kc_ref, vc_ref):
    nc = kc_ref.shape[2]
    bf = jnp.bfloat16
    hw = NSA_KV_HEADS * NSA_DK
    for kv in range(2):
        halves = []
        for half in range(CMP_LEN // CMP_STRIDE):
            acc = None
            for r in range(CMP_STRIDE):
                l = half * CMP_STRIDE + r
                x = c_ref[0, kv, pl.ds(r, nc, stride=CMP_STRIDE), :]
                part = _dot((x + pos_ref[kv, l]).astype(bf), w1_ref[kv, l])
                acc = part if acc is None else acc + part
            halves.append(acc)
        hid = halves[0] + pltpu.roll(halves[1], nc - 1, axis=0)
        act = jax.nn.gelu(hid).astype(bf)
        for hk in range(NSA_KV_HEADS):
            act_h = act[:, hk * CMP_HIDDEN:(hk + 1) * CMP_HIDDEN]
            if kv == 0:
                kc_ref[0, hk] = _dot(act_h, w2k_ref[...]).astype(bf)
            else:
                vc_ref[0, hk] = _dot_nt(w2vt_ref[...], act_h).astype(bf)


def _compress(kvc, pos, w1, w2k, w2vt):
    B, _, S, width = kvc.shape
    nc = S // CMP_STRIDE
    return pl.pallas_call(
        _compress_kernel, grid=(B,),
        in_specs=[pl.BlockSpec((1, 2, S, width), lambda b: (b, 0, 0, 0)),
                  _const_spec(pos.shape), _const_spec(w1.shape), _const_spec(w2k.shape),
                  _const_spec(w2vt.shape)],
        out_specs=(pl.BlockSpec((1, NSA_KV_HEADS, nc, LANE), lambda b: (b, 0, 0, 0)),
                   pl.BlockSpec((1, NSA_KV_HEADS, NSA_DK, nc), lambda b: (b, 0, 0, 0))),
        out_shape=(jax.ShapeDtypeStruct((B, NSA_KV_HEADS, nc, LANE), jnp.bfloat16),
                   jax.ShapeDtypeStruct((B, NSA_KV_HEADS, NSA_DK, nc), jnp.bfloat16)),
        compiler_params=_params(1), name="compress",
    )(kvc, pos, w1, w2k, w2vt)


def _t5_lookup(dist, tab_ref, head, thr):
    val = jnp.full(dist.shape, tab_ref[0, head], jnp.float32)
    for b in range(1, T5_BUCKETS):
        val = jnp.where(dist >= thr[b], tab_ref[b, head], val)
    return (val - tab_ref[T5_BUCKETS - 1, head]) * LOG2E


def _bias_cmp_kernel(tab_ref, out_ref, *, thr, tq, n_cmp):
    hk = pl.program_id(0)
    t = pl.program_id(1)
    nc = out_ref.shape[2]
    per_tile = tq // CMP_STRIDE
    band = per_tile + (T5_MAX_DIST + CMP_LEN) // CMP_STRIDE + SUBLANE
    band = min(-(-band // SUBLANE) * SUBLANE, nc)
    start = jnp.clip(t * per_tile - (band - per_tile), 0, nc - band)
    start = pl.multiple_of(start // SUBLANE * SUBLANE, SUBLANE)
    n_all = lax.broadcasted_iota(jnp.int32, (nc, NSA_GROUP * tq), 0)
    out_ref[0, 0] = jnp.where(n_all < start, 0.0, NEG)
    n = start + lax.broadcasted_iota(jnp.int32, (band, tq), 0)
    i = lax.broadcasted_iota(jnp.int32, (band, tq), 1)
    dist = t * tq + i - (n * CMP_STRIDE + CMP_LEN - 1)
    ok = jnp.logical_and(dist >= 0, n < n_cmp)
    for g in range(NSA_GROUP):
        val = _t5_lookup(dist, tab_ref, hk * NSA_GROUP + g, thr)
        out_ref[0, 0, pl.ds(start, band), g * tq:(g + 1) * tq] = jnp.where(ok, val, NEG)


def _bias_tile_kernel(tab_ref, out_ref, *, thr, tq):
    hk = pl.program_id(0)
    tk = out_ref.shape[2]
    j = lax.broadcasted_iota(jnp.int32, (tk, tq), 0)
    i = lax.broadcasted_iota(jnp.int32, (tk, tq), 1)
    for g in range(NSA_GROUP):
        head = hk * NSA_GROUP + g
        cols = slice(g * tq, (g + 1) * tq)
        d0 = i - j
        out_ref[0, 0, :, cols] = jnp.where(d0 >= 0, _t5_lookup(d0, tab_ref, head, thr), NEG)
        out_ref[0, 1, :, cols] = _t5_lookup(d0 + tk, tab_ref, head, thr)
        out_ref[0, 2, :, cols] = jnp.where(j > i, 0.0, NEG)
        out_ref[0, 3, :, cols] = jnp.full((tk, tq), NEG, jnp.float32)


def _bias_tables(t5_table, S):
    thr = _t5_thresholds(S)
    tq, tk = NSA_TQ, NSA_TK
    assert tq == tk and WINDOW == 2 * tk and tk >= T5_MAX_DIST
    nq = S // tq
    nc = S // CMP_STRIDE
    n_cmp = (S - CMP_LEN) // CMP_STRIDE + 1
    m = NSA_GROUP * tq
    smem = pl.BlockSpec(memory_space=pltpu.SMEM)
    bias_c = pl.pallas_call(
        functools.partial(_bias_cmp_kernel, thr=thr, tq=tq, n_cmp=n_cmp),
        grid=(NSA_KV_HEADS, nq), in_specs=[smem],
        out_specs=pl.BlockSpec((1, 1, nc, m), lambda h, t: (h, t, 0, 0)),
        out_shape=jax.ShapeDtypeStruct((NSA_KV_HEADS, nq, nc, m), jnp.float32),
        compiler_params=_params(2), name="bias_cmp",
    )(t5_table)
    tz = pl.pallas_call(
        functools.partial(_bias_tile_kernel, thr=thr, tq=tq),
        grid=(NSA_KV_HEADS,), in_specs=[smem],
        out_specs=pl.BlockSpec((1, 4, tk, m), lambda h: (h, 0, 0, 0)),
        out_shape=jax.ShapeDtypeStruct((NSA_KV_HEADS, 4, tk, m), jnp.float32),
        compiler_params=_params(1), name="bias_tile",
    )(t5_table)
    return bias_c, tz


def _pipeline(stages):
    pending = [issue() for issue, _ in stages[:FLASH_LOOKAHEAD]]
    for i, (_, consume) in enumerate(stages):
        s = pending.pop(0)
        if i + FLASH_LOOKAHEAD < len(stages):
            pending.append(stages[i + FLASH_LOOKAHEAD][0]())
        consume(s)


def _flash_stage(chain):
    k, q_t, v_aug, bias_t, m_ref, acc_ref = chain

    def issue():
        s = _dot(k, q_t)
        return s if bias_t is None else s + bias_t

    def consume(s):
        m_prev = m_ref[...]
        m_new = jnp.maximum(m_prev, jnp.max(s, axis=0, keepdims=True))
        alpha = jnp.exp2(m_prev - m_new)
        p = jnp.exp2(s - m_new).astype(jnp.bfloat16)
        acc_ref[...] = alpha * acc_ref[...] + _dot(v_aug, p)
        m_ref[...] = m_new

    return issue, consume


def _flash_chains(chains):
    _pipeline([_flash_stage(c) for c in chains])


def _with_ones(v_t):
    tk = v_t.shape[1]
    row = lax.broadcasted_iota(jnp.int32, (ONES_ROWS, tk), 0)
    return jnp.concatenate([v_t, jnp.where(row == 0, 1.0, 0.0).astype(v_t.dtype)], axis=0)


def _col_chains(k, q_ref, v_aug, bias_ref, m_ref, acc_ref, ncol, width):
    chains = []
    for c0 in range(0, ncol, width):
        cols = slice(c0, c0 + width)
        chains.append((k, q_ref[:, cols], v_aug, None if bias_ref is None else bias_ref[:, cols],
                       m_ref.at[:, cols], acc_ref.at[:, cols]))
    return chains


def _flash_init(m_ref, acc_ref):
    m_ref[...] = jnp.full(m_ref.shape, -jnp.inf, jnp.float32)
    acc_ref[...] = jnp.zeros(acc_ref.shape, jnp.float32)


def _flash_result(acc_ref, dv):
    return acc_ref[0:dv, :] * (1.0 / acc_ref[dv:dv + 1, :])


def _mla_kernel(q_ref, k_ref, v_ref, o_ref, m_ref, acc_ref, *, tk, cw):
    S = k_ref.shape[2]
    assert tk == cw
    kk = lax.broadcasted_iota(jnp.int32, (tk, cw), 0)
    qq = lax.broadcasted_iota(jnp.int32, (tk, cw), 1)
    causal = jnp.where(kk <= qq, 0.0, NEG)
    for e in range(2):
        _flash_init(m_ref.at[e], acc_ref.at[e])
    chains = []
    for k0 in range(0, S, tk):
        for e in range(2):
            k = k_ref[0, e, k0:k0 + tk, :]
            v_aug = _with_ones(v_ref[0, e, :, k0:k0 + tk])
            for c0 in range(0, S, cw):
                if k0 > c0 + cw - 1:
                    continue
                bias = causal if k0 + tk - 1 > c0 else None
                cols = slice(c0, c0 + cw)
                chains.append((k, q_ref[0, e, :, cols], v_aug, bias,
                               m_ref.at[e, :, cols], acc_ref.at[e, :, cols]))
    _flash_chains(chains)
    o_t = jnp.concatenate([_flash_result(acc_ref.at[e], MLA_V) for e in range(2)], axis=0)
    o_ref[0] = jnp.transpose(o_t)


def _mla(q_t, k, v_t):
    B, H, S, _ = k.shape
    tq = S
    tk = min(MLA_TK, S)
    cw = min(MLA_CW, S)
    return pl.pallas_call(
        functools.partial(_mla_kernel, tk=tk, cw=cw),
        grid=(B, H // 2),
        in_specs=[pl.BlockSpec((1, 2, LANE, S), lambda b, p: (b, p, 0, 0)),
                  pl.BlockSpec((1, 2, S, LANE), lambda b, p: (b, p, 0, 0)),
                  pl.BlockSpec((1, 2, MLA_V, S), lambda b, p: (b, p, 0, 0))],
        out_specs=pl.BlockSpec((1, S, 2 * MLA_V), lambda b, p: (b, 0, p)),
        out_shape=jax.ShapeDtypeStruct((B, S, H * MLA_V), jnp.float32),
        scratch_shapes=[pltpu.VMEM((2, 1, tq), jnp.float32),
                        pltpu.VMEM((2, MLA_V + ONES_ROWS, tq), jnp.float32)],
        compiler_params=_params(2), name="mla",
    )(q_t, k, v_t)


def _nsa_kernel(q_ref, ks_ref, vs_ref, kw_ref, vw_ref, kc_ref, vc_ref, gate_ref, bc_ref, tz_ref, ovt_ref,
                o_ref, qaug_ref, qpad_ref, m_ref, acc_ref, mw_ref, accw_ref, oc_ref, imp_ref, *, tq, tk):
    t = pl.program_id(1)
    G = NSA_GROUP
    HK = range(NSA_KV_HEADS)
    ncol = G * tq
    bf = jnp.bfloat16
    q0 = t * tq
    for hk in HK:
        for g in range(G):
            qpad_ref[hk, 0:NSA_DK, g * tq:(g + 1) * tq] = q_ref[0, hk * G + g]
            qaug_ref[hk, 0:NSA_DK, g * tq:(g + 1) * tq] = q_ref[0, hk * G + g]
        qpad_ref[hk, NSA_DK:, :] = jnp.zeros((SLC_ROWS, ncol), bf)

    def tile_chains(k_ref, v_ref, qx_ref, c, bias_idx, stats):
        k0 = pl.multiple_of(c * tk, tk)
        chains = []
        for hk in HK:
            bias_ref = None if bias_idx is None else tz_ref.at[hk, bias_idx]
            chains += _col_chains(k_ref[0, hk, pl.ds(k0, tk), :], qx_ref.at[hk],
                                  _with_ones(v_ref[0, hk, :, pl.ds(k0, tk)]),
                                  bias_ref, stats[0].at[hk], stats[1].at[hk], ncol, tq)
        return chains

    prev1 = jnp.maximum(t - 1, 0)
    bias1 = jnp.where(t >= 1, 1, 3)

    nb = ovt_ref.shape[0]
    sub = SUBLANE
    col_ok = q0 + lax.broadcasted_iota(jnp.int32, (1, tq), 1) >= CMP_LEN - 1
    jb = lax.broadcasted_iota(jnp.int32, (nb, tq), 0)
    cur = (q0 + lax.broadcasted_iota(jnp.int32, (nb, tq), 1)) // SLC_LEN
    forced = jnp.logical_or(jb == 0, jnp.logical_or(jb == cur, jb == cur - 1))
    jsub = lax.broadcasted_iota(jnp.int32, (sub, tq), 0)
    imp_sum = [None] * NSA_KV_HEADS

    def cmp_stage(hk, g):
        cols = slice(g * tq, (g + 1) * tq)

        def issue():
            return _dot(kc_ref[0, hk], qpad_ref[hk, :, cols]) + bc_ref[hk, 0, :, cols]

        def consume(s):
            mx = jnp.max(s, axis=0, keepdims=True)
            p = jnp.exp2(s - mx)
            lsum = jnp.sum(p, axis=0, keepdims=True)
            pcb = (p * jnp.where(col_ok, 1.0 / lsum, 0.0)).astype(bf)
            both = _dot(jnp.concatenate([vc_ref[0, hk], ovt_ref[...]], axis=0), pcb)
            oc_ref[hk, :, cols] = both[0:NSA_DK]
            imp_sum[hk] = both[NSA_DK:] if imp_sum[hk] is None else imp_sum[hk] + both[NSA_DK:]

        return issue, consume

    win = (mw_ref, accw_ref)
    _flash_init(*win)
    win_stages = [_flash_stage(c) for c in
                  tile_chains(kw_ref, vw_ref, qpad_ref, jnp.maximum(t - 2, 0), jnp.where(t >= 2, 2, 3), win)
                  + tile_chains(kw_ref, vw_ref, qpad_ref, prev1, bias1, win)
                  + tile_chains(kw_ref, vw_ref, qpad_ref, t, 0, win)]
    cmp_stages = [cmp_stage(hk, g) for g in range(G) for hk in HK]
    every = len(win_stages) // len(cmp_stages)
    stages = []
    for i, st in enumerate(cmp_stages):
        stages += [st] + win_stages[i * every:(i + 1) * every]
    _pipeline(stages + win_stages[len(cmp_stages) * every:])
    for hk in HK:
        imp_ref[hk] = jnp.where(forced, FORCE, jnp.where(jb <= cur, imp_sum[hk], NEG))

    tiles_per_group = SEL_GROUP * SLC_LEN // tq
    n_groups = nb // SEL_GROUP

    def write_mask(hk, selb_rows):
        selb = jnp.concatenate(selb_rows, axis=0).astype(bf)
        for g in range(G):
            qaug_ref[hk, NSA_DK:, g * tq:(g + 1) * tq] = selb

    def select(n_act):
        tail = [jnp.full((nb - n_act, tq), NEG, jnp.float32)] if n_act < nb else []
        for hk in HK:
            imp = imp_ref[hk, 0:n_act, :]
            if n_act <= SLC_TOPN:
                write_mask(hk, [jnp.where(imp > 0.5 * NEG, 0.0, NEG)] + tail)
                continue
            slabs = [imp[r0 * sub:(r0 + 1) * sub] for r0 in range(n_act // sub)]
            ranks = [jnp.zeros((sub, tq), jnp.int32) for _ in slabs]
            for jp in range(n_act):
                rowv = imp[jp:jp + 1, :]
                for r0, slab in enumerate(slabs):
                    lo = r0 * sub
                    if lo > jp:
                        one = jnp.where(rowv >= slab, 1, 0)
                    elif lo + sub - 1 <= jp:
                        one = jnp.where(rowv > slab, 1, 0)
                    else:
                        one = jnp.where(jsub + lo > jp, jnp.where(rowv >= slab, 1, 0),
                                        jnp.where(rowv > slab, 1, 0))
                    ranks[r0] = ranks[r0] + one
            write_mask(hk, [jnp.where(r < SLC_TOPN, 0.0, NEG) for r in ranks] + tail)

    for grp in range(n_groups):
        pl.when(t // tiles_per_group == grp)(functools.partial(select, SEL_GROUP * (grp + 1)))

    slc = (m_ref, acc_ref)
    _flash_init(*slc)
    n_far = jnp.maximum(t - 1, 0)

    def far(c, carry):
        chains = []
        for j in range(NSA_FAR_GROUP):
            chains += tile_chains(ks_ref, vs_ref, qaug_ref, c * NSA_FAR_GROUP + j, None, slc)
        _flash_chains(chains)
        return carry

    lax.fori_loop(0, n_far // NSA_FAR_GROUP, far, 0)
    rest0 = n_far // NSA_FAR_GROUP * NSA_FAR_GROUP

    def combine():
        for hk in HK:
            gate = gate_ref[0, hk]
            comb = []
            for g in range(G):
                cols = slice(g * tq, (g + 1) * tq)
                gr = lambda br: gate[g * 3 + br:g * 3 + br + 1, :]
                w_s = gr(1) / acc_ref[hk, NSA_DK:NSA_DK + 1, cols]
                w_w = gr(2) / accw_ref[hk, NSA_DK:NSA_DK + 1, cols]
                comb.append(gr(0) * oc_ref[hk, :, cols] + w_s * acc_ref[hk, 0:NSA_DK, cols]
                            + w_w * accw_ref[hk, 0:NSA_DK, cols])
            width = G * NSA_DK
            o_ref[0, :, hk * width:(hk + 1) * width] = jnp.transpose(jnp.concatenate(comb, axis=0))

    def tail(n_rest):
        chains = []
        for j in range(n_rest):
            chains += tile_chains(ks_ref, vs_ref, qaug_ref, rest0 + j, None, slc)
        _flash_chains(chains + tile_chains(ks_ref, vs_ref, qaug_ref, prev1, bias1, slc)
                      + tile_chains(ks_ref, vs_ref, qaug_ref, t, 0, slc))
        combine()

    for n_rest in range(NSA_FAR_GROUP):
        pl.when(n_far - rest0 == n_rest)(functools.partial(tail, n_rest))


def _nsa(q_t, ks, vs_t, kw, vw_t, kc, vc_t, gate_t, bias_c, tz, ovt):
    B, _, S, _ = ks.shape
    tq, tk = NSA_TQ, NSA_TK
    nq = S // tq
    nc = kc.shape[2]
    m = NSA_GROUP * tq
    hkv = NSA_KV_HEADS
    tok = lambda: pl.BlockSpec((1, hkv, S, LANE), lambda b, t: (b, 0, 0, 0))
    feat = lambda: pl.BlockSpec((1, hkv, NSA_DK, S), lambda b, t: (b, 0, 0, 0))
    return pl.pallas_call(
        functools.partial(_nsa_kernel, tq=tq, tk=tk),
        grid=(B, nq),
        in_specs=[pl.BlockSpec((1, NSA_HEADS, NSA_DK, tq), lambda b, t: (b, 0, 0, t)),
                  tok(), feat(), tok(), feat(),
                  pl.BlockSpec((1, hkv, nc, LANE), lambda b, t: (b, 0, 0, 0)),
                  pl.BlockSpec((1, hkv, NSA_DK, nc), lambda b, t: (b, 0, 0, 0)),
                  pl.BlockSpec((1, hkv, GATE_ROWS, tq), lambda b, t: (b, 0, 0, t)),
                  pl.BlockSpec((hkv, 1, nc, m), lambda b, t: (0, t, 0, 0)),
                  _const_spec(tz.shape),
                  _const_spec(ovt.shape)],
        out_specs=pl.BlockSpec((1, tq, NSA_HEADS * NSA_DK), lambda b, t: (b, t, 0)),
        out_shape=jax.ShapeDtypeStruct((B, S, NSA_HEADS * NSA_DK), jnp.float32),
        scratch_shapes=[pltpu.VMEM((hkv, NSA_DK + SLC_ROWS, m), jnp.bfloat16),
                        pltpu.VMEM((hkv, NSA_DK + SLC_ROWS, m), jnp.bfloat16),
                        pltpu.VMEM((hkv, 1, m), jnp.float32),
                        pltpu.VMEM((hkv, NSA_DK + ONES_ROWS, m), jnp.float32),
                        pltpu.VMEM((hkv, 1, m), jnp.float32),
                        pltpu.VMEM((hkv, NSA_DK + ONES_ROWS, m), jnp.float32),
                        pltpu.VMEM((hkv, NSA_DK, m), jnp.float32),
                        pltpu.VMEM((hkv, SLC_ROWS, tq), jnp.float32)],
        compiler_params=_params(2), name="nsa",
    )(q_t, ks, vs_t, kw, vw_t, kc, vc_t, gate_t, bias_c, tz, ovt)


def _post_kernel(x_ref, om_ref, on_ref, gm_ref, gn_ref, wo_ref, gf_ref, wg_ref, wu_ref, wd_ref,
                 gfin_ref, o_ref):
    half = om_ref.shape[2]
    mix_m = _rms(om_ref[0], gm_ref[...]).astype(jnp.bfloat16)
    mix_n = _rms(on_ref[0], gn_ref[...]).astype(jnp.bfloat16)
    h = x_ref[0] + _dot(mix_m, wo_ref[0:half, :]) + _dot(mix_n, wo_ref[half:2 * half, :])
    f = _rms(h, gf_ref[...]).astype(jnp.bfloat16)
    a = _dot(f, wg_ref[...])
    act = (a * jax.nn.sigmoid(a) * _dot(f, wu_ref[...])).astype(jnp.bfloat16)
    h = h + _dot(act, wd_ref[...])
    o_ref[0] = _rms(h, gfin_ref[...])


def _post(x, o_mla, o_nsa, gm, gn, w_out, gf, wg, wu, wd, gfin):
    B, S, D = x.shape
    tm = min(POST_TM, S)
    tok = lambda w: pl.BlockSpec((1, tm, w), lambda b, i: (b, i, 0))
    consts = (gm, gn, w_out, gf, wg, wu, wd, gfin)
    return pl.pallas_call(
        _post_kernel, grid=(B, S // tm),
        in_specs=[tok(D), tok(o_mla.shape[2]), tok(o_nsa.shape[2])] + [_const_spec(c.shape) for c in consts],
        out_specs=tok(D), out_shape=jax.ShapeDtypeStruct((B, S, D), jnp.float32),
        compiler_params=_params(2), name="post",
    )(x, o_mla, o_nsa, *consts)


def _rope_tables(S):
    pos = jnp.arange(S, dtype=jnp.float32)
    inv = ROPE_THETA ** (-jnp.arange(0, MLA_ROPE, 2, dtype=jnp.float32) / MLA_ROPE)
    ang = pos[:, None] * inv[None, :]
    cos, sin = jnp.cos(ang), jnp.sin(ang)
    cos2 = jnp.concatenate([cos, cos], axis=-1)
    sin2 = jnp.concatenate([sin, sin], axis=-1)
    scale = (MLA_NOPE + MLA_ROPE) ** -0.5 * LOG2E
    zq = jnp.zeros((S, LANE - MLA_NOPE - MLA_ROPE), jnp.float32)
    cosq = jnp.concatenate([jnp.ones((S, MLA_NOPE), jnp.float32), cos2, zq], axis=-1) * scale
    sinq = jnp.concatenate([jnp.zeros((S, MLA_NOPE), jnp.float32), sin2, zq], axis=-1) * scale
    zk0 = jnp.zeros((S, MLA_NOPE), jnp.float32)
    cosk = jnp.concatenate([zk0, cos2, zq], axis=-1)
    sink = jnp.concatenate([zk0, sin2, zq], axis=-1)
    return cosq.T, sinq.T, cosk, sink


def _overlap_t(S):
    n_cmp = (S - CMP_LEN) // CMP_STRIDE + 1
    n_slc = S // SLC_LEN
    assert n_slc <= SLC_ROWS
    cs = np.arange(n_cmp) * CMP_STRIDE
    ss = np.arange(n_slc) * SLC_LEN
    ov = np.maximum(0, np.minimum(cs[:, None] + CMP_LEN, ss[None, :] + SLC_LEN)
                    - np.maximum(cs[:, None], ss[None, :])).astype(np.float32) / CMP_STRIDE
    out = np.zeros((SLC_ROWS, S // CMP_STRIDE), np.float32)
    out[:n_slc, :n_cmp] = ov.T
    return jnp.asarray(out, jnp.bfloat16)


def kernel(x, norm_mix_g, w_in, mla_q_norm_g, mla_w_uq, mla_kv_norm_g, mla_w_ukv, nsa_cmp_pos_k, nsa_cmp_w1_k, nsa_cmp_w2_k, nsa_cmp_pos_v, nsa_cmp_w1_v, nsa_cmp_w2_v, t5_table, out_norm_mla_g, out_norm_nsa_g, w_out, norm_ffn_g, w_gate, w_up, w_down, final_norm_g):
    B, S, D = x.shape
    assert w_in.shape[0] == 1
    assert D == D_MODEL and S % NSA_TQ == 0 and S % CMP_STRIDE == 0
    bf = jnp.bfloat16
    l = 0
    cosq_t, sinq_t, cosk, sink = _rope_tables(S)
    bias_c, tz = _bias_tables(t5_table, S)
    ovt = _overlap_t(S)
    row = lambda v: v.reshape(1, -1)

    w_row = _w_in_row_layout().apply(w_in[l])
    w_col = _w_in_col_layout().apply(w_in[l]).T
    w_q2 = _w_uq_layout().apply(mla_w_uq[l]).T
    w_k = _w_uk_layout().apply(mla_w_ukv[l])
    w_v = _w_uv_layout().apply(mla_w_ukv[l]).T
    (q_mla, k_mla, v_mla, q_nsa, kv_cmp, k_slc, v_slc, k_win, v_win, gates) = _proj(
        x, row(norm_mix_g[l]), w_row, w_col, row(mla_q_norm_g[l]), w_q2, row(mla_kv_norm_g[l]), w_k, w_v,
        cosq_t, sinq_t, cosk, sink)

    def per_offset(pos_l, w1_l):
        pos2 = jnp.tile(pos_l, (1, NSA_KV_HEADS)).reshape(CMP_LEN, 1, NSA_KV_HEADS * NSA_DK)
        w = w1_l.reshape(CMP_LEN, NSA_DK, CMP_HIDDEN)
        z = jnp.zeros_like(w)
        return pos2, jnp.concatenate([jnp.concatenate([w, z], axis=2), jnp.concatenate([z, w], axis=2)], axis=1)

    pos_k, w1_k = per_offset(nsa_cmp_pos_k[l], nsa_cmp_w1_k[l])
    pos_v, w1_v = per_offset(nsa_cmp_pos_v[l], nsa_cmp_w1_v[l])
    pos = jnp.stack([pos_k, pos_v])
    w1 = jnp.stack([w1_k, w1_v]).astype(bf)
    w2k =jnp.concatenate([nsa_cmp_w2_k[l], jnp.zeros((CMP_HIDDEN, LANE - NSA_DK), jnp.float32)],
                          axis=1).astype(bf)
    w2vt = nsa_cmp_w2_v[l].T.astype(bf)
    k_cmp, v_cmp = _compress(kv_cmp, pos, w1, w2k, w2vt)

    o_mla = _mla(q_mla, k_mla, v_mla)
    o_nsa = _nsa(q_nsa, k_slc, v_slc, k_win, v_win, k_cmp, v_cmp, gates, bias_c, tz, ovt)

    return _post(x, o_mla, o_nsa, row(out_norm_mla_g[l]), row(out_norm_nsa_g[l]), w_out[l].astype(bf),
                 row(norm_ffn_g[l]), w_gate[l].astype(bf), w_up[l].astype(bf), w_down[l].astype(bf),
                 row(final_norm_g))
```
